```python
import math
import jax, jax.numpy as jnp
from jax import lax
import numpy as np

D_MODEL = 1024
BATCH = 8
SEQ = 8192
DEPTH = 4

N_MIXERS = 2
N_A = (DEPTH + 1) // 2
N_B = DEPTH // 2
NORM_EPS = 1e-6
D_RNN = D_MODEL
RG_HEADS = 8
RG_BW = D_RNN // RG_HEADS
RG_CONV_W = 4
RG_C = 8.0
RG_RAD_MIN = 0.9
RG_RAD_MAX = 0.999
D_S5 = D_MODEL
S5_GC = 16
S5_G = D_S5 // S5_GC
S5_P = 64
S5_DT_MIN = 0.001
S5_DT_MAX = 0.1
D_FF = 3 * D_MODEL
FFN_CONV_W = 3

kernel_name = "hybrid_rglru_s5_convffn_trunk"


def _rmsnorm(x, g):
    x32 = x.astype(jnp.float32)
    var = jnp.mean(x32 * x32, axis=-1, keepdims=True)
    return (x32 * lax.rsqrt(var + NORM_EPS) * g.astype(jnp.float32)).astype(x.dtype)


def _causal_dwconv(x, w, b):
    k_w = w.shape[0]
    s = x.shape[1]
    xp = jnp.pad(x, ((0, 0), (k_w - 1, 0), (0, 0)))
    out = b
    for k in range(k_w):
        out = out + xp[:, k:k + s, :] * w[k]
    return out


def _real_scan_combine(e1, e2):
    a1, b1 = e1
    a2, b2 = e2
    return a1 * a2, a2 * b1 + b2


def _complex_scan_combine(e1, e2):
    a1r, a1i, b1r, b1i = e1
    a2r, a2i, b2r, b2i = e2
    ar = a2r * a1r - a2i * a1i
    ai = a2r * a1i + a2i * a1r
    br = a2r * b1r - a2i * b1i + b2r
    bi = a2r * b1i + a2i * b1r + b2i
    return ar, ai, br, bi


def _rglru_mixer(h, w_in, conv_w, conv_b, w_a, b_a, w_x, b_x, lam, w_out):
    bsz, s, _ = h.shape
    xg = h @ w_in
    xr, gate = xg[..., :D_RNN], xg[..., D_RNN:]
    xr = _causal_dwconv(xr, conv_w, conv_b)
    xh = xr.reshape(bsz, s, RG_HEADS, RG_BW)
    r = jax.nn.sigmoid(jnp.einsum('bshi,hij->bshj', xh, w_a) + b_a).reshape(bsz, s, D_RNN)
    ig = jax.nn.sigmoid(jnp.einsum('bshi,hij->bshj', xh, w_x) + b_x).reshape(bsz, s, D_RNN)
    log_a = -RG_C * r.astype(jnp.float32) * jax.nn.softplus(-lam.astype(jnp.float32))
    a = jnp.exp(log_a)
    mult = jnp.sqrt(-jnp.expm1(2.0 * log_a))
    bterm = mult * (ig * xr).astype(jnp.float32)
    _, hs = lax.associative_scan(_real_scan_combine, (a, bterm), axis=1)
    y = hs.astype(h.dtype) * jax.nn.gelu(gate)
    return y @ w_out


def _s5_mixer(h, w_in, a_re, a_im, log_dt, b_re, b_im, c_re, c_im, d, w_glu, w_out):
    bsz, s, _ = h.shape
    u = h @ w_in
    ug = u.reshape(bsz, s, S5_G, S5_GC).astype(jnp.float32)
    ar = a_re.astype(jnp.float32)
    ai = a_im.astype(jnp.float32)
    dt = jnp.exp(log_dt.astype(jnp.float32))[:, None]
    mag = jnp.exp(ar * dt)
    abr = mag * jnp.cos(ai * dt)
    abi = mag * jnp.sin(ai * dt)
    ur, ui = abr - 1.0, abi
    den = ar * ar + ai * ai
    wr = (ur * ar + ui * ai) / den
    wi = (ui * ar - ur * ai) / den
    br32, bi32 = b_re.astype(jnp.float32), b_im.astype(jnp.float32)
    bbr = wr[..., None] * br32 - wi[..., None] * bi32
    bbi = wr[..., None] * bi32 + wi[..., None] * br32
    bu_r = jnp.einsum('bsgc,gpc->bsgp', ug, bbr)
    bu_i = jnp.einsum('bsgc,gpc->bsgp', ug, bbi)
    a_r = jnp.broadcast_to(abr, (1, s, S5_G, S5_P))
    a_i = jnp.broadcast_to(abi, (1, s, S5_G, S5_P))
    _, _, hr, hi = lax.associative_scan(_complex_scan_combine, (a_r, a_i, bu_r, bu_i), axis=1)
    y = (jnp.einsum('bsgp,gcp->bsgc', hr, c_re.astype(jnp.float32))
         - jnp.einsum('bsgp,gcp->bsgc', hi, c_im.astype(jnp.float32)))
    y = y.reshape(bsz, s, D_S5).astype(h.dtype) + d * u
    g = jax.nn.gelu(y)
    gl = g @ w_glu
    out = gl[..., :D_S5] * jax.nn.sigmoid(gl[..., D_S5:])
    return out @ w_out


def _conv_ffn(h, w_up, conv_w, conv_b, w_down):
    up = _causal_dwconv(h @ w_up, conv_w, conv_b)
    return (jax.nn.gelu(up[..., :D_FF]) * up[..., D_FF:]) @ w_down


def _fwd_setup_inputs(seed: int = 0) -> dict:
    key = jax.random.key(seed)
    ks = jax.random.split(key, 32)
    f32 = jnp.float32
    nrm = lambda k, shp, sc: jax.random.normal(k, shp, f32) * sc
    x = jax.random.normal(ks[0], (BATCH, SEQ, D_MODEL), f32)
    norm_mix_g = 1.0 + nrm(ks[1], (DEPTH, D_MODEL), 0.02)
    norm_ffn_g = 1.0 + nrm(ks[2], (DEPTH, D_MODEL), 0.02)
    norm_final_g = 1.0 + nrm(ks[3], (D_MODEL,), 0.02)
    rg_w_in = nrm(ks[4], (N_A, D_MODEL, 2 * D_RNN), D_MODEL ** -0.5)
    rg_conv_w = nrm(ks[5], (N_A, RG_CONV_W, D_RNN), RG_CONV_W ** -0.5)
    rg_conv_b = nrm(ks[6], (N_A, D_RNN), 0.01)
    rg_w_a = nrm(ks[7], (N_A, RG_HEADS, RG_BW, RG_BW), RG_BW ** -0.5)
    rg_b_a = nrm(ks[8], (N_A, RG_HEADS, RG_BW), 0.01)
    rg_w_x = nrm(ks[9], (N_A, RG_HEADS, RG_BW, RG_BW), RG_BW ** -0.5)
    rg_b_x = nrm(ks[10], (N_A, RG_HEADS, RG_BW), 0.01)
    a0 = jnp.sqrt(jax.random.uniform(ks[11], (N_A, D_RNN), f32,
                                     RG_RAD_MIN ** 2, RG_RAD_MAX ** 2))
    rg_lambda = jnp.log(a0) - jnp.log1p(-a0)
    rg_w_out = nrm(ks[12], (N_A, D_RNN, D_MODEL), D_RNN ** -0.5)
    s5_w_in = nrm(ks[13], (N_B, D_MODEL, D_S5), D_MODEL ** -0.5)
    s5_a_re = -0.5 + nrm(ks[14], (N_B, S5_G, S5_P), 0.01)
    s5_a_im = (math.pi * jnp.arange(S5_P, dtype=f32))[None, None, :] + nrm(ks[15], (N_B, S5_G, S5_P), 0.01)
    s5_log_dt = jax.random.uniform(ks[16], (N_B, S5_G), f32,
                                   math.log(S5_DT_MIN), math.log(S5_DT_MAX))
    s5_b_re = nrm(ks[17], (N_B, S5_G, S5_P, S5_GC), (2 * S5_GC) ** -0.5)
    s5_b_im = nrm(ks[18], (N_B, S5_G, S5_P, S5_GC), (2 * S5_GC) ** -0.5)
    s5_c_re = nrm(ks[19], (N_B, S5_G, S5_GC, S5_P), (0.5 * S5_P) ** -0.5)
    s5_c_im = nrm(ks[20], (N_B, S5_G, S5_GC, S5_P), (0.5 * S5_P) ** -0.5)
    s5_d = nrm(ks[21], (N_B, D_S5), 1.0)
    s5_w_glu = nrm(ks[22], (N_B, D_S5, 2 * D_S5), D_S5 ** -0.5)
    s5_w_out = nrm(ks[23], (N_B, D_S5, D_MODEL), D_S5 ** -0.5)
    ffn_w_up = nrm(ks[24], (DEPTH, D_MODEL, 2 * D_FF), D_MODEL ** -0.5)
    ffn_conv_w = nrm(ks[25], (DEPTH, FFN_CONV_W, 2 * D_FF), FFN_CONV_W ** -0.5)
    ffn_conv_b = nrm(ks[26], (DEPTH, 2 * D_FF), 0.01)
    ffn_w_down = nrm(ks[27], (DEPTH, D_FF, D_MODEL), D_FF ** -0.5)
    return {"x": x, "norm_mix_g": norm_mix_g, "norm_ffn_g": norm_ffn_g, "norm_final_g": norm_final_g,
            "rg_w_in": rg_w_in, "rg_conv_w": rg_conv_w, "rg_conv_b": rg_conv_b,
            "rg_w_a": rg_w_a, "rg_b_a": rg_b_a, "rg_w_x": rg_w_x, "rg_b_x": rg_b_x,
            "rg_lambda": rg_lambda, "rg_w_out": rg_w_out,
            "s5_w_in": s5_w_in, "s5_a_re": s5_a_re, "s5_a_im": s5_a_im, "s5_log_dt": s5_log_dt,
            "s5_b_re": s5_b_re, "s5_b_im": s5_b_im, "s5_c_re": s5_c_re, "s5_c_im": s5_c_im,
            "s5_d": s5_d, "s5_w_glu": s5_w_glu, "s5_w_out": s5_w_out,
            "ffn_w_up": ffn_w_up, "ffn_conv_w": ffn_conv_w, "ffn_conv_b": ffn_conv_b,
            "ffn_w_down": ffn_w_down}


def _fwd_reference(x, norm_mix_g, norm_ffn_g, norm_final_g,
              rg_w_in, rg_conv_w, rg_conv_b, rg_w_a, rg_b_a, rg_w_x, rg_b_x, rg_lambda, rg_w_out,
              s5_w_in, s5_a_re, s5_a_im, s5_log_dt, s5_b_re, s5_b_im, s5_c_re, s5_c_im,
              s5_d, s5_w_glu, s5_w_out,
              ffn_w_up, ffn_conv_w, ffn_conv_b, ffn_w_down):
    h = x
    for i in range(DEPTH):
        hn = _rmsnorm(h, norm_mix_g[i])
        j = i // N_MIXERS
        if i % N_MIXERS == 0:
            mix = _rglru_mixer(hn, rg_w_in[j], rg_conv_w[j], rg_conv_b[j], rg_w_a[j], rg_b_a[j],
                               rg_w_x[j], rg_b_x[j], rg_lambda[j], rg_w_out[j])
        else:
            mix = _s5_mixer(hn, s5_w_in[j], s5_a_re[j], s5_a_im[j], s5_log_dt[j], s5_b_re[j],
                            s5_b_im[j], s5_c_re[j], s5_c_im[j], s5_d[j], s5_w_glu[j], s5_w_out[j])
        h = h + mix.astype(h.dtype)
        hn = _rmsnorm(h, norm_ffn_g[i])
        h = h + _conv_ffn(hn, ffn_w_up[i], ffn_conv_w[i], ffn_conv_b[i], ffn_w_down[i]).astype(h.dtype)
    return _rmsnorm(h, norm_final_g)


import jax as _jax
import jax.numpy as _jnp

TWIN_FORMAT = 'train_step'
FWD_PARAMS = ['x', 'norm_mix_g', 'norm_ffn_g', 'norm_final_g', 'rg_w_in', 'rg_conv_w', 'rg_conv_b', 'rg_w_a', 'rg_b_a', 'rg_w_x', 'rg_b_x', 'rg_lambda', 'rg_w_out', 's5_w_in', 's5_a_re', 's5_a_im', 's5_log_dt', 's5_b_re', 's5_b_im', 's5_c_re', 's5_c_im', 's5_d', 's5_w_glu', 's5_w_out', 'ffn_w_up', 'ffn_conv_w', 'ffn_conv_b', 'ffn_w_down']
TWIN_WEIGHTS = ['norm_mix_g', 'norm_ffn_g', 'norm_final_g', 'rg_w_in', 'rg_conv_w', 'rg_conv_b', 'rg_w_a', 'rg_b_a', 'rg_w_x', 'rg_b_x', 'rg_lambda', 'rg_w_out', 's5_w_in', 's5_a_re', 's5_a_im', 's5_log_dt', 's5_b_re', 's5_b_im', 's5_c_re', 's5_c_im', 's5_d', 's5_w_glu', 's5_w_out', 'ffn_w_up', 'ffn_conv_w', 'ffn_conv_b', 'ffn_w_down']
TWIN_DIFF_INPUT = 'x'
TWIN_INPUTS = ['x', 'norm_mix_g', 'norm_ffn_g', 'norm_final_g', 'rg_w_in', 'rg_conv_w', 'rg_conv_b', 'rg_w_a', 'rg_b_a', 'rg_w_x', 'rg_b_x', 'rg_lambda', 'rg_w_out', 's5_w_in', 's5_a_re', 's5_a_im', 's5_log_dt', 's5_b_re', 's5_b_im', 's5_c_re', 's5_c_im', 's5_d', 's5_w_glu', 's5_w_out', 'ffn_w_up', 'ffn_conv_w', 'ffn_conv_b', 'ffn_w_down', 'loss_target', 'm_norm_mix_g', 'm_norm_ffn_g', 'm_norm_final_g', 'm_rg_w_in', 'm_rg_conv_w', 'm_rg_conv_b', 'm_rg_w_a', 'm_rg_b_a', 'm_rg_w_x', 'm_rg_b_x', 'm_rg_lambda', 'm_rg_w_out', 'm_s5_w_in', 'm_s5_a_re', 'm_s5_a_im', 'm_s5_log_dt', 'm_s5_b_re', 'm_s5_b_im', 'm_s5_c_re', 'm_s5_c_im', 'm_s5_d', 'm_s5_w_glu', 'm_s5_w_out', 'm_ffn_w_up', 'm_ffn_conv_w', 'm_ffn_conv_b', 'm_ffn_w_down', 'v_norm_mix_g', 'v_norm_ffn_g', 'v_norm_final_g', 'v_rg_w_in', 'v_rg_conv_w', 'v_rg_conv_b', 'v_rg_w_a', 'v_rg_b_a', 'v_rg_w_x', 'v_rg_b_x', 'v_rg_lambda', 'v_rg_w_out', 'v_s5_w_in', 'v_s5_a_re', 'v_s5_a_im', 'v_s5_log_dt', 'v_s5_b_re', 'v_s5_b_im', 'v_s5_c_re', 'v_s5_c_im', 'v_s5_d', 'v_s5_w_glu', 'v_s5_w_out', 'v_ffn_w_up', 'v_ffn_conv_w', 'v_ffn_conv_b', 'v_ffn_w_down']
TWIN_OUTPUTS = ['loss', 'grad_x', 'grad_norm_mix_g', 'grad_norm_ffn_g', 'grad_norm_final_g', 'grad_rg_w_in', 'grad_rg_conv_w', 'grad_rg_conv_b', 'grad_rg_w_a', 'grad_rg_b_a', 'grad_rg_w_x', 'grad_rg_b_x', 'grad_rg_lambda', 'grad_rg_w_out', 'grad_s5_w_in', 'grad_s5_a_re', 'grad_s5_a_im', 'grad_s5_log_dt', 'grad_s5_b_re', 'grad_s5_b_im', 'grad_s5_c_re', 'grad_s5_c_im', 'grad_s5_d', 'grad_s5_w_glu', 'grad_s5_w_out', 'grad_ffn_w_up', 'grad_ffn_conv_w', 'grad_ffn_conv_b', 'grad_ffn_w_down', 'delta_norm_mix_g', 'delta_norm_ffn_g', 'delta_norm_final_g', 'delta_rg_w_in', 'delta_rg_conv_w', 'delta_rg_conv_b', 'delta_rg_w_a', 'delta_rg_b_a', 'delta_rg_w_x', 'delta_rg_b_x', 'delta_rg_lambda', 'delta_rg_w_out', 'delta_s5_w_in', 'delta_s5_a_re', 'delta_s5_a_im', 'delta_s5_log_dt', 'delta_s5_b_re', 'delta_s5_b_im', 'delta_s5_c_re', 'delta_s5_c_im', 'delta_s5_d', 'delta_s5_w_glu', 'delta_s5_w_out', 'delta_ffn_w_up', 'delta_ffn_conv_w', 'delta_ffn_conv_b', 'delta_ffn_w_down', 'new_m_norm_mix_g', 'new_m_norm_ffn_g', 'new_m_norm_final_g', 'new_m_rg_w_in', 'new_m_rg_conv_w', 'new_m_rg_conv_b', 'new_m_rg_w_a', 'new_m_rg_b_a', 'new_m_rg_w_x', 'new_m_rg_b_x', 'new_m_rg_lambda', 'new_m_rg_w_out', 'new_m_s5_w_in', 'new_m_s5_a_re', 'new_m_s5_a_im', 'new_m_s5_log_dt', 'new_m_s5_b_re', 'new_m_s5_b_im', 'new_m_s5_c_re', 'new_m_s5_c_im', 'new_m_s5_d', 'new_m_s5_w_glu', 'new_m_s5_w_out', 'new_m_ffn_w_up', 'new_m_ffn_conv_w', 'new_m_ffn_conv_b', 'new_m_ffn_w_down', 'new_v_norm_mix_g', 'new_v_norm_ffn_g', 'new_v_norm_final_g', 'new_v_rg_w_in', 'new_v_rg_conv_w', 'new_v_rg_conv_b', 'new_v_rg_w_a', 'new_v_rg_b_a', 'new_v_rg_w_x', 'new_v_rg_b_x', 'new_v_rg_lambda', 'new_v_rg_w_out', 'new_v_s5_w_in', 'new_v_s5_a_re', 'new_v_s5_a_im', 'new_v_s5_log_dt', 'new_v_s5_b_re', 'new_v_s5_b_im', 'new_v_s5_c_re', 'new_v_s5_c_im', 'new_v_s5_d', 'new_v_s5_w_glu', 'new_v_s5_w_out', 'new_v_ffn_w_up', 'new_v_ffn_conv_w', 'new_v_ffn_conv_b', 'new_v_ffn_w_down']
TWIN_LEAF_KINDS = {'loss': 'loss', 'grad_x': 'grad_x', 'grad_norm_mix_g': 'grad_w', 'grad_norm_ffn_g': 'grad_w', 'grad_norm_final_g': 'grad_w', 'grad_rg_w_in': 'grad_w', 'grad_rg_conv_w': 'grad_w', 'grad_rg_conv_b': 'grad_w', 'grad_rg_w_a': 'grad_w', 'grad_rg_b_a': 'grad_w', 'grad_rg_w_x': 'grad_w', 'grad_rg_b_x': 'grad_w', 'grad_rg_lambda': 'grad_w', 'grad_rg_w_out': 'grad_w', 'grad_s5_w_in': 'grad_w', 'grad_s5_a_re': 'grad_w', 'grad_s5_a_im': 'grad_w', 'grad_s5_log_dt': 'grad_w', 'grad_s5_b_re': 'grad_w', 'grad_s5_b_im': 'grad_w', 'grad_s5_c_re': 'grad_w', 'grad_s5_c_im': 'grad_w', 'grad_s5_d': 'grad_w', 'grad_s5_w_glu': 'grad_w', 'grad_s5_w_out': 'grad_w', 'grad_ffn_w_up': 'grad_w', 'grad_ffn_conv_w': 'grad_w', 'grad_ffn_conv_b': 'grad_w', 'grad_ffn_w_down': 'grad_w', 'delta_norm_mix_g': 'delta_w', 'delta_norm_ffn_g': 'delta_w', 'delta_norm_final_g': 'delta_w', 'delta_rg_w_in': 'delta_w', 'delta_rg_conv_w': 'delta_w', 'delta_rg_conv_b': 'delta_w', 'delta_rg_w_a': 'delta_w', 'delta_rg_b_a': 'delta_w', 'delta_rg_w_x': 'delta_w', 'delta_rg_b_x': 'delta_w', 'delta_rg_lambda': 'delta_w', 'delta_rg_w_out': 'delta_w', 'delta_s5_w_in': 'delta_w', 'delta_s5_a_re': 'delta_w', 'delta_s5_a_im': 'delta_w', 'delta_s5_log_dt': 'delta_w', 'delta_s5_b_re': 'delta_w', 'delta_s5_b_im': 'delta_w', 'delta_s5_c_re': 'delta_w', 'delta_s5_c_im': 'delta_w', 'delta_s5_d': 'delta_w', 'delta_s5_w_glu': 'delta_w', 'delta_s5_w_out': 'delta_w', 'delta_ffn_w_up': 'delta_w', 'delta_ffn_conv_w': 'delta_w', 'delta_ffn_conv_b': 'delta_w', 'delta_ffn_w_down': 'delta_w', 'new_m_norm_mix_g': 'new_m', 'new_m_norm_ffn_g': 'new_m', 'new_m_norm_final_g': 'new_m', 'new_m_rg_w_in': 'new_m', 'new_m_rg_conv_w': 'new_m', 'new_m_rg_conv_b': 'new_m', 'new_m_rg_w_a': 'new_m', 'new_m_rg_b_a': 'new_m', 'new_m_rg_w_x': 'new_m', 'new_m_rg_b_x': 'new_m', 'new_m_rg_lambda': 'new_m', 'new_m_rg_w_out': 'new_m', 'new_m_s5_w_in': 'new_m', 'new_m_s5_a_re': 'new_m', 'new_m_s5_a_im': 'new_m', 'new_m_s5_log_dt': 'new_m', 'new_m_s5_b_re': 'new_m', 'new_m_s5_b_im': 'new_m', 'new_m_s5_c_re': 'new_m', 'new_m_s5_c_im': 'new_m', 'new_m_s5_d': 'new_m', 'new_m_s5_w_glu': 'new_m', 'new_m_s5_w_out': 'new_m', 'new_m_ffn_w_up': 'new_m', 'new_m_ffn_conv_w': 'new_m', 'new_m_ffn_conv_b': 'new_m', 'new_m_ffn_w_down': 'new_m', 'new_v_norm_mix_g': 'new_v', 'new_v_norm_ffn_g': 'new_v', 'new_v_norm_final_g': 'new_v', 'new_v_rg_w_in': 'new_v', 'new_v_rg_conv_w': 'new_v', 'new_v_rg_conv_b': 'new_v', 'new_v_rg_w_a': 'new_v', 'new_v_rg_b_a': 'new_v', 'new_v_rg_w_x': 'new_v', 'new_v_rg_b_x': 'new_v', 'new_v_rg_lambda': 'new_v', 'new_v_rg_w_out': 'new_v', 'new_v_s5_w_in': 'new_v', 'new_v_s5_a_re': 'new_v', 'new_v_s5_a_im': 'new_v', 'new_v_s5_log_dt': 'new_v', 'new_v_s5_b_re': 'new_v', 'new_v_s5_b_im': 'new_v', 'new_v_s5_c_re': 'new_v', 'new_v_s5_c_im': 'new_v', 'new_v_s5_d': 'new_v', 'new_v_s5_w_glu': 'new_v', 'new_v_s5_w_out': 'new_v', 'new_v_ffn_w_up': 'new_v', 'new_v_ffn_conv_w': 'new_v', 'new_v_ffn_conv_b': 'new_v', 'new_v_ffn_w_down': 'new_v'}


def _forward(args):
    return _fwd_reference(*[args[k] for k in FWD_PARAMS])


def _output_shape():
    def fwd():
        inp = _fwd_setup_inputs(0)
        return _fwd_reference(*[inp[k] for k in FWD_PARAMS])
    out = _jax.eval_shape(fwd)
    return out.shape, out.dtype

N_MICROBATCH = 1
ADAM_LR = 0.001
ADAM_B1 = 0.9
ADAM_B2 = 0.999
ADAM_EPS = 1e-08
ADAM_WD = 0.01
ADAM_STEP = 10
PER_EXAMPLE_BATCH_AXIS = {'x': 0, 'loss_target': 0}
SHARED_INPUTS = []
_WEIGHT_DTYPES = {'norm_mix_g': _jnp.float32, 'norm_ffn_g': _jnp.float32, 'norm_final_g': _jnp.float32, 'rg_w_in': _jnp.float32, 'rg_conv_w': _jnp.float32, 'rg_conv_b': _jnp.float32, 'rg_w_a': _jnp.float32, 'rg_b_a': _jnp.float32, 'rg_w_x': _jnp.float32, 'rg_b_x': _jnp.float32, 'rg_lambda': _jnp.float32, 'rg_w_out': _jnp.float32, 's5_w_in': _jnp.float32, 's5_a_re': _jnp.float32, 's5_a_im': _jnp.float32, 's5_log_dt': _jnp.float32, 's5_b_re': _jnp.float32, 's5_b_im': _jnp.float32, 's5_c_re': _jnp.float32, 's5_c_im': _jnp.float32, 's5_d': _jnp.float32, 's5_w_glu': _jnp.float32, 's5_w_out': _jnp.float32, 'ffn_w_up': _jnp.float32, 'ffn_conv_w': _jnp.float32, 'ffn_conv_b': _jnp.float32, 'ffn_w_down': _jnp.float32}
MOMENT_SCALE = {'norm_mix_g': 1.359954e-01, 'norm_ffn_g': 1.888998e-01, 'norm_final_g': 6.405413e+01, 'rg_w_in': 1.167704e-01, 'rg_conv_w': 1.208695e-01, 'rg_conv_b': 6.207006e-01, 'rg_w_a': 2.740639e-02, 'rg_b_a': 2.799869e-02, 'rg_w_x': 4.711592e-02, 'rg_b_x': 4.097976e-02, 'rg_lambda': 6.033894e-02, 'rg_w_out': 1.122435e-01, 's5_w_in': 8.190092e-02, 's5_a_re': 8.362824e-03, 's5_a_im': 8.602576e-03, 's5_log_dt': 3.936784e+00, 's5_b_re': 5.511679e-03, 's5_b_im': 5.379650e-03, 's5_c_re': 5.502761e-03, 's5_c_im': 5.560958e-03, 's5_d': 9.376679e-02, 's5_w_glu': 6.181603e-02, 's5_w_out': 8.530109e-02, 'ffn_w_up': 7.738297e-02, 'ffn_conv_w': 7.846319e-02, 'ffn_conv_b': 8.164870e-02, 'ffn_w_down': 1.320443e-01}


def _to_microbatches(a, axis):
    t = _jnp.moveaxis(a, axis, 0)
    t = t.reshape((N_MICROBATCH, t.shape[0] // N_MICROBATCH) + t.shape[1:])
    return _jnp.moveaxis(t, 1, axis + 1)


def setup_inputs(seed: int = 0) -> dict:
    inp = _fwd_setup_inputs(seed)
    key = _jax.random.fold_in(_jax.random.key(seed), 7919)
    shape, _ = _output_shape()
    out = dict(inp)
    out["loss_target"] = _jax.random.normal(_jax.random.fold_in(key, 0), shape, _jnp.float32)
    for i, name in enumerate(TWIN_WEIGHTS):
        w = inp[name].astype(_jnp.float32)
        if MOMENT_SCALE is None:
            s = _jnp.sqrt(_jnp.mean(_jnp.square(w)) + 1e-30)
        else:
            s = MOMENT_SCALE[name]
        km, kv = _jax.random.split(_jax.random.fold_in(key, i + 1))
        out[name] = w
        out["m_" + name] = s * _jax.random.normal(km, w.shape, _jnp.float32)
        out["v_" + name] = (s * s) * _jax.random.uniform(kv, w.shape, _jnp.float32, 0.5, 1.5)
    if N_MICROBATCH > 1:
        for name, axis in PER_EXAMPLE_BATCH_AXIS.items():
            out[name] = _to_microbatches(out[name], axis)
    return {'x': out['x'], 'norm_mix_g': out['norm_mix_g'], 'norm_ffn_g': out['norm_ffn_g'], 'norm_final_g': out['norm_final_g'], 'rg_w_in': out['rg_w_in'], 'rg_conv_w': out['rg_conv_w'], 'rg_conv_b': out['rg_conv_b'], 'rg_w_a': out['rg_w_a'], 'rg_b_a': out['rg_b_a'], 'rg_w_x': out['rg_w_x'], 'rg_b_x': out['rg_b_x'], 'rg_lambda': out['rg_lambda'], 'rg_w_out': out['rg_w_out'], 's5_w_in': out['s5_w_in'], 's5_a_re': out['s5_a_re'], 's5_a_im': out['s5_a_im'], 's5_log_dt': out['s5_log_dt'], 's5_b_re': out['s5_b_re'], 's5_b_im': out['s5_b_im'], 's5_c_re': out['s5_c_re'], 's5_c_im': out['s5_c_im'], 's5_d': out['s5_d'], 's5_w_glu': out['s5_w_glu'], 's5_w_out': out['s5_w_out'], 'ffn_w_up': out['ffn_w_up'], 'ffn_conv_w': out['ffn_conv_w'], 'ffn_conv_b': out['ffn_conv_b'], 'ffn_w_down': out['ffn_w_down'], 'loss_target': out['loss_target'], 'm_norm_mix_g': out['m_norm_mix_g'], 'm_norm_ffn_g': out['m_norm_ffn_g'], 'm_norm_final_g': out['m_norm_final_g'], 'm_rg_w_in': out['m_rg_w_in'], 'm_rg_conv_w': out['m_rg_conv_w'], 'm_rg_conv_b': out['m_rg_conv_b'], 'm_rg_w_a': out['m_rg_w_a'], 'm_rg_b_a': out['m_rg_b_a'], 'm_rg_w_x': out['m_rg_w_x'], 'm_rg_b_x': out['m_rg_b_x'], 'm_rg_lambda': out['m_rg_lambda'], 'm_rg_w_out': out['m_rg_w_out'], 'm_s5_w_in': out['m_s5_w_in'], 'm_s5_a_re': out['m_s5_a_re'], 'm_s5_a_im': out['m_s5_a_im'], 'm_s5_log_dt': out['m_s5_log_dt'], 'm_s5_b_re': out['m_s5_b_re'], 'm_s5_b_im': out['m_s5_b_im'], 'm_s5_c_re': out['m_s5_c_re'], 'm_s5_c_im': out['m_s5_c_im'], 'm_s5_d': out['m_s5_d'], 'm_s5_w_glu': out['m_s5_w_glu'], 'm_s5_w_out': out['m_s5_w_out'], 'm_ffn_w_up': out['m_ffn_w_up'], 'm_ffn_conv_w': out['m_ffn_conv_w'], 'm_ffn_conv_b': out['m_ffn_conv_b'], 'm_ffn_w_down': out['m_ffn_w_down'], 'v_norm_mix_g': out['v_norm_mix_g'], 'v_norm_ffn_g': out['v_norm_ffn_g'], 'v_norm_final_g': out['v_norm_final_g'], 'v_rg_w_in': out['v_rg_w_in'], 'v_rg_conv_w': out['v_rg_conv_w'], 'v_rg_conv_b': out['v_rg_conv_b'], 'v_rg_w_a': out['v_rg_w_a'], 'v_rg_b_a': out['v_rg_b_a'], 'v_rg_w_x': out['v_rg_w_x'], 'v_rg_b_x': out['v_rg_b_x'], 'v_rg_lambda': out['v_rg_lambda'], 'v_rg_w_out': out['v_rg_w_out'], 'v_s5_w_in': out['v_s5_w_in'], 'v_s5_a_re': out['v_s5_a_re'], 'v_s5_a_im': out['v_s5_a_im'], 'v_s5_log_dt': out['v_s5_log_dt'], 'v_s5_b_re': out['v_s5_b_re'], 'v_s5_b_im': out['v_s5_b_im'], 'v_s5_c_re': out['v_s5_c_re'], 'v_s5_c_im': out['v_s5_c_im'], 'v_s5_d': out['v_s5_d'], 'v_s5_w_glu': out['v_s5_w_glu'], 'v_s5_w_out': out['v_s5_w_out'], 'v_ffn_w_up': out['v_ffn_w_up'], 'v_ffn_conv_w': out['v_ffn_conv_w'], 'v_ffn_conv_b': out['v_ffn_conv_b'], 'v_ffn_w_down': out['v_ffn_w_down']}


def _loss(weights, diff, rest, loss_target):
    with _jax.named_scope("forward"):
        args = {**rest, TWIN_DIFF_INPUT: diff, **{k: w.astype(_WEIGHT_DTYPES[k]) for k, w in weights.items()}}
        y = _forward(args)
    with _jax.named_scope("loss_head"):
        err = _jnp.square(y.astype(_jnp.float32) - loss_target)
        return 0.5 * _jnp.sum(_jnp.mean(err, axis=-1)) if err.ndim else 0.5 * err


def _adamw(w, g, m, v):
    m = ADAM_B1 * m + (1.0 - ADAM_B1) * g
    v = ADAM_B2 * v + (1.0 - ADAM_B2) * _jnp.square(g)
    m_hat = m / (1.0 - ADAM_B1 ** ADAM_STEP)
    v_hat = v / (1.0 - ADAM_B2 ** ADAM_STEP)
    delta = -ADAM_LR * (m_hat / (_jnp.sqrt(v_hat) + ADAM_EPS) + ADAM_WD * w)
    return delta, m, v


def reference(x, norm_mix_g, norm_ffn_g, norm_final_g, rg_w_in, rg_conv_w, rg_conv_b, rg_w_a, rg_b_a, rg_w_x, rg_b_x, rg_lambda, rg_w_out, s5_w_in, s5_a_re, s5_a_im, s5_log_dt, s5_b_re, s5_b_im, s5_c_re, s5_c_im, s5_d, s5_w_glu, s5_w_out, ffn_w_up, ffn_conv_w, ffn_conv_b, ffn_w_down, loss_target, m_norm_mix_g, m_norm_ffn_g, m_norm_final_g, m_rg_w_in, m_rg_conv_w, m_rg_conv_b, m_rg_w_a, m_rg_b_a, m_rg_w_x, m_rg_b_x, m_rg_lambda, m_rg_w_out, m_s5_w_in, m_s5_a_re, m_s5_a_im, m_s5_log_dt, m_s5_b_re, m_s5_b_im, m_s5_c_re, m_s5_c_im, m_s5_d, m_s5_w_glu, m_s5_w_out, m_ffn_w_up, m_ffn_conv_w, m_ffn_conv_b, m_ffn_w_down, v_norm_mix_g, v_norm_ffn_g, v_norm_final_g, v_rg_w_in, v_rg_conv_w, v_rg_conv_b, v_rg_w_a, v_rg_b_a, v_rg_w_x, v_rg_b_x, v_rg_lambda, v_rg_w_out, v_s5_w_in, v_s5_a_re, v_s5_a_im, v_s5_log_dt, v_s5_b_re, v_s5_b_im, v_s5_c_re, v_s5_c_im, v_s5_d, v_s5_w_glu, v_s5_w_out, v_ffn_w_up, v_ffn_conv_w, v_ffn_conv_b, v_ffn_w_down):
    given = dict(x=x, norm_mix_g=norm_mix_g, norm_ffn_g=norm_ffn_g, norm_final_g=norm_final_g, rg_w_in=rg_w_in, rg_conv_w=rg_conv_w, rg_conv_b=rg_conv_b, rg_w_a=rg_w_a, rg_b_a=rg_b_a, rg_w_x=rg_w_x, rg_b_x=rg_b_x, rg_lambda=rg_lambda, rg_w_out=rg_w_out, s5_w_in=s5_w_in, s5_a_re=s5_a_re, s5_a_im=s5_a_im, s5_log_dt=s5_log_dt, s5_b_re=s5_b_re, s5_b_im=s5_b_im, s5_c_re=s5_c_re, s5_c_im=s5_c_im, s5_d=s5_d, s5_w_glu=s5_w_glu, s5_w_out=s5_w_out, ffn_w_up=ffn_w_up, ffn_conv_w=ffn_conv_w, ffn_conv_b=ffn_conv_b, ffn_w_down=ffn_w_down, loss_target=loss_target, m_norm_mix_g=m_norm_mix_g, m_norm_ffn_g=m_norm_ffn_g, m_norm_final_g=m_norm_final_g, m_rg_w_in=m_rg_w_in, m_rg_conv_w=m_rg_conv_w, m_rg_conv_b=m_rg_conv_b, m_rg_w_a=m_rg_w_a, m_rg_b_a=m_rg_b_a, m_rg_w_x=m_rg_w_x, m_rg_b_x=m_rg_b_x, m_rg_lambda=m_rg_lambda, m_rg_w_out=m_rg_w_out, m_s5_w_in=m_s5_w_in, m_s5_a_re=m_s5_a_re, m_s5_a_im=m_s5_a_im, m_s5_log_dt=m_s5_log_dt, m_s5_b_re=m_s5_b_re, m_s5_b_im=m_s5_b_im, m_s5_c_re=m_s5_c_re, m_s5_c_im=m_s5_c_im, m_s5_d=m_s5_d, m_s5_w_glu=m_s5_w_glu, m_s5_w_out=m_s5_w_out, m_ffn_w_up=m_ffn_w_up, m_ffn_conv_w=m_ffn_conv_w, m_ffn_conv_b=m_ffn_conv_b, m_ffn_w_down=m_ffn_w_down, v_norm_mix_g=v_norm_mix_g, v_norm_ffn_g=v_norm_ffn_g, v_norm_final_g=v_norm_final_g, v_rg_w_in=v_rg_w_in, v_rg_conv_w=v_rg_conv_w, v_rg_conv_b=v_rg_conv_b, v_rg_w_a=v_rg_w_a, v_rg_b_a=v_rg_b_a, v_rg_w_x=v_rg_w_x, v_rg_b_x=v_rg_b_x, v_rg_lambda=v_rg_lambda, v_rg_w_out=v_rg_w_out, v_s5_w_in=v_s5_w_in, v_s5_a_re=v_s5_a_re, v_s5_a_im=v_s5_a_im, v_s5_log_dt=v_s5_log_dt, v_s5_b_re=v_s5_b_re, v_s5_b_im=v_s5_b_im, v_s5_c_re=v_s5_c_re, v_s5_c_im=v_s5_c_im, v_s5_d=v_s5_d, v_s5_w_glu=v_s5_w_glu, v_s5_w_out=v_s5_w_out, v_ffn_w_up=v_ffn_w_up, v_ffn_conv_w=v_ffn_conv_w, v_ffn_conv_b=v_ffn_conv_b, v_ffn_w_down=v_ffn_w_down)
    weights = {n: given[n] for n in TWIN_WEIGHTS}
    shared = {n: given[n] for n in SHARED_INPUTS}
    per_example = {n: given[n] for n in ['x']}
    grad_fn = _jax.value_and_grad(_loss, argnums=(0, 1))

    def one_microbatch(ex, loss_target):
        ex = dict(ex)
        diff = ex.pop(TWIN_DIFF_INPUT)
        return grad_fn(weights, diff, {**shared, **ex}, loss_target)

    if N_MICROBATCH == 1:
        loss, (grad_w, grad_x) = one_microbatch(per_example, given["loss_target"])
    else:
        def body(carry, xs):
            loss_sum, grad_sum = carry
            l_k, (gw_k, gx_k) = one_microbatch(xs[0], xs[1])
            with _jax.named_scope("update"):
                return (loss_sum + l_k, _jax.tree.map(_jnp.add, grad_sum, gw_k)), gx_k

        init = (_jnp.zeros((), _jnp.float32), _jax.tree.map(_jnp.zeros_like, weights))
        (loss, grad_w), grad_x = _jax.lax.scan(body, init, (per_example, given["loss_target"]))
    with _jax.named_scope("update"):
        delta_w, new_m, new_v = {}, {}, {}
        for n in TWIN_WEIGHTS:
            delta_w[n], new_m[n], new_v[n] = _adamw(weights[n], grad_w[n], given["m_" + n], given["v_" + n])
    return (loss, grad_x, *[grad_w[n] for n in TWIN_WEIGHTS], *[delta_w[n] for n in TWIN_WEIGHTS],
            *[new_m[n] for n in TWIN_WEIGHTS], *[new_v[n] for n in TWIN_WEIGHTS])
```

```python
import functools
import math

import jax
import jax.numpy as jnp
from jax import lax
from jax.experimental import pallas as pl
from jax.experimental.pallas import tpu as pltpu

F32 = jnp.float32
BF16 = jnp.bfloat16
MXU_DTYPE = jnp.bfloat16

NORM_EPS = 1e-6
RG_C = 8.0
RG_HEADS = 8
RG_BW = 128
S5_G = 64
S5_GC = 16
S5_P = 64
S5_SB = 8
S5_NSB = S5_G // S5_SB
S5_NS = 2 * S5_G * S5_P
ADAM_LR = 0.001
ADAM_B1 = 0.9
ADAM_B2 = 0.999
ADAM_EPS = 1e-08
ADAM_WD = 0.01
ADAM_STEP = 10
N_DEV = 8
VMEM_LIMIT = 56 * 1024 * 1024


def _call(body, **kw):
    return pl.pallas_call(body, **kw)


def _params(sem, vmem=VMEM_LIMIT):
    return pltpu.CompilerParams(dimension_semantics=sem, vmem_limit_bytes=vmem)


_GELU_C = 0.7978845608028654
_GELU_A = 0.044715


def _gelu(x):
    return 0.5 * x * (1.0 + jnp.tanh(_GELU_C * (x + _GELU_A * x * x * x)))


def _gelu_and_grad(x):
    x2 = x * x
    t = jnp.tanh(_GELU_C * (x + _GELU_A * x2 * x))
    g = 0.5 * x * (1.0 + t)
    dg = 0.5 * (1.0 + t) + 0.5 * x * (1.0 - t * t) * _GELU_C * (1.0 + 3.0 * _GELU_A * x2)
    return g, dg


def _sigmoid(x):
    return 1.0 / (1.0 + jnp.exp(-x))


def _neg_expm1(x):
    series = -x * (1.0 + x * (0.5 + x * (1.0 / 6.0 + x * (1.0 / 24.0 + x * (1.0 / 120.0 + x * (1.0 / 720.0))))))
    return jnp.where(x > -0.1, series, 1.0 - jnp.exp(x))


def _rowsum8(x):
    r, c = x.shape
    return x.reshape(r // 8, 8, c).sum(axis=0)


def _dot(a, b):
    return jnp.dot(a.astype(MXU_DTYPE), b.astype(MXU_DTYPE), preferred_element_type=F32)


def _dot_nt(a, b):
    return lax.dot_general(a.astype(MXU_DTYPE), b.astype(MXU_DTYPE), (((1,), (1,)), ((), ())),
                           preferred_element_type=F32)


def _dot_tn(a, b):
    return lax.dot_general(a.astype(MXU_DTYPE), b.astype(MXU_DTYPE), (((0,), (0,)), ((), ())),
                           preferred_element_type=F32)


def _shift_down(x, s, fills, row):
    y = pltpu.roll(x, s, 0)
    for t in range(s):
        y = jnp.where(row == t, fills[s - 1 - t], y)
    return y


def _shift_up(x, s, fills, row):
    n = x.shape[0]
    y = pltpu.roll(x, n - s, 0)
    for t in range(s):
        y = jnp.where(row == n - s + t, fills[t], y)
    return y


def _rms(x):
    r = lax.rsqrt(jnp.mean(x * x, axis=-1, keepdims=True) + NORM_EPS)
    return r, x * r


def _rms_bwd(dhn, xhat, r, g):
    dz = dhn * g
    return r * (dz - xhat * jnp.mean(dz * xhat, axis=-1, keepdims=True))


def norm_matmul(h, g, w, out_dtype, tm, tn, name):
    T, D = h.shape
    N = w.shape[1]

    def body(h_ref, g_ref, w_ref, o_ref, hn_ref, hn_s):
        @pl.when(pl.program_id(1) == 0)
        def _():
            _, xhat = _rms(h_ref[...])
            v = (xhat * g_ref[...]).astype(MXU_DTYPE)
            hn_s[...] = v
            hn_ref[...] = v

        o_ref[...] = jnp.dot(hn_s[...], w_ref[...].astype(MXU_DTYPE), preferred_element_type=F32).astype(o_ref.dtype)

    return _call(
        body, name=name,
        out_shape=(jax.ShapeDtypeStruct((T, N), out_dtype), jax.ShapeDtypeStruct((T, D), MXU_DTYPE)),
        grid=(T // tm, N // tn),
        in_specs=[pl.BlockSpec((tm, D), lambda i, j: (i, 0)), pl.BlockSpec((1, D), lambda i, j: (0, 0)),
                  pl.BlockSpec((D, tn), lambda i, j: (0, j))],
        out_specs=(pl.BlockSpec((tm, tn), lambda i, j: (i, j)), pl.BlockSpec((tm, D), lambda i, j: (i, 0))),
        scratch_shapes=[pltpu.VMEM((tm, D), MXU_DTYPE)],
        compiler_params=_params(("parallel", "arbitrary")),
    )(h, g, w)


def matmul_tn(a, b, bm, bn, bt, name, out_dtype=BF16):
    T, M = a.shape
    N = b.shape[1]
    nk = T // bt

    def body(a_ref, b_ref, o_ref, acc):
        k = pl.program_id(2)

        @pl.when(k == 0)
        def _():
            acc[...] = jnp.zeros_like(acc)

        acc[...] += _dot_tn(a_ref[...], b_ref[...])

        @pl.when(k == nk - 1)
        def _():
            o_ref[...] = acc[...].astype(o_ref.dtype)

    return _call(
        body, name=name,
        out_shape=jax.ShapeDtypeStruct((M, N), out_dtype),
        grid=(M // bm, N // bn, nk),
        in_specs=[pl.BlockSpec((bt, bm), lambda i, j, k: (k, i)), pl.BlockSpec((bt, bn), lambda i, j, k: (k, j))],
        out_specs=pl.BlockSpec((bm, bn), lambda i, j, k: (i, j)),
        scratch_shapes=[pltpu.VMEM((bm, bn), F32)],
        compiler_params=_params(("parallel", "parallel", "arbitrary")),
    )(a, b)


def dx_norm_bwd(dz, w, h, g, dres, tm, name, conv_w=None, chunk=1024):
    T, N = dz.shape
    D = w.shape[0]
    nt = T // tm
    has_conv = conv_w is not None
    kw = conv_w.shape[0] if has_conv else 0

    def body(*refs):
        if has_conv:
            dz_ref, cw_ref, w_ref, h_ref, g_ref, dres_ref, dh_ref, dg_ref, dzp_ref, carry, zs = refs
        else:
            dz_ref, w_ref, h_ref, g_ref, dres_ref, dh_ref, dg_ref = refs
        i = pl.program_id(0)

        @pl.when(i == 0)
        def _():
            dg_ref[...] = jnp.zeros_like(dg_ref)
            if has_conv:
                carry[...] = jnp.zeros_like(carry)

        if has_conv:
            row = lax.broadcasted_iota(jnp.int32, (tm, 1), 0)
            for c0 in range(0, N, chunk):
                sl = slice(c0, c0 + chunk)
                x = dz_ref[:, sl].astype(F32)
                fills = [carry[t:t + 1, sl] for t in range(kw - 1)]
                acc = cw_ref[kw - 1:kw, sl] * x
                for s in range(1, kw):
                    acc = acc + cw_ref[kw - 1 - s:kw - s, sl] * _shift_up(x, s, fills, row)
                zb = acc.astype(MXU_DTYPE)
                zs[:, sl] = zb
                dzp_ref[:, sl] = zb
            carry[...] = dz_ref[0:16, :].astype(F32)
            z = zs[...]
        else:
            z = dz_ref[...]
        dhn = _dot_nt(z, w_ref[...])
        r, xhat = _rms(h_ref[...])
        dg_ref[...] += _rowsum8(dhn * xhat)
        dh_ref[...] = dres_ref[...] + _rms_bwd(dhn, xhat, r, g_ref[...])

    if has_conv:
        ti = lambda i: nt - 1 - i
    else:
        ti = lambda i: i
    row_spec = lambda c: pl.BlockSpec((tm, c), lambda i: (ti(i), 0))
    full = lambda a: pl.BlockSpec(a.shape, lambda i: (0,) * a.ndim)
    in_specs = [row_spec(N)] + ([full(conv_w)] if has_conv else []) + [full(w), row_spec(D), full(g), row_spec(D)]
    out_shape = [jax.ShapeDtypeStruct((T, D), F32), jax.ShapeDtypeStruct((8, D), F32)]
    out_specs = [row_spec(D), pl.BlockSpec((8, D), lambda i: (0, 0))]
    scratch = []
    if has_conv:
        out_shape.append(jax.ShapeDtypeStruct((T, N), MXU_DTYPE))
        out_specs.append(row_spec(N))
        scratch = [pltpu.VMEM((16, N), F32), pltpu.VMEM((tm, N), MXU_DTYPE)]
    args = [dz] + ([conv_w] if has_conv else []) + [w, h, g, dres]
    return _call(
        body, name=name, out_shape=tuple(out_shape), grid=(nt,), in_specs=in_specs, out_specs=tuple(out_specs),
        scratch_shapes=scratch, compiler_params=_params(("arbitrary",)),
    )(*args)


def _ffn_conv_chunk(up_ref, cw_ref, cb_ref, carry, row, sl):
    x = up_ref[:, sl].astype(F32)
    fills = [carry[15:16, sl], carry[14:15, sl]]
    x1 = _shift_down(x, 1, fills, row)
    x2 = _shift_down(x, 2, fills, row)
    out = cb_ref[:, sl] + cw_ref[2:3, sl] * x + cw_ref[1:2, sl] * x1 + cw_ref[0:1, sl] * x2
    return out, x, x1, x2


def ffn_fwd(up, h, conv_w, conv_b, w_down, tm, name, chunk=512):
    T, C = up.shape
    F = C // 2
    D = h.shape[1]

    def body(up_ref, h_ref, cw_ref, cb_ref, wd_ref, o_ref, carry, act_s):
        @pl.when(pl.program_id(0) == 0)
        def _():
            carry[...] = jnp.zeros_like(carry)

        row = lax.broadcasted_iota(jnp.int32, (tm, 1), 0)
        for c0 in range(0, F, chunk):
            a = _ffn_conv_chunk(up_ref, cw_ref, cb_ref, carry, row, slice(c0, c0 + chunk))[0]
            v = _ffn_conv_chunk(up_ref, cw_ref, cb_ref, carry, row, slice(F + c0, F + c0 + chunk))[0]
            act_s[:, c0:c0 + chunk] = (_gelu(a) * v).astype(MXU_DTYPE)
        carry[...] = up_ref[tm - 16:tm, :].astype(F32)
        o_ref[...] = h_ref[...] + jnp.dot(act_s[...], wd_ref[...].astype(MXU_DTYPE), preferred_element_type=F32)

    full = lambda a: pl.BlockSpec(a.shape, lambda i: (0,) * a.ndim)
    return _call(
        body, name=name, out_shape=jax.ShapeDtypeStruct((T, D), F32), grid=(T // tm,),
        in_specs=[pl.BlockSpec((tm, C), lambda i: (i, 0)), pl.BlockSpec((tm, D), lambda i: (i, 0)),
                  full(conv_w), full(conv_b), full(w_down)],
        out_specs=pl.BlockSpec((tm, D), lambda i: (i, 0)),
        scratch_shapes=[pltpu.VMEM((16, C), F32), pltpu.VMEM((tm, F), MXU_DTYPE)],
        compiler_params=_params(("arbitrary",)),
    )(up, h, conv_w, conv_b, w_down)


def ffn_bwd_act(dout, up, conv_w, conv_b, w_down, tm, name, chunk=512):
    T, C = up.shape
    F = C // 2
    D = dout.shape[1]
    kw = conv_w.shape[0]

    def body(do_ref, up_ref, cw_ref, cb_ref, wd_ref, dup_ref, act_ref, dcb_ref, dcw_ref, carry, dact_s):
        @pl.when(pl.program_id(0) == 0)
        def _():
            carry[...] = jnp.zeros_like(carry)
            dcb_ref[...] = jnp.zeros_like(dcb_ref)
            dcw_ref[...] = jnp.zeros_like(dcw_ref)

        dact_s[...] = _dot_nt(do_ref[...], wd_ref[...])
        row = lax.broadcasted_iota(jnp.int32, (tm, 1), 0)
        for c0 in range(0, F, chunk):
            sa = slice(c0, c0 + chunk)
            sv = slice(F + c0, F + c0 + chunk)
            a, xa, xa1, xa2 = _ffn_conv_chunk(up_ref, cw_ref, cb_ref, carry, row, sa)
            v, xv, xv1, xv2 = _ffn_conv_chunk(up_ref, cw_ref, cb_ref, carry, row, sv)
            ga, dga = _gelu_and_grad(a)
            act_ref[:, sa] = (ga * v).astype(MXU_DTYPE)
            dact = dact_s[:, sa]
            da = dact * v * dga
            dv = dact * ga
            dup_ref[:, sa] = da.astype(MXU_DTYPE)
            dup_ref[:, sv] = dv.astype(MXU_DTYPE)
            for d, xs, s in ((da, (xa2, xa1, xa), sa), (dv, (xv2, xv1, xv), sv)):
                dcb_ref[:, s] += _rowsum8(d)
                for k in range(kw):
                    dcw_ref[k, :, s] += _rowsum8(d * xs[k])
        carry[...] = up_ref[tm - 16:tm, :].astype(F32)

    full = lambda a: pl.BlockSpec(a.shape, lambda i: (0,) * a.ndim)
    return _call(
        body, name=name,
        out_shape=(jax.ShapeDtypeStruct((T, C), MXU_DTYPE), jax.ShapeDtypeStruct((T, F), MXU_DTYPE),
                   jax.ShapeDtypeStruct((8, C), F32), jax.ShapeDtypeStruct((kw, 8, C), F32)),
        grid=(T // tm,),
        in_specs=[pl.BlockSpec((tm, D), lambda i: (i, 0)), pl.BlockSpec((tm, C), lambda i: (i, 0)),
                  full(conv_w), full(conv_b), full(w_down)],
        out_specs=(pl.BlockSpec((tm, C), lambda i: (i, 0)), pl.BlockSpec((tm, F), lambda i: (i, 0)),
                   pl.BlockSpec((8, C), lambda i: (0, 0)), pl.BlockSpec((kw, 8, C), lambda i: (0, 0, 0))),
        scratch_shapes=[pltpu.VMEM((16, C), F32), pltpu.VMEM((tm, F), F32)],
        compiler_params=_params(("arbitrary",)),
    )(dout, up, conv_w, conv_b, w_down)


def _rg_gates(xr, wa_ref, ba_ref, wx_ref, bx_ref, sp_ref):
    xb = xr.astype(MXU_DTYPE)
    pa, px = [], []
    for hd in range(RG_HEADS):
        sl = slice(hd * RG_BW, (hd + 1) * RG_BW)
        pa.append(jnp.dot(xb[:, sl], wa_ref[hd].astype(MXU_DTYPE), preferred_element_type=F32))
        px.append(jnp.dot(xb[:, sl], wx_ref[hd].astype(MXU_DTYPE), preferred_element_type=F32))
    r = _sigmoid(jnp.concatenate(pa, axis=1) + ba_ref[...])
    ig = _sigmoid(jnp.concatenate(px, axis=1) + bx_ref[...])
    la = -RG_C * r * sp_ref[...]
    a = jnp.exp(la)
    mult = jnp.sqrt(_neg_expm1(2.0 * la))
    return xb, r, ig, a, mult


def _rg_conv(x, fills, cw_ref, cb_ref, row):
    x1 = _shift_down(x, 1, fills, row)
    x2 = _shift_down(x, 2, fills, row)
    x3 = _shift_down(x, 3, fills, row)
    xr = cb_ref[...] + cw_ref[3:4, :] * x + cw_ref[2:3, :] * x1 + cw_ref[1:2, :] * x2 + cw_ref[0:1, :] * x3
    return xr, (x3, x2, x1, x)


def rg_fwd(xg, h, conv_w, conv_b, w_a, b_a, w_x, b_x, sp, w_out, tm, name):
    T, D2 = xg.shape
    D = D2 // 2
    nb = tm // 8

    def body(xg_ref, h_ref, cw_ref, cb_ref, wa_ref, ba_ref, wx_ref, bx_ref, sp_ref, wo_ref, o_ref, hs_ref,
             xcarry, hcarry, a_s, b_s):
        @pl.when(pl.program_id(0) == 0)
        def _():
            xcarry[...] = jnp.zeros_like(xcarry)
            hcarry[...] = jnp.zeros_like(hcarry)

        row = lax.broadcasted_iota(jnp.int32, (tm, 1), 0)
        x = xg_ref[:, 0:D]
        fills = [xcarry[7:8, :], xcarry[6:7, :], xcarry[5:6, :]]
        xr, _ = _rg_conv(x, fills, cw_ref, cb_ref, row)
        xcarry[...] = xg_ref[tm - 8:tm, 0:D]
        _, r, ig, a, mult = _rg_gates(xr, wa_ref, ba_ref, wx_ref, bx_ref, sp_ref)
        a_s[...] = a
        b_s[...] = mult * ig * xr
        row8 = lax.broadcasted_iota(jnp.int32, (8, 1), 0)

        def blk(j, c):
            o = pl.multiple_of(j * 8, 8)
            A = a_s[pl.ds(o, 8), :]
            H = b_s[pl.ds(o, 8), :]
            for s in (1, 2, 4):
                m = row8 >= s
                H = H + A * jnp.where(m, pltpu.roll(H, s, 0), 0.0)
                A = A * jnp.where(m, pltpu.roll(A, s, 0), 1.0)
            H = H + A * c
            hs_ref[pl.ds(o, 8), :] = H
            return H[7:8, :]

        c = lax.fori_loop(0, nb, blk, hcarry[0:1, :])
        hcarry[0:1, :] = c
        y = hs_ref[...] * _gelu(xg_ref[:, D:D2])
        o_ref[...] = h_ref[...] + _dot(y, wo_ref[...])

    full = lambda a: pl.BlockSpec(a.shape, lambda i: (0,) * a.ndim)
    args = (xg, h, conv_w, conv_b, w_a, b_a, w_x, b_x, sp, w_out)
    return _call(
        body, name=name,
        out_shape=(jax.ShapeDtypeStruct((T, D), F32), jax.ShapeDtypeStruct((T, D), F32)),
        grid=(T // tm,),
        in_specs=[pl.BlockSpec((tm, D2), lambda i: (i, 0)), pl.BlockSpec((tm, D), lambda i: (i, 0))]
        + [full(a) for a in args[2:]],
        out_specs=(pl.BlockSpec((tm, D), lambda i: (i, 0)), pl.BlockSpec((tm, D), lambda i: (i, 0))),
        scratch_shapes=[pltpu.VMEM((8, D), F32), pltpu.VMEM((8, D), F32), pltpu.VMEM((tm, D), F32),
                        pltpu.VMEM((tm, D), F32)],
        compiler_params=_params(("arbitrary",)),
    )(*args)


def rg_bwd(dout, xg, hs, conv_w, conv_b, w_a, b_a, w_x, b_x, sp, w_out, tm, name):
    T, D2 = xg.shape
    D = D2 // 2
    nt = T // tm
    nb = tm // 8
    kw = conv_w.shape[0]

    def body(do_ref, xg_ref, xh_ref, hs_ref, hh_ref, cw_ref, cb_ref, wa_ref, ba_ref, wx_ref, bx_ref, sp_ref, wo_ref,
             dxg_ref, y_ref, dwa_ref, dwx_ref, dba_ref, dbx_ref, dsp_ref, dcb_ref, dcw_ref,
             acarry, lcarry, dcarry, a_s, b_s, l_s):
        i = pl.program_id(0)
        first_tile = i == nt - 1

        @pl.when(i == 0)
        def _():
            for ref in (acarry, lcarry, dcarry, dwa_ref, dwx_ref, dba_ref, dbx_ref, dsp_ref, dcb_ref, dcw_ref):
                ref[...] = jnp.zeros_like(ref)

        row = lax.broadcasted_iota(jnp.int32, (tm, 1), 0)
        keep = jnp.where(first_tile, 0.0, 1.0)
        x = xg_ref[:, 0:D]
        gate = xg_ref[:, D:D2]
        xh = xh_ref[...] * keep
        fills = [xh[7:8, :], xh[6:7, :], xh[5:6, :]]
        xr, taps = _rg_conv(x, fills, cw_ref, cb_ref, row)
        xb, r, ig, a, mult = _rg_gates(xr, wa_ref, ba_ref, wx_ref, bx_ref, sp_ref)
        hs = hs_ref[...]
        hprev = _shift_down(hs, 1, [hh_ref[7:8, :] * keep], row)
        dy = _dot_nt(do_ref[...], wo_ref[...])
        gg, dgg = _gelu_and_grad(gate)
        y_ref[...] = (hs * gg).astype(MXU_DTYPE)
        dxg_ref[:, D:D2] = (dy * hs * dgg).astype(MXU_DTYPE)
        a_s[...] = _shift_up(a, 1, [acarry[0:1, :]], row)
        b_s[...] = dy * gg
        row8 = lax.broadcasted_iota(jnp.int32, (8, 1), 0)

        def blk(jj, c):
            o = pl.multiple_of((nb - 1 - jj) * 8, 8)
            A = a_s[pl.ds(o, 8), :]
            H = b_s[pl.ds(o, 8), :]
            for s in (1, 2, 4):
                m = row8 < 8 - s
                H = H + A * jnp.where(m, pltpu.roll(H, 8 - s, 0), 0.0)
                A = A * jnp.where(m, pltpu.roll(A, 8 - s, 0), 1.0)
            H = H + A * c
            l_s[pl.ds(o, 8), :] = H
            return H[0:1, :]

        c = lax.fori_loop(0, nb, blk, lcarry[0:1, :])
        lcarry[0:1, :] = c
        acarry[0:1, :] = a[0:1, :]
        lam = l_s[...]
        dla = lam * hprev * a - (lam * ig * xr) * (a * a) / mult
        dig = lam * mult * xr
        dxr = lam * mult * ig
        spv = sp_ref[...]
        dsp_ref[...] += _rowsum8(dla * (-RG_C) * r)
        dpa = (dla * (-RG_C) * spv) * r * (1.0 - r)
        dpx = dig * ig * (1.0 - ig)
        dba_ref[...] += _rowsum8(dpa)
        dbx_ref[...] += _rowsum8(dpx)
        dpab = dpa.astype(MXU_DTYPE)
        dpxb = dpx.astype(MXU_DTYPE)
        back = []
        for hd in range(RG_HEADS):
            sl = slice(hd * RG_BW, (hd + 1) * RG_BW)
            dwa_ref[hd] += _dot_tn(xb[:, sl], dpab[:, sl])
            dwx_ref[hd] += _dot_tn(xb[:, sl], dpxb[:, sl])
            back.append(_dot_nt(dpab[:, sl], wa_ref[hd]) + _dot_nt(dpxb[:, sl], wx_ref[hd]))
        dxr = dxr + jnp.concatenate(back, axis=1)
        nfills = [dcarry[0:1, :], dcarry[1:2, :], dcarry[2:3, :]]
        dxp = cw_ref[kw - 1:kw, :] * dxr
        for s in range(1, kw):
            dxp = dxp + cw_ref[kw - 1 - s:kw - s, :] * _shift_up(dxr, s, nfills, row)
        dcarry[...] = dxr[0:8, :]
        dxg_ref[:, 0:D] = dxp.astype(MXU_DTYPE)
        dcb_ref[...] += _rowsum8(dxr)
        for k in range(kw):
            dcw_ref[k] += _rowsum8(dxr * taps[k])

    rt = lambda i: nt - 1 - i
    halo = lambda i: jnp.maximum((nt - 1 - i) * (tm // 8) - 1, 0)
    full = lambda a: pl.BlockSpec(a.shape, lambda i: (0,) * a.ndim)
    params = (conv_w, conv_b, w_a, b_a, w_x, b_x, sp, w_out)
    acc = lambda shape: pl.BlockSpec(shape, lambda i: (0,) * len(shape))
    return _call(
        body, name=name,
        out_shape=(jax.ShapeDtypeStruct((T, D2), MXU_DTYPE), jax.ShapeDtypeStruct((T, D), MXU_DTYPE),
                   jax.ShapeDtypeStruct((RG_HEADS, RG_BW, RG_BW), F32), jax.ShapeDtypeStruct((RG_HEADS, RG_BW, RG_BW), F32),
                   jax.ShapeDtypeStruct((8, D), F32), jax.ShapeDtypeStruct((8, D), F32), jax.ShapeDtypeStruct((8, D), F32),
                   jax.ShapeDtypeStruct((8, D), F32), jax.ShapeDtypeStruct((kw, 8, D), F32)),
        grid=(nt,),
        in_specs=[pl.BlockSpec((tm, D), lambda i: (rt(i), 0)), pl.BlockSpec((tm, D2), lambda i: (rt(i), 0)),
                  pl.BlockSpec((8, D), lambda i: (halo(i), 0)), pl.BlockSpec((tm, D), lambda i: (rt(i), 0)),
                  pl.BlockSpec((8, D), lambda i: (halo(i), 0))] + [full(a) for a in params],
        out_specs=(pl.BlockSpec((tm, D2), lambda i: (rt(i), 0)), pl.BlockSpec((tm, D), lambda i: (rt(i), 0)),
                   acc((RG_HEADS, RG_BW, RG_BW)), acc((RG_HEADS, RG_BW, RG_BW)), acc((8, D)), acc((8, D)), acc((8, D)),
                   acc((8, D)), acc((kw, 8, D))),
        scratch_shapes=[pltpu.VMEM((8, D), F32), pltpu.VMEM((8, D), F32), pltpu.VMEM((8, D), F32),
                        pltpu.VMEM((tm, D), F32), pltpu.VMEM((tm, D), F32), pltpu.VMEM((tm, D), F32)],
        compiler_params=_params(("arbitrary",)),
    )(dout, xg, xg, hs, hs, *params)


_SBW = 2 * S5_SB * S5_P
_SBH = S5_SB * S5_P
_SBU = S5_SB * S5_GC


def _s5_scan_fwd(S, carry, apr_ref, api_ref, nb):
    row8 = lax.broadcasted_iota(jnp.int32, (8, 1), 0)

    def blk(j, _):
        o = pl.multiple_of(j * 8, 8)
        for sb in range(S5_NSB):
            lr = slice(sb * _SBW, sb * _SBW + _SBH)
            li = slice(sb * _SBW + _SBH, (sb + 1) * _SBW)
            la = slice(sb * _SBH, (sb + 1) * _SBH)
            hr = S[pl.ds(o, 8), lr]
            hi = S[pl.ds(o, 8), li]
            for s, idx in ((1, 0), (2, 1), (4, 3)):
                ar = apr_ref[idx:idx + 1, la]
                ai = api_ref[idx:idx + 1, la]
                m = row8 >= s
                sr = jnp.where(m, pltpu.roll(hr, s, 0), 0.0)
                si = jnp.where(m, pltpu.roll(hi, s, 0), 0.0)
                hr, hi = hr + ar * sr - ai * si, hi + ar * si + ai * sr
            cr = carry[0:1, lr]
            ci = carry[0:1, li]
            pr = apr_ref[:, la]
            pi = api_ref[:, la]
            hr, hi = hr + pr * cr - pi * ci, hi + pr * ci + pi * cr
            S[pl.ds(o, 8), lr] = hr
            S[pl.ds(o, 8), li] = hi
            carry[0:1, lr] = hr[7:8, :]
            carry[0:1, li] = hi[7:8, :]
        return 0

    lax.fori_loop(0, nb, blk, 0)


def s5_fwd(u, h, wb, wc, apr, api, d, w_glu, w_out, tm, name):
    T, D = u.shape
    nt = T // tm
    nb = tm // 8

    def body(u_ref, h_ref, wb_ref, wc_ref, apr_ref, api_ref, d_ref, wg_ref, wo_ref, o_ref, yp_ref, gl_ref, st_ref,
             S, carry):
        @pl.when(pl.program_id(0) == 0)
        def _():
            carry[...] = jnp.zeros_like(carry)

        st_ref[...] = carry[...]
        uv = u_ref[...]
        ub = uv.astype(MXU_DTYPE)
        for sb in range(S5_NSB):
            S[:, sb * _SBW:(sb + 1) * _SBW] = jnp.dot(ub[:, sb * _SBU:(sb + 1) * _SBU], wb_ref[sb].astype(MXU_DTYPE),
                                                      preferred_element_type=F32)
        _s5_scan_fwd(S, carry, apr_ref, api_ref, nb)
        ys = [jnp.dot(S[:, sb * _SBW:(sb + 1) * _SBW].astype(MXU_DTYPE), wc_ref[sb].astype(MXU_DTYPE),
                      preferred_element_type=F32) for sb in range(S5_NSB)]
        yp = jnp.concatenate(ys, axis=1) + d_ref[...] * uv
        yp_ref[...] = yp
        gl = _dot(_gelu(yp), wg_ref[...])
        gl_ref[...] = gl
        out = gl[:, 0:D] * _sigmoid(gl[:, D:2 * D])
        o_ref[...] = h_ref[...] + _dot(out, wo_ref[...])

    full = lambda a: pl.BlockSpec(a.shape, lambda i: (0,) * a.ndim)
    args = (u, h, wb, wc, apr, api, d, w_glu, w_out)
    return _call(
        body, name=name,
        out_shape=(jax.ShapeDtypeStruct((T, D), F32), jax.ShapeDtypeStruct((T, D), F32),
                   jax.ShapeDtypeStruct((T, 2 * D), F32), jax.ShapeDtypeStruct((nt, 8, S5_NS), F32)),
        grid=(nt,),
        in_specs=[pl.BlockSpec((tm, D), lambda i: (i, 0)), pl.BlockSpec((tm, D), lambda i: (i, 0))]
        + [full(a) for a in args[2:]],
        out_specs=(pl.BlockSpec((tm, D), lambda i: (i, 0)), pl.BlockSpec((tm, D), lambda i: (i, 0)),
                   pl.BlockSpec((tm, 2 * D), lambda i: (i, 0)), pl.BlockSpec((None, 8, S5_NS), lambda i: (i, 0, 0))),
        scratch_shapes=[pltpu.VMEM((tm, S5_NS), F32), pltpu.VMEM((8, S5_NS), F32)],
        compiler_params=_params(("arbitrary",)),
    )(*args)


def s5_bwd_glu(dout, gl, ypre, u, w_glu, w_out, tm, name):
    T, D = u.shape

    def body(do_ref, gl_ref, yp_ref, u_ref, wg_ref, wo_ref, dy_ref, oact_ref, dgl_ref, gact_ref, dd_ref):
        @pl.when(pl.program_id(0) == 0)
        def _():
            dd_ref[...] = jnp.zeros_like(dd_ref)

        gl1 = gl_ref[:, 0:D]
        sg = _sigmoid(gl_ref[:, D:2 * D])
        oact_ref[...] = (gl1 * sg).astype(MXU_DTYPE)
        dgo = _dot_nt(do_ref[...], wo_ref[...])
        d1 = (dgo * sg).astype(MXU_DTYPE)
        d2 = (dgo * gl1 * sg * (1.0 - sg)).astype(MXU_DTYPE)
        dgl_ref[:, 0:D] = d1
        dgl_ref[:, D:2 * D] = d2
        dg = _dot_nt(d1, wg_ref[:, 0:D]) + _dot_nt(d2, wg_ref[:, D:2 * D])
        g, gd = _gelu_and_grad(yp_ref[...])
        gact_ref[...] = g.astype(MXU_DTYPE)
        dy = dg * gd
        dy_ref[...] = dy
        dd_ref[...] += _rowsum8(dy * u_ref[...])

    full = lambda a: pl.BlockSpec(a.shape, lambda i: (0,) * a.ndim)
    rs = lambda c: pl.BlockSpec((tm, c), lambda i: (i, 0))
    return _call(
        body, name=name,
        out_shape=(jax.ShapeDtypeStruct((T, D), F32), jax.ShapeDtypeStruct((T, D), MXU_DTYPE),
                   jax.ShapeDtypeStruct((T, 2 * D), MXU_DTYPE), jax.ShapeDtypeStruct((T, D), MXU_DTYPE),
                   jax.ShapeDtypeStruct((8, D), F32)),
        grid=(T // tm,),
        in_specs=[rs(D), rs(2 * D), rs(D), rs(D), full(w_glu), full(w_out)],
        out_specs=(rs(D), rs(D), rs(2 * D), rs(D), pl.BlockSpec((8, D), lambda i: (0, 0))),
        compiler_params=_params(("arbitrary",)),
    )(dout, gl, ypre, u, w_glu, w_out)


def s5_bwd_ssm(dy, u, st, wb, wc, apr, api, d, tm, name):
    T, D = u.shape
    nt = T // tm
    nb = tm // 8
    GP = S5_G * S5_P

    aprr, apir = apr[::-1], api[::-1]

    def body(dy_ref, u_ref, st_ref, wb_ref, wc_ref, apr_ref, api_ref, aprr_ref, apir_ref, d_ref,
             du_ref, dwb_ref, dwc_ref, dar_ref, dai_ref, S, L, carry, lcarry):
        @pl.when(pl.program_id(0) == 0)
        def _():
            for ref in (lcarry, dwb_ref, dwc_ref, dar_ref, dai_ref):
                ref[...] = jnp.zeros_like(ref)

        uv = u_ref[...]
        ub = uv.astype(MXU_DTYPE)
        dyv = dy_ref[...]
        dyb = dyv.astype(MXU_DTYPE)
        carry[...] = st_ref[...]
        for sb in range(S5_NSB):
            S[:, sb * _SBW:(sb + 1) * _SBW] = jnp.dot(ub[:, sb * _SBU:(sb + 1) * _SBU], wb_ref[sb].astype(MXU_DTYPE),
                                                      preferred_element_type=F32)
            L[:, sb * _SBW:(sb + 1) * _SBW] = _dot_nt(dyb[:, sb * _SBU:(sb + 1) * _SBU], wc_ref[sb])
        _s5_scan_fwd(S, carry, apr_ref, api_ref, nb)
        row8 = lax.broadcasted_iota(jnp.int32, (8, 1), 0)

        def blk(jj, _):
            j = nb - 1 - jj
            o = pl.multiple_of(j * 8, 8)
            op = pl.multiple_of(jnp.maximum(j - 1, 0) * 8, 8)
            for sb in range(S5_NSB):
                lr = slice(sb * _SBW, sb * _SBW + _SBH)
                li = slice(sb * _SBW + _SBH, (sb + 1) * _SBW)
                la = slice(sb * _SBH, (sb + 1) * _SBH)
                gr = L[pl.ds(o, 8), lr]
                gi = L[pl.ds(o, 8), li]
                for s, idx in ((1, 0), (2, 1), (4, 3)):
                    ar = apr_ref[idx:idx + 1, la]
                    ai = api_ref[idx:idx + 1, la]
                    m = row8 < 8 - s
                    sr = jnp.where(m, pltpu.roll(gr, 8 - s, 0), 0.0)
                    si = jnp.where(m, pltpu.roll(gi, 8 - s, 0), 0.0)
                    gr, gi = gr + ar * sr + ai * si, gi + ar * si - ai * sr
                cr = lcarry[0:1, lr]
                ci = lcarry[0:1, li]
                prr = aprr_ref[:, la]
                pii = apir_ref[:, la]
                gr, gi = gr + prr * cr + pii * ci, gi + prr * ci - pii * cr
                L[pl.ds(o, 8), lr] = gr
                L[pl.ds(o, 8), li] = gi
                lcarry[0:1, lr] = gr[0:1, :]
                lcarry[0:1, li] = gi[0:1, :]
                hr = S[pl.ds(o, 8), lr]
                hi = S[pl.ds(o, 8), li]
                pvr = jnp.where(j == 0, st_ref[0:1, lr], S[pl.ds(op, 8), lr][7:8, :])
                pvi = jnp.where(j == 0, st_ref[0:1, li], S[pl.ds(op, 8), li][7:8, :])
                hpr = jnp.where(row8 == 0, pvr, pltpu.roll(hr, 1, 0))
                hpi = jnp.where(row8 == 0, pvi, pltpu.roll(hi, 1, 0))
                dar_ref[:, la] += gr * hpr + gi * hpi
                dai_ref[:, la] += gi * hpr - gr * hpi
            return 0

        lax.fori_loop(0, nb, blk, 0)
        dus = []
        for sb in range(S5_NSB):
            ls = slice(sb * _SBW, (sb + 1) * _SBW)
            us = slice(sb * _SBU, (sb + 1) * _SBU)
            lb = L[:, ls].astype(MXU_DTYPE)
            dwb_ref[sb] += _dot_tn(lb, ub[:, us])
            dwc_ref[sb] += _dot_tn(S[:, ls].astype(MXU_DTYPE), dyb[:, us])
            dus.append(_dot_nt(lb, wb_ref[sb]))
        du_ref[...] = (jnp.concatenate(dus, axis=1) + dyv * d_ref[...]).astype(MXU_DTYPE)

    rt = lambda i: nt - 1 - i
    full = lambda a: pl.BlockSpec(a.shape, lambda i: (0,) * a.ndim)
    acc = lambda shape: pl.BlockSpec(shape, lambda i: (0,) * len(shape))
    return _call(
        body, name=name,
        out_shape=(jax.ShapeDtypeStruct((T, D), MXU_DTYPE), jax.ShapeDtypeStruct((S5_NSB, _SBW, _SBU), F32),
                   jax.ShapeDtypeStruct((S5_NSB, _SBW, _SBU), F32), jax.ShapeDtypeStruct((8, GP), F32),
                   jax.ShapeDtypeStruct((8, GP), F32)),
        grid=(nt,),
        in_specs=[pl.BlockSpec((tm, D), lambda i: (rt(i), 0)), pl.BlockSpec((tm, D), lambda i: (rt(i), 0)),
                  pl.BlockSpec((None, 8, S5_NS), lambda i: (rt(i), 0, 0)), full(wb), full(wc), full(apr), full(api),
                  full(apr), full(api), full(d)],
        out_specs=(pl.BlockSpec((tm, D), lambda i: (rt(i), 0)), acc((S5_NSB, _SBW, _SBU)), acc((S5_NSB, _SBW, _SBU)),
                   acc((8, GP)), acc((8, GP))),
        scratch_shapes=[pltpu.VMEM((tm, S5_NS), F32), pltpu.VMEM((tm, S5_NS), F32), pltpu.VMEM((8, S5_NS), F32),
                        pltpu.VMEM((8, S5_NS), F32)],
        compiler_params=_params(("arbitrary",)),
    )(dy, u, st, wb, wc, apr, api, aprr, apir, d)


def final_loss(h, g, target, tm, name):
    T, D = h.shape

    def body(h_ref, g_ref, t_ref, dh_ref, se_ref, dg_ref):
        @pl.when(pl.program_id(0) == 0)
        def _():
            se_ref[...] = jnp.zeros_like(se_ref)
            dg_ref[...] = jnp.zeros_like(dg_ref)

        r, xhat = _rms(h_ref[...])
        gv = g_ref[...]
        e = xhat * gv - t_ref[...]
        se_ref[...] += _rowsum8(e * e)
        dy = e * (1.0 / D)
        dg_ref[...] += _rowsum8(dy * xhat)
        dh_ref[...] = _rms_bwd(dy, xhat, r, gv)

    rs = pl.BlockSpec((tm, D), lambda i: (i, 0))
    acc = pl.BlockSpec((8, D), lambda i: (0, 0))
    return _call(
        body, name=name,
        out_shape=(jax.ShapeDtypeStruct((T, D), F32), jax.ShapeDtypeStruct((8, D), F32), jax.ShapeDtypeStruct((8, D), F32)),
        grid=(T // tm,), in_specs=[rs, pl.BlockSpec((1, D), lambda i: (0, 0)), rs], out_specs=(rs, acc, acc),
        compiler_params=_params(("arbitrary",)),
    )(h, g, target)


def _s5_discretize(a_re, a_im, log_dt, b_re, b_im):
    dt = jnp.exp(log_dt)[:, None]
    mag = jnp.exp(a_re * dt)
    abr = mag * jnp.cos(a_im * dt)
    abi = mag * jnp.sin(a_im * dt)
    ur, ui = abr - 1.0, abi
    den = a_re * a_re + a_im * a_im
    wr = (ur * a_re + ui * a_im) / den
    wi = (ui * a_re - ur * a_im) / den
    bbr = wr[..., None] * b_re - wi[..., None] * b_im
    bbi = wr[..., None] * b_im + wi[..., None] * b_re
    return abr, abi, bbr, bbi


def _s5_pack(abr, abi, bbr, bbi, c_re, c_im):
    eye = jnp.eye(S5_SB, dtype=F32)
    b = jnp.stack([bbr, bbi], 0).reshape(2, S5_NSB, S5_SB, S5_P, S5_GC)
    wb = jnp.einsum('rsgpc,gh->shcrgp', b, eye).reshape(S5_NSB, _SBU, _SBW)
    c = jnp.stack([c_re, -c_im], 0).reshape(2, S5_NSB, S5_SB, S5_GC, S5_P)
    wc = jnp.einsum('rsgcp,gh->srgphc', c, eye).reshape(S5_NSB, _SBW, _SBU)
    pr, pi = [abr.reshape(1, -1)], [abi.reshape(1, -1)]
    for _ in range(7):
        r, i = pr[-1], pi[-1]
        pr.append(r * pr[0] - i * pi[0])
        pi.append(r * pi[0] + i * pr[0])
    return wb.astype(MXU_DTYPE), wc.astype(MXU_DTYPE), jnp.concatenate(pr, 0), jnp.concatenate(pi, 0)


def _s5_unpack_grads(dwb_t, dwc, dar8, dai8):
    eye = jnp.eye(S5_SB, dtype=F32)
    t = dwb_t.reshape(S5_NSB, 2, S5_SB, S5_P, S5_SB, S5_GC)
    db = jnp.einsum('srgphc,gh->rsgpc', t, eye).reshape(2, S5_G, S5_P, S5_GC)
    t = dwc.reshape(S5_NSB, 2, S5_SB, S5_P, S5_SB, S5_GC)
    dc = jnp.einsum('srgphc,gh->rsgcp', t, eye).reshape(2, S5_G, S5_GC, S5_P)
    return db[0], db[1], dc[0], -dc[1], dar8.sum(0).reshape(S5_G, S5_P), dai8.sum(0).reshape(S5_G, S5_P)


TM = 256
TM_S5 = 128


def _tn(a, b, name):
    M, N = a.shape[1], b.shape[1]
    return matmul_tn(a, b, min(M, 1024), min(N, 1024), 512 if a.shape[0] % 512 == 0 else a.shape[0], name)


def local_step(x, target, W):
    T, D = x.shape
    depth = W['norm_mix_g'].shape[0]
    row = lambda v: v.reshape(1, -1)
    saved = []
    h = x
    s5c = []
    tr = min(512, T)
    for j in range(W['s5_a_re'].shape[0]):
        prm = (W['s5_a_re'][j], W['s5_a_im'][j], W['s5_log_dt'][j], W['s5_b_re'][j], W['s5_b_im'][j])
        disc, disc_vjp = jax.vjp(_s5_discretize, *prm)
        wb, wc, apr, api = _s5_pack(*disc, W['s5_c_re'][j], W['s5_c_im'][j])
        s5c.append((wb, wc, apr, api, disc_vjp))
    sp_all = jax.nn.softplus(-W['rg_lambda'])
    for i in range(depth):
        j = i // 2
        if i % 2 == 0:
            xg, hn = norm_matmul(h, row(W['norm_mix_g'][i]), W['rg_w_in'][j], F32, tr, 1024, f"rg_in_{i}")
            h1, hs = rg_fwd(xg, h, W['rg_conv_w'][j], row(W['rg_conv_b'][j]), W['rg_w_a'][j].astype(MXU_DTYPE),
                            row(W['rg_b_a'][j]), W['rg_w_x'][j].astype(MXU_DTYPE), row(W['rg_b_x'][j]), row(sp_all[j]),
                            W['rg_w_out'][j], TM, f"rg_fwd_{i}")
            mix = (xg, hn, hs)
        else:
            wb, wc, apr, api, _ = s5c[j]
            u, hn = norm_matmul(h, row(W['norm_mix_g'][i]), W['s5_w_in'][j], F32, tr, 1024, f"s5_in_{i}")
            h1, ypre, gl, st = s5_fwd(u, h, wb, wc, apr, api, row(W['s5_d'][j]), W['s5_w_glu'][j], W['s5_w_out'][j],
                                      TM_S5, f"s5_fwd_{i}")
            mix = (u, hn, ypre, gl, st)
        up, hn2 = norm_matmul(h1, row(W['norm_ffn_g'][i]), W['ffn_w_up'][i], MXU_DTYPE, tr, 768, f"ffn_up_{i}")
        h2 = ffn_fwd(up, h1, W['ffn_conv_w'][i], row(W['ffn_conv_b'][i]), W['ffn_w_down'][i], TM, f"ffn_fwd_{i}")
        saved.append((h, mix, h1, hn2, up))
        h = h2
    dh, se8, dgf8 = final_loss(h, row(W['norm_final_g']), target, tr, "final_loss")
    G = {k: [None] * len(v) for k, v in W.items() if k != 'norm_final_g'}
    G['norm_final_g'] = dgf8.sum(0)
    for i in reversed(range(depth)):
        j = i // 2
        h0, mix, h1, hn2, up = saved[i]
        dup, act, dcb8, dcw8 = ffn_bwd_act(dh, up, W['ffn_conv_w'][i], row(W['ffn_conv_b'][i]), W['ffn_w_down'][i], TM,
                                           f"ffn_bwd_act_{i}")
        G['ffn_w_down'][i] = _tn(act, dh, f"ffn_dwdown_{i}")
        dh1, dg8, dupp = dx_norm_bwd(dup, W['ffn_w_up'][i], h1, row(W['norm_ffn_g'][i]), dh, TM, f"ffn_bwd_in_{i}",
                                     conv_w=W['ffn_conv_w'][i])
        G['ffn_w_up'][i] = _tn(hn2, dupp, f"ffn_dwup_{i}")
        G['ffn_conv_b'][i] = dcb8.sum(0)
        G['ffn_conv_w'][i] = dcw8.sum(1)
        G['norm_ffn_g'][i] = dg8.sum(0)
        if i % 2 == 0:
            xg, hn, hs = mix
            dxg, y, dwa, dwx, dba8, dbx8, dsp8, dcb8, dcw8 = rg_bwd(
                dh1, xg, hs, W['rg_conv_w'][j], row(W['rg_conv_b'][j]), W['rg_w_a'][j].astype(MXU_DTYPE),
                row(W['rg_b_a'][j]), W['rg_w_x'][j].astype(MXU_DTYPE), row(W['rg_b_x'][j]), row(sp_all[j]),
                W['rg_w_out'][j], TM, f"rg_bwd_{i}")
            G['rg_w_out'][j] = _tn(y, dh1, f"rg_dwout_{i}")
            dh, dg8 = dx_norm_bwd(dxg, W['rg_w_in'][j], h0, row(W['norm_mix_g'][i]), dh1, TM, f"rg_bwd_in_{i}")
            G['rg_w_in'][j] = _tn(hn, dxg, f"rg_dwin_{i}")
            G['rg_w_a'][j], G['rg_w_x'][j] = dwa, dwx
            G['rg_b_a'][j] = dba8.sum(0).reshape(RG_HEADS, RG_BW)
            G['rg_b_x'][j] = dbx8.sum(0).reshape(RG_HEADS, RG_BW)
            G['rg_lambda'][j] = dsp8.sum(0) * (-jax.nn.sigmoid(-W['rg_lambda'][j]))
            G['rg_conv_b'][j] = dcb8.sum(0)
            G['rg_conv_w'][j] = dcw8.sum(1)
        else:
            u, hn, ypre, gl, st = mix
            wb, wc, apr, api, disc_vjp = s5c[j]
            dy, oact, dgl, gact, dd8 = s5_bwd_glu(dh1, gl, ypre, u, W['s5_w_glu'][j], W['s5_w_out'][j], TM,
                                                  f"s5_bwd_glu_{i}")
            G['s5_w_out'][j] = _tn(oact, dh1, f"s5_dwout_{i}")
            G['s5_w_glu'][j] = _tn(gact, dgl, f"s5_dwglu_{i}")
            du, dwb_t, dwc, dar8, dai8 = s5_bwd_ssm(dy, u, st, wb, wc, apr, api, row(W['s5_d'][j]),
                                                    TM_S5, f"s5_bwd_ssm_{i}")
            dh, dg8 = dx_norm_bwd(du, W['s5_w_in'][j], h0, row(W['norm_mix_g'][i]), dh1, TM, f"s5_bwd_in_{i}")
            G['s5_w_in'][j] = _tn(hn, du, f"s5_dwin_{i}")
            dbbr, dbbi, dcr, dci, dabr, dabi = _s5_unpack_grads(dwb_t, dwc, dar8, dai8)
            da_re, da_im, dlog_dt, db_re, db_im = disc_vjp((dabr, dabi, dbbr, dbbi))
            G['s5_a_re'][j], G['s5_a_im'][j], G['s5_log_dt'][j] = da_re, da_im, dlog_dt
            G['s5_b_re'][j], G['s5_b_im'][j], G['s5_c_re'][j], G['s5_c_im'][j] = db_re, db_im, dcr, dci
            G['s5_d'][j] = dd8.sum(0)
        G['norm_mix_g'][i] = dg8.sum(0)
    G = {k: (v if (k == 'norm_final_g' or k in BIG) else jnp.stack(v, 0)) for k, v in G.items()}
    return se8, dh, G


BIG = {'rg_w_in': 1, 'rg_w_out': 0, 's5_w_in': 0, 's5_w_glu': 1, 's5_w_out': 0, 'ffn_w_up': 1, 'ffn_w_down': 0}
SMALL_SHARDED = ('rg_conv_w', 'ffn_conv_w', 's5_d')
REPLICATED = ('norm_mix_g', 'norm_ffn_g', 'norm_final_g', 'rg_conv_b', 'rg_w_a', 'rg_b_a', 'rg_w_x', 'rg_b_x',
              'rg_lambda', 's5_a_re', 's5_a_im', 's5_log_dt', 's5_b_re', 's5_b_im', 's5_c_re', 's5_c_im', 'ffn_conv_b')
MESH = pl.DeviceIdType.MESH
ANY = pl.BlockSpec(memory_space=pl.ANY)


def _me():
    x, y, c = lax.axis_index("x"), lax.axis_index("y"), lax.axis_index("c")
    return x, y, c, 4 * x + 2 * y + c


def _win(ref, axis, dev, width):
    idx = [slice(None)] * len(ref.shape)
    idx[axis] = pl.ds(pl.multiple_of(dev * width, width), width)
    return ref.at[tuple(idx)]


def all_gather(items, name):
    n = len(items)
    srcs = [it[0] for it in items]

    def body(*refs):
        ins, outs = refs[:n], refs[n:2 * n]
        send_sems, recv_sems, local_sems = refs[2 * n:]
        x, y, c, me = _me()
        sib = (x, y, 1 - c)
        chips = [(1 - x, y), (x, 1 - y), (1 - x, 1 - y)]
        num = lambda px, py, pc: 4 * px + 2 * py + pc

        def src_of(a):
            return ins[a] if items[a][1] is None else ins[a].at[items[a][1]]

        def block(a, dev):
            return _win(outs[a], items[a][2], dev, src_of(a).shape[items[a][2]])

        def copy(a, k, dev, to, own=False):
            return pltpu.make_async_remote_copy(
                src_ref=src_of(a) if own else block(a, dev), dst_ref=block(a, dev),
                send_sem=send_sems.at[a, k], recv_sem=recv_sems.at[a, k], device_id=to, device_id_type=MESH)

        mine = [pltpu.make_async_copy(src_of(a), block(a, me), local_sems.at[a]) for a in range(n)]
        for cp in mine:
            cp.start()
        sent = []
        for a in range(n):
            sent.append(copy(a, 0, me, sib, own=True))
            sent += [copy(a, 1 + j, me, (*chip, c), own=True) for j, chip in enumerate(chips)]
        for cp in sent:
            cp.start()
        for j, chip in enumerate(chips):
            for a in range(n):
                copy(a, 1 + j, num(*chip, c), (x, y, c)).wait_recv()
                fwd = copy(a, 4 + j, num(*chip, c), sib)
                fwd.start()
                sent.append(fwd)
        for a in range(n):
            copy(a, 0, num(x, y, 1 - c), (x, y, c)).wait_recv()
            for j, chip in enumerate(chips):
                copy(a, 4 + j, num(*chip, 1 - c), (x, y, c)).wait_recv()
        for cp in sent:
            cp.wait_send()
        for cp in mine:
            cp.wait()

    return _call(
        body, name=name,
        out_shape=tuple(jax.ShapeDtypeStruct(it[3], it[0].dtype) for it in items),
        in_specs=[ANY] * n, out_specs=tuple([ANY] * n),
        scratch_shapes=[pltpu.SemaphoreType.DMA((n, 7)), pltpu.SemaphoreType.DMA((n, 7)), pltpu.SemaphoreType.DMA((n,))],
    )(*srcs)


def exchange_parts(items, name):
    flat = []
    ops = []
    for a, (arrs, axis, shp) in enumerate(items):
        if isinstance(arrs, (list, tuple)):
            for l, arr in enumerate(arrs):
                flat.append((len(ops), a, l))
                ops.append(arr)
        else:
            flat.append((len(ops), a, None))
            ops.append(arrs)
    n_in, n_out, nf = len(ops), len(items), len(flat)

    def body(*refs):
        ins, outs = refs[:n_in], refs[n_in:n_in + n_out]
        send_sems, recv_sems, local_sems = refs[n_in + n_out:]
        x, y, c, me = _me()

        def piece(f, dev):
            o, a, l = flat[f]
            axis, shp = items[a][1], items[a][2]
            if l is None:
                return _win(ins[o], len(shp) - 1, dev, shp[-1])
            return _win(ins[o], axis, dev, shp[1 + axis])

        def slot(f, dev):
            o, a, l = flat[f]
            return outs[a].at[dev] if l is None else outs[a].at[dev, l]

        mine = [pltpu.make_async_copy(piece(f, me), slot(f, me), local_sems.at[f]) for f in range(nf)]
        for cp in mine:
            cp.start()
        sent = []
        for k in range(1, N_DEV):
            px, py, pc = (1 - x) if k & 4 else x, (1 - y) if k & 2 else y, (1 - c) if k & 1 else c
            peer = 4 * px + 2 * py + pc
            for f in range(nf):
                cp = pltpu.make_async_remote_copy(
                    src_ref=piece(f, peer), dst_ref=slot(f, me), send_sem=send_sems.at[f, k - 1],
                    recv_sem=recv_sems.at[f, k - 1], device_id=(px, py, pc), device_id_type=MESH)
                cp.start()
                sent.append(cp)
        for k in range(1, N_DEV):
            px, py, pc = (1 - x) if k & 4 else x, (1 - y) if k & 2 else y, (1 - c) if k & 1 else c
            peer = 4 * px + 2 * py + pc
            for f in range(nf):
                pltpu.make_async_remote_copy(
                    src_ref=piece(f, me), dst_ref=slot(f, peer), send_sem=send_sems.at[f, k - 1],
                    recv_sem=recv_sems.at[f, k - 1], device_id=(px, py, pc), device_id_type=MESH).wait_recv()
        for cp in sent:
            cp.wait_send()
        for cp in mine:
            cp.wait()

    return _call(
        body, name=name,
        out_shape=tuple(jax.ShapeDtypeStruct((N_DEV,) + tuple(it[2]), ops[[f[0] for f in flat if f[1] == a][0]].dtype)
                        for a, it in enumerate(items)),
        in_specs=[ANY] * n_in, out_specs=tuple([ANY] * n_out),
        scratch_shapes=[pltpu.SemaphoreType.DMA((nf, 7)), pltpu.SemaphoreType.DMA((nf, 7)), pltpu.SemaphoreType.DMA((nf,))],
    )(*ops)


def adam_update(parts, w, m, v, name):
    R, C = w.shape
    br = next((b for b in (256, 128) if R > b and R % b == 0), R)
    c1 = 1.0 / (1.0 - ADAM_B1 ** ADAM_STEP)
    c2 = 1.0 / (1.0 - ADAM_B2 ** ADAM_STEP)

    def body(p_ref, w_ref, m_ref, v_ref, g_ref, d_ref, nm_ref, nv_ref):
        g = p_ref[0].astype(F32)
        for p in range(1, N_DEV):
            g = g + p_ref[p].astype(F32)
        nm = ADAM_B1 * m_ref[...] + (1.0 - ADAM_B1) * g
        nv = ADAM_B2 * v_ref[...] + (1.0 - ADAM_B2) * (g * g)
        g_ref[...] = g
        nm_ref[...] = nm
        nv_ref[...] = nv
        d_ref[...] = -ADAM_LR * ((nm * c1) / (jnp.sqrt(nv * c2) + ADAM_EPS) + ADAM_WD * w_ref[...])

    bs = pl.BlockSpec((br, C), lambda i: (i, 0))
    out = jax.ShapeDtypeStruct((R, C), F32)
    return _call(
        body, name=name, out_shape=(out, out, out, out), grid=(R // br,),
        in_specs=[pl.BlockSpec((N_DEV, br, C), lambda i: (0, i, 0)), bs, bs, bs], out_specs=(bs, bs, bs, bs),
        compiler_params=_params(("parallel",)),
    )(parts, w, m, v)


def _pack_rows(arrs):
    rows = []
    for a in arrs:
        f = a.reshape(-1)
        r = -(-f.shape[0] // 1024)
        r8 = -(-r // 8) * 8
        rows.append(jnp.pad(f, (0, r8 * 1024 - f.shape[0])).reshape(r8, 1024))
    packed = jnp.concatenate(rows, 0)
    return jnp.pad(packed, ((0, -packed.shape[0] % 128), (0, 0)))


def _unpack_rows(packed, shapes):
    out, o = [], 0
    for s in shapes:
        nel = math.prod(s)
        r8 = -(-(-(-nel // 1024)) // 8) * 8
        out.append(packed[o:o + r8].reshape(-1)[:nel].reshape(s))
        o += r8
    return out


def kernel(x, norm_mix_g, norm_ffn_g, norm_final_g, rg_w_in, rg_conv_w, rg_conv_b, rg_w_a, rg_b_a, rg_w_x, rg_b_x, rg_lambda, rg_w_out, s5_w_in, s5_a_re, s5_a_im, s5_log_dt, s5_b_re, s5_b_im, s5_c_re, s5_c_im, s5_d, s5_w_glu, s5_w_out, ffn_w_up, ffn_conv_w, ffn_conv_b, ffn_w_down, loss_target, m_norm_mix_g, m_norm_ffn_g, m_norm_final_g, m_rg_w_in, m_rg_conv_w, m_rg_conv_b, m_rg_w_a, m_rg_b_a, m_rg_w_x, m_rg_b_x, m_rg_lambda, m_rg_w_out, m_s5_w_in, m_s5_a_re, m_s5_a_im, m_s5_log_dt, m_s5_b_re, m_s5_b_im, m_s5_c_re, m_s5_c_im, m_s5_d, m_s5_w_glu, m_s5_w_out, m_ffn_w_up, m_ffn_conv_w, m_ffn_conv_b, m_ffn_w_down, v_norm_mix_g, v_norm_ffn_g, v_norm_final_g, v_rg_w_in, v_rg_conv_w, v_rg_conv_b, v_rg_w_a, v_rg_b_a, v_rg_w_x, v_rg_b_x, v_rg_lambda, v_rg_w_out, v_s5_w_in, v_s5_a_re, v_s5_a_im, v_s5_log_dt, v_s5_b_re, v_s5_b_im, v_s5_c_re, v_s5_c_im, v_s5_d, v_s5_w_glu, v_s5_w_out, v_ffn_w_up, v_ffn_conv_w, v_ffn_conv_b, v_ffn_w_down):
    names = ('norm_mix_g', 'norm_ffn_g', 'norm_final_g', 'rg_w_in', 'rg_conv_w', 'rg_conv_b', 'rg_w_a', 'rg_b_a',
             'rg_w_x', 'rg_b_x', 'rg_lambda', 'rg_w_out', 's5_w_in', 's5_a_re', 's5_a_im', 's5_log_dt', 's5_b_re',
             's5_b_im', 's5_c_re', 's5_c_im', 's5_d', 's5_w_glu', 's5_w_out', 'ffn_w_up', 'ffn_conv_w', 'ffn_conv_b',
             'ffn_w_down')
    loc = locals()
    Wl = {k: loc[k] for k in names}
    Ml = {k: loc['m_' + k] for k in names}
    Vl = {k: loc['v_' + k] for k in names}

    items, keys = [], []
    for k, axis in BIG.items():
        sh = Wl[k].astype(BF16)
        L, r, c = sh.shape
        full = (r * N_DEV, c) if axis == 0 else (r, c * N_DEV)
        for l in range(L):
            items.append((sh, l, axis, full))
            keys.append((k, l))
    for k in SMALL_SHARDED:
        sh = Wl[k]
        items.append((sh, None, sh.ndim - 1, sh.shape[:-1] + (sh.shape[-1] * N_DEV,)))
        keys.append((k, None))
    gathered = all_gather(items, "gather_weights")
    W = {k: Wl[k] for k in REPLICATED}
    for (k, l), arr in zip(keys, gathered):
        if l is None:
            W[k] = arr
        else:
            W.setdefault(k, []).append(arr)

    se8, gx, G = local_step(x[0], loss_target[0], W)
    loss = lax.psum(0.5 * jnp.sum(se8) / x.shape[-1], ("x", "y", "c"))

    ex_items = []
    for k, axis in BIG.items():
        L, r, c = Wl[k].shape
        ex_items.append((G[k], axis, (L, r, c)))
    for k in SMALL_SHARDED:
        ex_items.append((G[k], None, Wl[k].shape))
    parts = exchange_parts(ex_items, "exchange_grads")
    out_g, out_d, out_m, out_v = {}, {}, {}, {}
    for k, p in zip(list(BIG) + list(SMALL_SHARDED), parts):
        shp = Wl[k].shape
        r2 = (math.prod(shp[:-1]), shp[-1])
        res = adam_update(p.reshape((N_DEV,) + r2), Wl[k].reshape(r2), Ml[k].reshape(r2), Vl[k].reshape(r2), f"adam_{k}")
        out_g[k], out_d[k], out_m[k], out_v[k] = [t.reshape(shp) for t in res]

    rep_shapes = [Wl[k].shape for k in REPLICATED]
    gp = _pack_rows([G[k].astype(F32) for k in REPLICATED])
    allp = all_gather([(gp, None, 0, (N_DEV * gp.shape[0], 1024))], "gather_small_grads")[0]
    res = adam_update(allp.reshape(N_DEV, gp.shape[0], 1024), _pack_rows([Wl[k] for k in REPLICATED]),
                      _pack_rows([Ml[k] for k in REPLICATED]), _pack_rows([Vl[k] for k in REPLICATED]), "adam_replicated")
    for dst, packed in zip((out_g, out_d, out_m, out_v), res):
        for k, t in zip(REPLICATED, _unpack_rows(packed, rep_shapes)):
            dst[k] = t
    return (loss, gx[None], *[out_g[k] for k in names], *[out_d[k] for k in names], *[out_m[k] for k in names],
            *[out_v[k] for k in names])
```

```python
import functools
import math

import jax
import jax.numpy as jnp
from jax import lax
from jax.experimental import pallas as pl
from jax.experimental.pallas import tpu as pltpu

F32 = jnp.float32
BF16 = jnp.bfloat16
MXU_DTYPE = jnp.bfloat16

NORM_EPS = 1e-6
RG_C = 8.0
RG_HEADS = 8
RG_BW = 128
S5_G = 64
S5_GC = 16
S5_P = 64
S5_SB = 8
S5_NSB = S5_G // S5_SB
S5_NS = 2 * S5_G * S5_P
ADAM_LR = 0.001
ADAM_B1 = 0.9
ADAM_B2 = 0.999
ADAM_EPS = 1e-08
ADAM_WD = 0.01
ADAM_STEP = 10
N_DEV = 8
VMEM_LIMIT = 56 * 1024 * 1024


def _call(body, **kw):
    return pl.pallas_call(body, **kw)


def _params(sem, vmem=VMEM_LIMIT):
    return pltpu.CompilerParams(dimension_semantics=sem, vmem_limit_bytes=vmem)


MESH = pl.DeviceIdType.MESH
ANY = pl.BlockSpec(memory_space=pl.ANY)


class Comm:
    def __init__(self, operands, out_shape, scratch, start, middle, finish, deliver):
        self.operands, self.out_shape, self.scratch = list(operands), list(out_shape), list(scratch)
        self.start, self.middle, self.finish, self.deliver = start, middle, finish, deliver


def run_comm(comm, name):
    ci, co = len(comm.operands), len(comm.out_shape)

    def body(*refs):
        parts = (refs[:ci], refs[ci:ci + co], refs[ci + co:])
        comm.start(*parts)
        comm.middle(*parts)
        comm.finish(*parts)

    comm.deliver(_call(body, name=name, out_shape=tuple(comm.out_shape), in_specs=[ANY] * ci,
                       out_specs=tuple([ANY] * co), scratch_shapes=comm.scratch)(*comm.operands))


def _hosted(body, comm, args, *, name, out_shape, grid, in_specs, out_specs, scratch_shapes, compiler_params):
    if comm is None:
        return _call(body, name=name, out_shape=tuple(out_shape), grid=grid, in_specs=in_specs,
                     out_specs=tuple(out_specs), scratch_shapes=scratch_shapes, compiler_params=compiler_params)(*args)
    n_in, n_out, n_sc = len(in_specs), len(out_shape), len(scratch_shapes)
    ci, co = len(comm.operands), len(comm.out_shape)
    nsteps = math.prod(grid)
    mid = (2 * nsteps) // 3

    def wrapped(*refs):
        ins, refs = refs[:n_in], refs[n_in:]
        cins, refs = refs[:ci], refs[ci:]
        outs, refs = refs[:n_out], refs[n_out:]
        couts, refs = refs[:co], refs[co:]
        sc, csc = refs[:n_sc], refs[n_sc:]
        step = pl.program_id(0)
        for d in range(1, len(grid)):
            step = step * grid[d] + pl.program_id(d)

        @pl.when(step == 0)
        def _():
            comm.start(cins, couts, csc)

        body(*ins, *outs, *sc)

        @pl.when(step == mid)
        def _():
            comm.middle(cins, couts, csc)

        @pl.when(step == nsteps - 1)
        def _():
            comm.finish(cins, couts, csc)

    res = _call(wrapped, name=name, out_shape=(*out_shape, *comm.out_shape), grid=grid,
                in_specs=[*in_specs, *[ANY] * ci], out_specs=(*out_specs, *[ANY] * co),
                scratch_shapes=[*scratch_shapes, *comm.scratch],
                compiler_params=_params(("arbitrary",) * len(grid)))(*args, *comm.operands)
    comm.deliver(res[n_out:])
    return res[:n_out]


_GELU_C = 0.7978845608028654
_GELU_A = 0.044715


def _gelu(x):
    return 0.5 * x * (1.0 + jnp.tanh(_GELU_C * (x + _GELU_A * x * x * x)))


def _gelu_and_grad(x):
    x2 = x * x
    t = jnp.tanh(_GELU_C * (x + _GELU_A * x2 * x))
    g = 0.5 * x * (1.0 + t)
    dg = 0.5 * (1.0 + t) + 0.5 * x * (1.0 - t * t) * _GELU_C * (1.0 + 3.0 * _GELU_A * x2)
    return g, dg


def _sigmoid(x):
    return 1.0 / (1.0 + jnp.exp(-x))


def _neg_expm1(x):
    series = -x * (1.0 + x * (0.5 + x * (1.0 / 6.0 + x * (1.0 / 24.0 + x * (1.0 / 120.0 + x * (1.0 / 720.0))))))
    return jnp.where(x > -0.1, series, 1.0 - jnp.exp(x))


def _rowsum8(x):
    r, c = x.shape
    return x.reshape(r // 8, 8, c).sum(axis=0)


def _dot(a, b):
    return jnp.dot(a.astype(MXU_DTYPE), b.astype(MXU_DTYPE), preferred_element_type=F32)


def _dot_nt(a, b):
    return lax.dot_general(a.astype(MXU_DTYPE), b.astype(MXU_DTYPE), (((1,), (1,)), ((), ())),
                           preferred_element_type=F32)


def _dot_tn(a, b):
    return lax.dot_general(a.astype(MXU_DTYPE), b.astype(MXU_DTYPE), (((0,), (0,)), ((), ())),
                           preferred_element_type=F32)


def _shift_down(x, s, fills, row):
    y = pltpu.roll(x, s, 0)
    for t in range(s):
        y = jnp.where(row == t, fills[s - 1 - t], y)
    return y


def _shift_up(x, s, fills, row):
    n = x.shape[0]
    y = pltpu.roll(x, n - s, 0)
    for t in range(s):
        y = jnp.where(row == n - s + t, fills[t], y)
    return y


def _rms(x):
    r = lax.rsqrt(jnp.mean(x * x, axis=-1, keepdims=True) + NORM_EPS)
    return r, x * r


def _rms_bwd(dhn, xhat, r, g):
    dz = dhn * g
    return r * (dz - xhat * jnp.mean(dz * xhat, axis=-1, keepdims=True))


def norm_matmul(h, g, w, out_dtype, tm, tn, name, comm=None):
    T, D = h.shape
    N = w.shape[1]

    def body(h_ref, g_ref, w_ref, o_ref, hn_ref, hn_s):
        @pl.when(pl.program_id(1) == 0)
        def _():
            _, xhat = _rms(h_ref[...])
            v = (xhat * g_ref[...]).astype(MXU_DTYPE)
            hn_s[...] = v
            hn_ref[...] = v

        o_ref[...] = jnp.dot(hn_s[...], w_ref[...].astype(MXU_DTYPE), preferred_element_type=F32).astype(o_ref.dtype)

    return _hosted(
        body, comm, (h, g, w), name=name,
        out_shape=(jax.ShapeDtypeStruct((T, N), out_dtype), jax.ShapeDtypeStruct((T, D), MXU_DTYPE)),
        grid=(T // tm, N // tn),
        in_specs=[pl.BlockSpec((tm, D), lambda i, j: (i, 0)), pl.BlockSpec((1, D), lambda i, j: (0, 0)),
                  pl.BlockSpec((D, tn), lambda i, j: (0, j))],
        out_specs=(pl.BlockSpec((tm, tn), lambda i, j: (i, j)), pl.BlockSpec((tm, D), lambda i, j: (i, 0))),
        scratch_shapes=[pltpu.VMEM((tm, D), MXU_DTYPE)],
        compiler_params=_params(("parallel", "arbitrary")),
    )


def matmul_tn(a, b, bm, bn, bt, name, out_dtype=BF16):
    T, M = a.shape
    N = b.shape[1]
    nk = T // bt

    def body(a_ref, b_ref, o_ref, acc):
        k = pl.program_id(2)

        @pl.when(k == 0)
        def _():
            acc[...] = jnp.zeros_like(acc)

        acc[...] += _dot_tn(a_ref[...], b_ref[...])

        @pl.when(k == nk - 1)
        def _():
            o_ref[...] = acc[...].astype(o_ref.dtype)

    return _call(
        body, name=name,
        out_shape=jax.ShapeDtypeStruct((M, N), out_dtype),
        grid=(M // bm, N // bn, nk),
        in_specs=[pl.BlockSpec((bt, bm), lambda i, j, k: (k, i)), pl.BlockSpec((bt, bn), lambda i, j, k: (k, j))],
        out_specs=pl.BlockSpec((bm, bn), lambda i, j, k: (i, j)),
        scratch_shapes=[pltpu.VMEM((bm, bn), F32)],
        compiler_params=_params(("parallel", "parallel", "arbitrary")),
    )(a, b)


def dx_norm_bwd(dz, w, h, g, dres, tm, name, conv_w=None, chunk=1024):
    T, N = dz.shape
    D = w.shape[0]
    nt = T // tm
    has_conv = conv_w is not None
    kw = conv_w.shape[0] if has_conv else 0

    def body(*refs):
        if has_conv:
            dz_ref, cw_ref, w_ref, h_ref, g_ref, dres_ref, dh_ref, dg_ref, dzp_ref, carry, zs = refs
        else:
            dz_ref, w_ref, h_ref, g_ref, dres_ref, dh_ref, dg_ref = refs
        i = pl.program_id(0)

        @pl.when(i == 0)
        def _():
            dg_ref[...] = jnp.zeros_like(dg_ref)
            if has_conv:
                carry[...] = jnp.zeros_like(carry)

        if has_conv:
            row = lax.broadcasted_iota(jnp.int32, (tm, 1), 0)
            for c0 in range(0, N, chunk):
                sl = slice(c0, c0 + chunk)
                x = dz_ref[:, sl].astype(F32)
                fills = [carry[t:t + 1, sl] for t in range(kw - 1)]
                acc = cw_ref[kw - 1:kw, sl] * x
                for s in range(1, kw):
                    acc = acc + cw_ref[kw - 1 - s:kw - s, sl] * _shift_up(x, s, fills, row)
                zb = acc.astype(MXU_DTYPE)
                zs[:, sl] = zb
                dzp_ref[:, sl] = zb
            carry[...] = dz_ref[0:16, :].astype(F32)
            z = zs[...]
        else:
            z = dz_ref[...]
        dhn = _dot_nt(z, w_ref[...])
        r, xhat = _rms(h_ref[...])
        dg_ref[...] += _rowsum8(dhn * xhat)
        dh_ref[...] = dres_ref[...] + _rms_bwd(dhn, xhat, r, g_ref[...])

    if has_conv:
        ti = lambda i: nt - 1 - i
    else:
        ti = lambda i: i
    row_spec = lambda c: pl.BlockSpec((tm, c), lambda i: (ti(i), 0))
    full = lambda a: pl.BlockSpec(a.shape, lambda i: (0,) * a.ndim)
    in_specs = [row_spec(N)] + ([full(conv_w)] if has_conv else []) + [full(w), row_spec(D), full(g), row_spec(D)]
    out_shape = [jax.ShapeDtypeStruct((T, D), F32), jax.ShapeDtypeStruct((8, D), F32)]
    out_specs = [row_spec(D), pl.BlockSpec((8, D), lambda i: (0, 0))]
    scratch = []
    if has_conv:
        out_shape.append(jax.ShapeDtypeStruct((T, N), MXU_DTYPE))
        out_specs.append(row_spec(N))
        scratch = [pltpu.VMEM((16, N), F32), pltpu.VMEM((tm, N), MXU_DTYPE)]
    args = [dz] + ([conv_w] if has_conv else []) + [w, h, g, dres]
    return _call(
        body, name=name, out_shape=tuple(out_shape), grid=(nt,), in_specs=in_specs, out_specs=tuple(out_specs),
        scratch_shapes=scratch, compiler_params=_params(("arbitrary",)),
    )(*args)


def _ffn_conv_chunk(up_ref, cw_ref, cb_ref, carry, row, sl):
    x = up_ref[:, sl].astype(F32)
    fills = [carry[15:16, sl], carry[14:15, sl]]
    x1 = _shift_down(x, 1, fills, row)
    x2 = _shift_down(x, 2, fills, row)
    out = cb_ref[:, sl] + cw_ref[2:3, sl] * x + cw_ref[1:2, sl] * x1 + cw_ref[0:1, sl] * x2
    return out, x, x1, x2


def ffn_fwd(up, h, conv_w, conv_b, w_down, tm, name, chunk=512):
    T, C = up.shape
    F = C // 2
    D = h.shape[1]

    def body(up_ref, h_ref, cw_ref, cb_ref, wd_ref, o_ref, carry, act_s):
        @pl.when(pl.program_id(0) == 0)
        def _():
            carry[...] = jnp.zeros_like(carry)

        row = lax.broadcasted_iota(jnp.int32, (tm, 1), 0)
        for c0 in range(0, F, chunk):
            a = _ffn_conv_chunk(up_ref, cw_ref, cb_ref, carry, row, slice(c0, c0 + chunk))[0]
            v = _ffn_conv_chunk(up_ref, cw_ref, cb_ref, carry, row, slice(F + c0, F + c0 + chunk))[0]
            act_s[:, c0:c0 + chunk] = (_gelu(a) * v).astype(MXU_DTYPE)
        carry[...] = up_ref[tm - 16:tm, :].astype(F32)
        o_ref[...] = h_ref[...] + jnp.dot(act_s[...], wd_ref[...].astype(MXU_DTYPE), preferred_element_type=F32)

    full = lambda a: pl.BlockSpec(a.shape, lambda i: (0,) * a.ndim)
    return _call(
        body, name=name, out_shape=jax.ShapeDtypeStruct((T, D), F32), grid=(T // tm,),
        in_specs=[pl.BlockSpec((tm, C), lambda i: (i, 0)), pl.BlockSpec((tm, D), lambda i: (i, 0)),
                  full(conv_w), full(conv_b), full(w_down)],
        out_specs=pl.BlockSpec((tm, D), lambda i: (i, 0)),
        scratch_shapes=[pltpu.VMEM((16, C), F32), pltpu.VMEM((tm, F), MXU_DTYPE)],
        compiler_params=_params(("arbitrary",)),
    )(up, h, conv_w, conv_b, w_down)


def ffn_bwd_act(dout, up, conv_w, conv_b, w_down, tm, name, chunk=512, comm=None):
    T, C = up.shape
    F = C // 2
    D = dout.shape[1]
    kw = conv_w.shape[0]

    def body(do_ref, up_ref, cw_ref, cb_ref, wd_ref, dup_ref, act_ref, dcb_ref, dcw_ref, carry, dact_s):
        @pl.when(pl.program_id(0) == 0)
        def _():
            carry[...] = jnp.zeros_like(carry)
            dcb_ref[...] = jnp.zeros_like(dcb_ref)
            dcw_ref[...] = jnp.zeros_like(dcw_ref)

        dact_s[...] = _dot_nt(do_ref[...], wd_ref[...])
        row = lax.broadcasted_iota(jnp.int32, (tm, 1), 0)
        for c0 in range(0, F, chunk):
            sa = slice(c0, c0 + chunk)
            sv = slice(F + c0, F + c0 + chunk)
            a, xa, xa1, xa2 = _ffn_conv_chunk(up_ref, cw_ref, cb_ref, carry, row, sa)
            v, xv, xv1, xv2 = _ffn_conv_chunk(up_ref, cw_ref, cb_ref, carry, row, sv)
            ga, dga = _gelu_and_grad(a)
            act_ref[:, sa] = (ga * v).astype(MXU_DTYPE)
            dact = dact_s[:, sa]
            da = dact * v * dga
            dv = dact * ga
            dup_ref[:, sa] = da.astype(MXU_DTYPE)
            dup_ref[:, sv] = dv.astype(MXU_DTYPE)
            for d, xs, s in ((da, (xa2, xa1, xa), sa), (dv, (xv2, xv1, xv), sv)):
                dcb_ref[:, s] += _rowsum8(d)
                for k in range(kw):
                    dcw_ref[k, :, s] += _rowsum8(d * xs[k])
        carry[...] = up_ref[tm - 16:tm, :].astype(F32)

    full = lambda a: pl.BlockSpec(a.shape, lambda i: (0,) * a.ndim)
    return _hosted(
        body, comm, (dout, up, conv_w, conv_b, w_down), name=name,
        out_shape=(jax.ShapeDtypeStruct((T, C), MXU_DTYPE), jax.ShapeDtypeStruct((T, F), MXU_DTYPE),
                   jax.ShapeDtypeStruct((8, C), F32), jax.ShapeDtypeStruct((kw, 8, C), F32)),
        grid=(T // tm,),
        in_specs=[pl.BlockSpec((tm, D), lambda i: (i, 0)), pl.BlockSpec((tm, C), lambda i: (i, 0)),
                  full(conv_w), full(conv_b), full(w_down)],
        out_specs=(pl.BlockSpec((tm, C), lambda i: (i, 0)), pl.BlockSpec((tm, F), lambda i: (i, 0)),
                   pl.BlockSpec((8, C), lambda i: (0, 0)), pl.BlockSpec((kw, 8, C), lambda i: (0, 0, 0))),
        scratch_shapes=[pltpu.VMEM((16, C), F32), pltpu.VMEM((tm, F), F32)],
        compiler_params=_params(("arbitrary",)),
    )


def _rg_gates(xr, wa_ref, ba_ref, wx_ref, bx_ref, sp_ref):
    xb = xr.astype(MXU_DTYPE)
    pa, px = [], []
    for hd in range(RG_HEADS):
        sl = slice(hd * RG_BW, (hd + 1) * RG_BW)
        pa.append(jnp.dot(xb[:, sl], wa_ref[hd].astype(MXU_DTYPE), preferred_element_type=F32))
        px.append(jnp.dot(xb[:, sl], wx_ref[hd].astype(MXU_DTYPE), preferred_element_type=F32))
    r = _sigmoid(jnp.concatenate(pa, axis=1) + ba_ref[...])
    ig = _sigmoid(jnp.concatenate(px, axis=1) + bx_ref[...])
    la = -RG_C * r * sp_ref[...]
    a = jnp.exp(la)
    mult = jnp.sqrt(_neg_expm1(2.0 * la))
    return xb, r, ig, a, mult


def _rg_conv(x, fills, cw_ref, cb_ref, row):
    x1 = _shift_down(x, 1, fills, row)
    x2 = _shift_down(x, 2, fills, row)
    x3 = _shift_down(x, 3, fills, row)
    xr = cb_ref[...] + cw_ref[3:4, :] * x + cw_ref[2:3, :] * x1 + cw_ref[1:2, :] * x2 + cw_ref[0:1, :] * x3
    return xr, (x3, x2, x1, x)


def rg_fwd(xg, h, conv_w, conv_b, w_a, b_a, w_x, b_x, sp, w_out, tm, name, comm=None):
    T, D2 = xg.shape
    D = D2 // 2
    nb = tm // 8

    def body(xg_ref, h_ref, cw_ref, cb_ref, wa_ref, ba_ref, wx_ref, bx_ref, sp_ref, wo_ref, o_ref, hs_ref,
             xcarry, hcarry, a_s, b_s):
        @pl.when(pl.program_id(0) == 0)
        def _():
            xcarry[...] = jnp.zeros_like(xcarry)
            hcarry[...] = jnp.zeros_like(hcarry)

        row = lax.broadcasted_iota(jnp.int32, (tm, 1), 0)
        x = xg_ref[:, 0:D]
        fills = [xcarry[7:8, :], xcarry[6:7, :], xcarry[5:6, :]]
        xr, _ = _rg_conv(x, fills, cw_ref, cb_ref, row)
        xcarry[...] = xg_ref[tm - 8:tm, 0:D]
        _, r, ig, a, mult = _rg_gates(xr, wa_ref, ba_ref, wx_ref, bx_ref, sp_ref)
        a_s[...] = a
        b_s[...] = mult * ig * xr
        row8 = lax.broadcasted_iota(jnp.int32, (8, 1), 0)

        def blk(j, c):
            o = pl.multiple_of(j * 8, 8)
            A = a_s[pl.ds(o, 8), :]
            H = b_s[pl.ds(o, 8), :]
            for s in (1, 2, 4):
                m = row8 >= s
                H = H + A * jnp.where(m, pltpu.roll(H, s, 0), 0.0)
                A = A * jnp.where(m, pltpu.roll(A, s, 0), 1.0)
            H = H + A * c
            hs_ref[pl.ds(o, 8), :] = H
            return H[7:8, :]

        c = lax.fori_loop(0, nb, blk, hcarry[0:1, :])
        hcarry[0:1, :] = c
        y = hs_ref[...] * _gelu(xg_ref[:, D:D2])
        o_ref[...] = h_ref[...] + _dot(y, wo_ref[...])

    full = lambda a: pl.BlockSpec(a.shape, lambda i: (0,) * a.ndim)
    args = (xg, h, conv_w, conv_b, w_a, b_a, w_x, b_x, sp, w_out)
    return _hosted(
        body, comm, args, name=name,
        out_shape=(jax.ShapeDtypeStruct((T, D), F32), jax.ShapeDtypeStruct((T, D), F32)),
        grid=(T // tm,),
        in_specs=[pl.BlockSpec((tm, D2), lambda i: (i, 0)), pl.BlockSpec((tm, D), lambda i: (i, 0))]
        + [full(a) for a in args[2:]],
        out_specs=(pl.BlockSpec((tm, D), lambda i: (i, 0)), pl.BlockSpec((tm, D), lambda i: (i, 0))),
        scratch_shapes=[pltpu.VMEM((8, D), F32), pltpu.VMEM((8, D), F32), pltpu.VMEM((tm, D), F32),
                        pltpu.VMEM((tm, D), F32)],
        compiler_params=_params(("arbitrary",)),
    )


def rg_bwd(dout, xg, hs, conv_w, conv_b, w_a, b_a, w_x, b_x, sp, w_out, tm, name, comm=None):
    T, D2 = xg.shape
    D = D2 // 2
    nt = T // tm
    nb = tm // 8
    kw = conv_w.shape[0]

    def body(do_ref, xg_ref, xh_ref, hs_ref, hh_ref, cw_ref, cb_ref, wa_ref, ba_ref, wx_ref, bx_ref, sp_ref, wo_ref,
             dxg_ref, y_ref, dwa_ref, dwx_ref, dba_ref, dbx_ref, dsp_ref, dcb_ref, dcw_ref,
             acarry, lcarry, dcarry, a_s, b_s, l_s):
        i = pl.program_id(0)
        first_tile = i == nt - 1

        @pl.when(i == 0)
        def _():
            for ref in (acarry, lcarry, dcarry, dwa_ref, dwx_ref, dba_ref, dbx_ref, dsp_ref, dcb_ref, dcw_ref):
                ref[...] = jnp.zeros_like(ref)

        row = lax.broadcasted_iota(jnp.int32, (tm, 1), 0)
        keep = jnp.where(first_tile, 0.0, 1.0)
        x = xg_ref[:, 0:D]
        gate = xg_ref[:, D:D2]
        xh = xh_ref[...] * keep
        fills = [xh[7:8, :], xh[6:7, :], xh[5:6, :]]
        xr, taps = _rg_conv(x, fills, cw_ref, cb_ref, row)
        xb, r, ig, a, mult = _rg_gates(xr, wa_ref, ba_ref, wx_ref, bx_ref, sp_ref)
        hs = hs_ref[...]
        hprev = _shift_down(hs, 1, [hh_ref[7:8, :] * keep], row)
        dy = _dot_nt(do_ref[...], wo_ref[...])
        gg, dgg = _gelu_and_grad(gate)
        y_ref[...] = (hs * gg).astype(MXU_DTYPE)
        dxg_ref[:, D:D2] = (dy * hs * dgg).astype(MXU_DTYPE)
        a_s[...] = _shift_up(a, 1, [acarry[0:1, :]], row)
        b_s[...] = dy * gg
        row8 = lax.broadcasted_iota(jnp.int32, (8, 1), 0)

        def blk(jj, c):
            o = pl.multiple_of((nb - 1 - jj) * 8, 8)
            A = a_s[pl.ds(o, 8), :]
            H = b_s[pl.ds(o, 8), :]
            for s in (1, 2, 4):
                m = row8 < 8 - s
                H = H + A * jnp.where(m, pltpu.roll(H, 8 - s, 0), 0.0)
                A = A * jnp.where(m, pltpu.roll(A, 8 - s, 0), 1.0)
            H = H + A * c
            l_s[pl.ds(o, 8), :] = H
            return H[0:1, :]

        c = lax.fori_loop(0, nb, blk, lcarry[0:1, :])
        lcarry[0:1, :] = c
        acarry[0:1, :] = a[0:1, :]
        lam = l_s[...]
        dla = lam * hprev * a - (lam * ig * xr) * (a * a) / mult
        dig = lam * mult * xr
        dxr = lam * mult * ig
        spv = sp_ref[...]
        dsp_ref[...] += _rowsum8(dla * (-RG_C) * r)
        dpa = (dla * (-RG_C) * spv) * r * (1.0 - r)
        dpx = dig * ig * (1.0 - ig)
        dba_ref[...] += _rowsum8(dpa)
        dbx_ref[...] += _rowsum8(dpx)
        dpab = dpa.astype(MXU_DTYPE)
        dpxb = dpx.astype(MXU_DTYPE)
        back = []
        for hd in range(RG_HEADS):
            sl = slice(hd * RG_BW, (hd + 1) * RG_BW)
            dwa_ref[hd] += _dot_tn(xb[:, sl], dpab[:, sl])
            dwx_ref[hd] += _dot_tn(xb[:, sl], dpxb[:, sl])
            back.append(_dot_nt(dpab[:, sl], wa_ref[hd]) + _dot_nt(dpxb[:, sl], wx_ref[hd]))
        dxr = dxr + jnp.concatenate(back, axis=1)
        nfills = [dcarry[0:1, :], dcarry[1:2, :], dcarry[2:3, :]]
        dxp = cw_ref[kw - 1:kw, :] * dxr
        for s in range(1, kw):
            dxp = dxp + cw_ref[kw - 1 - s:kw - s, :] * _shift_up(dxr, s, nfills, row)
        dcarry[...] = dxr[0:8, :]
        dxg_ref[:, 0:D] = dxp.astype(MXU_DTYPE)
        dcb_ref[...] += _rowsum8(dxr)
        for k in range(kw):
            dcw_ref[k] += _rowsum8(dxr * taps[k])

    rt = lambda i: nt - 1 - i
    halo = lambda i: jnp.maximum((nt - 1 - i) * (tm // 8) - 1, 0)
    full = lambda a: pl.BlockSpec(a.shape, lambda i: (0,) * a.ndim)
    params = (conv_w, conv_b, w_a, b_a, w_x, b_x, sp, w_out)
    acc = lambda shape: pl.BlockSpec(shape, lambda i: (0,) * len(shape))
    return _hosted(
        body, comm, (dout, xg, xg, hs, hs, *params), name=name,
        out_shape=(jax.ShapeDtypeStruct((T, D2), MXU_DTYPE), jax.ShapeDtypeStruct((T, D), MXU_DTYPE),
                   jax.ShapeDtypeStruct((RG_HEADS, RG_BW, RG_BW), F32), jax.ShapeDtypeStruct((RG_HEADS, RG_BW, RG_BW), F32),
                   jax.ShapeDtypeStruct((8, D), F32), jax.ShapeDtypeStruct((8, D), F32), jax.ShapeDtypeStruct((8, D), F32),
                   jax.ShapeDtypeStruct((8, D), F32), jax.ShapeDtypeStruct((kw, 8, D), F32)),
        grid=(nt,),
        in_specs=[pl.BlockSpec((tm, D), lambda i: (rt(i), 0)), pl.BlockSpec((tm, D2), lambda i: (rt(i), 0)),
                  pl.BlockSpec((8, D), lambda i: (halo(i), 0)), pl.BlockSpec((tm, D), lambda i: (rt(i), 0)),
                  pl.BlockSpec((8, D), lambda i: (halo(i), 0))] + [full(a) for a in params],
        out_specs=(pl.BlockSpec((tm, D2), lambda i: (rt(i), 0)), pl.BlockSpec((tm, D), lambda i: (rt(i), 0)),
                   acc((RG_HEADS, RG_BW, RG_BW)), acc((RG_HEADS, RG_BW, RG_BW)), acc((8, D)), acc((8, D)), acc((8, D)),
                   acc((8, D)), acc((kw, 8, D))),
        scratch_shapes=[pltpu.VMEM((8, D), F32), pltpu.VMEM((8, D), F32), pltpu.VMEM((8, D), F32),
                        pltpu.VMEM((tm, D), F32), pltpu.VMEM((tm, D), F32), pltpu.VMEM((tm, D), F32)],
        compiler_params=_params(("arbitrary",)),
    )


_SBW = 2 * S5_SB * S5_P
_SBH = S5_SB * S5_P
_SBU = S5_SB * S5_GC


def _s5_scan_fwd(S, carry, apr_ref, api_ref, nb):
    row8 = lax.broadcasted_iota(jnp.int32, (8, 1), 0)

    def blk(j, _):
        o = pl.multiple_of(j * 8, 8)
        for sb in range(S5_NSB):
            lr = slice(sb * _SBW, sb * _SBW + _SBH)
            li = slice(sb * _SBW + _SBH, (sb + 1) * _SBW)
            la = slice(sb * _SBH, (sb + 1) * _SBH)
            hr = S[pl.ds(o, 8), lr]
            hi = S[pl.ds(o, 8), li]
            for s, idx in ((1, 0), (2, 1), (4, 3)):
                ar = apr_ref[idx:idx + 1, la]
                ai = api_ref[idx:idx + 1, la]
                m = row8 >= s
                sr = jnp.where(m, pltpu.roll(hr, s, 0), 0.0)
                si = jnp.where(m, pltpu.roll(hi, s, 0), 0.0)
                hr, hi = hr + ar * sr - ai * si, hi + ar * si + ai * sr
            cr = carry[0:1, lr]
            ci = carry[0:1, li]
            pr = apr_ref[:, la]
            pi = api_ref[:, la]
            hr, hi = hr + pr * cr - pi * ci, hi + pr * ci + pi * cr
            S[pl.ds(o, 8), lr] = hr
            S[pl.ds(o, 8), li] = hi
            carry[0:1, lr] = hr[7:8, :]
            carry[0:1, li] = hi[7:8, :]
        return 0

    lax.fori_loop(0, nb, blk, 0)


def s5_fwd(u, h, wb, wc, apr, api, d, w_glu, w_out, tm, name):
    T, D = u.shape
    nt = T // tm
    nb = tm // 8

    def body(u_ref, h_ref, wb_ref, wc_ref, apr_ref, api_ref, d_ref, wg_ref, wo_ref, o_ref, yp_ref, gl_ref, st_ref,
             S, carry):
        @pl.when(pl.program_id(0) == 0)
        def _():
            carry[...] = jnp.zeros_like(carry)

        st_ref[...] = carry[...]
        uv = u_ref[...]
        ub = uv.astype(MXU_DTYPE)
        for sb in range(S5_NSB):
            S[:, sb * _SBW:(sb + 1) * _SBW] = jnp.dot(ub[:, sb * _SBU:(sb + 1) * _SBU], wb_ref[sb].astype(MXU_DTYPE),
                                                      preferred_element_type=F32)
        _s5_scan_fwd(S, carry, apr_ref, api_ref, nb)
        ys = [jnp.dot(S[:, sb * _SBW:(sb + 1) * _SBW].astype(MXU_DTYPE), wc_ref[sb].astype(MXU_DTYPE),
                      preferred_element_type=F32) for sb in range(S5_NSB)]
        yp = jnp.concatenate(ys, axis=1) + d_ref[...] * uv
        yp_ref[...] = yp
        gl = _dot(_gelu(yp), wg_ref[...])
        gl_ref[...] = gl
        out = gl[:, 0:D] * _sigmoid(gl[:, D:2 * D])
        o_ref[...] = h_ref[...] + _dot(out, wo_ref[...])

    full = lambda a: pl.BlockSpec(a.shape, lambda i: (0,) * a.ndim)
    args = (u, h, wb, wc, apr, api, d, w_glu, w_out)
    return _call(
        body, name=name,
        out_shape=(jax.ShapeDtypeStruct((T, D), F32), jax.ShapeDtypeStruct((T, D), F32),
                   jax.ShapeDtypeStruct((T, 2 * D), F32), jax.ShapeDtypeStruct((nt, 8, S5_NS), F32)),
        grid=(nt,),
        in_specs=[pl.BlockSpec((tm, D), lambda i: (i, 0)), pl.BlockSpec((tm, D), lambda i: (i, 0))]
        + [full(a) for a in args[2:]],
        out_specs=(pl.BlockSpec((tm, D), lambda i: (i, 0)), pl.BlockSpec((tm, D), lambda i: (i, 0)),
                   pl.BlockSpec((tm, 2 * D), lambda i: (i, 0)), pl.BlockSpec((None, 8, S5_NS), lambda i: (i, 0, 0))),
        scratch_shapes=[pltpu.VMEM((tm, S5_NS), F32), pltpu.VMEM((8, S5_NS), F32)],
        compiler_params=_params(("arbitrary",)),
    )(*args)


def s5_bwd_glu(dout, gl, ypre, u, w_glu, w_out, tm, name):
    T, D = u.shape

    def body(do_ref, gl_ref, yp_ref, u_ref, wg_ref, wo_ref, dy_ref, oact_ref, dgl_ref, gact_ref, dd_ref):
        @pl.when(pl.program_id(0) == 0)
        def _():
            dd_ref[...] = jnp.zeros_like(dd_ref)

        gl1 = gl_ref[:, 0:D]
        sg = _sigmoid(gl_ref[:, D:2 * D])
        oact_ref[...] = (gl1 * sg).astype(MXU_DTYPE)
        dgo = _dot_nt(do_ref[...], wo_ref[...])
        d1 = (dgo * sg).astype(MXU_DTYPE)
        d2 = (dgo * gl1 * sg * (1.0 - sg)).astype(MXU_DTYPE)
        dgl_ref[:, 0:D] = d1
        dgl_ref[:, D:2 * D] = d2
        dg = _dot_nt(d1, wg_ref[:, 0:D]) + _dot_nt(d2, wg_ref[:, D:2 * D])
        g, gd = _gelu_and_grad(yp_ref[...])
        gact_ref[...] = g.astype(MXU_DTYPE)
        dy = dg * gd
        dy_ref[...] = dy
        dd_ref[...] += _rowsum8(dy * u_ref[...])

    full = lambda a: pl.BlockSpec(a.shape, lambda i: (0,) * a.ndim)
    rs = lambda c: pl.BlockSpec((tm, c), lambda i: (i, 0))
    return _call(
        body, name=name,
        out_shape=(jax.ShapeDtypeStruct((T, D), F32), jax.ShapeDtypeStruct((T, D), MXU_DTYPE),
                   jax.ShapeDtypeStruct((T, 2 * D), MXU_DTYPE), jax.ShapeDtypeStruct((T, D), MXU_DTYPE),
                   jax.ShapeDtypeStruct((8, D), F32)),
        grid=(T // tm,),
        in_specs=[rs(D), rs(2 * D), rs(D), rs(D), full(w_glu), full(w_out)],
        out_specs=(rs(D), rs(D), rs(2 * D), rs(D), pl.BlockSpec((8, D), lambda i: (0, 0))),
        compiler_params=_params(("arbitrary",)),
    )(dout, gl, ypre, u, w_glu, w_out)


def s5_bwd_ssm(dy, u, st, wb, wc, apr, api, d, tm, name):
    T, D = u.shape
    nt = T // tm
    nb = tm // 8
    GP = S5_G * S5_P

    aprr, apir = apr[::-1], api[::-1]

    def body(dy_ref, u_ref, st_ref, wb_ref, wc_ref, apr_ref, api_ref, aprr_ref, apir_ref, d_ref,
             du_ref, dwb_ref, dwc_ref, dar_ref, dai_ref, S, L, carry, lcarry):
        @pl.when(pl.program_id(0) == 0)
        def _():
            for ref in (lcarry, dwb_ref, dwc_ref, dar_ref, dai_ref):
                ref[...] = jnp.zeros_like(ref)

        uv = u_ref[...]
        ub = uv.astype(MXU_DTYPE)
        dyv = dy_ref[...]
        dyb = dyv.astype(MXU_DTYPE)
        carry[...] = st_ref[...]
        for sb in range(S5_NSB):
            S[:, sb * _SBW:(sb + 1) * _SBW] = jnp.dot(ub[:, sb * _SBU:(sb + 1) * _SBU], wb_ref[sb].astype(MXU_DTYPE),
                                                      preferred_element_type=F32)
            L[:, sb * _SBW:(sb + 1) * _SBW] = _dot_nt(dyb[:, sb * _SBU:(sb + 1) * _SBU], wc_ref[sb])
        _s5_scan_fwd(S, carry, apr_ref, api_ref, nb)
        row8 = lax.broadcasted_iota(jnp.int32, (8, 1), 0)

        def blk(jj, _):
            j = nb - 1 - jj
            o = pl.multiple_of(j * 8, 8)
            op = pl.multiple_of(jnp.maximum(j - 1, 0) * 8, 8)
            for sb in range(S5_NSB):
                lr = slice(sb * _SBW, sb * _SBW + _SBH)
                li = slice(sb * _SBW + _SBH, (sb + 1) * _SBW)
                la = slice(sb * _SBH, (sb + 1) * _SBH)
                gr = L[pl.ds(o, 8), lr]
                gi = L[pl.ds(o, 8), li]
                for s, idx in ((1, 0), (2, 1), (4, 3)):
                    ar = apr_ref[idx:idx + 1, la]
                    ai = api_ref[idx:idx + 1, la]
                    m = row8 < 8 - s
                    sr = jnp.where(m, pltpu.roll(gr, 8 - s, 0), 0.0)
                    si = jnp.where(m, pltpu.roll(gi, 8 - s, 0), 0.0)
                    gr, gi = gr + ar * sr + ai * si, gi + ar * si - ai * sr
                cr = lcarry[0:1, lr]
                ci = lcarry[0:1, li]
                prr = aprr_ref[:, la]
                pii = apir_ref[:, la]
                gr, gi = gr + prr * cr + pii * ci, gi + prr * ci - pii * cr
                L[pl.ds(o, 8), lr] = gr
                L[pl.ds(o, 8), li] = gi
                lcarry[0:1, lr] = gr[0:1, :]
                lcarry[0:1, li] = gi[0:1, :]
                hr = S[pl.ds(o, 8), lr]
                hi = S[pl.ds(o, 8), li]
                pvr = jnp.where(j == 0, st_ref[0:1, lr], S[pl.ds(op, 8), lr][7:8, :])
                pvi = jnp.where(j == 0, st_ref[0:1, li], S[pl.ds(op, 8), li][7:8, :])
                hpr = jnp.where(row8 == 0, pvr, pltpu.roll(hr, 1, 0))
                hpi = jnp.where(row8 == 0, pvi, pltpu.roll(hi, 1, 0))
                dar_ref[:, la] += gr * hpr + gi * hpi
                dai_ref[:, la] += gi * hpr - gr * hpi
            return 0

        lax.fori_loop(0, nb, blk, 0)
        dus = []
        for sb in range(S5_NSB):
            ls = slice(sb * _SBW, (sb + 1) * _SBW)
            us = slice(sb * _SBU, (sb + 1) * _SBU)
            lb = L[:, ls].astype(MXU_DTYPE)
            dwb_ref[sb] += _dot_tn(lb, ub[:, us])
            dwc_ref[sb] += _dot_tn(S[:, ls].astype(MXU_DTYPE), dyb[:, us])
            dus.append(_dot_nt(lb, wb_ref[sb]))
        du_ref[...] = (jnp.concatenate(dus, axis=1) + dyv * d_ref[...]).astype(MXU_DTYPE)

    rt = lambda i: nt - 1 - i
    full = lambda a: pl.BlockSpec(a.shape, lambda i: (0,) * a.ndim)
    acc = lambda shape: pl.BlockSpec(shape, lambda i: (0,) * len(shape))
    return _call(
        body, name=name,
        out_shape=(jax.ShapeDtypeStruct((T, D), MXU_DTYPE), jax.ShapeDtypeStruct((S5_NSB, _SBW, _SBU), F32),
                   jax.ShapeDtypeStruct((S5_NSB, _SBW, _SBU), F32), jax.ShapeDtypeStruct((8, GP), F32),
                   jax.ShapeDtypeStruct((8, GP), F32)),
        grid=(nt,),
        in_specs=[pl.BlockSpec((tm, D), lambda i: (rt(i), 0)), pl.BlockSpec((tm, D), lambda i: (rt(i), 0)),
                  pl.BlockSpec((None, 8, S5_NS), lambda i: (rt(i), 0, 0)), full(wb), full(wc), full(apr), full(api),
                  full(apr), full(api), full(d)],
        out_specs=(pl.BlockSpec((tm, D), lambda i: (rt(i), 0)), acc((S5_NSB, _SBW, _SBU)), acc((S5_NSB, _SBW, _SBU)),
                   acc((8, GP)), acc((8, GP))),
        scratch_shapes=[pltpu.VMEM((tm, S5_NS), F32), pltpu.VMEM((tm, S5_NS), F32), pltpu.VMEM((8, S5_NS), F32),
                        pltpu.VMEM((8, S5_NS), F32)],
        compiler_params=_params(("arbitrary",)),
    )(dy, u, st, wb, wc, apr, api, aprr, apir, d)


def final_loss(h, g, target, tm, name):
    T, D = h.shape

    def body(h_ref, g_ref, t_ref, dh_ref, se_ref, dg_ref):
        @pl.when(pl.program_id(0) == 0)
        def _():
            se_ref[...] = jnp.zeros_like(se_ref)
            dg_ref[...] = jnp.zeros_like(dg_ref)

        r, xhat = _rms(h_ref[...])
        gv = g_ref[...]
        e = xhat * gv - t_ref[...]
        se_ref[...] += _rowsum8(e * e)
        dy = e * (1.0 / D)
        dg_ref[...] += _rowsum8(dy * xhat)
        dh_ref[...] = _rms_bwd(dy, xhat, r, gv)

    rs = pl.BlockSpec((tm, D), lambda i: (i, 0))
    acc = pl.BlockSpec((8, D), lambda i: (0, 0))
    return _call(
        body, name=name,
        out_shape=(jax.ShapeDtypeStruct((T, D), F32), jax.ShapeDtypeStruct((8, D), F32), jax.ShapeDtypeStruct((8, D), F32)),
        grid=(T // tm,), in_specs=[rs, pl.BlockSpec((1, D), lambda i: (0, 0)), rs], out_specs=(rs, acc, acc),
        compiler_params=_params(("arbitrary",)),
    )(h, g, target)


def _s5_discretize(a_re, a_im, log_dt, b_re, b_im):
    dt = jnp.exp(log_dt)[:, None]
    mag = jnp.exp(a_re * dt)
    abr = mag * jnp.cos(a_im * dt)
    abi = mag * jnp.sin(a_im * dt)
    ur, ui = abr - 1.0, abi
    den = a_re * a_re + a_im * a_im
    wr = (ur * a_re + ui * a_im) / den
    wi = (ui * a_re - ur * a_im) / den
    bbr = wr[..., None] * b_re - wi[..., None] * b_im
    bbi = wr[..., None] * b_im + wi[..., None] * b_re
    return abr, abi, bbr, bbi


def _s5_pack(abr, abi, bbr, bbi, c_re, c_im):
    eye = jnp.eye(S5_SB, dtype=F32)
    b = jnp.stack([bbr, bbi], 0).reshape(2, S5_NSB, S5_SB, S5_P, S5_GC)
    wb = jnp.einsum('rsgpc,gh->shcrgp', b, eye).reshape(S5_NSB, _SBU, _SBW)
    c = jnp.stack([c_re, -c_im], 0).reshape(2, S5_NSB, S5_SB, S5_GC, S5_P)
    wc = jnp.einsum('rsgcp,gh->srgphc', c, eye).reshape(S5_NSB, _SBW, _SBU)
    pr, pi = [abr.reshape(1, -1)], [abi.reshape(1, -1)]
    for _ in range(7):
        r, i = pr[-1], pi[-1]
        pr.append(r * pr[0] - i * pi[0])
        pi.append(r * pi[0] + i * pr[0])
    return wb.astype(MXU_DTYPE), wc.astype(MXU_DTYPE), jnp.concatenate(pr, 0), jnp.concatenate(pi, 0)


def _s5_unpack_grads(dwb_t, dwc, dar8, dai8):
    eye = jnp.eye(S5_SB, dtype=F32)
    t = dwb_t.reshape(S5_NSB, 2, S5_SB, S5_P, S5_SB, S5_GC)
    db = jnp.einsum('srgphc,gh->rsgpc', t, eye).reshape(2, S5_G, S5_P, S5_GC)
    t = dwc.reshape(S5_NSB, 2, S5_SB, S5_P, S5_SB, S5_GC)
    dc = jnp.einsum('srgphc,gh->rsgcp', t, eye).reshape(2, S5_G, S5_GC, S5_P)
    return db[0], db[1], dc[0], -dc[1], dar8.sum(0).reshape(S5_G, S5_P), dai8.sum(0).reshape(S5_G, S5_P)


TM = 256
TM_S5 = 128


def _tn(a, b, name):
    M, N = a.shape[1], b.shape[1]
    return matmul_tn(a, b, min(M, 1024), min(N, 1024), 512 if a.shape[0] % 512 == 0 else a.shape[0], name)


def local_step(x, target, W, sched):
    T, D = x.shape
    depth = W['norm_mix_g'].shape[0]
    row = lambda v: v.reshape(1, -1)
    saved = []
    h = x
    s5c = []
    tr = min(512, T)
    for j in range(W['s5_a_re'].shape[0]):
        prm = (W['s5_a_re'][j], W['s5_a_im'][j], W['s5_log_dt'][j], W['s5_b_re'][j], W['s5_b_im'][j])
        disc, disc_vjp = jax.vjp(_s5_discretize, *prm)
        wb, wc, apr, api = _s5_pack(*disc, W['s5_c_re'][j], W['s5_c_im'][j])
        s5c.append((wb, wc, apr, api, disc_vjp))
    sp_all = jax.nn.softplus(-W['rg_lambda'])
    for i in range(depth):
        j = i // 2
        if i % 2 == 0:
            xg, hn = norm_matmul(h, row(W['norm_mix_g'][i]), W['rg_w_in'][j], F32, tr, 1024, f"rg_in_{i}")
            h1, hs = rg_fwd(xg, h, W['rg_conv_w'][j], row(W['rg_conv_b'][j]), W['rg_w_a'][j].astype(MXU_DTYPE),
                            row(W['rg_b_a'][j]), W['rg_w_x'][j].astype(MXU_DTYPE), row(W['rg_b_x'][j]), row(sp_all[j]),
                            W['rg_w_out'][j], TM, f"rg_fwd_{i}", comm=sched.comm(f"rg_fwd_{i}"))
            mix = (xg, hn, hs)
        else:
            wb, wc, apr, api, _ = s5c[j]
            u, hn = norm_matmul(h, row(W['norm_mix_g'][i]), W['s5_w_in'][j], F32, tr, 1024, f"s5_in_{i}")
            h1, ypre, gl, st = s5_fwd(u, h, wb, wc, apr, api, row(W['s5_d'][j]), W['s5_w_glu'][j], W['s5_w_out'][j],
                                      TM_S5, f"s5_fwd_{i}")
            mix = (u, hn, ypre, gl, st)
        up, hn2 = norm_matmul(h1, row(W['norm_ffn_g'][i]), W['ffn_w_up'][i], MXU_DTYPE, tr, 768, f"ffn_up_{i}",
                              comm=sched.comm(f"ffn_up_{i}"))
        h2 = ffn_fwd(up, h1, W['ffn_conv_w'][i], row(W['ffn_conv_b'][i]), W['ffn_w_down'][i], TM, f"ffn_fwd_{i}")
        saved.append((h, mix, h1, hn2, up))
        h = h2
    dh, se8, dgf8 = final_loss(h, row(W['norm_final_g']), target, tr, "final_loss")
    G = {k: [None] * len(v) for k, v in W.items() if k != 'norm_final_g'}
    G['norm_final_g'] = dgf8.sum(0)
    for i in reversed(range(depth)):
        j = i // 2
        h0, mix, h1, hn2, up = saved[i]
        dup, act, dcb8, dcw8 = ffn_bwd_act(dh, up, W['ffn_conv_w'][i], row(W['ffn_conv_b'][i]), W['ffn_w_down'][i], TM,
                                           f"ffn_bwd_act_{i}", comm=sched.comm(f"ffn_bwd_act_{i}", G))
        G['ffn_w_down'][i] = _tn(act, dh, f"ffn_dwdown_{i}")
        dh1, dg8, dupp = dx_norm_bwd(dup, W['ffn_w_up'][i], h1, row(W['norm_ffn_g'][i]), dh, TM, f"ffn_bwd_in_{i}",
                                     conv_w=W['ffn_conv_w'][i])
        G['ffn_w_up'][i] = _tn(hn2, dupp, f"ffn_dwup_{i}")
        G['ffn_conv_b'][i] = dcb8.sum(0)
        G['ffn_conv_w'][i] = dcw8.sum(1)
        G['norm_ffn_g'][i] = dg8.sum(0)
        if i % 2 == 0:
            xg, hn, hs = mix
            dxg, y, dwa, dwx, dba8, dbx8, dsp8, dcb8, dcw8 = rg_bwd(
                dh1, xg, hs, W['rg_conv_w'][j], row(W['rg_conv_b'][j]), W['rg_w_a'][j].astype(MXU_DTYPE),
                row(W['rg_b_a'][j]), W['rg_w_x'][j].astype(MXU_DTYPE), row(W['rg_b_x'][j]), row(sp_all[j]),
                W['rg_w_out'][j], TM, f"rg_bwd_{i}", comm=sched.comm(f"rg_bwd_{i}", G))
            G['rg_w_out'][j] = _tn(y, dh1, f"rg_dwout_{i}")
            dh, dg8 = dx_norm_bwd(dxg, W['rg_w_in'][j], h0, row(W['norm_mix_g'][i]), dh1, TM, f"rg_bwd_in_{i}")
            G['rg_w_in'][j] = _tn(hn, dxg, f"rg_dwin_{i}")
            G['rg_w_a'][j], G['rg_w_x'][j] = dwa, dwx
            G['rg_b_a'][j] = dba8.sum(0).reshape(RG_HEADS, RG_BW)
            G['rg_b_x'][j] = dbx8.sum(0).reshape(RG_HEADS, RG_BW)
            G['rg_lambda'][j] = dsp8.sum(0) * (-jax.nn.sigmoid(-W['rg_lambda'][j]))
            G['rg_conv_b'][j] = dcb8.sum(0)
            G['rg_conv_w'][j] = dcw8.sum(1)
        else:
            u, hn, ypre, gl, st = mix
            wb, wc, apr, api, disc_vjp = s5c[j]
            dy, oact, dgl, gact, dd8 = s5_bwd_glu(dh1, gl, ypre, u, W['s5_w_glu'][j], W['s5_w_out'][j], TM,
                                                  f"s5_bwd_glu_{i}")
            G['s5_w_out'][j] = _tn(oact, dh1, f"s5_dwout_{i}")
            G['s5_w_glu'][j] = _tn(gact, dgl, f"s5_dwglu_{i}")
            du, dwb_t, dwc, dar8, dai8 = s5_bwd_ssm(dy, u, st, wb, wc, apr, api, row(W['s5_d'][j]),
                                                    TM_S5, f"s5_bwd_ssm_{i}")
            dh, dg8 = dx_norm_bwd(du, W['s5_w_in'][j], h0, row(W['norm_mix_g'][i]), dh1, TM, f"s5_bwd_in_{i}")
            G['s5_w_in'][j] = _tn(hn, du, f"s5_dwin_{i}")
            dbbr, dbbi, dcr, dci, dabr, dabi = _s5_unpack_grads(dwb_t, dwc, dar8, dai8)
            da_re, da_im, dlog_dt, db_re, db_im = disc_vjp((dabr, dabi, dbbr, dbbi))
            G['s5_a_re'][j], G['s5_a_im'][j], G['s5_log_dt'][j] = da_re, da_im, dlog_dt
            G['s5_b_re'][j], G['s5_b_im'][j], G['s5_c_re'][j], G['s5_c_im'][j] = db_re, db_im, dcr, dci
            G['s5_d'][j] = dd8.sum(0)
        G['norm_mix_g'][i] = dg8.sum(0)
    G = {k: (v if (k == 'norm_final_g' or k in BIG) else jnp.stack(v, 0)) for k, v in G.items()}
    return se8, dh, G


BIG = {'rg_w_in': 1, 'rg_w_out': 0, 's5_w_in': 0, 's5_w_glu': 1, 's5_w_out': 0, 'ffn_w_up': 1, 'ffn_w_down': 0}
SMALL_SHARDED = ('rg_conv_w', 'ffn_conv_w', 's5_d')
REPLICATED = ('norm_mix_g', 'norm_ffn_g', 'norm_final_g', 'rg_conv_b', 'rg_w_a', 'rg_b_a', 'rg_w_x', 'rg_b_x',
              'rg_lambda', 's5_a_re', 's5_a_im', 's5_log_dt', 's5_b_re', 's5_b_im', 's5_c_re', 's5_c_im', 'ffn_conv_b')


def _me():
    x, y, c = lax.axis_index("x"), lax.axis_index("y"), lax.axis_index("c")
    return x, y, c, 4 * x + 2 * y + c


def _win(ref, axis, dev, width):
    idx = [slice(None)] * len(ref.shape)
    idx[axis] = pl.ds(pl.multiple_of(dev * width, width), width)
    return ref.at[tuple(idx)]


def gather_plan(items, deliver):
    n = len(items)

    def tools(ins, outs, sems):
        send_sems, recv_sems, local_sems = sems
        x, y, c, me = _me()
        sib = (x, y, 1 - c)
        chips = [(1 - x, y), (x, 1 - y), (1 - x, 1 - y)]
        num = lambda px, py, pc: 4 * px + 2 * py + pc

        def src_of(a):
            return ins[a] if items[a][1] is None else ins[a].at[items[a][1]]

        def block(a, dev):
            return _win(outs[a], items[a][2], dev, src_of(a).shape[items[a][2]])

        def copy(a, k, dev, to, own=False):
            return pltpu.make_async_remote_copy(
                src_ref=src_of(a) if own else block(a, dev), dst_ref=block(a, dev),
                send_sem=send_sems.at[a, k], recv_sem=recv_sems.at[a, k], device_id=to, device_id_type=MESH)

        mine = lambda: [pltpu.make_async_copy(src_of(a), block(a, me), local_sems.at[a]) for a in range(n)]
        own = lambda: [cp for a in range(n) for cp in
                       [copy(a, 0, me, sib, own=True)] + [copy(a, 1 + j, me, (*chip, c), own=True)
                                                          for j, chip in enumerate(chips)]]
        arrived = lambda j, a: copy(a, 1 + j, num(*chips[j], c), (x, y, c))
        passed = lambda j, a: copy(a, 4 + j, num(*chips[j], c), sib)
        from_sib = lambda: ([copy(a, 0, num(x, y, 1 - c), (x, y, c)) for a in range(n)]
                            + [copy(a, 4 + j, num(*chip, 1 - c), (x, y, c)) for j, chip in enumerate(chips)
                               for a in range(n)])
        return mine, own, arrived, passed, from_sib

    def start(ins, outs, sems):
        mine, own, _, _, _ = tools(ins, outs, sems)
        for cp in mine() + own():
            cp.start()

    def middle(ins, outs, sems):
        _, _, arrived, passed, _ = tools(ins, outs, sems)
        for j in range(3):
            for a in range(n):
                arrived(j, a).wait_recv()
                passed(j, a).start()

    def finish(ins, outs, sems):
        mine, own, _, passed, from_sib = tools(ins, outs, sems)
        for cp in from_sib():
            cp.wait_recv()
        for cp in own() + [passed(j, a) for j in range(3) for a in range(n)]:
            cp.wait_send()
        for cp in mine():
            cp.wait()

    return Comm([it[0] for it in items], [jax.ShapeDtypeStruct(it[3], it[0].dtype) for it in items],
                [pltpu.SemaphoreType.DMA((n, 7)), pltpu.SemaphoreType.DMA((n, 7)), pltpu.SemaphoreType.DMA((n,))],
                start, middle, finish, deliver)


def exchange_plan(items, deliver):
    n = len(items)
    width = [arr.shape[axis] // N_DEV for arr, axis in items]
    shard = [arr.shape[:axis] + (w,) + arr.shape[axis + 1:] for (arr, axis), w in zip(items, width)]

    def tools(ins, outs, sems):
        send_sems, recv_sems, local_sems = sems
        x, y, c, me = _me()
        piece = lambda a, dev: _win(ins[a], items[a][1], dev, width[a])
        mine = lambda: [pltpu.make_async_copy(piece(a, me), outs[a].at[me], local_sems.at[a]) for a in range(n)]

        def remote(sending):
            cps = []
            for k in range(1, N_DEV):
                px, py, pc = (1 - x) if k & 4 else x, (1 - y) if k & 2 else y, (1 - c) if k & 1 else c
                peer = 4 * px + 2 * py + pc
                for a in range(n):
                    src, dst = (piece(a, peer), outs[a].at[me]) if sending else (piece(a, me), outs[a].at[peer])
                    cps.append(pltpu.make_async_remote_copy(
                        src_ref=src, dst_ref=dst, send_sem=send_sems.at[a, k - 1], recv_sem=recv_sems.at[a, k - 1],
                        device_id=(px, py, pc), device_id_type=MESH))
            return cps

        return mine, remote

    def start(ins, outs, sems):
        mine, remote = tools(ins, outs, sems)
        for cp in mine() + remote(True):
            cp.start()

    def middle(ins, outs, sems):
        pass

    def finish(ins, outs, sems):
        mine, remote = tools(ins, outs, sems)
        for cp in remote(False):
            cp.wait_recv()
        for cp in remote(True):
            cp.wait_send()
        for cp in mine():
            cp.wait()

    return Comm([it[0] for it in items], [jax.ShapeDtypeStruct((N_DEV,) + s, it[0].dtype) for it, s in zip(items, shard)],
                [pltpu.SemaphoreType.DMA((n, 7)), pltpu.SemaphoreType.DMA((n, 7)), pltpu.SemaphoreType.DMA((n,))],
                start, middle, finish, deliver)


def adam_update(parts, w, m, v, name):
    R, C = w.shape
    br = next((b for b in (256, 128) if R > b and R % b == 0), R)
    np_ = parts.shape[0]

    def body(p_ref, w_ref, m_ref, v_ref, g_ref, d_ref, nm_ref, nv_ref):
        _adam_body(np_, p_ref, w_ref, m_ref, v_ref, g_ref, d_ref, nm_ref, nv_ref)

    bs = pl.BlockSpec((br, C), lambda i: (i, 0))
    out = jax.ShapeDtypeStruct((R, C), F32)
    return _call(
        body, name=name, out_shape=(out, out, out, out), grid=(R // br,),
        in_specs=[pl.BlockSpec((np_, br, C), lambda i: (0, i, 0)), bs, bs, bs], out_specs=(bs, bs, bs, bs),
        compiler_params=_params(("parallel",)),
    )(parts, w, m, v)


def _adam_body(np_, p_ref, w_ref, m_ref, v_ref, g_ref, d_ref, nm_ref, nv_ref):
    c1 = 1.0 / (1.0 - ADAM_B1 ** ADAM_STEP)
    c2 = 1.0 / (1.0 - ADAM_B2 ** ADAM_STEP)
    g = p_ref[0].astype(F32)
    for p in range(1, np_):
        g = g + p_ref[p].astype(F32)
    nm = ADAM_B1 * m_ref[...] + (1.0 - ADAM_B1) * g
    nv = ADAM_B2 * v_ref[...] + (1.0 - ADAM_B2) * (g * g)
    g_ref[...] = g
    nm_ref[...] = nm
    nv_ref[...] = nv
    d_ref[...] = -ADAM_LR * ((nm * c1) / (jnp.sqrt(nv * c2) + ADAM_EPS) + ADAM_WD * w_ref[...])


def adam_layer(parts, w, m, v, l, prev, name):
    L, R, C = w.shape
    br = next((b for b in (256, 128) if R > b and R % b == 0), R)

    def body(p_ref, w_ref, m_ref, v_ref, *rest):
        _adam_body(N_DEV, p_ref, w_ref, m_ref, v_ref, *rest[-4:])

    bs = pl.BlockSpec((None, br, C), lambda i: (l, i, 0))
    out = jax.ShapeDtypeStruct((L, R, C), F32)
    extra = {} if prev is None else dict(input_output_aliases={4 + q: q for q in range(4)})
    return _call(
        body, name=name, out_shape=(out, out, out, out), grid=(R // br,),
        in_specs=[pl.BlockSpec((N_DEV, br, C), lambda i: (0, i, 0)), bs, bs, bs] + ([] if prev is None else [ANY] * 4),
        out_specs=(bs, bs, bs, bs), compiler_params=_params(("parallel",)), **extra,
    )(parts, w, m, v, *(() if prev is None else prev))


def sum_parts(parts, name):
    n, R, C = parts.shape

    def body(p_ref, o_ref):
        g = p_ref[0]
        for p in range(1, n):
            g = g + p_ref[p]
        o_ref[...] = g

    return _call(body, name=name, out_shape=jax.ShapeDtypeStruct((R, C), parts.dtype),
                 compiler_params=pltpu.CompilerParams(vmem_limit_bytes=VMEM_LIMIT))(parts)


def _pack_rows(arrs):
    rows = []
    for a in arrs:
        f = a.reshape(-1)
        r = -(-f.shape[0] // 1024)
        r8 = -(-r // 8) * 8
        rows.append(jnp.pad(f, (0, r8 * 1024 - f.shape[0])).reshape(r8, 1024))
    packed = jnp.concatenate(rows, 0)
    return jnp.pad(packed, ((0, -packed.shape[0] % 128), (0, 0)))


def _unpack_rows(packed, shapes):
    out, o = [], 0
    for s in shapes:
        nel = math.prod(s)
        r8 = -(-(-(-nel // 1024)) // 8) * 8
        out.append(packed[o:o + r8].reshape(-1)[:nel].reshape(s))
        o += r8
    return out


def kernel(x, norm_mix_g, norm_ffn_g, norm_final_g, rg_w_in, rg_conv_w, rg_conv_b, rg_w_a, rg_b_a, rg_w_x, rg_b_x, rg_lambda, rg_w_out, s5_w_in, s5_a_re, s5_a_im, s5_log_dt, s5_b_re, s5_b_im, s5_c_re, s5_c_im, s5_d, s5_w_glu, s5_w_out, ffn_w_up, ffn_conv_w, ffn_conv_b, ffn_w_down, loss_target, m_norm_mix_g, m_norm_ffn_g, m_norm_final_g, m_rg_w_in, m_rg_conv_w, m_rg_conv_b, m_rg_w_a, m_rg_b_a, m_rg_w_x, m_rg_b_x, m_rg_lambda, m_rg_w_out, m_s5_w_in, m_s5_a_re, m_s5_a_im, m_s5_log_dt, m_s5_b_re, m_s5_b_im, m_s5_c_re, m_s5_c_im, m_s5_d, m_s5_w_glu, m_s5_w_out, m_ffn_w_up, m_ffn_conv_w, m_ffn_conv_b, m_ffn_w_down, v_norm_mix_g, v_norm_ffn_g, v_norm_final_g, v_rg_w_in, v_rg_conv_w, v_rg_conv_b, v_rg_w_a, v_rg_b_a, v_rg_w_x, v_rg_b_x, v_rg_lambda, v_rg_w_out, v_s5_w_in, v_s5_a_re, v_s5_a_im, v_s5_log_dt, v_s5_b_re, v_s5_b_im, v_s5_c_re, v_s5_c_im, v_s5_d, v_s5_w_glu, v_s5_w_out, v_ffn_w_up, v_ffn_conv_w, v_ffn_conv_b, v_ffn_w_down):
    names = ('norm_mix_g', 'norm_ffn_g', 'norm_final_g', 'rg_w_in', 'rg_conv_w', 'rg_conv_b', 'rg_w_a', 'rg_b_a',
             'rg_w_x', 'rg_b_x', 'rg_lambda', 'rg_w_out', 's5_w_in', 's5_a_re', 's5_a_im', 's5_log_dt', 's5_b_re',
             's5_b_im', 's5_c_re', 's5_c_im', 's5_d', 's5_w_glu', 's5_w_out', 'ffn_w_up', 'ffn_conv_w', 'ffn_conv_b',
             'ffn_w_down')
    loc = locals()
    Wl = {k: loc[k] for k in names}
    Ml = {k: loc['m_' + k] for k in names}
    Vl = {k: loc['v_' + k] for k in names}

    depth = norm_mix_g.shape[0]
    mixer_keys = lambda i: ([('rg_w_in', i // 2), ('rg_w_out', i // 2)] if i % 2 == 0 else
                            [('s5_w_in', i // 2), ('s5_w_glu', i // 2), ('s5_w_out', i // 2)])
    ffn_keys = lambda i: [('ffn_w_up', i), ('ffn_w_down', i)]
    shards = {k: Wl[k].astype(BF16) for k in BIG}
    W = {k: Wl[k] for k in REPLICATED}
    W.update({k: [None] * Wl[k].shape[0] for k in BIG})
    parts = {}

    def gather_of(keys, small=False):
        items = []
        for k, l in keys:
            _, r, c = shards[k].shape
            items.append((shards[k], l, BIG[k], (r * N_DEV, c) if BIG[k] == 0 else (r, c * N_DEV)))
        if small:
            items += [(Wl[k], None, Wl[k].ndim - 1, Wl[k].shape[:-1] + (Wl[k].shape[-1] * N_DEV,)) for k in SMALL_SHARDED]

        def deliver(outs):
            for (k, l), arr in zip(keys, outs):
                W[k][l] = arr
            if small:
                W.update(zip(SMALL_SHARDED, outs[len(keys):]))

        return gather_plan(items, deliver)

    def exchange_of(keys, G, extra=()):
        items = [(G[k][l], BIG[k]) for k, l in keys] + [(arr, axis) for _, arr, axis in extra]
        return exchange_plan(items, lambda outs: parts.update(zip(list(keys) + [e[0] for e in extra], outs)))

    class Sched:
        @staticmethod
        def comm(host, G=None):
            kind, _, i = host.rpartition("_")
            i = int(i)
            if host == "rg_fwd_0":
                return gather_of(ffn_keys(0))
            if kind == "ffn_up" and i + 1 < depth:
                return gather_of(mixer_keys(i + 1) + ffn_keys(i + 1))
            if kind == "ffn_bwd_act" and i + 1 < depth:
                return exchange_of(mixer_keys(i + 1) + ffn_keys(i + 1), G)
            if host == "rg_bwd_0":
                return exchange_of(ffn_keys(0), G)
            return None

    run_comm(gather_of(mixer_keys(0), small=True), "gather_first")

    se8, gx, G = local_step(x[0], loss_target[0], W, Sched)
    loss = lax.psum(0.5 * jnp.sum(se8) / x.shape[-1], ("x", "y", "c"))

    rep_shapes = [Wl[k].shape for k in REPLICATED]
    gp = _pack_rows([G[k].astype(F32) for k in REPLICATED])
    extra = [(k, G[k], G[k].ndim - 1) for k in SMALL_SHARDED] + [('replicated', gp, 0)]
    run_comm(exchange_of(mixer_keys(0), G, extra), "exchange_last")
    out_g, out_d, out_m, out_v = {}, {}, {}, {}
    for k in BIG:
        res = None
        for l in range(Wl[k].shape[0]):
            res = adam_layer(parts[(k, l)], Wl[k], Ml[k], Vl[k], l, res, f"adam_{k}_{l}")
        out_g[k], out_d[k], out_m[k], out_v[k] = res
    for k in SMALL_SHARDED:
        shp = Wl[k].shape
        r2 = (math.prod(shp[:-1]), shp[-1])
        res = adam_update(parts[k].reshape((N_DEV,) + r2), Wl[k].reshape(r2), Ml[k].reshape(r2), Vl[k].reshape(r2),
                          f"adam_{k}")
        out_g[k], out_d[k], out_m[k], out_v[k] = [t.reshape(shp) for t in res]
    rsum = sum_parts(parts['replicated'], "sum_replicated")
    run_comm(gather_plan([(rsum, None, 0, gp.shape)], lambda outs: parts.update(rep_full=outs[0])), "gather_small_grads")
    res = adam_update(parts['rep_full'][None], _pack_rows([Wl[k] for k in REPLICATED]),
                      _pack_rows([Ml[k] for k in REPLICATED]), _pack_rows([Vl[k] for k in REPLICATED]), "adam_replicated")
    for dst, packed in zip((out_g, out_d, out_m, out_v), res):
        for k, t in zip(REPLICATED, _unpack_rows(packed, rep_shapes)):
            dst[k] = t
    return (loss, gx[None], *[out_g[k] for k in names], *[out_d[k] for k in names], *[out_m[k] for k in names],
            *[out_v[k] for k in names])
```

```python
import functools
import math

import jax
import jax.numpy as jnp
from jax import lax
from jax.experimental import pallas as pl
from jax.experimental.pallas import tpu as pltpu

F32 = jnp.float32
BF16 = jnp.bfloat16
MXU_DTYPE = jnp.bfloat16

NORM_EPS = 1e-6
RG_C = 8.0
RG_HEADS = 8
RG_BW = 128
S5_G = 64
S5_GC = 16
S5_P = 64
S5_SB = 8
S5_NSB = S5_G // S5_SB
S5_NS = 2 * S5_G * S5_P
ADAM_LR = 0.001
ADAM_B1 = 0.9
ADAM_B2 = 0.999
ADAM_EPS = 1e-08
ADAM_WD = 0.01
ADAM_STEP = 10
N_DEV = 8
VMEM_LIMIT = 56 * 1024 * 1024


def _call(body, **kw):
    return pl.pallas_call(body, **kw)


def _params(sem, vmem=VMEM_LIMIT):
    return pltpu.CompilerParams(dimension_semantics=sem, vmem_limit_bytes=vmem)


MESH = pl.DeviceIdType.MESH
ANY = pl.BlockSpec(memory_space=pl.ANY)


class Comm:
    def __init__(self, operands, out_shape, scratch, start, middle, finish, deliver):
        self.operands, self.out_shape, self.scratch = list(operands), list(out_shape), list(scratch)
        self.start, self.middle, self.finish, self.deliver = start, middle, finish, deliver


def run_comm(comm, name):
    ci, co = len(comm.operands), len(comm.out_shape)

    def body(*refs):
        parts = (refs[:ci], refs[ci:ci + co], refs[ci + co:])
        comm.start(*parts)
        comm.middle(*parts)
        comm.finish(*parts)

    comm.deliver(_call(body, name=name, out_shape=tuple(comm.out_shape), in_specs=[ANY] * ci,
                       out_specs=tuple([ANY] * co), scratch_shapes=comm.scratch)(*comm.operands))


def _hosted(body, comm, args, *, name, out_shape, grid, in_specs, out_specs, scratch_shapes, compiler_params):
    if comm is None:
        return _call(body, name=name, out_shape=tuple(out_shape), grid=grid, in_specs=in_specs,
                     out_specs=tuple(out_specs), scratch_shapes=scratch_shapes, compiler_params=compiler_params)(*args)
    n_in, n_out, n_sc = len(in_specs), len(out_shape), len(scratch_shapes)
    ci, co = len(comm.operands), len(comm.out_shape)
    nsteps = math.prod(grid)
    mid = (2 * nsteps) // 3

    def wrapped(*refs):
        ins, refs = refs[:n_in], refs[n_in:]
        cins, refs = refs[:ci], refs[ci:]
        outs, refs = refs[:n_out], refs[n_out:]
        couts, refs = refs[:co], refs[co:]
        sc, csc = refs[:n_sc], refs[n_sc:]
        step = pl.program_id(0)
        for d in range(1, len(grid)):
            step = step * grid[d] + pl.program_id(d)

        @pl.when(step == 0)
        def _():
            comm.start(cins, couts, csc)

        body(*ins, *outs, *sc)

        @pl.when(step == mid)
        def _():
            comm.middle(cins, couts, csc)

        @pl.when(step == nsteps - 1)
        def _():
            comm.finish(cins, couts, csc)

    res = _call(wrapped, name=name, out_shape=(*out_shape, *comm.out_shape), grid=grid,
                in_specs=[*in_specs, *[ANY] * ci], out_specs=(*out_specs, *[ANY] * co),
                scratch_shapes=[*scratch_shapes, *comm.scratch],
                compiler_params=_params(("arbitrary",) * len(grid)))(*args, *comm.operands)
    comm.deliver(res[n_out:])
    return res[:n_out]


_GELU_C = 0.7978845608028654
_GELU_A = 0.044715


def _gelu(x):
    return 0.5 * x * (1.0 + jnp.tanh(_GELU_C * (x + _GELU_A * x * x * x)))


def _gelu_and_grad(x):
    x2 = x * x
    t = jnp.tanh(_GELU_C * (x + _GELU_A * x2 * x))
    g = 0.5 * x * (1.0 + t)
    dg = 0.5 * (1.0 + t) + 0.5 * x * (1.0 - t * t) * _GELU_C * (1.0 + 3.0 * _GELU_A * x2)
    return g, dg


def _sigmoid(x):
    return 1.0 / (1.0 + jnp.exp(-x))


def _neg_expm1(x):
    series = -x * (1.0 + x * (0.5 + x * (1.0 / 6.0 + x * (1.0 / 24.0 + x * (1.0 / 120.0 + x * (1.0 / 720.0))))))
    return jnp.where(x > -0.1, series, 1.0 - jnp.exp(x))


def _rowsum8(x):
    r, c = x.shape
    return x.reshape(r // 8, 8, c).sum(axis=0)


def _dot(a, b):
    return jnp.dot(a.astype(MXU_DTYPE), b.astype(MXU_DTYPE), preferred_element_type=F32)


def _dot_nt(a, b):
    return lax.dot_general(a.astype(MXU_DTYPE), b.astype(MXU_DTYPE), (((1,), (1,)), ((), ())),
                           preferred_element_type=F32)


def _dot_tn(a, b):
    return lax.dot_general(a.astype(MXU_DTYPE), b.astype(MXU_DTYPE), (((0,), (0,)), ((), ())),
                           preferred_element_type=F32)


def _shift_down(x, s, fills, row):
    y = pltpu.roll(x, s, 0)
    for t in range(s):
        y = jnp.where(row == t, fills[s - 1 - t], y)
    return y


def _shift_up(x, s, fills, row):
    n = x.shape[0]
    y = pltpu.roll(x, n - s, 0)
    for t in range(s):
        y = jnp.where(row == n - s + t, fills[t], y)
    return y


def _rms(x):
    r = lax.rsqrt(jnp.mean(x * x, axis=-1, keepdims=True) + NORM_EPS)
    return r, x * r


def _rms_bwd(dhn, xhat, r, g):
    dz = dhn * g
    return r * (dz - xhat * jnp.mean(dz * xhat, axis=-1, keepdims=True))


def norm_matmul(h, g, w, out_dtype, tm, tn, name, comm=None):
    T, D = h.shape
    N = w.shape[1]
    tn = N if tn is None else tn

    def body(h_ref, g_ref, w_ref, o_ref, hn_ref, hn_s):
        @pl.when(pl.program_id(1) == 0)
        def _():
            _, xhat = _rms(h_ref[...])
            v = (xhat * g_ref[...]).astype(MXU_DTYPE)
            hn_s[...] = v
            hn_ref[...] = v

        o_ref[...] = jnp.dot(hn_s[...], w_ref[...].astype(MXU_DTYPE), preferred_element_type=F32).astype(o_ref.dtype)

    return _hosted(
        body, comm, (h, g, w), name=name,
        out_shape=(jax.ShapeDtypeStruct((T, N), out_dtype), jax.ShapeDtypeStruct((T, D), MXU_DTYPE)),
        grid=(T // tm, N // tn),
        in_specs=[pl.BlockSpec((tm, D), lambda i, j: (i, 0)), pl.BlockSpec((1, D), lambda i, j: (0, 0)),
                  pl.BlockSpec((D, tn), lambda i, j: (0, j))],
        out_specs=(pl.BlockSpec((tm, tn), lambda i, j: (i, j)), pl.BlockSpec((tm, D), lambda i, j: (i, 0))),
        scratch_shapes=[pltpu.VMEM((tm, D), MXU_DTYPE)],
        compiler_params=_params(("parallel", "arbitrary")),
    )


def matmul_tn(a, b, bm, bn, bt, name, out_dtype=BF16):
    T, M = a.shape
    N = b.shape[1]
    nk = T // bt

    def body(a_ref, b_ref, o_ref, acc):
        k = pl.program_id(2)

        @pl.when(k == 0)
        def _():
            acc[...] = jnp.zeros_like(acc)

        acc[...] += _dot_tn(a_ref[...], b_ref[...])

        @pl.when(k == nk - 1)
        def _():
            o_ref[...] = acc[...].astype(o_ref.dtype)

    return _call(
        body, name=name,
        out_shape=jax.ShapeDtypeStruct((M, N), out_dtype),
        grid=(M // bm, N // bn, nk),
        in_specs=[pl.BlockSpec((bt, bm), lambda i, j, k: (k, i)), pl.BlockSpec((bt, bn), lambda i, j, k: (k, j))],
        out_specs=pl.BlockSpec((bm, bn), lambda i, j, k: (i, j)),
        scratch_shapes=[pltpu.VMEM((bm, bn), F32)],
        compiler_params=_params(("parallel", "parallel", "arbitrary")),
    )(a, b)


def dx_norm_bwd(dz, w, h, g, dres, tm, name, conv_w=None, chunk=1024):
    T, N = dz.shape
    D = w.shape[0]
    nt = T // tm
    has_conv = conv_w is not None
    kw = conv_w.shape[0] if has_conv else 0

    def body(*refs):
        if has_conv:
            dz_ref, cw_ref, w_ref, h_ref, g_ref, dres_ref, dh_ref, dg_ref, dzp_ref, carry, zs = refs
        else:
            dz_ref, w_ref, h_ref, g_ref, dres_ref, dh_ref, dg_ref = refs
        i = pl.program_id(0)

        @pl.when(i == 0)
        def _():
            dg_ref[...] = jnp.zeros_like(dg_ref)
            if has_conv:
                carry[...] = jnp.zeros_like(carry)

        if has_conv:
            row = lax.broadcasted_iota(jnp.int32, (tm, 1), 0)
            for c0 in range(0, N, chunk):
                sl = slice(c0, c0 + chunk)
                x = dz_ref[:, sl].astype(F32)
                fills = [carry[t:t + 1, sl] for t in range(kw - 1)]
                acc = cw_ref[kw - 1:kw, sl] * x
                for s in range(1, kw):
                    acc = acc + cw_ref[kw - 1 - s:kw - s, sl] * _shift_up(x, s, fills, row)
                zb = acc.astype(MXU_DTYPE)
                zs[:, sl] = zb
                dzp_ref[:, sl] = zb
            carry[...] = dz_ref[0:16, :].astype(F32)
            z = zs[...]
        else:
            z = dz_ref[...]
        dhn = _dot_nt(z, w_ref[...])
        r, xhat = _rms(h_ref[...])
        dg_ref[...] += _rowsum8(dhn * xhat)
        dh_ref[...] = dres_ref[...] + _rms_bwd(dhn, xhat, r, g_ref[...])

    if has_conv:
        ti = lambda i: nt - 1 - i
    else:
        ti = lambda i: i
    row_spec = lambda c: pl.BlockSpec((tm, c), lambda i: (ti(i), 0))
    full = lambda a: pl.BlockSpec(a.shape, lambda i: (0,) * a.ndim)
    in_specs = [row_spec(N)] + ([full(conv_w)] if has_conv else []) + [full(w), row_spec(D), full(g), row_spec(D)]
    out_shape = [jax.ShapeDtypeStruct((T, D), F32), jax.ShapeDtypeStruct((8, D), F32)]
    out_specs = [row_spec(D), pl.BlockSpec((8, D), lambda i: (0, 0))]
    scratch = []
    if has_conv:
        out_shape.append(jax.ShapeDtypeStruct((T, N), MXU_DTYPE))
        out_specs.append(row_spec(N))
        scratch = [pltpu.VMEM((16, N), F32), pltpu.VMEM((tm, N), MXU_DTYPE)]
    args = [dz] + ([conv_w] if has_conv else []) + [w, h, g, dres]
    return _call(
        body, name=name, out_shape=tuple(out_shape), grid=(nt,), in_specs=in_specs, out_specs=tuple(out_specs),
        scratch_shapes=scratch, compiler_params=_params(("arbitrary",)),
    )(*args)


def _ffn_conv_chunk(up_ref, cw_ref, cb_ref, carry, row, sl):
    x = up_ref[:, sl].astype(F32)
    fills = [carry[15:16, sl], carry[14:15, sl]]
    x1 = _shift_down(x, 1, fills, row)
    x2 = _shift_down(x, 2, fills, row)
    out = cb_ref[:, sl] + cw_ref[2:3, sl] * x + cw_ref[1:2, sl] * x1 + cw_ref[0:1, sl] * x2
    return out, x, x1, x2


def ffn_fwd(up, h, conv_w, conv_b, w_down, tm, name, chunk=512):
    T, C = up.shape
    F = C // 2
    D = h.shape[1]

    def body(up_ref, h_ref, cw_ref, cb_ref, wd_ref, o_ref, carry, act_s):
        @pl.when(pl.program_id(0) == 0)
        def _():
            carry[...] = jnp.zeros_like(carry)

        row = lax.broadcasted_iota(jnp.int32, (tm, 1), 0)
        for c0 in range(0, F, chunk):
            a = _ffn_conv_chunk(up_ref, cw_ref, cb_ref, carry, row, slice(c0, c0 + chunk))[0]
            v = _ffn_conv_chunk(up_ref, cw_ref, cb_ref, carry, row, slice(F + c0, F + c0 + chunk))[0]
            act_s[:, c0:c0 + chunk] = (_gelu(a) * v).astype(MXU_DTYPE)
        carry[...] = up_ref[tm - 16:tm, :].astype(F32)
        o_ref[...] = h_ref[...] + jnp.dot(act_s[...], wd_ref[...].astype(MXU_DTYPE), preferred_element_type=F32)

    full = lambda a: pl.BlockSpec(a.shape, lambda i: (0,) * a.ndim)
    return _call(
        body, name=name, out_shape=jax.ShapeDtypeStruct((T, D), F32), grid=(T // tm,),
        in_specs=[pl.BlockSpec((tm, C), lambda i: (i, 0)), pl.BlockSpec((tm, D), lambda i: (i, 0)),
                  full(conv_w), full(conv_b), full(w_down)],
        out_specs=pl.BlockSpec((tm, D), lambda i: (i, 0)),
        scratch_shapes=[pltpu.VMEM((16, C), F32), pltpu.VMEM((tm, F), MXU_DTYPE)],
        compiler_params=_params(("arbitrary",)),
    )(up, h, conv_w, conv_b, w_down)


def ffn_bwd_act(dout, up, conv_w, conv_b, w_down, tm, name, chunk=512, comm=None):
    T, C = up.shape
    F = C // 2
    D = dout.shape[1]
    kw = conv_w.shape[0]

    def body(do_ref, up_ref, cw_ref, cb_ref, wd_ref, dup_ref, act_ref, dcb_ref, dcw_ref, carry, dact_s):
        @pl.when(pl.program_id(0) == 0)
        def _():
            carry[...] = jnp.zeros_like(carry)
            dcb_ref[...] = jnp.zeros_like(dcb_ref)
            dcw_ref[...] = jnp.zeros_like(dcw_ref)

        dact_s[...] = _dot_nt(do_ref[...], wd_ref[...])
        row = lax.broadcasted_iota(jnp.int32, (tm, 1), 0)
        for c0 in range(0, F, chunk):
            sa = slice(c0, c0 + chunk)
            sv = slice(F + c0, F + c0 + chunk)
            a, xa, xa1, xa2 = _ffn_conv_chunk(up_ref, cw_ref, cb_ref, carry, row, sa)
            v, xv, xv1, xv2 = _ffn_conv_chunk(up_ref, cw_ref, cb_ref, carry, row, sv)
            ga, dga = _gelu_and_grad(a)
            act_ref[:, sa] = (ga * v).astype(MXU_DTYPE)
            dact = dact_s[:, sa]
            da = dact * v * dga
            dv = dact * ga
            dup_ref[:, sa] = da.astype(MXU_DTYPE)
            dup_ref[:, sv] = dv.astype(MXU_DTYPE)
            for d, xs, s in ((da, (xa2, xa1, xa), sa), (dv, (xv2, xv1, xv), sv)):
                dcb_ref[:, s] += _rowsum8(d)
                for k in range(kw):
                    dcw_ref[k, :, s] += _rowsum8(d * xs[k])
        carry[...] = up_ref[tm - 16:tm, :].astype(F32)

    full = lambda a: pl.BlockSpec(a.shape, lambda i: (0,) * a.ndim)
    return _hosted(
        body, comm, (dout, up, conv_w, conv_b, w_down), name=name,
        out_shape=(jax.ShapeDtypeStruct((T, C), MXU_DTYPE), jax.ShapeDtypeStruct((T, F), MXU_DTYPE),
                   jax.ShapeDtypeStruct((8, C), F32), jax.ShapeDtypeStruct((kw, 8, C), F32)),
        grid=(T // tm,),
        in_specs=[pl.BlockSpec((tm, D), lambda i: (i, 0)), pl.BlockSpec((tm, C), lambda i: (i, 0)),
                  full(conv_w), full(conv_b), full(w_down)],
        out_specs=(pl.BlockSpec((tm, C), lambda i: (i, 0)), pl.BlockSpec((tm, F), lambda i: (i, 0)),
                   pl.BlockSpec((8, C), lambda i: (0, 0)), pl.BlockSpec((kw, 8, C), lambda i: (0, 0, 0))),
        scratch_shapes=[pltpu.VMEM((16, C), F32), pltpu.VMEM((tm, F), F32)],
        compiler_params=_params(("arbitrary",)),
    )


def _rg_gates(xr, wa_ref, ba_ref, wx_ref, bx_ref, sp_ref):
    xb = xr.astype(MXU_DTYPE)
    pa, px = [], []
    for hd in range(RG_HEADS):
        sl = slice(hd * RG_BW, (hd + 1) * RG_BW)
        pa.append(jnp.dot(xb[:, sl], wa_ref[hd].astype(MXU_DTYPE), preferred_element_type=F32))
        px.append(jnp.dot(xb[:, sl], wx_ref[hd].astype(MXU_DTYPE), preferred_element_type=F32))
    r = _sigmoid(jnp.concatenate(pa, axis=1) + ba_ref[...])
    ig = _sigmoid(jnp.concatenate(px, axis=1) + bx_ref[...])
    la = -RG_C * r * sp_ref[...]
    a = jnp.exp(la)
    mult = jnp.sqrt(_neg_expm1(2.0 * la))
    return xb, r, ig, a, mult


def _rg_conv(x, fills, cw_ref, cb_ref, row):
    x1 = _shift_down(x, 1, fills, row)
    x2 = _shift_down(x, 2, fills, row)
    x3 = _shift_down(x, 3, fills, row)
    xr = cb_ref[...] + cw_ref[3:4, :] * x + cw_ref[2:3, :] * x1 + cw_ref[1:2, :] * x2 + cw_ref[0:1, :] * x3
    return xr, (x3, x2, x1, x)


def rg_fwd(xg, h, conv_w, conv_b, w_a, b_a, w_x, b_x, sp, w_out, tm, name, comm=None):
    T, D2 = xg.shape
    D = D2 // 2
    nb = tm // 8

    def body(xg_ref, h_ref, cw_ref, cb_ref, wa_ref, ba_ref, wx_ref, bx_ref, sp_ref, wo_ref, o_ref, hs_ref,
             xcarry, hcarry, a_s, b_s):
        @pl.when(pl.program_id(0) == 0)
        def _():
            xcarry[...] = jnp.zeros_like(xcarry)
            hcarry[...] = jnp.zeros_like(hcarry)

        row = lax.broadcasted_iota(jnp.int32, (tm, 1), 0)
        x = xg_ref[:, 0:D]
        fills = [xcarry[7:8, :], xcarry[6:7, :], xcarry[5:6, :]]
        xr, _ = _rg_conv(x, fills, cw_ref, cb_ref, row)
        xcarry[...] = xg_ref[tm - 8:tm, 0:D]
        _, r, ig, a, mult = _rg_gates(xr, wa_ref, ba_ref, wx_ref, bx_ref, sp_ref)
        a_s[...] = a
        b_s[...] = mult * ig * xr
        row8 = lax.broadcasted_iota(jnp.int32, (8, 1), 0)

        def blk(j, c):
            o = pl.multiple_of(j * 8, 8)
            A = a_s[pl.ds(o, 8), :]
            H = b_s[pl.ds(o, 8), :]
            for s in (1, 2, 4):
                m = row8 >= s
                H = H + A * jnp.where(m, pltpu.roll(H, s, 0), 0.0)
                A = A * jnp.where(m, pltpu.roll(A, s, 0), 1.0)
            H = H + A * c
            hs_ref[pl.ds(o, 8), :] = H
            return H[7:8, :]

        c = lax.fori_loop(0, nb, blk, hcarry[0:1, :])
        hcarry[0:1, :] = c
        y = hs_ref[...] * _gelu(xg_ref[:, D:D2])
        o_ref[...] = h_ref[...] + _dot(y, wo_ref[...])

    full = lambda a: pl.BlockSpec(a.shape, lambda i: (0,) * a.ndim)
    args = (xg, h, conv_w, conv_b, w_a, b_a, w_x, b_x, sp, w_out)
    return _hosted(
        body, comm, args, name=name,
        out_shape=(jax.ShapeDtypeStruct((T, D), F32), jax.ShapeDtypeStruct((T, D), F32)),
        grid=(T // tm,),
        in_specs=[pl.BlockSpec((tm, D2), lambda i: (i, 0)), pl.BlockSpec((tm, D), lambda i: (i, 0))]
        + [full(a) for a in args[2:]],
        out_specs=(pl.BlockSpec((tm, D), lambda i: (i, 0)), pl.BlockSpec((tm, D), lambda i: (i, 0))),
        scratch_shapes=[pltpu.VMEM((8, D), F32), pltpu.VMEM((8, D), F32), pltpu.VMEM((tm, D), F32),
                        pltpu.VMEM((tm, D), F32)],
        compiler_params=_params(("arbitrary",)),
    )


def rg_bwd(dout, xg, hs, conv_w, conv_b, w_a, b_a, w_x, b_x, sp, w_out, tm, name, comm=None):
    T, D2 = xg.shape
    D = D2 // 2
    nt = T // tm
    nb = tm // 8
    kw = conv_w.shape[0]

    def body(do_ref, xg_ref, xh_ref, hs_ref, hh_ref, cw_ref, cb_ref, wa_ref, ba_ref, wx_ref, bx_ref, sp_ref, wo_ref,
             dxg_ref, y_ref, dwa_ref, dwx_ref, dba_ref, dbx_ref, dsp_ref, dcb_ref, dcw_ref,
             acarry, lcarry, dcarry, a_s, b_s, l_s):
        i = pl.program_id(0)
        first_tile = i == nt - 1

        @pl.when(i == 0)
        def _():
            for ref in (acarry, lcarry, dcarry, dwa_ref, dwx_ref, dba_ref, dbx_ref, dsp_ref, dcb_ref, dcw_ref):
                ref[...] = jnp.zeros_like(ref)

        row = lax.broadcasted_iota(jnp.int32, (tm, 1), 0)
        keep = jnp.where(first_tile, 0.0, 1.0)
        x = xg_ref[:, 0:D]
        gate = xg_ref[:, D:D2]
        xh = xh_ref[...] * keep
        fills = [xh[7:8, :], xh[6:7, :], xh[5:6, :]]
        xr, taps = _rg_conv(x, fills, cw_ref, cb_ref, row)
        xb, r, ig, a, mult = _rg_gates(xr, wa_ref, ba_ref, wx_ref, bx_ref, sp_ref)
        hs = hs_ref[...]
        hprev = _shift_down(hs, 1, [hh_ref[7:8, :] * keep], row)
        dy = _dot_nt(do_ref[...], wo_ref[...])
        gg, dgg = _gelu_and_grad(gate)
        y_ref[...] = (hs * gg).astype(MXU_DTYPE)
        dxg_ref[:, D:D2] = (dy * hs * dgg).astype(MXU_DTYPE)
        a_s[...] = _shift_up(a, 1, [acarry[0:1, :]], row)
        b_s[...] = dy * gg
        row8 = lax.broadcasted_iota(jnp.int32, (8, 1), 0)

        def blk(jj, c):
            o = pl.multiple_of((nb - 1 - jj) * 8, 8)
            A = a_s[pl.ds(o, 8), :]
            H = b_s[pl.ds(o, 8), :]
            for s in (1, 2, 4):
                m = row8 < 8 - s
                H = H + A * jnp.where(m, pltpu.roll(H, 8 - s, 0), 0.0)
                A = A * jnp.where(m, pltpu.roll(A, 8 - s, 0), 1.0)
            H = H + A * c
            l_s[pl.ds(o, 8), :] = H
            return H[0:1, :]

        c = lax.fori_loop(0, nb, blk, lcarry[0:1, :])
        lcarry[0:1, :] = c
        acarry[0:1, :] = a[0:1, :]
        lam = l_s[...]
        dla = lam * hprev * a - (lam * ig * xr) * (a * a) / mult
        dig = lam * mult * xr
        dxr = lam * mult * ig
        spv = sp_ref[...]
        dsp_ref[...] += _rowsum8(dla * (-RG_C) * r)
        dpa = (dla * (-RG_C) * spv) * r * (1.0 - r)
        dpx = dig * ig * (1.0 - ig)
        dba_ref[...] += _rowsum8(dpa)
        dbx_ref[...] += _rowsum8(dpx)
        dpab = dpa.astype(MXU_DTYPE)
        dpxb = dpx.astype(MXU_DTYPE)
        back = []
        for hd in range(RG_HEADS):
            sl = slice(hd * RG_BW, (hd + 1) * RG_BW)
            dwa_ref[hd] += _dot_tn(xb[:, sl], dpab[:, sl])
            dwx_ref[hd] += _dot_tn(xb[:, sl], dpxb[:, sl])
            back.append(_dot_nt(dpab[:, sl], wa_ref[hd]) + _dot_nt(dpxb[:, sl], wx_ref[hd]))
        dxr = dxr + jnp.concatenate(back, axis=1)
        nfills = [dcarry[0:1, :], dcarry[1:2, :], dcarry[2:3, :]]
        dxp = cw_ref[kw - 1:kw, :] * dxr
        for s in range(1, kw):
            dxp = dxp + cw_ref[kw - 1 - s:kw - s, :] * _shift_up(dxr, s, nfills, row)
        dcarry[...] = dxr[0:8, :]
        dxg_ref[:, 0:D] = dxp.astype(MXU_DTYPE)
        dcb_ref[...] += _rowsum8(dxr)
        for k in range(kw):
            dcw_ref[k] += _rowsum8(dxr * taps[k])

    rt = lambda i: nt - 1 - i
    halo = lambda i: jnp.maximum((nt - 1 - i) * (tm // 8) - 1, 0)
    full = lambda a: pl.BlockSpec(a.shape, lambda i: (0,) * a.ndim)
    params = (conv_w, conv_b, w_a, b_a, w_x, b_x, sp, w_out)
    acc = lambda shape: pl.BlockSpec(shape, lambda i: (0,) * len(shape))
    return _hosted(
        body, comm, (dout, xg, xg, hs, hs, *params), name=name,
        out_shape=(jax.ShapeDtypeStruct((T, D2), MXU_DTYPE), jax.ShapeDtypeStruct((T, D), MXU_DTYPE),
                   jax.ShapeDtypeStruct((RG_HEADS, RG_BW, RG_BW), F32), jax.ShapeDtypeStruct((RG_HEADS, RG_BW, RG_BW), F32),
                   jax.ShapeDtypeStruct((8, D), F32), jax.ShapeDtypeStruct((8, D), F32), jax.ShapeDtypeStruct((8, D), F32),
                   jax.ShapeDtypeStruct((8, D), F32), jax.ShapeDtypeStruct((kw, 8, D), F32)),
        grid=(nt,),
        in_specs=[pl.BlockSpec((tm, D), lambda i: (rt(i), 0)), pl.BlockSpec((tm, D2), lambda i: (rt(i), 0)),
                  pl.BlockSpec((8, D), lambda i: (halo(i), 0)), pl.BlockSpec((tm, D), lambda i: (rt(i), 0)),
                  pl.BlockSpec((8, D), lambda i: (halo(i), 0))] + [full(a) for a in params],
        out_specs=(pl.BlockSpec((tm, D2), lambda i: (rt(i), 0)), pl.BlockSpec((tm, D), lambda i: (rt(i), 0)),
                   acc((RG_HEADS, RG_BW, RG_BW)), acc((RG_HEADS, RG_BW, RG_BW)), acc((8, D)), acc((8, D)), acc((8, D)),
                   acc((8, D)), acc((kw, 8, D))),
        scratch_shapes=[pltpu.VMEM((8, D), F32), pltpu.VMEM((8, D), F32), pltpu.VMEM((8, D), F32),
                        pltpu.VMEM((tm, D), F32), pltpu.VMEM((tm, D), F32), pltpu.VMEM((tm, D), F32)],
        compiler_params=_params(("arbitrary",)),
    )


_SBW = 2 * S5_SB * S5_P
_SBH = S5_SB * S5_P
_SBU = S5_SB * S5_GC


def _s5_scan_fwd(S, carry, apr_ref, api_ref, nb):
    row8 = lax.broadcasted_iota(jnp.int32, (8, 1), 0)

    def blk(j, _):
        o = pl.multiple_of(j * 8, 8)
        for sb in range(S5_NSB):
            lr = slice(sb * _SBW, sb * _SBW + _SBH)
            li = slice(sb * _SBW + _SBH, (sb + 1) * _SBW)
            la = slice(sb * _SBH, (sb + 1) * _SBH)
            hr = S[pl.ds(o, 8), lr]
            hi = S[pl.ds(o, 8), li]
            for s, idx in ((1, 0), (2, 1), (4, 3)):
                ar = apr_ref[idx:idx + 1, la]
                ai = api_ref[idx:idx + 1, la]
                m = row8 >= s
                sr = jnp.where(m, pltpu.roll(hr, s, 0), 0.0)
                si = jnp.where(m, pltpu.roll(hi, s, 0), 0.0)
                hr, hi = hr + ar * sr - ai * si, hi + ar * si + ai * sr
            cr = carry[0:1, lr]
            ci = carry[0:1, li]
            pr = apr_ref[:, la]
            pi = api_ref[:, la]
            hr, hi = hr + pr * cr - pi * ci, hi + pr * ci + pi * cr
            S[pl.ds(o, 8), lr] = hr
            S[pl.ds(o, 8), li] = hi
            carry[0:1, lr] = hr[7:8, :]
            carry[0:1, li] = hi[7:8, :]
        return 0

    lax.fori_loop(0, nb, blk, 0)


def s5_fwd(u, h, wb, wc, apr, api, d, w_glu, w_out, tm, name):
    T, D = u.shape
    nt = T // tm
    nb = tm // 8

    def body(u_ref, h_ref, wb_ref, wc_ref, apr_ref, api_ref, d_ref, wg_ref, wo_ref, o_ref, yp_ref, gl_ref, st_ref,
             S, carry):
        @pl.when(pl.program_id(0) == 0)
        def _():
            carry[...] = jnp.zeros_like(carry)

        st_ref[...] = carry[...]
        uv = u_ref[...]
        ub = uv.astype(MXU_DTYPE)
        for sb in range(S5_NSB):
            S[:, sb * _SBW:(sb + 1) * _SBW] = jnp.dot(ub[:, sb * _SBU:(sb + 1) * _SBU], wb_ref[sb].astype(MXU_DTYPE),
                                                      preferred_element_type=F32)
        _s5_scan_fwd(S, carry, apr_ref, api_ref, nb)
        ys = [jnp.dot(S[:, sb * _SBW:(sb + 1) * _SBW].astype(MXU_DTYPE), wc_ref[sb].astype(MXU_DTYPE),
                      preferred_element_type=F32) for sb in range(S5_NSB)]
        yp = jnp.concatenate(ys, axis=1) + d_ref[...] * uv
        yp_ref[...] = yp
        gl = _dot(_gelu(yp), wg_ref[...])
        gl_ref[...] = gl
        out = gl[:, 0:D] * _sigmoid(gl[:, D:2 * D])
        o_ref[...] = h_ref[...] + _dot(out, wo_ref[...])

    full = lambda a: pl.BlockSpec(a.shape, lambda i: (0,) * a.ndim)
    args = (u, h, wb, wc, apr, api, d, w_glu, w_out)
    return _call(
        body, name=name,
        out_shape=(jax.ShapeDtypeStruct((T, D), F32), jax.ShapeDtypeStruct((T, D), F32),
                   jax.ShapeDtypeStruct((T, 2 * D), F32), jax.ShapeDtypeStruct((nt, 8, S5_NS), F32)),
        grid=(nt,),
        in_specs=[pl.BlockSpec((tm, D), lambda i: (i, 0)), pl.BlockSpec((tm, D), lambda i: (i, 0))]
        + [full(a) for a in args[2:]],
        out_specs=(pl.BlockSpec((tm, D), lambda i: (i, 0)), pl.BlockSpec((tm, D), lambda i: (i, 0)),
                   pl.BlockSpec((tm, 2 * D), lambda i: (i, 0)), pl.BlockSpec((None, 8, S5_NS), lambda i: (i, 0, 0))),
        scratch_shapes=[pltpu.VMEM((tm, S5_NS), F32), pltpu.VMEM((8, S5_NS), F32)],
        compiler_params=_params(("arbitrary",)),
    )(*args)


def s5_bwd_glu(dout, gl, ypre, u, w_glu, w_out, tm, name):
    T, D = u.shape

    def body(do_ref, gl_ref, yp_ref, u_ref, wg_ref, wo_ref, dy_ref, oact_ref, dgl_ref, gact_ref, dd_ref):
        @pl.when(pl.program_id(0) == 0)
        def _():
            dd_ref[...] = jnp.zeros_like(dd_ref)

        gl1 = gl_ref[:, 0:D]
        sg = _sigmoid(gl_ref[:, D:2 * D])
        oact_ref[...] = (gl1 * sg).astype(MXU_DTYPE)
        dgo = _dot_nt(do_ref[...], wo_ref[...])
        d1 = (dgo * sg).astype(MXU_DTYPE)
        d2 = (dgo * gl1 * sg * (1.0 - sg)).astype(MXU_DTYPE)
        dgl_ref[:, 0:D] = d1
        dgl_ref[:, D:2 * D] = d2
        dg = _dot_nt(d1, wg_ref[:, 0:D]) + _dot_nt(d2, wg_ref[:, D:2 * D])
        g, gd = _gelu_and_grad(yp_ref[...])
        gact_ref[...] = g.astype(MXU_DTYPE)
        dy = dg * gd
        dy_ref[...] = dy
        dd_ref[...] += _rowsum8(dy * u_ref[...])

    full = lambda a: pl.BlockSpec(a.shape, lambda i: (0,) * a.ndim)
    rs = lambda c: pl.BlockSpec((tm, c), lambda i: (i, 0))
    return _call(
        body, name=name,
        out_shape=(jax.ShapeDtypeStruct((T, D), F32), jax.ShapeDtypeStruct((T, D), MXU_DTYPE),
                   jax.ShapeDtypeStruct((T, 2 * D), MXU_DTYPE), jax.ShapeDtypeStruct((T, D), MXU_DTYPE),
                   jax.ShapeDtypeStruct((8, D), F32)),
        grid=(T // tm,),
        in_specs=[rs(D), rs(2 * D), rs(D), rs(D), full(w_glu), full(w_out)],
        out_specs=(rs(D), rs(D), rs(2 * D), rs(D), pl.BlockSpec((8, D), lambda i: (0, 0))),
        compiler_params=_params(("arbitrary",)),
    )(dout, gl, ypre, u, w_glu, w_out)


def s5_bwd_ssm(dy, u, st, wb, wc, apr, api, d, tm, name):
    T, D = u.shape
    nt = T // tm
    nb = tm // 8
    GP = S5_G * S5_P

    aprr, apir = apr[::-1], api[::-1]

    def body(dy_ref, u_ref, st_ref, wb_ref, wc_ref, apr_ref, api_ref, aprr_ref, apir_ref, d_ref,
             du_ref, dwb_ref, dwc_ref, dar_ref, dai_ref, S, L, carry, lcarry):
        @pl.when(pl.program_id(0) == 0)
        def _():
            for ref in (lcarry, dwb_ref, dwc_ref, dar_ref, dai_ref):
                ref[...] = jnp.zeros_like(ref)

        uv = u_ref[...]
        ub = uv.astype(MXU_DTYPE)
        dyv = dy_ref[...]
        dyb = dyv.astype(MXU_DTYPE)
        carry[...] = st_ref[...]
        for sb in range(S5_NSB):
            S[:, sb * _SBW:(sb + 1) * _SBW] = jnp.dot(ub[:, sb * _SBU:(sb + 1) * _SBU], wb_ref[sb].astype(MXU_DTYPE),
                                                      preferred_element_type=F32)
            L[:, sb * _SBW:(sb + 1) * _SBW] = _dot_nt(dyb[:, sb * _SBU:(sb + 1) * _SBU], wc_ref[sb])
        _s5_scan_fwd(S, carry, apr_ref, api_ref, nb)
        row8 = lax.broadcasted_iota(jnp.int32, (8, 1), 0)

        def blk(jj, _):
            j = nb - 1 - jj
            o = pl.multiple_of(j * 8, 8)
            op = pl.multiple_of(jnp.maximum(j - 1, 0) * 8, 8)
            for sb in range(S5_NSB):
                lr = slice(sb * _SBW, sb * _SBW + _SBH)
                li = slice(sb * _SBW + _SBH, (sb + 1) * _SBW)
                la = slice(sb * _SBH, (sb + 1) * _SBH)
                gr = L[pl.ds(o, 8), lr]
                gi = L[pl.ds(o, 8), li]
                for s, idx in ((1, 0), (2, 1), (4, 3)):
                    ar = apr_ref[idx:idx + 1, la]
                    ai = api_ref[idx:idx + 1, la]
                    m = row8 < 8 - s
                    sr = jnp.where(m, pltpu.roll(gr, 8 - s, 0), 0.0)
                    si = jnp.where(m, pltpu.roll(gi, 8 - s, 0), 0.0)
                    gr, gi = gr + ar * sr + ai * si, gi + ar * si - ai * sr
                cr = lcarry[0:1, lr]
                ci = lcarry[0:1, li]
                prr = aprr_ref[:, la]
                pii = apir_ref[:, la]
                gr, gi = gr + prr * cr + pii * ci, gi + prr * ci - pii * cr
                L[pl.ds(o, 8), lr] = gr
                L[pl.ds(o, 8), li] = gi
                lcarry[0:1, lr] = gr[0:1, :]
                lcarry[0:1, li] = gi[0:1, :]
                hr = S[pl.ds(o, 8), lr]
                hi = S[pl.ds(o, 8), li]
                pvr = jnp.where(j == 0, st_ref[0:1, lr], S[pl.ds(op, 8), lr][7:8, :])
                pvi = jnp.where(j == 0, st_ref[0:1, li], S[pl.ds(op, 8), li][7:8, :])
                hpr = jnp.where(row8 == 0, pvr, pltpu.roll(hr, 1, 0))
                hpi = jnp.where(row8 == 0, pvi, pltpu.roll(hi, 1, 0))
                dar_ref[:, la] += gr * hpr + gi * hpi
                dai_ref[:, la] += gi * hpr - gr * hpi
            return 0

        lax.fori_loop(0, nb, blk, 0)
        dus = []
        for sb in range(S5_NSB):
            ls = slice(sb * _SBW, (sb + 1) * _SBW)
            us = slice(sb * _SBU, (sb + 1) * _SBU)
            lb = L[:, ls].astype(MXU_DTYPE)
            dwb_ref[sb] += _dot_tn(lb, ub[:, us])
            dwc_ref[sb] += _dot_tn(S[:, ls].astype(MXU_DTYPE), dyb[:, us])
            dus.append(_dot_nt(lb, wb_ref[sb]))
        du_ref[...] = (jnp.concatenate(dus, axis=1) + dyv * d_ref[...]).astype(MXU_DTYPE)

    rt = lambda i: nt - 1 - i
    full = lambda a: pl.BlockSpec(a.shape, lambda i: (0,) * a.ndim)
    acc = lambda shape: pl.BlockSpec(shape, lambda i: (0,) * len(shape))
    return _call(
        body, name=name,
        out_shape=(jax.ShapeDtypeStruct((T, D), MXU_DTYPE), jax.ShapeDtypeStruct((S5_NSB, _SBW, _SBU), F32),
                   jax.ShapeDtypeStruct((S5_NSB, _SBW, _SBU), F32), jax.ShapeDtypeStruct((8, GP), F32),
                   jax.ShapeDtypeStruct((8, GP), F32)),
        grid=(nt,),
        in_specs=[pl.BlockSpec((tm, D), lambda i: (rt(i), 0)), pl.BlockSpec((tm, D), lambda i: (rt(i), 0)),
                  pl.BlockSpec((None, 8, S5_NS), lambda i: (rt(i), 0, 0)), full(wb), full(wc), full(apr), full(api),
                  full(apr), full(api), full(d)],
        out_specs=(pl.BlockSpec((tm, D), lambda i: (rt(i), 0)), acc((S5_NSB, _SBW, _SBU)), acc((S5_NSB, _SBW, _SBU)),
                   acc((8, GP)), acc((8, GP))),
        scratch_shapes=[pltpu.VMEM((tm, S5_NS), F32), pltpu.VMEM((tm, S5_NS), F32), pltpu.VMEM((8, S5_NS), F32),
                        pltpu.VMEM((8, S5_NS), F32)],
        compiler_params=_params(("arbitrary",)),
    )(dy, u, st, wb, wc, apr, api, aprr, apir, d)


def final_loss(h, g, target, tm, name):
    T, D = h.shape

    def body(h_ref, g_ref, t_ref, dh_ref, se_ref, dg_ref):
        @pl.when(pl.program_id(0) == 0)
        def _():
            se_ref[...] = jnp.zeros_like(se_ref)
            dg_ref[...] = jnp.zeros_like(dg_ref)

        r, xhat = _rms(h_ref[...])
        gv = g_ref[...]
        e = xhat * gv - t_ref[...]
        se_ref[...] += _rowsum8(e * e)
        dy = e * (1.0 / D)
        dg_ref[...] += _rowsum8(dy * xhat)
        dh_ref[...] = _rms_bwd(dy, xhat, r, gv)

    rs = pl.BlockSpec((tm, D), lambda i: (i, 0))
    acc = pl.BlockSpec((8, D), lambda i: (0, 0))
    return _call(
        body, name=name,
        out_shape=(jax.ShapeDtypeStruct((T, D), F32), jax.ShapeDtypeStruct((8, D), F32), jax.ShapeDtypeStruct((8, D), F32)),
        grid=(T // tm,), in_specs=[rs, pl.BlockSpec((1, D), lambda i: (0, 0)), rs], out_specs=(rs, acc, acc),
        compiler_params=_params(("arbitrary",)),
    )(h, g, target)


def _s5_discretize(a_re, a_im, log_dt, b_re, b_im):
    dt = jnp.exp(log_dt)[:, None]
    mag = jnp.exp(a_re * dt)
    abr = mag * jnp.cos(a_im * dt)
    abi = mag * jnp.sin(a_im * dt)
    ur, ui = abr - 1.0, abi
    den = a_re * a_re + a_im * a_im
    wr = (ur * a_re + ui * a_im) / den
    wi = (ui * a_re - ur * a_im) / den
    bbr = wr[..., None] * b_re - wi[..., None] * b_im
    bbi = wr[..., None] * b_im + wi[..., None] * b_re
    return abr, abi, bbr, bbi


def _s5_pack(abr, abi, bbr, bbi, c_re, c_im):
    eye = jnp.eye(S5_SB, dtype=F32)
    b = jnp.stack([bbr, bbi], 0).reshape(2, S5_NSB, S5_SB, S5_P, S5_GC)
    wb = jnp.einsum('rsgpc,gh->shcrgp', b, eye).reshape(S5_NSB, _SBU, _SBW)
    c = jnp.stack([c_re, -c_im], 0).reshape(2, S5_NSB, S5_SB, S5_GC, S5_P)
    wc = jnp.einsum('rsgcp,gh->srgphc', c, eye).reshape(S5_NSB, _SBW, _SBU)
    pr, pi = [abr.reshape(1, -1)], [abi.reshape(1, -1)]
    for _ in range(7):
        r, i = pr[-1], pi[-1]
        pr.append(r * pr[0] - i * pi[0])
        pi.append(r * pi[0] + i * pr[0])
    return wb.astype(MXU_DTYPE), wc.astype(MXU_DTYPE), jnp.concatenate(pr, 0), jnp.concatenate(pi, 0)


def _s5_unpack_grads(dwb_t, dwc, dar8, dai8):
    eye = jnp.eye(S5_SB, dtype=F32)
    t = dwb_t.reshape(S5_NSB, 2, S5_SB, S5_P, S5_SB, S5_GC)
    db = jnp.einsum('srgphc,gh->rsgpc', t, eye).reshape(2, S5_G, S5_P, S5_GC)
    t = dwc.reshape(S5_NSB, 2, S5_SB, S5_P, S5_SB, S5_GC)
    dc = jnp.einsum('srgphc,gh->rsgcp', t, eye).reshape(2, S5_G, S5_GC, S5_P)
    return db[0], db[1], dc[0], -dc[1], dar8.sum(0).reshape(S5_G, S5_P), dai8.sum(0).reshape(S5_G, S5_P)


TM = 256
TM_S5 = 128


def _tn(a, b, name):
    T, M, N = a.shape[0], a.shape[1], b.shape[1]
    bt = 4096 if a.dtype.itemsize + b.dtype.itemsize <= 4 else 2048
    return matmul_tn(a, b, min(M, 1024), min(N, 1024), bt if T % bt == 0 else T, name)


def local_step(x, target, W, sched):
    T, D = x.shape
    depth = W['norm_mix_g'].shape[0]
    row = lambda v: v.reshape(1, -1)
    saved = []
    h = x
    s5c = []
    tr = min(512, T)
    for j in range(W['s5_a_re'].shape[0]):
        prm = (W['s5_a_re'][j], W['s5_a_im'][j], W['s5_log_dt'][j], W['s5_b_re'][j], W['s5_b_im'][j])
        disc, disc_vjp = jax.vjp(_s5_discretize, *prm)
        wb, wc, apr, api = _s5_pack(*disc, W['s5_c_re'][j], W['s5_c_im'][j])
        s5c.append((wb, wc, apr, api, disc_vjp))
    sp_all = jax.nn.softplus(-W['rg_lambda'])
    for i in range(depth):
        j = i // 2
        if i % 2 == 0:
            xg, hn = norm_matmul(h, row(W['norm_mix_g'][i]), W['rg_w_in'][j], F32, tr, None, f"rg_in_{i}")
            h1, hs = rg_fwd(xg, h, W['rg_conv_w'][j], row(W['rg_conv_b'][j]), W['rg_w_a'][j].astype(MXU_DTYPE),
                            row(W['rg_b_a'][j]), W['rg_w_x'][j].astype(MXU_DTYPE), row(W['rg_b_x'][j]), row(sp_all[j]),
                            W['rg_w_out'][j], TM, f"rg_fwd_{i}", comm=sched.comm(f"rg_fwd_{i}"))
            mix = (xg, hn, hs)
        else:
            wb, wc, apr, api, _ = s5c[j]
            u, hn = norm_matmul(h, row(W['norm_mix_g'][i]), W['s5_w_in'][j], F32, tr, None, f"s5_in_{i}")
            h1, ypre, gl, st = s5_fwd(u, h, wb, wc, apr, api, row(W['s5_d'][j]), W['s5_w_glu'][j], W['s5_w_out'][j],
                                      TM_S5, f"s5_fwd_{i}")
            mix = (u, hn, ypre, gl, st)
        up, hn2 = norm_matmul(h1, row(W['norm_ffn_g'][i]), W['ffn_w_up'][i], MXU_DTYPE, tr, None, f"ffn_up_{i}",
                              comm=sched.comm(f"ffn_up_{i}"))
        h2 = ffn_fwd(up, h1, W['ffn_conv_w'][i], row(W['ffn_conv_b'][i]), W['ffn_w_down'][i], TM, f"ffn_fwd_{i}")
        saved.append((h, mix, h1, hn2, up))
        h = h2
    dh, se8, dgf8 = final_loss(h, row(W['norm_final_g']), target, tr, "final_loss")
    G = {k: [None] * len(v) for k, v in W.items() if k != 'norm_final_g'}
    G['norm_final_g'] = dgf8.sum(0)
    for i in reversed(range(depth)):
        j = i // 2
        h0, mix, h1, hn2, up = saved[i]
        dup, act, dcb8, dcw8 = ffn_bwd_act(dh, up, W['ffn_conv_w'][i], row(W['ffn_conv_b'][i]), W['ffn_w_down'][i], TM,
                                           f"ffn_bwd_act_{i}", comm=sched.comm(f"ffn_bwd_act_{i}", G))
        G['ffn_w_down'][i] = _tn(act, dh, f"ffn_dwdown_{i}")
        dh1, dg8, dupp = dx_norm_bwd(dup, W['ffn_w_up'][i], h1, row(W['norm_ffn_g'][i]), dh, TM, f"ffn_bwd_in_{i}",
                                     conv_w=W['ffn_conv_w'][i])
        G['ffn_w_up'][i] = _tn(hn2, dupp, f"ffn_dwup_{i}")
        G['ffn_conv_b'][i] = dcb8.sum(0)
        G['ffn_conv_w'][i] = dcw8.sum(1)
        G['norm_ffn_g'][i] = dg8.sum(0)
        if i % 2 == 0:
            xg, hn, hs = mix
            dxg, y, dwa, dwx, dba8, dbx8, dsp8, dcb8, dcw8 = rg_bwd(
                dh1, xg, hs, W['rg_conv_w'][j], row(W['rg_conv_b'][j]), W['rg_w_a'][j].astype(MXU_DTYPE),
                row(W['rg_b_a'][j]), W['rg_w_x'][j].astype(MXU_DTYPE), row(W['rg_b_x'][j]), row(sp_all[j]),
                W['rg_w_out'][j], TM, f"rg_bwd_{i}", comm=sched.comm(f"rg_bwd_{i}", G))
            G['rg_w_out'][j] = _tn(y, dh1, f"rg_dwout_{i}")
            dh, dg8 = dx_norm_bwd(dxg, W['rg_w_in'][j], h0, row(W['norm_mix_g'][i]), dh1, TM, f"rg_bwd_in_{i}")
            G['rg_w_in'][j] = _tn(hn, dxg, f"rg_dwin_{i}")
            G['rg_w_a'][j], G['rg_w_x'][j] = dwa, dwx
            G['rg_b_a'][j] = dba8.sum(0).reshape(RG_HEADS, RG_BW)
            G['rg_b_x'][j] = dbx8.sum(0).reshape(RG_HEADS, RG_BW)
            G['rg_lambda'][j] = dsp8.sum(0) * (-jax.nn.sigmoid(-W['rg_lambda'][j]))
            G['rg_conv_b'][j] = dcb8.sum(0)
            G['rg_conv_w'][j] = dcw8.sum(1)
        else:
            u, hn, ypre, gl, st = mix
            wb, wc, apr, api, disc_vjp = s5c[j]
            dy, oact, dgl, gact, dd8 = s5_bwd_glu(dh1, gl, ypre, u, W['s5_w_glu'][j], W['s5_w_out'][j], TM,
                                                  f"s5_bwd_glu_{i}")
            G['s5_w_out'][j] = _tn(oact, dh1, f"s5_dwout_{i}")
            G['s5_w_glu'][j] = _tn(gact, dgl, f"s5_dwglu_{i}")
            du, dwb_t, dwc, dar8, dai8 = s5_bwd_ssm(dy, u, st, wb, wc, apr, api, row(W['s5_d'][j]),
                                                    TM_S5, f"s5_bwd_ssm_{i}")
            dh, dg8 = dx_norm_bwd(du, W['s5_w_in'][j], h0, row(W['norm_mix_g'][i]), dh1, TM, f"s5_bwd_in_{i}")
            G['s5_w_in'][j] = _tn(hn, du, f"s5_dwin_{i}")
            dbbr, dbbi, dcr, dci, dabr, dabi = _s5_unpack_grads(dwb_t, dwc, dar8, dai8)
            da_re, da_im, dlog_dt, db_re, db_im = disc_vjp((dabr, dabi, dbbr, dbbi))
            G['s5_a_re'][j], G['s5_a_im'][j], G['s5_log_dt'][j] = da_re, da_im, dlog_dt
            G['s5_b_re'][j], G['s5_b_im'][j], G['s5_c_re'][j], G['s5_c_im'][j] = db_re, db_im, dcr, dci
            G['s5_d'][j] = dd8.sum(0)
        G['norm_mix_g'][i] = dg8.sum(0)
    G = {k: (v if (k == 'norm_final_g' or k in BIG) else jnp.stack(v, 0)) for k, v in G.items()}
    return se8, dh, G


BIG = {'rg_w_in': 1, 'rg_w_out': 0, 's5_w_in': 0, 's5_w_glu': 1, 's5_w_out': 0, 'ffn_w_up': 1, 'ffn_w_down': 0}
SMALL_SHARDED = ('rg_conv_w', 'ffn_conv_w', 's5_d')
REPLICATED = ('norm_mix_g', 'norm_ffn_g', 'norm_final_g', 'rg_conv_b', 'rg_w_a', 'rg_b_a', 'rg_w_x', 'rg_b_x',
              'rg_lambda', 's5_a_re', 's5_a_im', 's5_log_dt', 's5_b_re', 's5_b_im', 's5_c_re', 's5_c_im', 'ffn_conv_b')


def _me():
    x, y, c = lax.axis_index("x"), lax.axis_index("y"), lax.axis_index("c")
    return x, y, c, 4 * x + 2 * y + c


def _win(ref, axis, dev, width):
    idx = [slice(None)] * len(ref.shape)
    idx[axis] = pl.ds(pl.multiple_of(dev * width, width), width)
    return ref.at[tuple(idx)]


def gather_plan(items, deliver):
    n = len(items)

    def tools(ins, outs, sems):
        send_sems, recv_sems, local_sems = sems
        x, y, c, me = _me()
        sib = (x, y, 1 - c)
        chips = [(1 - x, y), (x, 1 - y), (1 - x, 1 - y)]
        num = lambda px, py, pc: 4 * px + 2 * py + pc

        def src_of(a):
            return ins[a] if items[a][1] is None else ins[a].at[items[a][1]]

        def block(a, dev):
            return _win(outs[a], items[a][2], dev, src_of(a).shape[items[a][2]])

        def copy(a, k, dev, to, own=False):
            return pltpu.make_async_remote_copy(
                src_ref=src_of(a) if own else block(a, dev), dst_ref=block(a, dev),
                send_sem=send_sems.at[a, k], recv_sem=recv_sems.at[a, k], device_id=to, device_id_type=MESH)

        mine = lambda: [pltpu.make_async_copy(src_of(a), block(a, me), local_sems.at[a]) for a in range(n)]
        own = lambda: [cp for a in range(n) for cp in
                       [copy(a, 0, me, sib, own=True)] + [copy(a, 1 + j, me, (*chip, c), own=True)
                                                          for j, chip in enumerate(chips)]]
        arrived = lambda j, a: copy(a, 1 + j, num(*chips[j], c), (x, y, c))
        passed = lambda j, a: copy(a, 4 + j, num(*chips[j], c), sib)
        from_sib = lambda: ([copy(a, 0, num(x, y, 1 - c), (x, y, c)) for a in range(n)]
                            + [copy(a, 4 + j, num(*chip, 1 - c), (x, y, c)) for j, chip in enumerate(chips)
                               for a in range(n)])
        return mine, own, arrived, passed, from_sib

    def start(ins, outs, sems):
        mine, own, _, _, _ = tools(ins, outs, sems)
        for cp in mine() + own():
            cp.start()

    def middle(ins, outs, sems):
        _, _, arrived, passed, _ = tools(ins, outs, sems)
        for j in range(3):
            for a in range(n):
                arrived(j, a).wait_recv()
                passed(j, a).start()

    def finish(ins, outs, sems):
        mine, own, _, passed, from_sib = tools(ins, outs, sems)
        for cp in from_sib():
            cp.wait_recv()
        for cp in own() + [passed(j, a) for j in range(3) for a in range(n)]:
            cp.wait_send()
        for cp in mine():
            cp.wait()

    return Comm([it[0] for it in items], [jax.ShapeDtypeStruct(it[3], it[0].dtype) for it in items],
                [pltpu.SemaphoreType.DMA((n, 7)), pltpu.SemaphoreType.DMA((n, 7)), pltpu.SemaphoreType.DMA((n,))],
                start, middle, finish, deliver)


def exchange_plan(items, deliver):
    n = len(items)
    width = [arr.shape[axis] // N_DEV for arr, axis in items]
    shard = [arr.shape[:axis] + (w,) + arr.shape[axis + 1:] for (arr, axis), w in zip(items, width)]

    def tools(ins, outs, sems):
        send_sems, recv_sems, local_sems = sems
        x, y, c, me = _me()
        piece = lambda a, dev: _win(ins[a], items[a][1], dev, width[a])
        mine = lambda: [pltpu.make_async_copy(piece(a, me), outs[a].at[me], local_sems.at[a]) for a in range(n)]

        def remote(sending):
            cps = []
            for k in range(1, N_DEV):
                px, py, pc = (1 - x) if k & 4 else x, (1 - y) if k & 2 else y, (1 - c) if k & 1 else c
                peer = 4 * px + 2 * py + pc
                for a in range(n):
                    src, dst = (piece(a, peer), outs[a].at[me]) if sending else (piece(a, me), outs[a].at[peer])
                    cps.append(pltpu.make_async_remote_copy(
                        src_ref=src, dst_ref=dst, send_sem=send_sems.at[a, k - 1], recv_sem=recv_sems.at[a, k - 1],
                        device_id=(px, py, pc), device_id_type=MESH))
            return cps

        return mine, remote

    def start(ins, outs, sems):
        mine, remote = tools(ins, outs, sems)
        for cp in mine() + remote(True):
            cp.start()

    def middle(ins, outs, sems):
        pass

    def finish(ins, outs, sems):
        mine, remote = tools(ins, outs, sems)
        for cp in remote(False):
            cp.wait_recv()
        for cp in remote(True):
            cp.wait_send()
        for cp in mine():
            cp.wait()

    return Comm([it[0] for it in items], [jax.ShapeDtypeStruct((N_DEV,) + s, it[0].dtype) for it, s in zip(items, shard)],
                [pltpu.SemaphoreType.DMA((n, 7)), pltpu.SemaphoreType.DMA((n, 7)), pltpu.SemaphoreType.DMA((n,))],
                start, middle, finish, deliver)


def adam_update(parts, w, m, v, name):
    R, C = w.shape
    br = next((b for b in (256, 128) if R > b and R % b == 0), R)
    np_ = parts.shape[0]

    def body(p_ref, w_ref, m_ref, v_ref, g_ref, d_ref, nm_ref, nv_ref):
        _adam_body(np_, p_ref, w_ref, m_ref, v_ref, g_ref, d_ref, nm_ref, nv_ref)

    bs = pl.BlockSpec((br, C), lambda i: (i, 0))
    out = jax.ShapeDtypeStruct((R, C), F32)
    return _call(
        body, name=name, out_shape=(out, out, out, out), grid=(R // br,),
        in_specs=[pl.BlockSpec((np_, br, C), lambda i: (0, i, 0)), bs, bs, bs], out_specs=(bs, bs, bs, bs),
        compiler_params=_params(("parallel",)),
    )(parts, w, m, v)


def _adam_body(np_, p_ref, w_ref, m_ref, v_ref, g_ref, d_ref, nm_ref, nv_ref):
    c1 = 1.0 / (1.0 - ADAM_B1 ** ADAM_STEP)
    c2 = 1.0 / (1.0 - ADAM_B2 ** ADAM_STEP)
    g = p_ref[0].astype(F32)
    for p in range(1, np_):
        g = g + p_ref[p].astype(F32)
    nm = ADAM_B1 * m_ref[...] + (1.0 - ADAM_B1) * g
    nv = ADAM_B2 * v_ref[...] + (1.0 - ADAM_B2) * (g * g)
    g_ref[...] = g
    nm_ref[...] = nm
    nv_ref[...] = nv
    d_ref[...] = -ADAM_LR * ((nm * c1) / (jnp.sqrt(nv * c2) + ADAM_EPS) + ADAM_WD * w_ref[...])


def adam_layer(parts, w, m, v, l, prev, name):
    L, R, C = w.shape
    br = next((b for b in (256, 128) if R > b and R % b == 0), R)

    def body(p_ref, w_ref, m_ref, v_ref, *rest):
        _adam_body(N_DEV, p_ref, w_ref, m_ref, v_ref, *rest[-4:])

    bs = pl.BlockSpec((None, br, C), lambda i: (l, i, 0))
    out = jax.ShapeDtypeStruct((L, R, C), F32)
    extra = {} if prev is None else dict(input_output_aliases={4 + q: q for q in range(4)})
    return _call(
        body, name=name, out_shape=(out, out, out, out), grid=(R // br,),
        in_specs=[pl.BlockSpec((N_DEV, br, C), lambda i: (0, i, 0)), bs, bs, bs] + ([] if prev is None else [ANY] * 4),
        out_specs=(bs, bs, bs, bs), compiler_params=_params(("parallel",)), **extra,
    )(parts, w, m, v, *(() if prev is None else prev))


def sum_parts(parts, name):
    n, R, C = parts.shape

    def body(p_ref, o_ref):
        g = p_ref[0]
        for p in range(1, n):
            g = g + p_ref[p]
        o_ref[...] = g

    return _call(body, name=name, out_shape=jax.ShapeDtypeStruct((R, C), parts.dtype),
                 compiler_params=pltpu.CompilerParams(vmem_limit_bytes=VMEM_LIMIT))(parts)


def _pack_rows(arrs):
    rows = []
    for a in arrs:
        f = a.reshape(-1)
        r = -(-f.shape[0] // 1024)
        r8 = -(-r // 8) * 8
        rows.append(jnp.pad(f, (0, r8 * 1024 - f.shape[0])).reshape(r8, 1024))
    packed = jnp.concatenate(rows, 0)
    return jnp.pad(packed, ((0, -packed.shape[0] % 128), (0, 0)))


def _unpack_rows(packed, shapes):
    out, o = [], 0
    for s in shapes:
        nel = math.prod(s)
        r8 = -(-(-(-nel // 1024)) // 8) * 8
        out.append(packed[o:o + r8].reshape(-1)[:nel].reshape(s))
        o += r8
    return out


def kernel(x, norm_mix_g, norm_ffn_g, norm_final_g, rg_w_in, rg_conv_w, rg_conv_b, rg_w_a, rg_b_a, rg_w_x, rg_b_x, rg_lambda, rg_w_out, s5_w_in, s5_a_re, s5_a_im, s5_log_dt, s5_b_re, s5_b_im, s5_c_re, s5_c_im, s5_d, s5_w_glu, s5_w_out, ffn_w_up, ffn_conv_w, ffn_conv_b, ffn_w_down, loss_target, m_norm_mix_g, m_norm_ffn_g, m_norm_final_g, m_rg_w_in, m_rg_conv_w, m_rg_conv_b, m_rg_w_a, m_rg_b_a, m_rg_w_x, m_rg_b_x, m_rg_lambda, m_rg_w_out, m_s5_w_in, m_s5_a_re, m_s5_a_im, m_s5_log_dt, m_s5_b_re, m_s5_b_im, m_s5_c_re, m_s5_c_im, m_s5_d, m_s5_w_glu, m_s5_w_out, m_ffn_w_up, m_ffn_conv_w, m_ffn_conv_b, m_ffn_w_down, v_norm_mix_g, v_norm_ffn_g, v_norm_final_g, v_rg_w_in, v_rg_conv_w, v_rg_conv_b, v_rg_w_a, v_rg_b_a, v_rg_w_x, v_rg_b_x, v_rg_lambda, v_rg_w_out, v_s5_w_in, v_s5_a_re, v_s5_a_im, v_s5_log_dt, v_s5_b_re, v_s5_b_im, v_s5_c_re, v_s5_c_im, v_s5_d, v_s5_w_glu, v_s5_w_out, v_ffn_w_up, v_ffn_conv_w, v_ffn_conv_b, v_ffn_w_down):
    names = ('norm_mix_g', 'norm_ffn_g', 'norm_final_g', 'rg_w_in', 'rg_conv_w', 'rg_conv_b', 'rg_w_a', 'rg_b_a',
             'rg_w_x', 'rg_b_x', 'rg_lambda', 'rg_w_out', 's5_w_in', 's5_a_re', 's5_a_im', 's5_log_dt', 's5_b_re',
             's5_b_im', 's5_c_re', 's5_c_im', 's5_d', 's5_w_glu', 's5_w_out', 'ffn_w_up', 'ffn_conv_w', 'ffn_conv_b',
             'ffn_w_down')
    loc = locals()
    Wl = {k: loc[k] for k in names}
    Ml = {k: loc['m_' + k] for k in names}
    Vl = {k: loc['v_' + k] for k in names}

    depth = norm_mix_g.shape[0]
    mixer_keys = lambda i: ([('rg_w_in', i // 2), ('rg_w_out', i // 2)] if i % 2 == 0 else
                            [('s5_w_in', i // 2), ('s5_w_glu', i // 2), ('s5_w_out', i // 2)])
    ffn_keys = lambda i: [('ffn_w_up', i), ('ffn_w_down', i)]
    shards = {k: Wl[k].astype(BF16) for k in BIG}
    W = {k: Wl[k] for k in REPLICATED}
    W.update({k: [None] * Wl[k].shape[0] for k in BIG})
    parts = {}

    def gather_of(keys, small=False):
        items = []
        for k, l in keys:
            _, r, c = shards[k].shape
            items.append((shards[k], l, BIG[k], (r * N_DEV, c) if BIG[k] == 0 else (r, c * N_DEV)))
        if small:
            items += [(Wl[k], None, Wl[k].ndim - 1, Wl[k].shape[:-1] + (Wl[k].shape[-1] * N_DEV,)) for k in SMALL_SHARDED]

        def deliver(outs):
            for (k, l), arr in zip(keys, outs):
                W[k][l] = arr
            if small:
                W.update(zip(SMALL_SHARDED, outs[len(keys):]))

        return gather_plan(items, deliver)

    def exchange_of(keys, G, extra=()):
        items = [(G[k][l], BIG[k]) for k, l in keys] + [(arr, axis) for _, arr, axis in extra]
        return exchange_plan(items, lambda outs: parts.update(zip(list(keys) + [e[0] for e in extra], outs)))

    class Sched:
        @staticmethod
        def comm(host, G=None):
            kind, _, i = host.rpartition("_")
            i = int(i)
            if host == "rg_fwd_0":
                return gather_of(ffn_keys(0))
            if kind == "ffn_up" and i + 1 < depth:
                return gather_of(mixer_keys(i + 1) + ffn_keys(i + 1))
            if kind == "ffn_bwd_act" and i + 1 < depth:
                return exchange_of(mixer_keys(i + 1) + ffn_keys(i + 1), G)
            if host == "rg_bwd_0":
                return exchange_of(ffn_keys(0), G)
            return None

    run_comm(gather_of(mixer_keys(0), small=True), "gather_first")

    se8, gx, G = local_step(x[0], loss_target[0], W, Sched)
    loss = lax.psum(0.5 * jnp.sum(se8) / x.shape[-1], ("x", "y", "c"))

    rep_shapes = [Wl[k].shape for k in REPLICATED]
    gp = _pack_rows([G[k].astype(F32) for k in REPLICATED])
    extra = [(k, G[k], G[k].ndim - 1) for k in SMALL_SHARDED] + [('replicated', gp, 0)]
    run_comm(exchange_of(mixer_keys(0), G, extra), "exchange_last")
    out_g, out_d, out_m, out_v = {}, {}, {}, {}
    for k in BIG:
        res = None
        for l in range(Wl[k].shape[0]):
            res = adam_layer(parts[(k, l)], Wl[k], Ml[k], Vl[k], l, res, f"adam_{k}_{l}")
        out_g[k], out_d[k], out_m[k], out_v[k] = res
    for k in SMALL_SHARDED:
        shp = Wl[k].shape
        r2 = (math.prod(shp[:-1]), shp[-1])
        res = adam_update(parts[k].reshape((N_DEV,) + r2), Wl[k].reshape(r2), Ml[k].reshape(r2), Vl[k].reshape(r2),
                          f"adam_{k}")
        out_g[k], out_d[k], out_m[k], out_v[k] = [t.reshape(shp) for t in res]
    rsum = sum_parts(parts['replicated'], "sum_replicated")
    run_comm(gather_plan([(rsum, None, 0, gp.shape)], lambda outs: parts.update(rep_full=outs[0])), "gather_small_grads")
    res = adam_update(parts['rep_full'][None], _pack_rows([Wl[k] for k in REPLICATED]),
                      _pack_rows([Ml[k] for k in REPLICATED]), _pack_rows([Vl[k] for k in REPLICATED]), "adam_replicated")
    for dst, packed in zip((out_g, out_d, out_m, out_v), res):
        for k, t in zip(REPLICATED, _unpack_rows(packed, rep_shapes)):
            dst[k] = t
    return (loss, gx[None], *[out_g[k] for k in names], *[out_d[k] for k in names], *[out_m[k] for k in names],
            *[out_v[k] for k in names])
```

```python
import functools
import math

import jax
import jax.numpy as jnp
from jax import lax
from jax.experimental import pallas as pl
from jax.experimental.pallas import tpu as pltpu

F32 = jnp.float32
BF16 = jnp.bfloat16
MXU_DTYPE = jnp.bfloat16

NORM_EPS = 1e-6
RG_C = 8.0
RG_HEADS = 8
RG_BW = 128
S5_G = 64
S5_GC = 16
S5_P = 64
S5_SB = 8
S5_NSB = S5_G // S5_SB
S5_NS = 2 * S5_G * S5_P
ADAM_LR = 0.001
ADAM_B1 = 0.9
ADAM_B2 = 0.999
ADAM_EPS = 1e-08
ADAM_WD = 0.01
ADAM_STEP = 10
N_DEV = 8
VMEM_LIMIT = 56 * 1024 * 1024


def _call(body, **kw):
    return pl.pallas_call(body, **kw)


def _params(sem, vmem=VMEM_LIMIT):
    return pltpu.CompilerParams(dimension_semantics=sem, vmem_limit_bytes=vmem)


MESH = pl.DeviceIdType.MESH
ANY = pl.BlockSpec(memory_space=pl.ANY)


class Comm:
    def __init__(self, operands, out_shape, scratch, start, middle, finish, deliver):
        self.operands, self.out_shape, self.scratch = list(operands), list(out_shape), list(scratch)
        self.start, self.middle, self.finish, self.deliver = start, middle, finish, deliver


def run_comm(comm, name):
    ci, co = len(comm.operands), len(comm.out_shape)

    def body(*refs):
        parts = (refs[:ci], refs[ci:ci + co], refs[ci + co:])
        comm.start(*parts)
        comm.middle(*parts)
        comm.finish(*parts)

    comm.deliver(_call(body, name=name, out_shape=tuple(comm.out_shape), in_specs=[ANY] * ci,
                       out_specs=tuple([ANY] * co), scratch_shapes=comm.scratch)(*comm.operands))


def _hosted(body, comm, args, *, name, out_shape, grid, in_specs, out_specs, scratch_shapes, compiler_params):
    if comm is None:
        return _call(body, name=name, out_shape=tuple(out_shape), grid=grid, in_specs=in_specs,
                     out_specs=tuple(out_specs), scratch_shapes=scratch_shapes, compiler_params=compiler_params)(*args)
    n_in, n_out, n_sc = len(in_specs), len(out_shape), len(scratch_shapes)
    ci, co = len(comm.operands), len(comm.out_shape)
    nsteps = math.prod(grid)
    mid = (2 * nsteps) // 3

    def wrapped(*refs):
        ins, refs = refs[:n_in], refs[n_in:]
        cins, refs = refs[:ci], refs[ci:]
        outs, refs = refs[:n_out], refs[n_out:]
        couts, refs = refs[:co], refs[co:]
        sc, csc = refs[:n_sc], refs[n_sc:]
        step = pl.program_id(0)
        for d in range(1, len(grid)):
            step = step * grid[d] + pl.program_id(d)

        @pl.when(step == 0)
        def _():
            comm.start(cins, couts, csc)

        body(*ins, *outs, *sc)

        @pl.when(step == mid)
        def _():
            comm.middle(cins, couts, csc)

        @pl.when(step == nsteps - 1)
        def _():
            comm.finish(cins, couts, csc)

    res = _call(wrapped, name=name, out_shape=(*out_shape, *comm.out_shape), grid=grid,
                in_specs=[*in_specs, *[ANY] * ci], out_specs=(*out_specs, *[ANY] * co),
                scratch_shapes=[*scratch_shapes, *comm.scratch],
                compiler_params=_params(("arbitrary",) * len(grid)))(*args, *comm.operands)
    comm.deliver(res[n_out:])
    return res[:n_out]


_GELU_C = 0.7978845608028654
_GELU_A = 0.044715


def _gelu(x):
    return 0.5 * x * (1.0 + jnp.tanh(_GELU_C * (x + _GELU_A * x * x * x)))


def _gelu_and_grad(x):
    x2 = x * x
    t = jnp.tanh(_GELU_C * (x + _GELU_A * x2 * x))
    g = 0.5 * x * (1.0 + t)
    dg = 0.5 * (1.0 + t) + 0.5 * x * (1.0 - t * t) * _GELU_C * (1.0 + 3.0 * _GELU_A * x2)
    return g, dg


def _sigmoid(x):
    return 1.0 / (1.0 + jnp.exp(-x))


def _neg_expm1(x):
    series = -x * (1.0 + x * (0.5 + x * (1.0 / 6.0 + x * (1.0 / 24.0 + x * (1.0 / 120.0 + x * (1.0 / 720.0))))))
    return jnp.where(x > -0.1, series, 1.0 - jnp.exp(x))


def _rowsum8(x):
    r, c = x.shape
    return x.reshape(r // 8, 8, c).sum(axis=0)


def _dot(a, b):
    return jnp.dot(a.astype(MXU_DTYPE), b.astype(MXU_DTYPE), preferred_element_type=F32)


def _dot_nt(a, b):
    return lax.dot_general(a.astype(MXU_DTYPE), b.astype(MXU_DTYPE), (((1,), (1,)), ((), ())),
                           preferred_element_type=F32)


def _dot_tn(a, b):
    return lax.dot_general(a.astype(MXU_DTYPE), b.astype(MXU_DTYPE), (((0,), (0,)), ((), ())),
                           preferred_element_type=F32)


def _shift_down(x, s, fills, row):
    y = pltpu.roll(x, s, 0)
    for t in range(s):
        y = jnp.where(row == t, fills[s - 1 - t], y)
    return y


def _shift_up(x, s, fills, row):
    n = x.shape[0]
    y = pltpu.roll(x, n - s, 0)
    for t in range(s):
        y = jnp.where(row == n - s + t, fills[t], y)
    return y


def _rms(x):
    r = lax.rsqrt(jnp.mean(x * x, axis=-1, keepdims=True) + NORM_EPS)
    return r, x * r


def _rms_bwd(dhn, xhat, r, g):
    dz = dhn * g
    return r * (dz - xhat * jnp.mean(dz * xhat, axis=-1, keepdims=True))


def norm_matmul(h, g, w, out_dtype, tm, tn, name, comm=None):
    T, D = h.shape
    N = w.shape[1]
    tn = N if tn is None else tn

    def body(h_ref, g_ref, w_ref, o_ref, hn_ref, hn_s):
        @pl.when(pl.program_id(1) == 0)
        def _():
            _, xhat = _rms(h_ref[...])
            v = (xhat * g_ref[...]).astype(MXU_DTYPE)
            hn_s[...] = v
            hn_ref[...] = v

        o_ref[...] = jnp.dot(hn_s[...], w_ref[...].astype(MXU_DTYPE), preferred_element_type=F32).astype(o_ref.dtype)

    return _hosted(
        body, comm, (h, g, w), name=name,
        out_shape=(jax.ShapeDtypeStruct((T, N), out_dtype), jax.ShapeDtypeStruct((T, D), MXU_DTYPE)),
        grid=(T // tm, N // tn),
        in_specs=[pl.BlockSpec((tm, D), lambda i, j: (i, 0)), pl.BlockSpec((1, D), lambda i, j: (0, 0)),
                  pl.BlockSpec((D, tn), lambda i, j: (0, j))],
        out_specs=(pl.BlockSpec((tm, tn), lambda i, j: (i, j)), pl.BlockSpec((tm, D), lambda i, j: (i, 0))),
        scratch_shapes=[pltpu.VMEM((tm, D), MXU_DTYPE)],
        compiler_params=_params(("parallel", "arbitrary")),
    )


def matmul_tn(a, b, bm, bn, bt, name, out_dtype=BF16):
    T, M = a.shape
    N = b.shape[1]
    nk = T // bt

    def body(a_ref, b_ref, o_ref, acc):
        k = pl.program_id(2)

        @pl.when(k == 0)
        def _():
            acc[...] = jnp.zeros_like(acc)

        acc[...] += _dot_tn(a_ref[...], b_ref[...])

        @pl.when(k == nk - 1)
        def _():
            o_ref[...] = acc[...].astype(o_ref.dtype)

    return _call(
        body, name=name,
        out_shape=jax.ShapeDtypeStruct((M, N), out_dtype),
        grid=(M // bm, N // bn, nk),
        in_specs=[pl.BlockSpec((bt, bm), lambda i, j, k: (k, i)), pl.BlockSpec((bt, bn), lambda i, j, k: (k, j))],
        out_specs=pl.BlockSpec((bm, bn), lambda i, j, k: (i, j)),
        scratch_shapes=[pltpu.VMEM((bm, bn), F32)],
        compiler_params=_params(("parallel", "parallel", "arbitrary")),
    )(a, b)


def dx_norm_bwd(dz, w, h, g, dres, tm, name, conv_w=None, chunk=1024):
    T, N = dz.shape
    D = w.shape[0]
    nt = T // tm
    has_conv = conv_w is not None
    kw = conv_w.shape[0] if has_conv else 0

    def body(*refs):
        if has_conv:
            dz_ref, cw_ref, w_ref, h_ref, g_ref, dres_ref, dh_ref, dg_ref, dzp_ref, carry, zs = refs
        else:
            dz_ref, w_ref, h_ref, g_ref, dres_ref, dh_ref, dg_ref = refs
        i = pl.program_id(0)

        @pl.when(i == 0)
        def _():
            dg_ref[...] = jnp.zeros_like(dg_ref)
            if has_conv:
                carry[...] = jnp.zeros_like(carry)

        if has_conv:
            row = lax.broadcasted_iota(jnp.int32, (tm, 1), 0)
            for c0 in range(0, N, chunk):
                sl = slice(c0, c0 + chunk)
                x = dz_ref[:, sl].astype(F32)
                fills = [carry[t:t + 1, sl] for t in range(kw - 1)]
                acc = cw_ref[kw - 1:kw, sl] * x
                for s in range(1, kw):
                    acc = acc + cw_ref[kw - 1 - s:kw - s, sl] * _shift_up(x, s, fills, row)
                zb = acc.astype(MXU_DTYPE)
                zs[:, sl] = zb
                dzp_ref[:, sl] = zb
            carry[...] = dz_ref[0:16, :].astype(F32)
            z = zs[...]
        else:
            z = dz_ref[...]
        dhn = _dot_nt(z, w_ref[...])
        r, xhat = _rms(h_ref[...])
        dg_ref[...] += _rowsum8(dhn * xhat)
        dh_ref[...] = dres_ref[...] + _rms_bwd(dhn, xhat, r, g_ref[...])

    if has_conv:
        ti = lambda i: nt - 1 - i
    else:
        ti = lambda i: i
    row_spec = lambda c: pl.BlockSpec((tm, c), lambda i: (ti(i), 0))
    full = lambda a: pl.BlockSpec(a.shape, lambda i: (0,) * a.ndim)
    in_specs = [row_spec(N)] + ([full(conv_w)] if has_conv else []) + [full(w), row_spec(D), full(g), row_spec(D)]
    out_shape = [jax.ShapeDtypeStruct((T, D), F32), jax.ShapeDtypeStruct((8, D), F32)]
    out_specs = [row_spec(D), pl.BlockSpec((8, D), lambda i: (0, 0))]
    scratch = []
    if has_conv:
        out_shape.append(jax.ShapeDtypeStruct((T, N), MXU_DTYPE))
        out_specs.append(row_spec(N))
        scratch = [pltpu.VMEM((16, N), F32), pltpu.VMEM((tm, N), MXU_DTYPE)]
    args = [dz] + ([conv_w] if has_conv else []) + [w, h, g, dres]
    return _call(
        body, name=name, out_shape=tuple(out_shape), grid=(nt,), in_specs=in_specs, out_specs=tuple(out_specs),
        scratch_shapes=scratch, compiler_params=_params(("arbitrary",)),
    )(*args)


def _ffn_conv_chunk(up_ref, cw_ref, cb_ref, carry, row, sl):
    x = up_ref[:, sl].astype(F32)
    fills = [carry[15:16, sl], carry[14:15, sl]]
    x1 = _shift_down(x, 1, fills, row)
    x2 = _shift_down(x, 2, fills, row)
    out = cb_ref[:, sl] + cw_ref[2:3, sl] * x + cw_ref[1:2, sl] * x1 + cw_ref[0:1, sl] * x2
    return out, x, x1, x2


def ffn_fwd(up, h, conv_w, conv_b, w_down, tm, name, chunk=512, comm=None):
    T, C = up.shape
    F = C // 2
    D = h.shape[1]

    def body(up_ref, h_ref, cw_ref, cb_ref, wd_ref, o_ref, carry, act_s):
        @pl.when(pl.program_id(0) == 0)
        def _():
            carry[...] = jnp.zeros_like(carry)

        row = lax.broadcasted_iota(jnp.int32, (tm, 1), 0)
        for c0 in range(0, F, chunk):
            a = _ffn_conv_chunk(up_ref, cw_ref, cb_ref, carry, row, slice(c0, c0 + chunk))[0]
            v = _ffn_conv_chunk(up_ref, cw_ref, cb_ref, carry, row, slice(F + c0, F + c0 + chunk))[0]
            act_s[:, c0:c0 + chunk] = (_gelu(a) * v).astype(MXU_DTYPE)
        carry[...] = up_ref[tm - 16:tm, :].astype(F32)
        o_ref[...] = h_ref[...] + jnp.dot(act_s[...], wd_ref[...].astype(MXU_DTYPE), preferred_element_type=F32)

    full = lambda a: pl.BlockSpec(a.shape, lambda i: (0,) * a.ndim)
    return _hosted(
        body, comm, (up, h, conv_w, conv_b, w_down), name=name, out_shape=(jax.ShapeDtypeStruct((T, D), F32),),
        grid=(T // tm,),
        in_specs=[pl.BlockSpec((tm, C), lambda i: (i, 0)), pl.BlockSpec((tm, D), lambda i: (i, 0)),
                  full(conv_w), full(conv_b), full(w_down)],
        out_specs=(pl.BlockSpec((tm, D), lambda i: (i, 0)),),
        scratch_shapes=[pltpu.VMEM((16, C), F32), pltpu.VMEM((tm, F), MXU_DTYPE)],
        compiler_params=_params(("arbitrary",)),
    )[0]


def ffn_bwd_act(dout, up, conv_w, conv_b, w_down, tm, name, chunk=512, comm=None):
    T, C = up.shape
    F = C // 2
    D = dout.shape[1]
    kw = conv_w.shape[0]

    def body(do_ref, up_ref, cw_ref, cb_ref, wd_ref, dup_ref, act_ref, dcb_ref, dcw_ref, carry, dact_s):
        @pl.when(pl.program_id(0) == 0)
        def _():
            carry[...] = jnp.zeros_like(carry)
            dcb_ref[...] = jnp.zeros_like(dcb_ref)
            dcw_ref[...] = jnp.zeros_like(dcw_ref)

        dact_s[...] = _dot_nt(do_ref[...], wd_ref[...])
        row = lax.broadcasted_iota(jnp.int32, (tm, 1), 0)
        for c0 in range(0, F, chunk):
            sa = slice(c0, c0 + chunk)
            sv = slice(F + c0, F + c0 + chunk)
            a, xa, xa1, xa2 = _ffn_conv_chunk(up_ref, cw_ref, cb_ref, carry, row, sa)
            v, xv, xv1, xv2 = _ffn_conv_chunk(up_ref, cw_ref, cb_ref, carry, row, sv)
            ga, dga = _gelu_and_grad(a)
            act_ref[:, sa] = (ga * v).astype(MXU_DTYPE)
            dact = dact_s[:, sa]
            da = dact * v * dga
            dv = dact * ga
            dup_ref[:, sa] = da.astype(MXU_DTYPE)
            dup_ref[:, sv] = dv.astype(MXU_DTYPE)
            for d, xs, s in ((da, (xa2, xa1, xa), sa), (dv, (xv2, xv1, xv), sv)):
                dcb_ref[:, s] += _rowsum8(d)
                for k in range(kw):
                    dcw_ref[k, :, s] += _rowsum8(d * xs[k])
        carry[...] = up_ref[tm - 16:tm, :].astype(F32)

    full = lambda a: pl.BlockSpec(a.shape, lambda i: (0,) * a.ndim)
    return _hosted(
        body, comm, (dout, up, conv_w, conv_b, w_down), name=name,
        out_shape=(jax.ShapeDtypeStruct((T, C), MXU_DTYPE), jax.ShapeDtypeStruct((T, F), MXU_DTYPE),
                   jax.ShapeDtypeStruct((8, C), F32), jax.ShapeDtypeStruct((kw, 8, C), F32)),
        grid=(T // tm,),
        in_specs=[pl.BlockSpec((tm, D), lambda i: (i, 0)), pl.BlockSpec((tm, C), lambda i: (i, 0)),
                  full(conv_w), full(conv_b), full(w_down)],
        out_specs=(pl.BlockSpec((tm, C), lambda i: (i, 0)), pl.BlockSpec((tm, F), lambda i: (i, 0)),
                   pl.BlockSpec((8, C), lambda i: (0, 0)), pl.BlockSpec((kw, 8, C), lambda i: (0, 0, 0))),
        scratch_shapes=[pltpu.VMEM((16, C), F32), pltpu.VMEM((tm, F), F32)],
        compiler_params=_params(("arbitrary",)),
    )


def _rg_gates(xr, wa_ref, ba_ref, wx_ref, bx_ref, sp_ref):
    xb = xr.astype(MXU_DTYPE)
    pa, px = [], []
    for hd in range(RG_HEADS):
        sl = slice(hd * RG_BW, (hd + 1) * RG_BW)
        pa.append(jnp.dot(xb[:, sl], wa_ref[hd].astype(MXU_DTYPE), preferred_element_type=F32))
        px.append(jnp.dot(xb[:, sl], wx_ref[hd].astype(MXU_DTYPE), preferred_element_type=F32))
    r = _sigmoid(jnp.concatenate(pa, axis=1) + ba_ref[...])
    ig = _sigmoid(jnp.concatenate(px, axis=1) + bx_ref[...])
    la = -RG_C * r * sp_ref[...]
    a = jnp.exp(la)
    mult = jnp.sqrt(_neg_expm1(2.0 * la))
    return xb, r, ig, a, mult


def _rg_conv(x, fills, cw_ref, cb_ref, row):
    x1 = _shift_down(x, 1, fills, row)
    x2 = _shift_down(x, 2, fills, row)
    x3 = _shift_down(x, 3, fills, row)
    xr = cb_ref[...] + cw_ref[3:4, :] * x + cw_ref[2:3, :] * x1 + cw_ref[1:2, :] * x2 + cw_ref[0:1, :] * x3
    return xr, (x3, x2, x1, x)


def rg_fwd(xg, h, conv_w, conv_b, w_a, b_a, w_x, b_x, sp, w_out, tm, name, comm=None):
    T, D2 = xg.shape
    D = D2 // 2
    nb = tm // 8

    def body(xg_ref, h_ref, cw_ref, cb_ref, wa_ref, ba_ref, wx_ref, bx_ref, sp_ref, wo_ref, o_ref, hs_ref,
             xcarry, hcarry, a_s, b_s):
        @pl.when(pl.program_id(0) == 0)
        def _():
            xcarry[...] = jnp.zeros_like(xcarry)
            hcarry[...] = jnp.zeros_like(hcarry)

        row = lax.broadcasted_iota(jnp.int32, (tm, 1), 0)
        x = xg_ref[:, 0:D]
        fills = [xcarry[7:8, :], xcarry[6:7, :], xcarry[5:6, :]]
        xr, _ = _rg_conv(x, fills, cw_ref, cb_ref, row)
        xcarry[...] = xg_ref[tm - 8:tm, 0:D]
        _, r, ig, a, mult = _rg_gates(xr, wa_ref, ba_ref, wx_ref, bx_ref, sp_ref)
        a_s[...] = a
        b_s[...] = mult * ig * xr
        row8 = lax.broadcasted_iota(jnp.int32, (8, 1), 0)

        def blk(j, c):
            o = pl.multiple_of(j * 8, 8)
            A = a_s[pl.ds(o, 8), :]
            H = b_s[pl.ds(o, 8), :]
            for s in (1, 2, 4):
                m = row8 >= s
                H = H + A * jnp.where(m, pltpu.roll(H, s, 0), 0.0)
                A = A * jnp.where(m, pltpu.roll(A, s, 0), 1.0)
            H = H + A * c
            hs_ref[pl.ds(o, 8), :] = H
            return H[7:8, :]

        c = lax.fori_loop(0, nb, blk, hcarry[0:1, :])
        hcarry[0:1, :] = c
        y = hs_ref[...] * _gelu(xg_ref[:, D:D2])
        o_ref[...] = h_ref[...] + _dot(y, wo_ref[...])

    full = lambda a: pl.BlockSpec(a.shape, lambda i: (0,) * a.ndim)
    args = (xg, h, conv_w, conv_b, w_a, b_a, w_x, b_x, sp, w_out)
    return _hosted(
        body, comm, args, name=name,
        out_shape=(jax.ShapeDtypeStruct((T, D), F32), jax.ShapeDtypeStruct((T, D), F32)),
        grid=(T // tm,),
        in_specs=[pl.BlockSpec((tm, D2), lambda i: (i, 0)), pl.BlockSpec((tm, D), lambda i: (i, 0))]
        + [full(a) for a in args[2:]],
        out_specs=(pl.BlockSpec((tm, D), lambda i: (i, 0)), pl.BlockSpec((tm, D), lambda i: (i, 0))),
        scratch_shapes=[pltpu.VMEM((8, D), F32), pltpu.VMEM((8, D), F32), pltpu.VMEM((tm, D), F32),
                        pltpu.VMEM((tm, D), F32)],
        compiler_params=_params(("arbitrary",)),
    )


def rg_bwd(dout, xg, hs, conv_w, conv_b, w_a, b_a, w_x, b_x, sp, w_out, tm, name, comm=None):
    T, D2 = xg.shape
    D = D2 // 2
    nt = T // tm
    nb = tm // 8
    kw = conv_w.shape[0]

    def body(do_ref, xg_ref, xh_ref, hs_ref, hh_ref, cw_ref, cb_ref, wa_ref, ba_ref, wx_ref, bx_ref, sp_ref, wo_ref,
             dxg_ref, y_ref, dwa_ref, dwx_ref, dba_ref, dbx_ref, dsp_ref, dcb_ref, dcw_ref,
             acarry, lcarry, dcarry, a_s, b_s, l_s):
        i = pl.program_id(0)
        first_tile = i == nt - 1

        @pl.when(i == 0)
        def _():
            for ref in (acarry, lcarry, dcarry, dwa_ref, dwx_ref, dba_ref, dbx_ref, dsp_ref, dcb_ref, dcw_ref):
                ref[...] = jnp.zeros_like(ref)

        row = lax.broadcasted_iota(jnp.int32, (tm, 1), 0)
        keep = jnp.where(first_tile, 0.0, 1.0)
        x = xg_ref[:, 0:D]
        gate = xg_ref[:, D:D2]
        xh = xh_ref[...] * keep
        fills = [xh[7:8, :], xh[6:7, :], xh[5:6, :]]
        xr, taps = _rg_conv(x, fills, cw_ref, cb_ref, row)
        xb, r, ig, a, mult = _rg_gates(xr, wa_ref, ba_ref, wx_ref, bx_ref, sp_ref)
        hs = hs_ref[...]
        hprev = _shift_down(hs, 1, [hh_ref[7:8, :] * keep], row)
        dy = _dot_nt(do_ref[...], wo_ref[...])
        gg, dgg = _gelu_and_grad(gate)
        y_ref[...] = (hs * gg).astype(MXU_DTYPE)
        dxg_ref[:, D:D2] = (dy * hs * dgg).astype(MXU_DTYPE)
        a_s[...] = _shift_up(a, 1, [acarry[0:1, :]], row)
        b_s[...] = dy * gg
        row8 = lax.broadcasted_iota(jnp.int32, (8, 1), 0)

        def blk(jj, c):
            o = pl.multiple_of((nb - 1 - jj) * 8, 8)
            A = a_s[pl.ds(o, 8), :]
            H = b_s[pl.ds(o, 8), :]
            for s in (1, 2, 4):
                m = row8 < 8 - s
                H = H + A * jnp.where(m, pltpu.roll(H, 8 - s, 0), 0.0)
                A = A * jnp.where(m, pltpu.roll(A, 8 - s, 0), 1.0)
            H = H + A * c
            l_s[pl.ds(o, 8), :] = H
            return H[0:1, :]

        c = lax.fori_loop(0, nb, blk, lcarry[0:1, :])
        lcarry[0:1, :] = c
        acarry[0:1, :] = a[0:1, :]
        lam = l_s[...]
        dla = lam * hprev * a - (lam * ig * xr) * (a * a) / mult
        dig = lam * mult * xr
        dxr = lam * mult * ig
        spv = sp_ref[...]
        dsp_ref[...] += _rowsum8(dla * (-RG_C) * r)
        dpa = (dla * (-RG_C) * spv) * r * (1.0 - r)
        dpx = dig * ig * (1.0 - ig)
        dba_ref[...] += _rowsum8(dpa)
        dbx_ref[...] += _rowsum8(dpx)
        dpab = dpa.astype(MXU_DTYPE)
        dpxb = dpx.astype(MXU_DTYPE)
        back = []
        for hd in range(RG_HEADS):
            sl = slice(hd * RG_BW, (hd + 1) * RG_BW)
            dwa_ref[hd] += _dot_tn(xb[:, sl], dpab[:, sl])
            dwx_ref[hd] += _dot_tn(xb[:, sl], dpxb[:, sl])
            back.append(_dot_nt(dpab[:, sl], wa_ref[hd]) + _dot_nt(dpxb[:, sl], wx_ref[hd]))
        dxr = dxr + jnp.concatenate(back, axis=1)
        nfills = [dcarry[0:1, :], dcarry[1:2, :], dcarry[2:3, :]]
        dxp = cw_ref[kw - 1:kw, :] * dxr
        for s in range(1, kw):
            dxp = dxp + cw_ref[kw - 1 - s:kw - s, :] * _shift_up(dxr, s, nfills, row)
        dcarry[...] = dxr[0:8, :]
        dxg_ref[:, 0:D] = dxp.astype(MXU_DTYPE)
        dcb_ref[...] += _rowsum8(dxr)
        for k in range(kw):
            dcw_ref[k] += _rowsum8(dxr * taps[k])

    rt = lambda i: nt - 1 - i
    halo = lambda i: jnp.maximum((nt - 1 - i) * (tm // 8) - 1, 0)
    full = lambda a: pl.BlockSpec(a.shape, lambda i: (0,) * a.ndim)
    params = (conv_w, conv_b, w_a, b_a, w_x, b_x, sp, w_out)
    acc = lambda shape: pl.BlockSpec(shape, lambda i: (0,) * len(shape))
    return _hosted(
        body, comm, (dout, xg, xg, hs, hs, *params), name=name,
        out_shape=(jax.ShapeDtypeStruct((T, D2), MXU_DTYPE), jax.ShapeDtypeStruct((T, D), MXU_DTYPE),
                   jax.ShapeDtypeStruct((RG_HEADS, RG_BW, RG_BW), F32), jax.ShapeDtypeStruct((RG_HEADS, RG_BW, RG_BW), F32),
                   jax.ShapeDtypeStruct((8, D), F32), jax.ShapeDtypeStruct((8, D), F32), jax.ShapeDtypeStruct((8, D), F32),
                   jax.ShapeDtypeStruct((8, D), F32), jax.ShapeDtypeStruct((kw, 8, D), F32)),
        grid=(nt,),
        in_specs=[pl.BlockSpec((tm, D), lambda i: (rt(i), 0)), pl.BlockSpec((tm, D2), lambda i: (rt(i), 0)),
                  pl.BlockSpec((8, D), lambda i: (halo(i), 0)), pl.BlockSpec((tm, D), lambda i: (rt(i), 0)),
                  pl.BlockSpec((8, D), lambda i: (halo(i), 0))] + [full(a) for a in params],
        out_specs=(pl.BlockSpec((tm, D2), lambda i: (rt(i), 0)), pl.BlockSpec((tm, D), lambda i: (rt(i), 0)),
                   acc((RG_HEADS, RG_BW, RG_BW)), acc((RG_HEADS, RG_BW, RG_BW)), acc((8, D)), acc((8, D)), acc((8, D)),
                   acc((8, D)), acc((kw, 8, D))),
        scratch_shapes=[pltpu.VMEM((8, D), F32), pltpu.VMEM((8, D), F32), pltpu.VMEM((8, D), F32),
                        pltpu.VMEM((tm, D), F32), pltpu.VMEM((tm, D), F32), pltpu.VMEM((tm, D), F32)],
        compiler_params=_params(("arbitrary",)),
    )


_SBW = 2 * S5_SB * S5_P
_SBH = S5_SB * S5_P
_SBU = S5_SB * S5_GC


def _regroup(x, seg_len):
    n, c = x.shape
    return jnp.swapaxes(x.reshape(8, seg_len, c), 0, 1).reshape(n, c)


def _ungroup(x, seg_len):
    n, c = x.shape
    return jnp.swapaxes(x.reshape(seg_len, 8, c), 0, 1).reshape(n, c)


def _s5_lanes(sb):
    return (slice(sb * _SBW, sb * _SBW + _SBH), slice(sb * _SBW + _SBH, (sb + 1) * _SBW),
            slice(sb * _SBH, (sb + 1) * _SBH))


def _s5_scan_fwd(S, carry, cin, ap_r, ap_i, aq_r, aq_i, seg_len):
    row8 = lax.broadcasted_iota(jnp.int32, (8, 1), 0)
    for sb in range(S5_NSB):
        lr, li, la = _s5_lanes(sb)
        ar, ai = ap_r[0:1, la], ap_i[0:1, la]
        hr, hi = S[0:8, lr], S[0:8, li]
        for i in range(1, seg_len):
            blk = slice(i * 8, (i + 1) * 8)
            hr, hi = ar * hr - ai * hi + S[blk, lr], ar * hi + ai * hr + S[blk, li]
            S[blk, lr] = hr
            S[blk, li] = hi
        for s, idx in ((1, 0), (2, 1), (4, 3)):
            qr, qi = aq_r[idx:idx + 1, la], aq_i[idx:idx + 1, la]
            m = row8 >= s
            sr = jnp.where(m, pltpu.roll(hr, s, 0), 0.0)
            si = jnp.where(m, pltpu.roll(hi, s, 0), 0.0)
            hr, hi = hr + qr * sr - qi * si, hi + qr * si + qi * sr
        cr, ci = carry[0:1, lr], carry[0:1, li]
        pr, pi = aq_r[:, la], aq_i[:, la]
        hr, hi = hr + pr * cr - pi * ci, hi + pr * ci + pi * cr
        xr = jnp.where(row8 == 0, cr, pltpu.roll(hr, 1, 0))
        xi = jnp.where(row8 == 0, ci, pltpu.roll(hi, 1, 0))
        carry[0:1, lr] = hr[7:8, :]
        carry[0:1, li] = hi[7:8, :]
        if cin is not None:
            cin[:, lr] = xr
            cin[:, li] = xi
        for i in range(seg_len):
            blk = slice(i * 8, (i + 1) * 8)
            pr, pi = ap_r[i:i + 1, la], ap_i[i:i + 1, la]
            S[blk, lr] += pr * xr - pi * xi
            S[blk, li] += pr * xi + pi * xr


def s5_fwd(u, h, wb, wc, ap_r, ap_i, aq_r, aq_i, d, w_glu, w_out, tm, name):
    T, D = u.shape
    nt = T // tm
    seg = tm // 8

    def body(u_ref, h_ref, wb_ref, wc_ref, apr_ref, api_ref, aqr_ref, aqi_ref, d_ref, wg_ref, wo_ref,
             o_ref, yp_ref, gl_ref, st_ref, S, carry):
        @pl.when(pl.program_id(0) == 0)
        def _():
            carry[...] = jnp.zeros_like(carry)

        st_ref[...] = carry[...]
        uv = _regroup(u_ref[...], seg)
        ub = uv.astype(MXU_DTYPE)
        for sb in range(S5_NSB):
            S[:, sb * _SBW:(sb + 1) * _SBW] = jnp.dot(ub[:, sb * _SBU:(sb + 1) * _SBU], wb_ref[sb].astype(MXU_DTYPE),
                                                      preferred_element_type=F32)
        _s5_scan_fwd(S, carry, None, apr_ref, api_ref, aqr_ref, aqi_ref, seg)
        ys = [jnp.dot(S[:, sb * _SBW:(sb + 1) * _SBW].astype(MXU_DTYPE), wc_ref[sb].astype(MXU_DTYPE),
                      preferred_element_type=F32) for sb in range(S5_NSB)]
        yp = jnp.concatenate(ys, axis=1) + d_ref[...] * uv
        yp_ref[...] = _ungroup(yp, seg)
        gl = _dot(_gelu(yp), wg_ref[...])
        gl_ref[...] = _ungroup(gl, seg)
        out = gl[:, 0:D] * _sigmoid(gl[:, D:2 * D])
        o_ref[...] = h_ref[...] + _ungroup(_dot(out, wo_ref[...]), seg)

    full = lambda a: pl.BlockSpec(a.shape, lambda i: (0,) * a.ndim)
    args = (u, h, wb, wc, ap_r, ap_i, aq_r, aq_i, d, w_glu, w_out)
    return _call(
        body, name=name,
        out_shape=(jax.ShapeDtypeStruct((T, D), F32), jax.ShapeDtypeStruct((T, D), F32),
                   jax.ShapeDtypeStruct((T, 2 * D), F32), jax.ShapeDtypeStruct((nt, 8, S5_NS), F32)),
        grid=(nt,),
        in_specs=[pl.BlockSpec((tm, D), lambda i: (i, 0)), pl.BlockSpec((tm, D), lambda i: (i, 0))]
        + [full(a) for a in args[2:]],
        out_specs=(pl.BlockSpec((tm, D), lambda i: (i, 0)), pl.BlockSpec((tm, D), lambda i: (i, 0)),
                   pl.BlockSpec((tm, 2 * D), lambda i: (i, 0)), pl.BlockSpec((None, 8, S5_NS), lambda i: (i, 0, 0))),
        scratch_shapes=[pltpu.VMEM((tm, S5_NS), F32), pltpu.VMEM((8, S5_NS), F32)],
        compiler_params=_params(("arbitrary",)),
    )(*args)


def s5_bwd_glu(dout, gl, ypre, u, w_glu, w_out, tm, name):
    T, D = u.shape

    def body(do_ref, gl_ref, yp_ref, u_ref, wg_ref, wo_ref, dy_ref, oact_ref, dgl_ref, gact_ref, dd_ref):
        @pl.when(pl.program_id(0) == 0)
        def _():
            dd_ref[...] = jnp.zeros_like(dd_ref)

        gl1 = gl_ref[:, 0:D]
        sg = _sigmoid(gl_ref[:, D:2 * D])
        oact_ref[...] = (gl1 * sg).astype(MXU_DTYPE)
        dgo = _dot_nt(do_ref[...], wo_ref[...])
        d1 = (dgo * sg).astype(MXU_DTYPE)
        d2 = (dgo * gl1 * sg * (1.0 - sg)).astype(MXU_DTYPE)
        dgl_ref[:, 0:D] = d1
        dgl_ref[:, D:2 * D] = d2
        dg = _dot_nt(d1, wg_ref[:, 0:D]) + _dot_nt(d2, wg_ref[:, D:2 * D])
        g, gd = _gelu_and_grad(yp_ref[...])
        gact_ref[...] = g.astype(MXU_DTYPE)
        dy = dg * gd
        dy_ref[...] = dy
        dd_ref[...] += _rowsum8(dy * u_ref[...])

    full = lambda a: pl.BlockSpec(a.shape, lambda i: (0,) * a.ndim)
    rs = lambda c: pl.BlockSpec((tm, c), lambda i: (i, 0))
    return _call(
        body, name=name,
        out_shape=(jax.ShapeDtypeStruct((T, D), F32), jax.ShapeDtypeStruct((T, D), MXU_DTYPE),
                   jax.ShapeDtypeStruct((T, 2 * D), MXU_DTYPE), jax.ShapeDtypeStruct((T, D), MXU_DTYPE),
                   jax.ShapeDtypeStruct((8, D), F32)),
        grid=(T // tm,),
        in_specs=[rs(D), rs(2 * D), rs(D), rs(D), full(w_glu), full(w_out)],
        out_specs=(rs(D), rs(D), rs(2 * D), rs(D), pl.BlockSpec((8, D), lambda i: (0, 0))),
        compiler_params=_params(("arbitrary",)),
    )(dout, gl, ypre, u, w_glu, w_out)


def s5_bwd_ssm(dy, u, st, wb, wc, ap_r, ap_i, aq_r, aq_i, d, tm, name):
    T, D = u.shape
    nt = T // tm
    seg = tm // 8
    GP = S5_G * S5_P
    tables = (ap_r, ap_i, aq_r, aq_i, ap_r[::-1], ap_i[::-1], aq_r[::-1], aq_i[::-1])

    def body(dy_ref, u_ref, st_ref, wb_ref, wc_ref, apr_ref, api_ref, aqr_ref, aqi_ref, aprr_ref, apir_ref,
             aqrr_ref, aqir_ref, d_ref, du_ref, dwb_ref, dwc_ref, dar_ref, dai_ref, S, L, carry, lcarry, cin):
        @pl.when(pl.program_id(0) == 0)
        def _():
            for ref in (lcarry, dwb_ref, dwc_ref, dar_ref, dai_ref):
                ref[...] = jnp.zeros_like(ref)

        uv = _regroup(u_ref[...], seg)
        ub = uv.astype(MXU_DTYPE)
        dyv = _regroup(dy_ref[...], seg)
        dyb = dyv.astype(MXU_DTYPE)
        carry[...] = st_ref[...]
        for sb in range(S5_NSB):
            S[:, sb * _SBW:(sb + 1) * _SBW] = jnp.dot(ub[:, sb * _SBU:(sb + 1) * _SBU], wb_ref[sb].astype(MXU_DTYPE),
                                                      preferred_element_type=F32)
            L[:, sb * _SBW:(sb + 1) * _SBW] = _dot_nt(dyb[:, sb * _SBU:(sb + 1) * _SBU], wc_ref[sb])
        _s5_scan_fwd(S, carry, cin, apr_ref, api_ref, aqr_ref, aqi_ref, seg)
        row8 = lax.broadcasted_iota(jnp.int32, (8, 1), 0)
        for sb in range(S5_NSB):
            lr, li, la = _s5_lanes(sb)
            ar, ai = apr_ref[0:1, la], api_ref[0:1, la]
            last = slice((seg - 1) * 8, seg * 8)
            gr, gi = L[last, lr], L[last, li]
            for i in reversed(range(seg - 1)):
                blk = slice(i * 8, (i + 1) * 8)
                gr, gi = ar * gr + ai * gi + L[blk, lr], ar * gi - ai * gr + L[blk, li]
                L[blk, lr] = gr
                L[blk, li] = gi
            for s, idx in ((1, 0), (2, 1), (4, 3)):
                qr, qi = aqr_ref[idx:idx + 1, la], aqi_ref[idx:idx + 1, la]
                m = row8 < 8 - s
                sr = jnp.where(m, pltpu.roll(gr, 8 - s, 0), 0.0)
                si = jnp.where(m, pltpu.roll(gi, 8 - s, 0), 0.0)
                gr, gi = gr + qr * sr + qi * si, gi + qr * si - qi * sr
            cr, ci = lcarry[0:1, lr], lcarry[0:1, li]
            pr, pi = aqrr_ref[:, la], aqir_ref[:, la]
            gr, gi = gr + pr * cr + pi * ci, gi + pr * ci - pi * cr
            xr = jnp.where(row8 == 7, cr, pltpu.roll(gr, 7, 0))
            xi = jnp.where(row8 == 7, ci, pltpu.roll(gi, 7, 0))
            lcarry[0:1, lr] = gr[0:1, :]
            lcarry[0:1, li] = gi[0:1, :]
            acc_r = jnp.zeros((8, _SBH), F32)
            acc_i = jnp.zeros((8, _SBH), F32)
            for i in range(seg):
                blk = slice(i * 8, (i + 1) * 8)
                pr, pi = aprr_ref[i:i + 1, la], apir_ref[i:i + 1, la]
                gr = L[blk, lr] + pr * xr + pi * xi
                gi = L[blk, li] + pr * xi - pi * xr
                L[blk, lr] = gr
                L[blk, li] = gi
                if i == 0:
                    hpr, hpi = cin[:, lr], cin[:, li]
                else:
                    hpr, hpi = S[(i - 1) * 8:i * 8, lr], S[(i - 1) * 8:i * 8, li]
                acc_r = acc_r + gr * hpr + gi * hpi
                acc_i = acc_i + gi * hpr - gr * hpi
            dar_ref[:, la] += acc_r
            dai_ref[:, la] += acc_i
        dus = []
        for sb in range(S5_NSB):
            ls = slice(sb * _SBW, (sb + 1) * _SBW)
            us = slice(sb * _SBU, (sb + 1) * _SBU)
            lb = L[:, ls].astype(MXU_DTYPE)
            dwb_ref[sb] += _dot_tn(lb, ub[:, us])
            dwc_ref[sb] += _dot_tn(S[:, ls].astype(MXU_DTYPE), dyb[:, us])
            dus.append(_dot_nt(lb, wb_ref[sb]))
        du_ref[...] = _ungroup(jnp.concatenate(dus, axis=1) + dyv * d_ref[...], seg).astype(MXU_DTYPE)

    rt = lambda i: nt - 1 - i
    full = lambda a: pl.BlockSpec(a.shape, lambda i: (0,) * a.ndim)
    acc = lambda shape: pl.BlockSpec(shape, lambda i: (0,) * len(shape))
    return _call(
        body, name=name,
        out_shape=(jax.ShapeDtypeStruct((T, D), MXU_DTYPE), jax.ShapeDtypeStruct((S5_NSB, _SBW, _SBU), F32),
                   jax.ShapeDtypeStruct((S5_NSB, _SBW, _SBU), F32), jax.ShapeDtypeStruct((8, GP), F32),
                   jax.ShapeDtypeStruct((8, GP), F32)),
        grid=(nt,),
        in_specs=[pl.BlockSpec((tm, D), lambda i: (rt(i), 0)), pl.BlockSpec((tm, D), lambda i: (rt(i), 0)),
                  pl.BlockSpec((None, 8, S5_NS), lambda i: (rt(i), 0, 0)), full(wb), full(wc)]
        + [full(t) for t in tables] + [full(d)],
        out_specs=(pl.BlockSpec((tm, D), lambda i: (rt(i), 0)), acc((S5_NSB, _SBW, _SBU)), acc((S5_NSB, _SBW, _SBU)),
                   acc((8, GP)), acc((8, GP))),
        scratch_shapes=[pltpu.VMEM((tm, S5_NS), F32), pltpu.VMEM((tm, S5_NS), F32), pltpu.VMEM((8, S5_NS), F32),
                        pltpu.VMEM((8, S5_NS), F32), pltpu.VMEM((8, S5_NS), F32)],
        compiler_params=_params(("arbitrary",)),
    )(dy, u, st, wb, wc, *tables, d)


def final_loss(h, g, target, tm, name):
    T, D = h.shape

    def body(h_ref, g_ref, t_ref, dh_ref, se_ref, dg_ref):
        @pl.when(pl.program_id(0) == 0)
        def _():
            se_ref[...] = jnp.zeros_like(se_ref)
            dg_ref[...] = jnp.zeros_like(dg_ref)

        r, xhat = _rms(h_ref[...])
        gv = g_ref[...]
        e = xhat * gv - t_ref[...]
        se_ref[...] += _rowsum8(e * e)
        dy = e * (1.0 / D)
        dg_ref[...] += _rowsum8(dy * xhat)
        dh_ref[...] = _rms_bwd(dy, xhat, r, gv)

    rs = pl.BlockSpec((tm, D), lambda i: (i, 0))
    acc = pl.BlockSpec((8, D), lambda i: (0, 0))
    return _call(
        body, name=name,
        out_shape=(jax.ShapeDtypeStruct((T, D), F32), jax.ShapeDtypeStruct((8, D), F32), jax.ShapeDtypeStruct((8, D), F32)),
        grid=(T // tm,), in_specs=[rs, pl.BlockSpec((1, D), lambda i: (0, 0)), rs], out_specs=(rs, acc, acc),
        compiler_params=_params(("arbitrary",)),
    )(h, g, target)


def _s5_discretize(a_re, a_im, log_dt, b_re, b_im):
    dt = jnp.exp(log_dt)[:, None]
    mag = jnp.exp(a_re * dt)
    abr = mag * jnp.cos(a_im * dt)
    abi = mag * jnp.sin(a_im * dt)
    ur, ui = abr - 1.0, abi
    den = a_re * a_re + a_im * a_im
    wr = (ur * a_re + ui * a_im) / den
    wi = (ui * a_re - ur * a_im) / den
    bbr = wr[..., None] * b_re - wi[..., None] * b_im
    bbi = wr[..., None] * b_im + wi[..., None] * b_re
    return abr, abi, bbr, bbi


def _s5_pack(abr, abi, bbr, bbi, c_re, c_im, seg_len):
    eye = jnp.eye(S5_SB, dtype=F32)
    b = jnp.stack([bbr, bbi], 0).reshape(2, S5_NSB, S5_SB, S5_P, S5_GC)
    wb = jnp.einsum('rsgpc,gh->shcrgp', b, eye).reshape(S5_NSB, _SBU, _SBW)
    c = jnp.stack([c_re, -c_im], 0).reshape(2, S5_NSB, S5_SB, S5_GC, S5_P)
    wc = jnp.einsum('rsgcp,gh->srgphc', c, eye).reshape(S5_NSB, _SBW, _SBU)
    def powers(r, i, n):
        pr, pi = [r], [i]
        for _ in range(n - 1):
            pr.append(pr[-1] * r - pi[-1] * i)
            pi.append(pr[-2] * i + pi[-1] * r)
        return jnp.concatenate(pr, 0), jnp.concatenate(pi, 0)

    ap_r, ap_i = powers(abr.reshape(1, -1), abi.reshape(1, -1), seg_len)
    aq_r, aq_i = powers(ap_r[seg_len - 1:seg_len], ap_i[seg_len - 1:seg_len], 8)
    return wb.astype(MXU_DTYPE), wc.astype(MXU_DTYPE), ap_r, ap_i, aq_r, aq_i


def _s5_unpack_grads(dwb_t, dwc, dar8, dai8):
    eye = jnp.eye(S5_SB, dtype=F32)
    t = dwb_t.reshape(S5_NSB, 2, S5_SB, S5_P, S5_SB, S5_GC)
    db = jnp.einsum('srgphc,gh->rsgpc', t, eye).reshape(2, S5_G, S5_P, S5_GC)
    t = dwc.reshape(S5_NSB, 2, S5_SB, S5_P, S5_SB, S5_GC)
    dc = jnp.einsum('srgphc,gh->rsgcp', t, eye).reshape(2, S5_G, S5_GC, S5_P)
    return db[0], db[1], dc[0], -dc[1], dar8.sum(0).reshape(S5_G, S5_P), dai8.sum(0).reshape(S5_G, S5_P)


TM = 256
TM_S5 = 128


def _tn(a, b, name):
    T, M, N = a.shape[0], a.shape[1], b.shape[1]
    bt = 4096 if a.dtype.itemsize + b.dtype.itemsize <= 4 else 2048
    return matmul_tn(a, b, min(M, 1024), min(N, 1024), bt if T % bt == 0 else T, name)


def local_step(x, target, W, sched):
    T, D = x.shape
    depth = W['norm_mix_g'].shape[0]
    row = lambda v: v.reshape(1, -1)
    saved = []
    h = x
    s5c = []
    tr = min(512, T)
    for j in range(W['s5_a_re'].shape[0]):
        prm = (W['s5_a_re'][j], W['s5_a_im'][j], W['s5_log_dt'][j], W['s5_b_re'][j], W['s5_b_im'][j])
        disc, disc_vjp = jax.vjp(_s5_discretize, *prm)
        s5c.append((*_s5_pack(*disc, W['s5_c_re'][j], W['s5_c_im'][j], min(TM_S5, T) // 8), disc_vjp))
    sp_all = jax.nn.softplus(-W['rg_lambda'])
    for i in range(depth):
        j = i // 2
        if i % 2 == 0:
            xg, hn = norm_matmul(h, row(W['norm_mix_g'][i]), W['rg_w_in'][j], F32, tr, None, f"rg_in_{i}")
            h1, hs = rg_fwd(xg, h, W['rg_conv_w'][j], row(W['rg_conv_b'][j]), W['rg_w_a'][j].astype(MXU_DTYPE),
                            row(W['rg_b_a'][j]), W['rg_w_x'][j].astype(MXU_DTYPE), row(W['rg_b_x'][j]), row(sp_all[j]),
                            W['rg_w_out'][j], TM, f"rg_fwd_{i}", comm=sched.comm(f"rg_fwd_{i}"))
            mix = (xg, hn, hs)
        else:
            u, hn = norm_matmul(h, row(W['norm_mix_g'][i]), W['s5_w_in'][j], F32, tr, None, f"s5_in_{i}")
            h1, ypre, gl, st = s5_fwd(u, h, *s5c[j][:-1], row(W['s5_d'][j]), W['s5_w_glu'][j], W['s5_w_out'][j],
                                      min(TM_S5, T), f"s5_fwd_{i}")
            mix = (u, hn, ypre, gl, st)
        up, hn2 = norm_matmul(h1, row(W['norm_ffn_g'][i]), W['ffn_w_up'][i], MXU_DTYPE, tr, None, f"ffn_up_{i}",
                              comm=sched.comm(f"ffn_up_{i}"))
        h2 = ffn_fwd(up, h1, W['ffn_conv_w'][i], row(W['ffn_conv_b'][i]), W['ffn_w_down'][i], TM, f"ffn_fwd_{i}",
                     comm=sched.comm(f"ffn_fwd_{i}"))
        saved.append((h, mix, h1, hn2, up))
        h = h2
    dh, se8, dgf8 = final_loss(h, row(W['norm_final_g']), target, tr, "final_loss")
    G = {k: [None] * len(v) for k, v in W.items() if k != 'norm_final_g'}
    G['norm_final_g'] = dgf8.sum(0)
    for i in reversed(range(depth)):
        j = i // 2
        h0, mix, h1, hn2, up = saved[i]
        dup, act, dcb8, dcw8 = ffn_bwd_act(dh, up, W['ffn_conv_w'][i], row(W['ffn_conv_b'][i]), W['ffn_w_down'][i], TM,
                                           f"ffn_bwd_act_{i}", comm=sched.comm(f"ffn_bwd_act_{i}", G))
        G['ffn_w_down'][i] = _tn(act, dh, f"ffn_dwdown_{i}")
        dh1, dg8, dupp = dx_norm_bwd(dup, W['ffn_w_up'][i], h1, row(W['norm_ffn_g'][i]), dh, TM, f"ffn_bwd_in_{i}",
                                     conv_w=W['ffn_conv_w'][i])
        G['ffn_w_up'][i] = _tn(hn2, dupp, f"ffn_dwup_{i}")
        G['ffn_conv_b'][i] = dcb8.sum(0)
        G['ffn_conv_w'][i] = dcw8.sum(1)
        G['norm_ffn_g'][i] = dg8.sum(0)
        if i % 2 == 0:
            xg, hn, hs = mix
            dxg, y, dwa, dwx, dba8, dbx8, dsp8, dcb8, dcw8 = rg_bwd(
                dh1, xg, hs, W['rg_conv_w'][j], row(W['rg_conv_b'][j]), W['rg_w_a'][j].astype(MXU_DTYPE),
                row(W['rg_b_a'][j]), W['rg_w_x'][j].astype(MXU_DTYPE), row(W['rg_b_x'][j]), row(sp_all[j]),
                W['rg_w_out'][j], TM, f"rg_bwd_{i}", comm=sched.comm(f"rg_bwd_{i}", G))
            G['rg_w_out'][j] = _tn(y, dh1, f"rg_dwout_{i}")
            dh, dg8 = dx_norm_bwd(dxg, W['rg_w_in'][j], h0, row(W['norm_mix_g'][i]), dh1, TM, f"rg_bwd_in_{i}")
            G['rg_w_in'][j] = _tn(hn, dxg, f"rg_dwin_{i}")
            G['rg_w_a'][j], G['rg_w_x'][j] = dwa, dwx
            G['rg_b_a'][j] = dba8.sum(0).reshape(RG_HEADS, RG_BW)
            G['rg_b_x'][j] = dbx8.sum(0).reshape(RG_HEADS, RG_BW)
            G['rg_lambda'][j] = dsp8.sum(0) * (-jax.nn.sigmoid(-W['rg_lambda'][j]))
            G['rg_conv_b'][j] = dcb8.sum(0)
            G['rg_conv_w'][j] = dcw8.sum(1)
        else:
            u, hn, ypre, gl, st = mix
            disc_vjp = s5c[j][-1]
            dy, oact, dgl, gact, dd8 = s5_bwd_glu(dh1, gl, ypre, u, W['s5_w_glu'][j], W['s5_w_out'][j], TM,
                                                  f"s5_bwd_glu_{i}")
            G['s5_w_out'][j] = _tn(oact, dh1, f"s5_dwout_{i}")
            G['s5_w_glu'][j] = _tn(gact, dgl, f"s5_dwglu_{i}")
            du, dwb_t, dwc, dar8, dai8 = s5_bwd_ssm(dy, u, st, *s5c[j][:-1], row(W['s5_d'][j]),
                                                    min(TM_S5, T), f"s5_bwd_ssm_{i}")
            dh, dg8 = dx_norm_bwd(du, W['s5_w_in'][j], h0, row(W['norm_mix_g'][i]), dh1, TM, f"s5_bwd_in_{i}")
            G['s5_w_in'][j] = _tn(hn, du, f"s5_dwin_{i}")
            dbbr, dbbi, dcr, dci, dabr, dabi = _s5_unpack_grads(dwb_t, dwc, dar8, dai8)
            da_re, da_im, dlog_dt, db_re, db_im = disc_vjp((dabr, dabi, dbbr, dbbi))
            G['s5_a_re'][j], G['s5_a_im'][j], G['s5_log_dt'][j] = da_re, da_im, dlog_dt
            G['s5_b_re'][j], G['s5_b_im'][j], G['s5_c_re'][j], G['s5_c_im'][j] = db_re, db_im, dcr, dci
            G['s5_d'][j] = dd8.sum(0)
        G['norm_mix_g'][i] = dg8.sum(0)
    G = {k: (v if (k == 'norm_final_g' or k in BIG) else jnp.stack(v, 0)) for k, v in G.items()}
    return se8, dh, G


BIG = {'rg_w_in': 1, 'rg_w_out': 0, 's5_w_in': 0, 's5_w_glu': 1, 's5_w_out': 0, 'ffn_w_up': 1, 'ffn_w_down': 0}
SMALL_SHARDED = ('rg_conv_w', 'ffn_conv_w', 's5_d')
REPLICATED = ('norm_mix_g', 'norm_ffn_g', 'norm_final_g', 'rg_conv_b', 'rg_w_a', 'rg_b_a', 'rg_w_x', 'rg_b_x',
              'rg_lambda', 's5_a_re', 's5_a_im', 's5_log_dt', 's5_b_re', 's5_b_im', 's5_c_re', 's5_c_im', 'ffn_conv_b')


def _me():
    x, y, c = lax.axis_index("x"), lax.axis_index("y"), lax.axis_index("c")
    return x, y, c, 4 * x + 2 * y + c


def _win(ref, axis, dev, width):
    idx = [slice(None)] * len(ref.shape)
    idx[axis] = pl.ds(pl.multiple_of(dev * width, width), width)
    return ref.at[tuple(idx)]


def gather_plan(items, deliver):
    n = len(items)

    def tools(ins, outs, sems):
        send_sems, recv_sems, local_sems = sems
        x, y, c, me = _me()
        sib = (x, y, 1 - c)
        chips = [(1 - x, y), (x, 1 - y), (1 - x, 1 - y)]
        num = lambda px, py, pc: 4 * px + 2 * py + pc

        def src_of(a):
            return ins[a] if items[a][1] is None else ins[a].at[items[a][1]]

        def block(a, dev):
            return _win(outs[a], items[a][2], dev, src_of(a).shape[items[a][2]])

        def copy(a, k, dev, to, own=False):
            return pltpu.make_async_remote_copy(
                src_ref=src_of(a) if own else block(a, dev), dst_ref=block(a, dev),
                send_sem=send_sems.at[a, k], recv_sem=recv_sems.at[a, k], device_id=to, device_id_type=MESH)

        mine = lambda: [pltpu.make_async_copy(src_of(a), block(a, me), local_sems.at[a]) for a in range(n)]
        own = lambda: [cp for a in range(n) for cp in
                       [copy(a, 0, me, sib, own=True)] + [copy(a, 1 + j, me, (*chip, c), own=True)
                                                          for j, chip in enumerate(chips)]]
        arrived = lambda j, a: copy(a, 1 + j, num(*chips[j], c), (x, y, c))
        passed = lambda j, a: copy(a, 4 + j, num(*chips[j], c), sib)
        from_sib = lambda: ([copy(a, 0, num(x, y, 1 - c), (x, y, c)) for a in range(n)]
                            + [copy(a, 4 + j, num(*chip, 1 - c), (x, y, c)) for j, chip in enumerate(chips)
                               for a in range(n)])
        return mine, own, arrived, passed, from_sib

    def start(ins, outs, sems):
        mine, own, _, _, _ = tools(ins, outs, sems)
        for cp in mine() + own():
            cp.start()

    def middle(ins, outs, sems):
        _, _, arrived, passed, _ = tools(ins, outs, sems)
        for j in range(3):
            for a in range(n):
                arrived(j, a).wait_recv()
                passed(j, a).start()

    def finish(ins, outs, sems):
        mine, own, _, passed, from_sib = tools(ins, outs, sems)
        for cp in from_sib():
            cp.wait_recv()
        for cp in own() + [passed(j, a) for j in range(3) for a in range(n)]:
            cp.wait_send()
        for cp in mine():
            cp.wait()

    return Comm([it[0] for it in items], [jax.ShapeDtypeStruct(it[3], it[0].dtype) for it in items],
                [pltpu.SemaphoreType.DMA((n, 7)), pltpu.SemaphoreType.DMA((n, 7)), pltpu.SemaphoreType.DMA((n,))],
                start, middle, finish, deliver)


def exchange_plan(items, deliver):
    n = len(items)
    width = [arr.shape[axis] // N_DEV for arr, axis in items]
    shard = [arr.shape[:axis] + (w,) + arr.shape[axis + 1:] for (arr, axis), w in zip(items, width)]

    def tools(ins, outs, sems):
        send_sems, recv_sems, local_sems = sems
        x, y, c, me = _me()
        piece = lambda a, dev: _win(ins[a], items[a][1], dev, width[a])
        mine = lambda: [pltpu.make_async_copy(piece(a, me), outs[a].at[me], local_sems.at[a]) for a in range(n)]

        def remote(sending):
            cps = []
            for k in range(1, N_DEV):
                px, py, pc = (1 - x) if k & 4 else x, (1 - y) if k & 2 else y, (1 - c) if k & 1 else c
                peer = 4 * px + 2 * py + pc
                for a in range(n):
                    src, dst = (piece(a, peer), outs[a].at[me]) if sending else (piece(a, me), outs[a].at[peer])
                    cps.append(pltpu.make_async_remote_copy(
                        src_ref=src, dst_ref=dst, send_sem=send_sems.at[a, k - 1], recv_sem=recv_sems.at[a, k - 1],
                        device_id=(px, py, pc), device_id_type=MESH))
            return cps

        return mine, remote

    def start(ins, outs, sems):
        mine, remote = tools(ins, outs, sems)
        for cp in mine() + remote(True):
            cp.start()

    def middle(ins, outs, sems):
        pass

    def finish(ins, outs, sems):
        mine, remote = tools(ins, outs, sems)
        for cp in remote(False):
            cp.wait_recv()
        for cp in remote(True):
            cp.wait_send()
        for cp in mine():
            cp.wait()

    return Comm([it[0] for it in items], [jax.ShapeDtypeStruct((N_DEV,) + s, it[0].dtype) for it, s in zip(items, shard)],
                [pltpu.SemaphoreType.DMA((n, 7)), pltpu.SemaphoreType.DMA((n, 7)), pltpu.SemaphoreType.DMA((n,))],
                start, middle, finish, deliver)


def adam_update(parts, w, m, v, name):
    R, C = w.shape
    br = next((b for b in (256, 128) if R > b and R % b == 0), R)
    np_ = parts.shape[0]

    def body(p_ref, w_ref, m_ref, v_ref, g_ref, d_ref, nm_ref, nv_ref):
        _adam_body(np_, p_ref, w_ref, m_ref, v_ref, g_ref, d_ref, nm_ref, nv_ref)

    bs = pl.BlockSpec((br, C), lambda i: (i, 0))
    out = jax.ShapeDtypeStruct((R, C), F32)
    return _call(
        body, name=name, out_shape=(out, out, out, out), grid=(R // br,),
        in_specs=[pl.BlockSpec((np_, br, C), lambda i: (0, i, 0)), bs, bs, bs], out_specs=(bs, bs, bs, bs),
        compiler_params=_params(("parallel",)),
    )(parts, w, m, v)


def _adam_body(np_, p_ref, w_ref, m_ref, v_ref, g_ref, d_ref, nm_ref, nv_ref):
    c1 = 1.0 / (1.0 - ADAM_B1 ** ADAM_STEP)
    c2 = 1.0 / (1.0 - ADAM_B2 ** ADAM_STEP)
    g = p_ref[0].astype(F32)
    for p in range(1, np_):
        g = g + p_ref[p].astype(F32)
    nm = ADAM_B1 * m_ref[...] + (1.0 - ADAM_B1) * g
    nv = ADAM_B2 * v_ref[...] + (1.0 - ADAM_B2) * (g * g)
    g_ref[...] = g
    nm_ref[...] = nm
    nv_ref[...] = nv
    d_ref[...] = -ADAM_LR * ((nm * c1) / (jnp.sqrt(nv * c2) + ADAM_EPS) + ADAM_WD * w_ref[...])


def adam_layer(parts, w, m, v, l, prev, name):
    L, R, C = w.shape
    br = next((b for b in (256, 128) if R > b and R % b == 0), R)

    def body(p_ref, w_ref, m_ref, v_ref, *rest):
        _adam_body(N_DEV, p_ref, w_ref, m_ref, v_ref, *rest[-4:])

    bs = pl.BlockSpec((None, br, C), lambda i: (l, i, 0))
    out = jax.ShapeDtypeStruct((L, R, C), F32)
    extra = {} if prev is None else dict(input_output_aliases={4 + q: q for q in range(4)})
    return _call(
        body, name=name, out_shape=(out, out, out, out), grid=(R // br,),
        in_specs=[pl.BlockSpec((N_DEV, br, C), lambda i: (0, i, 0)), bs, bs, bs] + ([] if prev is None else [ANY] * 4),
        out_specs=(bs, bs, bs, bs), compiler_params=_params(("parallel",)), **extra,
    )(parts, w, m, v, *(() if prev is None else prev))


def sum_parts(parts, name):
    n, R, C = parts.shape

    def body(p_ref, o_ref):
        g = p_ref[0]
        for p in range(1, n):
            g = g + p_ref[p]
        o_ref[...] = g

    return _call(body, name=name, out_shape=jax.ShapeDtypeStruct((R, C), parts.dtype),
                 compiler_params=pltpu.CompilerParams(vmem_limit_bytes=VMEM_LIMIT))(parts)


def _pack_rows(arrs):
    rows = []
    for a in arrs:
        f = a.reshape(-1)
        r = -(-f.shape[0] // 1024)
        r8 = -(-r // 8) * 8
        rows.append(jnp.pad(f, (0, r8 * 1024 - f.shape[0])).reshape(r8, 1024))
    packed = jnp.concatenate(rows, 0)
    return jnp.pad(packed, ((0, -packed.shape[0] % 128), (0, 0)))


def _unpack_rows(packed, shapes):
    out, o = [], 0
    for s in shapes:
        nel = math.prod(s)
        r8 = -(-(-(-nel // 1024)) // 8) * 8
        out.append(packed[o:o + r8].reshape(-1)[:nel].reshape(s))
        o += r8
    return out


def kernel(x, norm_mix_g, norm_ffn_g, norm_final_g, rg_w_in, rg_conv_w, rg_conv_b, rg_w_a, rg_b_a, rg_w_x, rg_b_x, rg_lambda, rg_w_out, s5_w_in, s5_a_re, s5_a_im, s5_log_dt, s5_b_re, s5_b_im, s5_c_re, s5_c_im, s5_d, s5_w_glu, s5_w_out, ffn_w_up, ffn_conv_w, ffn_conv_b, ffn_w_down, loss_target, m_norm_mix_g, m_norm_ffn_g, m_norm_final_g, m_rg_w_in, m_rg_conv_w, m_rg_conv_b, m_rg_w_a, m_rg_b_a, m_rg_w_x, m_rg_b_x, m_rg_lambda, m_rg_w_out, m_s5_w_in, m_s5_a_re, m_s5_a_im, m_s5_log_dt, m_s5_b_re, m_s5_b_im, m_s5_c_re, m_s5_c_im, m_s5_d, m_s5_w_glu, m_s5_w_out, m_ffn_w_up, m_ffn_conv_w, m_ffn_conv_b, m_ffn_w_down, v_norm_mix_g, v_norm_ffn_g, v_norm_final_g, v_rg_w_in, v_rg_conv_w, v_rg_conv_b, v_rg_w_a, v_rg_b_a, v_rg_w_x, v_rg_b_x, v_rg_lambda, v_rg_w_out, v_s5_w_in, v_s5_a_re, v_s5_a_im, v_s5_log_dt, v_s5_b_re, v_s5_b_im, v_s5_c_re, v_s5_c_im, v_s5_d, v_s5_w_glu, v_s5_w_out, v_ffn_w_up, v_ffn_conv_w, v_ffn_conv_b, v_ffn_w_down):
    names = ('norm_mix_g', 'norm_ffn_g', 'norm_final_g', 'rg_w_in', 'rg_conv_w', 'rg_conv_b', 'rg_w_a', 'rg_b_a',
             'rg_w_x', 'rg_b_x', 'rg_lambda', 'rg_w_out', 's5_w_in', 's5_a_re', 's5_a_im', 's5_log_dt', 's5_b_re',
             's5_b_im', 's5_c_re', 's5_c_im', 's5_d', 's5_w_glu', 's5_w_out', 'ffn_w_up', 'ffn_conv_w', 'ffn_conv_b',
             'ffn_w_down')
    loc = locals()
    Wl = {k: loc[k] for k in names}
    Ml = {k: loc['m_' + k] for k in names}
    Vl = {k: loc['v_' + k] for k in names}

    depth = norm_mix_g.shape[0]
    mixer_keys = lambda i: ([('rg_w_in', i // 2), ('rg_w_out', i // 2)] if i % 2 == 0 else
                            [('s5_w_in', i // 2), ('s5_w_glu', i // 2), ('s5_w_out', i // 2)])
    ffn_keys = lambda i: [('ffn_w_up', i), ('ffn_w_down', i)]
    shards = {k: Wl[k].astype(BF16) for k in BIG}
    W = {k: Wl[k] for k in REPLICATED}
    W.update({k: [None] * Wl[k].shape[0] for k in BIG})
    parts = {}

    def gather_of(keys, small=False):
        items = []
        for k, l in keys:
            _, r, c = shards[k].shape
            items.append((shards[k], l, BIG[k], (r * N_DEV, c) if BIG[k] == 0 else (r, c * N_DEV)))
        if small:
            items += [(Wl[k], None, Wl[k].ndim - 1, Wl[k].shape[:-1] + (Wl[k].shape[-1] * N_DEV,)) for k in SMALL_SHARDED]

        def deliver(outs):
            for (k, l), arr in zip(keys, outs):
                W[k][l] = arr
            if small:
                W.update(zip(SMALL_SHARDED, outs[len(keys):]))

        return gather_plan(items, deliver)

    def exchange_of(keys, G, extra=()):
        items = [(G[k][l], BIG[k]) for k, l in keys] + [(arr, axis) for _, arr, axis in extra]
        return exchange_plan(items, lambda outs: parts.update(zip(list(keys) + [e[0] for e in extra], outs)))

    class Sched:
        @staticmethod
        def comm(host, G=None):
            kind, _, i = host.rpartition("_")
            i = int(i)
            if host == "rg_fwd_0":
                return gather_of(ffn_keys(0))
            if kind == "ffn_up" and i + 1 < depth:
                return gather_of(ffn_keys(i + 1)[:1])
            if kind == "ffn_fwd" and i + 1 < depth:
                return gather_of(mixer_keys(i + 1) + ffn_keys(i + 1)[1:])
            if kind == "ffn_bwd_act" and i + 1 < depth:
                return exchange_of(mixer_keys(i + 1) + ffn_keys(i + 1), G)
            if host == "rg_bwd_0":
                return exchange_of(ffn_keys(0), G)
            return None

    run_comm(gather_of(mixer_keys(0), small=True), "gather_first")

    se8, gx, G = local_step(x[0], loss_target[0], W, Sched)
    loss = lax.psum(0.5 * jnp.sum(se8) / x.shape[-1], ("x", "y", "c"))

    rep_shapes = [Wl[k].shape for k in REPLICATED]
    gp = _pack_rows([G[k].astype(F32) for k in REPLICATED])
    extra = [(k, G[k], G[k].ndim - 1) for k in SMALL_SHARDED] + [('replicated', gp, 0)]
    run_comm(exchange_of(mixer_keys(0), G, extra), "exchange_last")
    out_g, out_d, out_m, out_v = {}, {}, {}, {}
    for k in BIG:
        res = None
        for l in range(Wl[k].shape[0]):
            res = adam_layer(parts[(k, l)], Wl[k], Ml[k], Vl[k], l, res, f"adam_{k}_{l}")
        out_g[k], out_d[k], out_m[k], out_v[k] = res
    for k in SMALL_SHARDED:
        shp = Wl[k].shape
        r2 = (math.prod(shp[:-1]), shp[-1])
        res = adam_update(parts[k].reshape((N_DEV,) + r2), Wl[k].reshape(r2), Ml[k].reshape(r2), Vl[k].reshape(r2),
                          f"adam_{k}")
        out_g[k], out_d[k], out_m[k], out_v[k] = [t.reshape(shp) for t in res]
    rsum = sum_parts(parts['replicated'], "sum_replicated")
    run_comm(gather_plan([(rsum, None, 0, gp.shape)], lambda outs: parts.update(rep_full=outs[0])), "gather_small_grads")
    res = adam_update(parts['rep_full'][None], _pack_rows([Wl[k] for k in REPLICATED]),
                      _pack_rows([Ml[k] for k in REPLICATED]), _pack_rows([Vl[k] for k in REPLICATED]), "adam_replicated")
    for dst, packed in zip((out_g, out_d, out_m, out_v), res):
        for k, t in zip(REPLICATED, _unpack_rows(packed, rep_shapes)):
            dst[k] = t
    return (loss, gx[None], *[out_g[k] for k in names], *[out_d[k] for k in names], *[out_m[k] for k in names],
            *[out_v[k] for k in names])
```

```python
import functools
import math

import jax
import jax.numpy as jnp
from jax import lax
from jax.experimental import pallas as pl
from jax.experimental.pallas import tpu as pltpu

F32 = jnp.float32
BF16 = jnp.bfloat16
MXU_DTYPE = jnp.bfloat16

NORM_EPS = 1e-6
RG_C = 8.0
RG_HEADS = 8
RG_BW = 128
S5_G = 64
S5_GC = 16
S5_P = 64
S5_SB = 8
S5_NSB = S5_G // S5_SB
S5_NS = 2 * S5_G * S5_P
ADAM_LR = 0.001
ADAM_B1 = 0.9
ADAM_B2 = 0.999
ADAM_EPS = 1e-08
ADAM_WD = 0.01
ADAM_STEP = 10
N_DEV = 8
VMEM_LIMIT = 56 * 1024 * 1024


def _call(body, **kw):
    return pl.pallas_call(body, **kw)


def _params(sem, vmem=VMEM_LIMIT):
    return pltpu.CompilerParams(dimension_semantics=sem, vmem_limit_bytes=vmem)


MESH = pl.DeviceIdType.MESH
ANY = pl.BlockSpec(memory_space=pl.ANY)


class Comm:
    def __init__(self, operands, out_shape, scratch, start, middle, finish, deliver):
        self.operands, self.out_shape, self.scratch = list(operands), list(out_shape), list(scratch)
        self.start, self.middle, self.finish, self.deliver = start, middle, finish, deliver


def run_comm(comm, name):
    ci, co = len(comm.operands), len(comm.out_shape)

    def body(*refs):
        parts = (refs[:ci], refs[ci:ci + co], refs[ci + co:])
        comm.start(*parts)
        comm.middle(*parts)
        comm.finish(*parts)

    comm.deliver(_call(body, name=name, out_shape=tuple(comm.out_shape), in_specs=[ANY] * ci,
                       out_specs=tuple([ANY] * co), scratch_shapes=comm.scratch)(*comm.operands))


def _hosted(body, comm, args, *, name, out_shape, grid, in_specs, out_specs, scratch_shapes, compiler_params):
    if comm is None:
        return _call(body, name=name, out_shape=tuple(out_shape), grid=grid, in_specs=in_specs,
                     out_specs=tuple(out_specs), scratch_shapes=scratch_shapes, compiler_params=compiler_params)(*args)
    n_in, n_out, n_sc = len(in_specs), len(out_shape), len(scratch_shapes)
    ci, co = len(comm.operands), len(comm.out_shape)
    nsteps = math.prod(grid)
    mid = (2 * nsteps) // 3

    def wrapped(*refs):
        ins, refs = refs[:n_in], refs[n_in:]
        cins, refs = refs[:ci], refs[ci:]
        outs, refs = refs[:n_out], refs[n_out:]
        couts, refs = refs[:co], refs[co:]
        sc, csc = refs[:n_sc], refs[n_sc:]
        step = pl.program_id(0)
        for d in range(1, len(grid)):
            step = step * grid[d] + pl.program_id(d)

        @pl.when(step == 0)
        def _():
            comm.start(cins, couts, csc)

        body(*ins, *outs, *sc)

        @pl.when(step == mid)
        def _():
            comm.middle(cins, couts, csc)

        @pl.when(step == nsteps - 1)
        def _():
            comm.finish(cins, couts, csc)

    res = _call(wrapped, name=name, out_shape=(*out_shape, *comm.out_shape), grid=grid,
                in_specs=[*in_specs, *[ANY] * ci], out_specs=(*out_specs, *[ANY] * co),
                scratch_shapes=[*scratch_shapes, *comm.scratch],
                compiler_params=_params(("arbitrary",) * len(grid)))(*args, *comm.operands)
    comm.deliver(res[n_out:])
    return res[:n_out]


_GELU_C = 0.7978845608028654
_GELU_A = 0.044715


def _gelu(x):
    return 0.5 * x * (1.0 + jnp.tanh(_GELU_C * (x + _GELU_A * x * x * x)))


def _gelu_and_grad(x):
    x2 = x * x
    t = jnp.tanh(_GELU_C * (x + _GELU_A * x2 * x))
    g = 0.5 * x * (1.0 + t)
    dg = 0.5 * (1.0 + t) + 0.5 * x * (1.0 - t * t) * _GELU_C * (1.0 + 3.0 * _GELU_A * x2)
    return g, dg


def _sigmoid(x):
    return 1.0 / (1.0 + jnp.exp(-x))


def _neg_expm1(x):
    series = -x * (1.0 + x * (0.5 + x * (1.0 / 6.0 + x * (1.0 / 24.0 + x * (1.0 / 120.0 + x * (1.0 / 720.0))))))
    return jnp.where(x > -0.1, series, 1.0 - jnp.exp(x))


def _rowsum8(x):
    r, c = x.shape
    return x.reshape(r // 8, 8, c).sum(axis=0)


def _dot(a, b):
    return jnp.dot(a.astype(MXU_DTYPE), b.astype(MXU_DTYPE), preferred_element_type=F32)


def _dot_nt(a, b):
    return lax.dot_general(a.astype(MXU_DTYPE), b.astype(MXU_DTYPE), (((1,), (1,)), ((), ())),
                           preferred_element_type=F32)


def _dot_tn(a, b):
    return lax.dot_general(a.astype(MXU_DTYPE), b.astype(MXU_DTYPE), (((0,), (0,)), ((), ())),
                           preferred_element_type=F32)


def _shift_down(x, s, fills, row):
    y = pltpu.roll(x, s, 0)
    for t in range(s):
        y = jnp.where(row == t, fills[s - 1 - t], y)
    return y


def _shift_up(x, s, fills, row):
    n = x.shape[0]
    y = pltpu.roll(x, n - s, 0)
    for t in range(s):
        y = jnp.where(row == n - s + t, fills[t], y)
    return y


def _rms(x):
    r = lax.rsqrt(jnp.mean(x * x, axis=-1, keepdims=True) + NORM_EPS)
    return r, x * r


def _rms_bwd(dhn, xhat, r, g):
    dz = dhn * g
    return r * (dz - xhat * jnp.mean(dz * xhat, axis=-1, keepdims=True))


def norm_matmul(h, g, w, out_dtype, tm, tn, name, comm=None):
    T, D = h.shape
    N = w.shape[1]
    tn = N if tn is None else tn

    def body(h_ref, g_ref, w_ref, o_ref, hn_ref, hn_s):
        @pl.when(pl.program_id(1) == 0)
        def _():
            _, xhat = _rms(h_ref[...])
            v = (xhat * g_ref[...]).astype(MXU_DTYPE)
            hn_s[...] = v
            hn_ref[...] = v

        o_ref[...] = jnp.dot(hn_s[...], w_ref[...].astype(MXU_DTYPE), preferred_element_type=F32).astype(o_ref.dtype)

    return _hosted(
        body, comm, (h, g, w), name=name,
        out_shape=(jax.ShapeDtypeStruct((T, N), out_dtype), jax.ShapeDtypeStruct((T, D), MXU_DTYPE)),
        grid=(T // tm, N // tn),
        in_specs=[pl.BlockSpec((tm, D), lambda i, j: (i, 0)), pl.BlockSpec((1, D), lambda i, j: (0, 0)),
                  pl.BlockSpec((D, tn), lambda i, j: (0, j))],
        out_specs=(pl.BlockSpec((tm, tn), lambda i, j: (i, j)), pl.BlockSpec((tm, D), lambda i, j: (i, 0))),
        scratch_shapes=[pltpu.VMEM((tm, D), MXU_DTYPE)],
        compiler_params=_params(("parallel", "arbitrary")),
    )


def matmul_tn(a, b, bm, bn, bt, name, out_dtype=BF16):
    T, M = a.shape
    N = b.shape[1]
    nk = T // bt

    def body(a_ref, b_ref, o_ref, acc):
        k = pl.program_id(2)

        @pl.when(k == 0)
        def _():
            acc[...] = jnp.zeros_like(acc)

        acc[...] += _dot_tn(a_ref[...], b_ref[...])

        @pl.when(k == nk - 1)
        def _():
            o_ref[...] = acc[...].astype(o_ref.dtype)

    return _call(
        body, name=name,
        out_shape=jax.ShapeDtypeStruct((M, N), out_dtype),
        grid=(M // bm, N // bn, nk),
        in_specs=[pl.BlockSpec((bt, bm), lambda i, j, k: (k, i)), pl.BlockSpec((bt, bn), lambda i, j, k: (k, j))],
        out_specs=pl.BlockSpec((bm, bn), lambda i, j, k: (i, j)),
        scratch_shapes=[pltpu.VMEM((bm, bn), F32)],
        compiler_params=_params(("parallel", "parallel", "arbitrary")),
    )(a, b)


def dx_norm_bwd(dz, w, h, g, dres, tm, name, conv_w=None, conv_x=None, chunk=1024):
    T, N = dz.shape
    D = w.shape[0]
    nt = T // tm
    has_conv = conv_w is not None
    kw = conv_w.shape[0] if has_conv else 0

    def body(*refs):
        if has_conv:
            dz_ref, cw_ref, x_ref, w_ref, h_ref, g_ref, dres_ref, dh_ref, dg_ref, dzp_ref, dcw_ref, carry, zs = refs
        else:
            dz_ref, w_ref, h_ref, g_ref, dres_ref, dh_ref, dg_ref = refs
        i = pl.program_id(0)

        @pl.when(i == 0)
        def _():
            dg_ref[...] = jnp.zeros_like(dg_ref)
            if has_conv:
                carry[...] = jnp.zeros_like(carry)
                dcw_ref[...] = jnp.zeros_like(dcw_ref)

        if has_conv:
            row = lax.broadcasted_iota(jnp.int32, (tm, 1), 0)
            for c0 in range(0, N, chunk):
                sl = slice(c0, c0 + chunk)
                d0 = dz_ref[:, sl].astype(F32)
                xv = x_ref[:, sl].astype(F32)
                fills = [carry[t:t + 1, sl] for t in range(kw - 1)]
                acc = cw_ref[kw - 1:kw, sl] * d0
                dcw_ref[kw - 1, :, sl] += _rowsum8(d0 * xv)
                for s in range(1, kw):
                    ds = _shift_up(d0, s, fills, row)
                    acc = acc + cw_ref[kw - 1 - s:kw - s, sl] * ds
                    dcw_ref[kw - 1 - s, :, sl] += _rowsum8(ds * xv)
                zb = acc.astype(MXU_DTYPE)
                zs[:, sl] = zb
                dzp_ref[:, sl] = zb
            carry[...] = dz_ref[0:16, :].astype(F32)
            z = zs[...]
        else:
            z = dz_ref[...]
        dhn = _dot_nt(z, w_ref[...])
        r, xhat = _rms(h_ref[...])
        dg_ref[...] += _rowsum8(dhn * xhat)
        dh_ref[...] = dres_ref[...] + _rms_bwd(dhn, xhat, r, g_ref[...])

    if has_conv:
        ti = lambda i: nt - 1 - i
    else:
        ti = lambda i: i
    row_spec = lambda c: pl.BlockSpec((tm, c), lambda i: (ti(i), 0))
    full = lambda a: pl.BlockSpec(a.shape, lambda i: (0,) * a.ndim)
    in_specs = [row_spec(N)] + ([full(conv_w), row_spec(N)] if has_conv else []) + [full(w), row_spec(D), full(g), row_spec(D)]
    out_shape = [jax.ShapeDtypeStruct((T, D), F32), jax.ShapeDtypeStruct((8, D), F32)]
    out_specs = [row_spec(D), pl.BlockSpec((8, D), lambda i: (0, 0))]
    scratch = []
    if has_conv:
        out_shape += [jax.ShapeDtypeStruct((T, N), MXU_DTYPE), jax.ShapeDtypeStruct((kw, 8, N), F32)]
        out_specs += [row_spec(N), pl.BlockSpec((kw, 8, N), lambda i: (0, 0, 0))]
        scratch = [pltpu.VMEM((16, N), F32), pltpu.VMEM((tm, N), MXU_DTYPE)]
    args = [dz] + ([conv_w, conv_x] if has_conv else []) + [w, h, g, dres]
    return _call(
        body, name=name, out_shape=tuple(out_shape), grid=(nt,), in_specs=in_specs, out_specs=tuple(out_specs),
        scratch_shapes=scratch, compiler_params=_params(("arbitrary",)),
    )(*args)


def _ffn_conv_chunk(up_ref, cw_ref, cb_ref, carry, row, sl):
    x = up_ref[:, sl].astype(F32)
    fills = [carry[15:16, sl], carry[14:15, sl]]
    x1 = _shift_down(x, 1, fills, row)
    x2 = _shift_down(x, 2, fills, row)
    return cb_ref[:, sl] + cw_ref[2:3, sl] * x + cw_ref[1:2, sl] * x1 + cw_ref[0:1, sl] * x2


def ffn_fwd(up, h, conv_w, conv_b, w_down, tm, name, chunk=512, comm=None):
    T, C = up.shape
    F = C // 2
    D = h.shape[1]

    def body(up_ref, h_ref, cw_ref, cb_ref, wd_ref, o_ref, av_ref, carry, act_s):
        @pl.when(pl.program_id(0) == 0)
        def _():
            carry[...] = jnp.zeros_like(carry)

        row = lax.broadcasted_iota(jnp.int32, (tm, 1), 0)
        for c0 in range(0, F, chunk):
            sa, sv = slice(c0, c0 + chunk), slice(F + c0, F + c0 + chunk)
            a = _ffn_conv_chunk(up_ref, cw_ref, cb_ref, carry, row, sa)
            v = _ffn_conv_chunk(up_ref, cw_ref, cb_ref, carry, row, sv)
            av_ref[:, sa] = a.astype(MXU_DTYPE)
            av_ref[:, sv] = v.astype(MXU_DTYPE)
            act_s[:, sa] = (_gelu(a) * v).astype(MXU_DTYPE)
        carry[...] = up_ref[tm - 16:tm, :].astype(F32)
        o_ref[...] = h_ref[...] + jnp.dot(act_s[...], wd_ref[...].astype(MXU_DTYPE), preferred_element_type=F32)

    full = lambda a: pl.BlockSpec(a.shape, lambda i: (0,) * a.ndim)
    return _hosted(
        body, comm, (up, h, conv_w, conv_b, w_down), name=name,
        out_shape=(jax.ShapeDtypeStruct((T, D), F32), jax.ShapeDtypeStruct((T, C), MXU_DTYPE)),
        grid=(T // tm,),
        in_specs=[pl.BlockSpec((tm, C), lambda i: (i, 0)), pl.BlockSpec((tm, D), lambda i: (i, 0)),
                  full(conv_w), full(conv_b), full(w_down)],
        out_specs=(pl.BlockSpec((tm, D), lambda i: (i, 0)), pl.BlockSpec((tm, C), lambda i: (i, 0))),
        scratch_shapes=[pltpu.VMEM((16, C), F32), pltpu.VMEM((tm, F), MXU_DTYPE)],
        compiler_params=_params(("arbitrary",)),
    )


def ffn_bwd_act(dout, av, w_down, tm, name, chunk=512, comm=None):
    T, C = av.shape
    F = C // 2
    D = dout.shape[1]

    def body(do_ref, av_ref, wd_ref, dup_ref, act_ref, dcb_ref, dact_s):
        @pl.when(pl.program_id(0) == 0)
        def _():
            dcb_ref[...] = jnp.zeros_like(dcb_ref)

        dact_s[...] = _dot_nt(do_ref[...], wd_ref[...])
        for c0 in range(0, F, chunk):
            sa, sv = slice(c0, c0 + chunk), slice(F + c0, F + c0 + chunk)
            v = av_ref[:, sv].astype(F32)
            ga, dga = _gelu_and_grad(av_ref[:, sa].astype(F32))
            act_ref[:, sa] = (ga * v).astype(MXU_DTYPE)
            dact = dact_s[:, sa]
            da = dact * v * dga
            dv = dact * ga
            dup_ref[:, sa] = da.astype(MXU_DTYPE)
            dup_ref[:, sv] = dv.astype(MXU_DTYPE)
            dcb_ref[:, sa] += _rowsum8(da)
            dcb_ref[:, sv] += _rowsum8(dv)

    full = lambda a: pl.BlockSpec(a.shape, lambda i: (0,) * a.ndim)
    return _hosted(
        body, comm, (dout, av, w_down), name=name,
        out_shape=(jax.ShapeDtypeStruct((T, C), MXU_DTYPE), jax.ShapeDtypeStruct((T, F), MXU_DTYPE),
                   jax.ShapeDtypeStruct((8, C), F32)),
        grid=(T // tm,),
        in_specs=[pl.BlockSpec((tm, D), lambda i: (i, 0)), pl.BlockSpec((tm, C), lambda i: (i, 0)), full(w_down)],
        out_specs=(pl.BlockSpec((tm, C), lambda i: (i, 0)), pl.BlockSpec((tm, F), lambda i: (i, 0)),
                   pl.BlockSpec((8, C), lambda i: (0, 0))),
        scratch_shapes=[pltpu.VMEM((tm, F), F32)],
        compiler_params=_params(("arbitrary",)),
    )


def _rg_gates(xr, wa_ref, ba_ref, wx_ref, bx_ref, sp_ref):
    xb = xr.astype(MXU_DTYPE)
    pa, px = [], []
    for hd in range(RG_HEADS):
        sl = slice(hd * RG_BW, (hd + 1) * RG_BW)
        pa.append(jnp.dot(xb[:, sl], wa_ref[hd].astype(MXU_DTYPE), preferred_element_type=F32))
        px.append(jnp.dot(xb[:, sl], wx_ref[hd].astype(MXU_DTYPE), preferred_element_type=F32))
    r = _sigmoid(jnp.concatenate(pa, axis=1) + ba_ref[...])
    ig = _sigmoid(jnp.concatenate(px, axis=1) + bx_ref[...])
    la = -RG_C * r * sp_ref[...]
    a = jnp.exp(la)
    mult = jnp.sqrt(_neg_expm1(2.0 * la))
    return xb, r, ig, a, mult


def _rg_conv(x, fills, cw_ref, cb_ref, row):
    x1 = _shift_down(x, 1, fills, row)
    x2 = _shift_down(x, 2, fills, row)
    x3 = _shift_down(x, 3, fills, row)
    xr = cb_ref[...] + cw_ref[3:4, :] * x + cw_ref[2:3, :] * x1 + cw_ref[1:2, :] * x2 + cw_ref[0:1, :] * x3
    return xr, (x3, x2, x1, x)


def rg_fwd(xg, h, conv_w, conv_b, w_a, b_a, w_x, b_x, sp, w_out, tm, name, comm=None):
    T, D2 = xg.shape
    D = D2 // 2
    nb = tm // 8

    def body(xg_ref, h_ref, cw_ref, cb_ref, wa_ref, ba_ref, wx_ref, bx_ref, sp_ref, wo_ref, o_ref, hs_ref,
             xcarry, hcarry, a_s, b_s):
        @pl.when(pl.program_id(0) == 0)
        def _():
            xcarry[...] = jnp.zeros_like(xcarry)
            hcarry[...] = jnp.zeros_like(hcarry)

        row = lax.broadcasted_iota(jnp.int32, (tm, 1), 0)
        x = xg_ref[:, 0:D]
        fills = [xcarry[7:8, :], xcarry[6:7, :], xcarry[5:6, :]]
        xr, _ = _rg_conv(x, fills, cw_ref, cb_ref, row)
        xcarry[...] = xg_ref[tm - 8:tm, 0:D]
        _, r, ig, a, mult = _rg_gates(xr, wa_ref, ba_ref, wx_ref, bx_ref, sp_ref)
        a_s[...] = a
        b_s[...] = mult * ig * xr
        row8 = lax.broadcasted_iota(jnp.int32, (8, 1), 0)

        def blk(j, c):
            o = pl.multiple_of(j * 8, 8)
            A = a_s[pl.ds(o, 8), :]
            H = b_s[pl.ds(o, 8), :]
            for s in (1, 2, 4):
                m = row8 >= s
                H = H + A * jnp.where(m, pltpu.roll(H, s, 0), 0.0)
                A = A * jnp.where(m, pltpu.roll(A, s, 0), 1.0)
            H = H + A * c
            hs_ref[pl.ds(o, 8), :] = H
            return H[7:8, :]

        c = lax.fori_loop(0, nb, blk, hcarry[0:1, :])
        hcarry[0:1, :] = c
        y = hs_ref[...] * _gelu(xg_ref[:, D:D2])
        o_ref[...] = h_ref[...] + _dot(y, wo_ref[...])

    full = lambda a: pl.BlockSpec(a.shape, lambda i: (0,) * a.ndim)
    args = (xg, h, conv_w, conv_b, w_a, b_a, w_x, b_x, sp, w_out)
    return _hosted(
        body, comm, args, name=name,
        out_shape=(jax.ShapeDtypeStruct((T, D), F32), jax.ShapeDtypeStruct((T, D), F32)),
        grid=(T // tm,),
        in_specs=[pl.BlockSpec((tm, D2), lambda i: (i, 0)), pl.BlockSpec((tm, D), lambda i: (i, 0))]
        + [full(a) for a in args[2:]],
        out_specs=(pl.BlockSpec((tm, D), lambda i: (i, 0)), pl.BlockSpec((tm, D), lambda i: (i, 0))),
        scratch_shapes=[pltpu.VMEM((8, D), F32), pltpu.VMEM((8, D), F32), pltpu.VMEM((tm, D), F32),
                        pltpu.VMEM((tm, D), F32)],
        compiler_params=_params(("arbitrary",)),
    )


def rg_bwd(dout, xg, hs, conv_w, conv_b, w_a, b_a, w_x, b_x, sp, w_out, tm, name, comm=None):
    T, D2 = xg.shape
    D = D2 // 2
    nt = T // tm
    nb = tm // 8
    kw = conv_w.shape[0]

    def body(do_ref, xg_ref, xh_ref, hs_ref, hh_ref, cw_ref, cb_ref, wa_ref, ba_ref, wx_ref, bx_ref, sp_ref, wo_ref,
             dxg_ref, y_ref, dwa_ref, dwx_ref, dba_ref, dbx_ref, dsp_ref, dcb_ref, dcw_ref,
             acarry, lcarry, dcarry, a_s, b_s, l_s):
        i = pl.program_id(0)
        first_tile = i == nt - 1

        @pl.when(i == 0)
        def _():
            for ref in (acarry, lcarry, dcarry, dwa_ref, dwx_ref, dba_ref, dbx_ref, dsp_ref, dcb_ref, dcw_ref):
                ref[...] = jnp.zeros_like(ref)

        row = lax.broadcasted_iota(jnp.int32, (tm, 1), 0)
        keep = jnp.where(first_tile, 0.0, 1.0)
        x = xg_ref[:, 0:D]
        gate = xg_ref[:, D:D2]
        xh = xh_ref[...] * keep
        fills = [xh[7:8, :], xh[6:7, :], xh[5:6, :]]
        xr, taps = _rg_conv(x, fills, cw_ref, cb_ref, row)
        xb, r, ig, a, mult = _rg_gates(xr, wa_ref, ba_ref, wx_ref, bx_ref, sp_ref)
        hs = hs_ref[...]
        hprev = _shift_down(hs, 1, [hh_ref[7:8, :] * keep], row)
        dy = _dot_nt(do_ref[...], wo_ref[...])
        gg, dgg = _gelu_and_grad(gate)
        y_ref[...] = (hs * gg).astype(MXU_DTYPE)
        dxg_ref[:, D:D2] = (dy * hs * dgg).astype(MXU_DTYPE)
        a_s[...] = _shift_up(a, 1, [acarry[0:1, :]], row)
        b_s[...] = dy * gg
        row8 = lax.broadcasted_iota(jnp.int32, (8, 1), 0)

        def blk(jj, c):
            o = pl.multiple_of((nb - 1 - jj) * 8, 8)
            A = a_s[pl.ds(o, 8), :]
            H = b_s[pl.ds(o, 8), :]
            for s in (1, 2, 4):
                m = row8 < 8 - s
                H = H + A * jnp.where(m, pltpu.roll(H, 8 - s, 0), 0.0)
                A = A * jnp.where(m, pltpu.roll(A, 8 - s, 0), 1.0)
            H = H + A * c
            l_s[pl.ds(o, 8), :] = H
            return H[0:1, :]

        c = lax.fori_loop(0, nb, blk, lcarry[0:1, :])
        lcarry[0:1, :] = c
        acarry[0:1, :] = a[0:1, :]
        lam = l_s[...]
        dla = lam * hprev * a - (lam * ig * xr) * (a * a) / mult
        dig = lam * mult * xr
        dxr = lam * mult * ig
        spv = sp_ref[...]
        dsp_ref[...] += _rowsum8(dla * (-RG_C) * r)
        dpa = (dla * (-RG_C) * spv) * r * (1.0 - r)
        dpx = dig * ig * (1.0 - ig)
        dba_ref[...] += _rowsum8(dpa)
        dbx_ref[...] += _rowsum8(dpx)
        dpab = dpa.astype(MXU_DTYPE)
        dpxb = dpx.astype(MXU_DTYPE)
        back = []
        for hd in range(RG_HEADS):
            sl = slice(hd * RG_BW, (hd + 1) * RG_BW)
            dwa_ref[hd] += _dot_tn(xb[:, sl], dpab[:, sl])
            dwx_ref[hd] += _dot_tn(xb[:, sl], dpxb[:, sl])
            back.append(_dot_nt(dpab[:, sl], wa_ref[hd]) + _dot_nt(dpxb[:, sl], wx_ref[hd]))
        dxr = dxr + jnp.concatenate(back, axis=1)
        nfills = [dcarry[0:1, :], dcarry[1:2, :], dcarry[2:3, :]]
        dxp = cw_ref[kw - 1:kw, :] * dxr
        for s in range(1, kw):
            dxp = dxp + cw_ref[kw - 1 - s:kw - s, :] * _shift_up(dxr, s, nfills, row)
        dcarry[...] = dxr[0:8, :]
        dxg_ref[:, 0:D] = dxp.astype(MXU_DTYPE)
        dcb_ref[...] += _rowsum8(dxr)
        for k in range(kw):
            dcw_ref[k] += _rowsum8(dxr * taps[k])

    rt = lambda i: nt - 1 - i
    halo = lambda i: jnp.maximum((nt - 1 - i) * (tm // 8) - 1, 0)
    full = lambda a: pl.BlockSpec(a.shape, lambda i: (0,) * a.ndim)
    params = (conv_w, conv_b, w_a, b_a, w_x, b_x, sp, w_out)
    acc = lambda shape: pl.BlockSpec(shape, lambda i: (0,) * len(shape))
    return _hosted(
        body, comm, (dout, xg, xg, hs, hs, *params), name=name,
        out_shape=(jax.ShapeDtypeStruct((T, D2), MXU_DTYPE), jax.ShapeDtypeStruct((T, D), MXU_DTYPE),
                   jax.ShapeDtypeStruct((RG_HEADS, RG_BW, RG_BW), F32), jax.ShapeDtypeStruct((RG_HEADS, RG_BW, RG_BW), F32),
                   jax.ShapeDtypeStruct((8, D), F32), jax.ShapeDtypeStruct((8, D), F32), jax.ShapeDtypeStruct((8, D), F32),
                   jax.ShapeDtypeStruct((8, D), F32), jax.ShapeDtypeStruct((kw, 8, D), F32)),
        grid=(nt,),
        in_specs=[pl.BlockSpec((tm, D), lambda i: (rt(i), 0)), pl.BlockSpec((tm, D2), lambda i: (rt(i), 0)),
                  pl.BlockSpec((8, D), lambda i: (halo(i), 0)), pl.BlockSpec((tm, D), lambda i: (rt(i), 0)),
                  pl.BlockSpec((8, D), lambda i: (halo(i), 0))] + [full(a) for a in params],
        out_specs=(pl.BlockSpec((tm, D2), lambda i: (rt(i), 0)), pl.BlockSpec((tm, D), lambda i: (rt(i), 0)),
                   acc((RG_HEADS, RG_BW, RG_BW)), acc((RG_HEADS, RG_BW, RG_BW)), acc((8, D)), acc((8, D)), acc((8, D)),
                   acc((8, D)), acc((kw, 8, D))),
        scratch_shapes=[pltpu.VMEM((8, D), F32), pltpu.VMEM((8, D), F32), pltpu.VMEM((8, D), F32),
                        pltpu.VMEM((tm, D), F32), pltpu.VMEM((tm, D), F32), pltpu.VMEM((tm, D), F32)],
        compiler_params=_params(("arbitrary",)),
    )


_SBW = 2 * S5_SB * S5_P
_SBH = S5_SB * S5_P
_SBU = S5_SB * S5_GC


def _regroup(x, seg_len):
    n, c = x.shape
    return jnp.swapaxes(x.reshape(8, seg_len, c), 0, 1).reshape(n, c)


def _ungroup(x, seg_len):
    n, c = x.shape
    return jnp.swapaxes(x.reshape(seg_len, 8, c), 0, 1).reshape(n, c)


def _s5_lanes(sb):
    return (slice(sb * _SBW, sb * _SBW + _SBH), slice(sb * _SBW + _SBH, (sb + 1) * _SBW),
            slice(sb * _SBH, (sb + 1) * _SBH))


def _s5_scan_fwd(S, carry, cin, ap_r, ap_i, aq_r, aq_i, seg_len):
    row8 = lax.broadcasted_iota(jnp.int32, (8, 1), 0)
    for sb in range(S5_NSB):
        lr, li, la = _s5_lanes(sb)
        ar, ai = ap_r[0:1, la], ap_i[0:1, la]
        hr, hi = S[0:8, lr], S[0:8, li]
        for i in range(1, seg_len):
            blk = slice(i * 8, (i + 1) * 8)
            hr, hi = ar * hr - ai * hi + S[blk, lr], ar * hi + ai * hr + S[blk, li]
            S[blk, lr] = hr
            S[blk, li] = hi
        for s, idx in ((1, 0), (2, 1), (4, 3)):
            qr, qi = aq_r[idx:idx + 1, la], aq_i[idx:idx + 1, la]
            m = row8 >= s
            sr = jnp.where(m, pltpu.roll(hr, s, 0), 0.0)
            si = jnp.where(m, pltpu.roll(hi, s, 0), 0.0)
            hr, hi = hr + qr * sr - qi * si, hi + qr * si + qi * sr
        cr, ci = carry[0:1, lr], carry[0:1, li]
        pr, pi = aq_r[:, la], aq_i[:, la]
        hr, hi = hr + pr * cr - pi * ci, hi + pr * ci + pi * cr
        xr = jnp.where(row8 == 0, cr, pltpu.roll(hr, 1, 0))
        xi = jnp.where(row8 == 0, ci, pltpu.roll(hi, 1, 0))
        carry[0:1, lr] = hr[7:8, :]
        carry[0:1, li] = hi[7:8, :]
        if cin is not None:
            cin[:, lr] = xr
            cin[:, li] = xi
        for i in range(seg_len):
            blk = slice(i * 8, (i + 1) * 8)
            pr, pi = ap_r[i:i + 1, la], ap_i[i:i + 1, la]
            S[blk, lr] += pr * xr - pi * xi
            S[blk, li] += pr * xi + pi * xr


def s5_fwd(u, h, wb, wc, ap_r, ap_i, aq_r, aq_i, d, w_glu, w_out, tm, name):
    T, D = u.shape
    nt = T // tm
    seg = tm // 8

    def body(u_ref, h_ref, wb_ref, wc_ref, apr_ref, api_ref, aqr_ref, aqi_ref, d_ref, wg_ref, wo_ref,
             o_ref, yp_ref, gl_ref, st_ref, S, carry):
        @pl.when(pl.program_id(0) == 0)
        def _():
            carry[...] = jnp.zeros_like(carry)

        st_ref[...] = carry[...]
        uv = _regroup(u_ref[...], seg)
        ub = uv.astype(MXU_DTYPE)
        for sb in range(S5_NSB):
            S[:, sb * _SBW:(sb + 1) * _SBW] = jnp.dot(ub[:, sb * _SBU:(sb + 1) * _SBU], wb_ref[sb].astype(MXU_DTYPE),
                                                      preferred_element_type=F32)
        _s5_scan_fwd(S, carry, None, apr_ref, api_ref, aqr_ref, aqi_ref, seg)
        ys = [jnp.dot(S[:, sb * _SBW:(sb + 1) * _SBW].astype(MXU_DTYPE), wc_ref[sb].astype(MXU_DTYPE),
                      preferred_element_type=F32) for sb in range(S5_NSB)]
        yp = jnp.concatenate(ys, axis=1) + d_ref[...] * uv
        yp_ref[...] = _ungroup(yp, seg)
        gl = _dot(_gelu(yp), wg_ref[...])
        gl_ref[...] = _ungroup(gl, seg)
        out = gl[:, 0:D] * _sigmoid(gl[:, D:2 * D])
        o_ref[...] = h_ref[...] + _ungroup(_dot(out, wo_ref[...]), seg)

    full = lambda a: pl.BlockSpec(a.shape, lambda i: (0,) * a.ndim)
    args = (u, h, wb, wc, ap_r, ap_i, aq_r, aq_i, d, w_glu, w_out)
    return _call(
        body, name=name,
        out_shape=(jax.ShapeDtypeStruct((T, D), F32), jax.ShapeDtypeStruct((T, D), F32),
                   jax.ShapeDtypeStruct((T, 2 * D), F32), jax.ShapeDtypeStruct((nt, 8, S5_NS), F32)),
        grid=(nt,),
        in_specs=[pl.BlockSpec((tm, D), lambda i: (i, 0)), pl.BlockSpec((tm, D), lambda i: (i, 0))]
        + [full(a) for a in args[2:]],
        out_specs=(pl.BlockSpec((tm, D), lambda i: (i, 0)), pl.BlockSpec((tm, D), lambda i: (i, 0)),
                   pl.BlockSpec((tm, 2 * D), lambda i: (i, 0)), pl.BlockSpec((None, 8, S5_NS), lambda i: (i, 0, 0))),
        scratch_shapes=[pltpu.VMEM((tm, S5_NS), F32), pltpu.VMEM((8, S5_NS), F32)],
        compiler_params=_params(("arbitrary",)),
    )(*args)


def s5_bwd_glu(dout, gl, ypre, u, w_glu, w_out, tm, name):
    T, D = u.shape

    def body(do_ref, gl_ref, yp_ref, u_ref, wg_ref, wo_ref, dy_ref, oact_ref, dgl_ref, gact_ref, dd_ref):
        @pl.when(pl.program_id(0) == 0)
        def _():
            dd_ref[...] = jnp.zeros_like(dd_ref)

        gl1 = gl_ref[:, 0:D]
        sg = _sigmoid(gl_ref[:, D:2 * D])
        oact_ref[...] = (gl1 * sg).astype(MXU_DTYPE)
        dgo = _dot_nt(do_ref[...], wo_ref[...])
        d1 = (dgo * sg).astype(MXU_DTYPE)
        d2 = (dgo * gl1 * sg * (1.0 - sg)).astype(MXU_DTYPE)
        dgl_ref[:, 0:D] = d1
        dgl_ref[:, D:2 * D] = d2
        dg = _dot_nt(d1, wg_ref[:, 0:D]) + _dot_nt(d2, wg_ref[:, D:2 * D])
        g, gd = _gelu_and_grad(yp_ref[...])
        gact_ref[...] = g.astype(MXU_DTYPE)
        dy = dg * gd
        dy_ref[...] = dy
        dd_ref[...] += _rowsum8(dy * u_ref[...])

    full = lambda a: pl.BlockSpec(a.shape, lambda i: (0,) * a.ndim)
    rs = lambda c: pl.BlockSpec((tm, c), lambda i: (i, 0))
    return _call(
        body, name=name,
        out_shape=(jax.ShapeDtypeStruct((T, D), F32), jax.ShapeDtypeStruct((T, D), MXU_DTYPE),
                   jax.ShapeDtypeStruct((T, 2 * D), MXU_DTYPE), jax.ShapeDtypeStruct((T, D), MXU_DTYPE),
                   jax.ShapeDtypeStruct((8, D), F32)),
        grid=(T // tm,),
        in_specs=[rs(D), rs(2 * D), rs(D), rs(D), full(w_glu), full(w_out)],
        out_specs=(rs(D), rs(D), rs(2 * D), rs(D), pl.BlockSpec((8, D), lambda i: (0, 0))),
        compiler_params=_params(("arbitrary",)),
    )(dout, gl, ypre, u, w_glu, w_out)


def s5_bwd_ssm(dy, u, st, wb, wc, ap_r, ap_i, aq_r, aq_i, d, tm, name):
    T, D = u.shape
    nt = T // tm
    seg = tm // 8
    GP = S5_G * S5_P
    flip = lambda t: jnp.concatenate([t[k:k + 1] for k in reversed(range(t.shape[0]))], 0)
    tables = (ap_r, ap_i, aq_r, aq_i, flip(ap_r), flip(ap_i), flip(aq_r), flip(aq_i))

    def body(dy_ref, u_ref, st_ref, wb_ref, wc_ref, apr_ref, api_ref, aqr_ref, aqi_ref, aprr_ref, apir_ref,
             aqrr_ref, aqir_ref, d_ref, du_ref, dwb_ref, dwc_ref, dar_ref, dai_ref, S, L, carry, lcarry, cin):
        @pl.when(pl.program_id(0) == 0)
        def _():
            for ref in (lcarry, dwb_ref, dwc_ref, dar_ref, dai_ref):
                ref[...] = jnp.zeros_like(ref)

        uv = _regroup(u_ref[...], seg)
        ub = uv.astype(MXU_DTYPE)
        dyv = _regroup(dy_ref[...], seg)
        dyb = dyv.astype(MXU_DTYPE)
        carry[...] = st_ref[...]
        for sb in range(S5_NSB):
            S[:, sb * _SBW:(sb + 1) * _SBW] = jnp.dot(ub[:, sb * _SBU:(sb + 1) * _SBU], wb_ref[sb].astype(MXU_DTYPE),
                                                      preferred_element_type=F32)
            L[:, sb * _SBW:(sb + 1) * _SBW] = _dot_nt(dyb[:, sb * _SBU:(sb + 1) * _SBU], wc_ref[sb])
        _s5_scan_fwd(S, carry, cin, apr_ref, api_ref, aqr_ref, aqi_ref, seg)
        row8 = lax.broadcasted_iota(jnp.int32, (8, 1), 0)
        for sb in range(S5_NSB):
            lr, li, la = _s5_lanes(sb)
            ar, ai = apr_ref[0:1, la], api_ref[0:1, la]
            last = slice((seg - 1) * 8, seg * 8)
            gr, gi = L[last, lr], L[last, li]
            for i in reversed(range(seg - 1)):
                blk = slice(i * 8, (i + 1) * 8)
                gr, gi = ar * gr + ai * gi + L[blk, lr], ar * gi - ai * gr + L[blk, li]
                L[blk, lr] = gr
                L[blk, li] = gi
            for s, idx in ((1, 0), (2, 1), (4, 3)):
                qr, qi = aqr_ref[idx:idx + 1, la], aqi_ref[idx:idx + 1, la]
                m = row8 < 8 - s
                sr = jnp.where(m, pltpu.roll(gr, 8 - s, 0), 0.0)
                si = jnp.where(m, pltpu.roll(gi, 8 - s, 0), 0.0)
                gr, gi = gr + qr * sr + qi * si, gi + qr * si - qi * sr
            cr, ci = lcarry[0:1, lr], lcarry[0:1, li]
            pr, pi = aqrr_ref[:, la], aqir_ref[:, la]
            gr, gi = gr + pr * cr + pi * ci, gi + pr * ci - pi * cr
            xr = jnp.where(row8 == 7, cr, pltpu.roll(gr, 7, 0))
            xi = jnp.where(row8 == 7, ci, pltpu.roll(gi, 7, 0))
            lcarry[0:1, lr] = gr[0:1, :]
            lcarry[0:1, li] = gi[0:1, :]
            acc_r = jnp.zeros((8, _SBH), F32)
            acc_i = jnp.zeros((8, _SBH), F32)
            for i in range(seg):
                blk = slice(i * 8, (i + 1) * 8)
                pr, pi = aprr_ref[i:i + 1, la], apir_ref[i:i + 1, la]
                gr = L[blk, lr] + pr * xr + pi * xi
                gi = L[blk, li] + pr * xi - pi * xr
                L[blk, lr] = gr
                L[blk, li] = gi
                if i == 0:
                    hpr, hpi = cin[:, lr], cin[:, li]
                else:
                    hpr, hpi = S[(i - 1) * 8:i * 8, lr], S[(i - 1) * 8:i * 8, li]
                acc_r = acc_r + gr * hpr + gi * hpi
                acc_i = acc_i + gi * hpr - gr * hpi
            dar_ref[:, la] += acc_r
            dai_ref[:, la] += acc_i
        dus = []
        for sb in range(S5_NSB):
            ls = slice(sb * _SBW, (sb + 1) * _SBW)
            us = slice(sb * _SBU, (sb + 1) * _SBU)
            lb = L[:, ls].astype(MXU_DTYPE)
            dwb_ref[sb] += _dot_tn(lb, ub[:, us])
            dwc_ref[sb] += _dot_tn(S[:, ls].astype(MXU_DTYPE), dyb[:, us])
            dus.append(_dot_nt(lb, wb_ref[sb]))
        du_ref[...] = _ungroup(jnp.concatenate(dus, axis=1) + dyv * d_ref[...], seg).astype(MXU_DTYPE)

    rt = lambda i: nt - 1 - i
    full = lambda a: pl.BlockSpec(a.shape, lambda i: (0,) * a.ndim)
    acc = lambda shape: pl.BlockSpec(shape, lambda i: (0,) * len(shape))
    return _call(
        body, name=name,
        out_shape=(jax.ShapeDtypeStruct((T, D), MXU_DTYPE), jax.ShapeDtypeStruct((S5_NSB, _SBW, _SBU), F32),
                   jax.ShapeDtypeStruct((S5_NSB, _SBW, _SBU), F32), jax.ShapeDtypeStruct((8, GP), F32),
                   jax.ShapeDtypeStruct((8, GP), F32)),
        grid=(nt,),
        in_specs=[pl.BlockSpec((tm, D), lambda i: (rt(i), 0)), pl.BlockSpec((tm, D), lambda i: (rt(i), 0)),
                  pl.BlockSpec((None, 8, S5_NS), lambda i: (rt(i), 0, 0)), full(wb), full(wc)]
        + [full(t) for t in tables] + [full(d)],
        out_specs=(pl.BlockSpec((tm, D), lambda i: (rt(i), 0)), acc((S5_NSB, _SBW, _SBU)), acc((S5_NSB, _SBW, _SBU)),
                   acc((8, GP)), acc((8, GP))),
        scratch_shapes=[pltpu.VMEM((tm, S5_NS), F32), pltpu.VMEM((tm, S5_NS), F32), pltpu.VMEM((8, S5_NS), F32),
                        pltpu.VMEM((8, S5_NS), F32), pltpu.VMEM((8, S5_NS), F32)],
        compiler_params=_params(("arbitrary",)),
    )(dy, u, st, wb, wc, *tables, d)


def final_loss(h, g, target, tm, name):
    T, D = h.shape

    def body(h_ref, g_ref, t_ref, dh_ref, se_ref, dg_ref):
        @pl.when(pl.program_id(0) == 0)
        def _():
            se_ref[...] = jnp.zeros_like(se_ref)
            dg_ref[...] = jnp.zeros_like(dg_ref)

        r, xhat = _rms(h_ref[...])
        gv = g_ref[...]
        e = xhat * gv - t_ref[...]
        se_ref[...] += _rowsum8(e * e)
        dy = e * (1.0 / D)
        dg_ref[...] += _rowsum8(dy * xhat)
        dh_ref[...] = _rms_bwd(dy, xhat, r, gv)

    rs = pl.BlockSpec((tm, D), lambda i: (i, 0))
    acc = pl.BlockSpec((8, D), lambda i: (0, 0))
    return _call(
        body, name=name,
        out_shape=(jax.ShapeDtypeStruct((T, D), F32), jax.ShapeDtypeStruct((8, D), F32), jax.ShapeDtypeStruct((8, D), F32)),
        grid=(T // tm,), in_specs=[rs, pl.BlockSpec((1, D), lambda i: (0, 0)), rs], out_specs=(rs, acc, acc),
        compiler_params=_params(("arbitrary",)),
    )(h, g, target)


def _s5_discretize(a_re, a_im, log_dt, b_re, b_im):
    dt = jnp.exp(log_dt)[:, None]
    mag = jnp.exp(a_re * dt)
    abr = mag * jnp.cos(a_im * dt)
    abi = mag * jnp.sin(a_im * dt)
    ur, ui = abr - 1.0, abi
    den = a_re * a_re + a_im * a_im
    wr = (ur * a_re + ui * a_im) / den
    wi = (ui * a_re - ur * a_im) / den
    bbr = wr[..., None] * b_re - wi[..., None] * b_im
    bbi = wr[..., None] * b_im + wi[..., None] * b_re
    return abr, abi, bbr, bbi


def _s5_pack(abr, abi, bbr, bbi, c_re, c_im, seg_len):
    eye = jnp.eye(S5_SB, dtype=F32)
    b = jnp.stack([bbr, bbi], 0).reshape(2, S5_NSB, S5_SB, S5_P, S5_GC)
    wb = jnp.einsum('rsgpc,gh->shcrgp', b, eye).reshape(S5_NSB, _SBU, _SBW)
    c = jnp.stack([c_re, -c_im], 0).reshape(2, S5_NSB, S5_SB, S5_GC, S5_P)
    wc = jnp.einsum('rsgcp,gh->srgphc', c, eye).reshape(S5_NSB, _SBW, _SBU)
    def powers(r, i, n):
        pr, pi = [r], [i]
        for _ in range(n - 1):
            pr.append(pr[-1] * r - pi[-1] * i)
            pi.append(pr[-2] * i + pi[-1] * r)
        return jnp.concatenate(pr, 0), jnp.concatenate(pi, 0)

    ap_r, ap_i = powers(abr.reshape(1, -1), abi.reshape(1, -1), seg_len)
    aq_r, aq_i = powers(ap_r[seg_len - 1:seg_len], ap_i[seg_len - 1:seg_len], 8)
    return wb.astype(MXU_DTYPE), wc.astype(MXU_DTYPE), ap_r, ap_i, aq_r, aq_i


def _s5_unpack_grads(dwb_t, dwc, dar8, dai8):
    eye = jnp.eye(S5_SB, dtype=F32)
    t = dwb_t.reshape(S5_NSB, 2, S5_SB, S5_P, S5_SB, S5_GC)
    db = jnp.einsum('srgphc,gh->rsgpc', t, eye).reshape(2, S5_G, S5_P, S5_GC)
    t = dwc.reshape(S5_NSB, 2, S5_SB, S5_P, S5_SB, S5_GC)
    dc = jnp.einsum('srgphc,gh->rsgcp', t, eye).reshape(2, S5_G, S5_GC, S5_P)
    return db[0], db[1], dc[0], -dc[1], dar8.sum(0).reshape(S5_G, S5_P), dai8.sum(0).reshape(S5_G, S5_P)


TM = 256
TM_S5 = 256


def _tn(a, b, name):
    T, M, N = a.shape[0], a.shape[1], b.shape[1]
    bt = 4096 if a.dtype.itemsize + b.dtype.itemsize <= 4 else 2048
    return matmul_tn(a, b, min(M, 1024), min(N, 1024), bt if T % bt == 0 else T, name)


def local_step(x, target, W, sched):
    T, D = x.shape
    depth = W['norm_mix_g'].shape[0]
    row = lambda v: v.reshape(1, -1)
    saved = []
    h = x
    s5c = []
    tr = min(512, T)
    for j in range(W['s5_a_re'].shape[0]):
        prm = (W['s5_a_re'][j], W['s5_a_im'][j], W['s5_log_dt'][j], W['s5_b_re'][j], W['s5_b_im'][j])
        disc, disc_vjp = jax.vjp(_s5_discretize, *prm)
        s5c.append((*_s5_pack(*disc, W['s5_c_re'][j], W['s5_c_im'][j], min(TM_S5, T) // 8), disc_vjp))
    sp_all = jax.nn.softplus(-W['rg_lambda'])
    for i in range(depth):
        j = i // 2
        if i % 2 == 0:
            xg, hn = norm_matmul(h, row(W['norm_mix_g'][i]), W['rg_w_in'][j], F32, tr, None, f"rg_in_{i}")
            h1, hs = rg_fwd(xg, h, W['rg_conv_w'][j], row(W['rg_conv_b'][j]), W['rg_w_a'][j].astype(MXU_DTYPE),
                            row(W['rg_b_a'][j]), W['rg_w_x'][j].astype(MXU_DTYPE), row(W['rg_b_x'][j]), row(sp_all[j]),
                            W['rg_w_out'][j], TM, f"rg_fwd_{i}", comm=sched.comm(f"rg_fwd_{i}"))
            mix = (xg, hn, hs)
        else:
            u, hn = norm_matmul(h, row(W['norm_mix_g'][i]), W['s5_w_in'][j], F32, tr, None, f"s5_in_{i}")
            h1, ypre, gl, st = s5_fwd(u, h, *s5c[j][:-1], row(W['s5_d'][j]), W['s5_w_glu'][j], W['s5_w_out'][j],
                                      min(TM_S5, T), f"s5_fwd_{i}")
            mix = (u, hn, ypre, gl, st)
        up, hn2 = norm_matmul(h1, row(W['norm_ffn_g'][i]), W['ffn_w_up'][i], MXU_DTYPE, tr, None, f"ffn_up_{i}",
                              comm=sched.comm(f"ffn_up_{i}"))
        h2, av = ffn_fwd(up, h1, W['ffn_conv_w'][i], row(W['ffn_conv_b'][i]), W['ffn_w_down'][i], TM, f"ffn_fwd_{i}",
                         comm=sched.comm(f"ffn_fwd_{i}"))
        saved.append((h, mix, h1, hn2, up, av))
        h = h2
    dh, se8, dgf8 = final_loss(h, row(W['norm_final_g']), target, tr, "final_loss")
    G = {k: [None] * len(v) for k, v in W.items() if k != 'norm_final_g'}
    G['norm_final_g'] = dgf8.sum(0)
    for i in reversed(range(depth)):
        j = i // 2
        h0, mix, h1, hn2, up, av = saved[i]
        dup, act, dcb8 = ffn_bwd_act(dh, av, W['ffn_w_down'][i], TM, f"ffn_bwd_act_{i}",
                                     comm=sched.comm(f"ffn_bwd_act_{i}", G))
        G['ffn_w_down'][i] = _tn(act, dh, f"ffn_dwdown_{i}")
        dh1, dg8, dupp, dcw8 = dx_norm_bwd(dup, W['ffn_w_up'][i], h1, row(W['norm_ffn_g'][i]), dh, TM,
                                           f"ffn_bwd_in_{i}", conv_w=W['ffn_conv_w'][i], conv_x=up)
        G['ffn_w_up'][i] = _tn(hn2, dupp, f"ffn_dwup_{i}")
        G['ffn_conv_b'][i] = dcb8.sum(0)
        G['ffn_conv_w'][i] = dcw8.sum(1)
        G['norm_ffn_g'][i] = dg8.sum(0)
        if i % 2 == 0:
            xg, hn, hs = mix
            dxg, y, dwa, dwx, dba8, dbx8, dsp8, dcb8, dcw8 = rg_bwd(
                dh1, xg, hs, W['rg_conv_w'][j], row(W['rg_conv_b'][j]), W['rg_w_a'][j].astype(MXU_DTYPE),
                row(W['rg_b_a'][j]), W['rg_w_x'][j].astype(MXU_DTYPE), row(W['rg_b_x'][j]), row(sp_all[j]),
                W['rg_w_out'][j], TM, f"rg_bwd_{i}", comm=sched.comm(f"rg_bwd_{i}", G))
            G['rg_w_out'][j] = _tn(y, dh1, f"rg_dwout_{i}")
            dh, dg8 = dx_norm_bwd(dxg, W['rg_w_in'][j], h0, row(W['norm_mix_g'][i]), dh1, TM, f"rg_bwd_in_{i}")
            G['rg_w_in'][j] = _tn(hn, dxg, f"rg_dwin_{i}")
            G['rg_w_a'][j], G['rg_w_x'][j] = dwa, dwx
            G['rg_b_a'][j] = dba8.sum(0).reshape(RG_HEADS, RG_BW)
            G['rg_b_x'][j] = dbx8.sum(0).reshape(RG_HEADS, RG_BW)
            G['rg_lambda'][j] = dsp8.sum(0) * (-jax.nn.sigmoid(-W['rg_lambda'][j]))
            G['rg_conv_b'][j] = dcb8.sum(0)
            G['rg_conv_w'][j] = dcw8.sum(1)
        else:
            u, hn, ypre, gl, st = mix
            disc_vjp = s5c[j][-1]
            dy, oact, dgl, gact, dd8 = s5_bwd_glu(dh1, gl, ypre, u, W['s5_w_glu'][j], W['s5_w_out'][j], TM,
                                                  f"s5_bwd_glu_{i}")
            G['s5_w_out'][j] = _tn(oact, dh1, f"s5_dwout_{i}")
            G['s5_w_glu'][j] = _tn(gact, dgl, f"s5_dwglu_{i}")
            du, dwb_t, dwc, dar8, dai8 = s5_bwd_ssm(dy, u, st, *s5c[j][:-1], row(W['s5_d'][j]),
                                                    min(TM_S5, T), f"s5_bwd_ssm_{i}")
            dh, dg8 = dx_norm_bwd(du, W['s5_w_in'][j], h0, row(W['norm_mix_g'][i]), dh1, TM, f"s5_bwd_in_{i}")
            G['s5_w_in'][j] = _tn(hn, du, f"s5_dwin_{i}")
            dbbr, dbbi, dcr, dci, dabr, dabi = _s5_unpack_grads(dwb_t, dwc, dar8, dai8)
            da_re, da_im, dlog_dt, db_re, db_im = disc_vjp((dabr, dabi, dbbr, dbbi))
            G['s5_a_re'][j], G['s5_a_im'][j], G['s5_log_dt'][j] = da_re, da_im, dlog_dt
            G['s5_b_re'][j], G['s5_b_im'][j], G['s5_c_re'][j], G['s5_c_im'][j] = db_re, db_im, dcr, dci
            G['s5_d'][j] = dd8.sum(0)
        G['norm_mix_g'][i] = dg8.sum(0)
    G = {k: (v if (k == 'norm_final_g' or k in BIG) else jnp.stack(v, 0)) for k, v in G.items()}
    return se8, dh, G


BIG = {'rg_w_in': 1, 'rg_w_out': 0, 's5_w_in': 0, 's5_w_glu': 1, 's5_w_out': 0, 'ffn_w_up': 1, 'ffn_w_down': 0}
SMALL_SHARDED = ('rg_conv_w', 'ffn_conv_w', 's5_d')
REPLICATED = ('norm_mix_g', 'norm_ffn_g', 'norm_final_g', 'rg_conv_b', 'rg_w_a', 'rg_b_a', 'rg_w_x', 'rg_b_x',
              'rg_lambda', 's5_a_re', 's5_a_im', 's5_log_dt', 's5_b_re', 's5_b_im', 's5_c_re', 's5_c_im', 'ffn_conv_b')


def _me():
    x, y, c = lax.axis_index("x"), lax.axis_index("y"), lax.axis_index("c")
    return x, y, c, 4 * x + 2 * y + c


def _win(ref, axis, dev, width):
    idx = [slice(None)] * len(ref.shape)
    idx[axis] = pl.ds(pl.multiple_of(dev * width, width), width)
    return ref.at[tuple(idx)]


def gather_plan(items, deliver):
    n = len(items)

    def tools(ins, outs, sems):
        send_sems, recv_sems, local_sems = sems
        x, y, c, me = _me()
        sib = (x, y, 1 - c)
        chips = [(1 - x, y), (x, 1 - y), (1 - x, 1 - y)]
        num = lambda px, py, pc: 4 * px + 2 * py + pc

        def src_of(a):
            return ins[a] if items[a][1] is None else ins[a].at[items[a][1]]

        def block(a, dev):
            return _win(outs[a], items[a][2], dev, src_of(a).shape[items[a][2]])

        def copy(a, k, dev, to, own=False):
            return pltpu.make_async_remote_copy(
                src_ref=src_of(a) if own else block(a, dev), dst_ref=block(a, dev),
                send_sem=send_sems.at[a, k], recv_sem=recv_sems.at[a, k], device_id=to, device_id_type=MESH)

        mine = lambda: [pltpu.make_async_copy(src_of(a), block(a, me), local_sems.at[a]) for a in range(n)]
        own = lambda: [cp for a in range(n) for cp in
                       [copy(a, 0, me, sib, own=True)] + [copy(a, 1 + j, me, (*chip, c), own=True)
                                                          for j, chip in enumerate(chips)]]
        arrived = lambda j, a: copy(a, 1 + j, num(*chips[j], c), (x, y, c))
        passed = lambda j, a: copy(a, 4 + j, num(*chips[j], c), sib)
        from_sib = lambda: ([copy(a, 0, num(x, y, 1 - c), (x, y, c)) for a in range(n)]
                            + [copy(a, 4 + j, num(*chip, 1 - c), (x, y, c)) for j, chip in enumerate(chips)
                               for a in range(n)])
        return mine, own, arrived, passed, from_sib

    def start(ins, outs, sems):
        mine, own, _, _, _ = tools(ins, outs, sems)
        for cp in mine() + own():
            cp.start()

    def middle(ins, outs, sems):
        _, _, arrived, passed, _ = tools(ins, outs, sems)
        for j in range(3):
            for a in range(n):
                arrived(j, a).wait_recv()
                passed(j, a).start()

    def finish(ins, outs, sems):
        mine, own, _, passed, from_sib = tools(ins, outs, sems)
        for cp in from_sib():
            cp.wait_recv()
        for cp in own() + [passed(j, a) for j in range(3) for a in range(n)]:
            cp.wait_send()
        for cp in mine():
            cp.wait()

    return Comm([it[0] for it in items], [jax.ShapeDtypeStruct(it[3], it[0].dtype) for it in items],
                [pltpu.SemaphoreType.DMA((n, 7)), pltpu.SemaphoreType.DMA((n, 7)), pltpu.SemaphoreType.DMA((n,))],
                start, middle, finish, deliver)


def exchange_plan(items, deliver):
    n = len(items)
    width = [arr.shape[axis] // N_DEV for arr, axis in items]
    shard = [arr.shape[:axis] + (w,) + arr.shape[axis + 1:] for (arr, axis), w in zip(items, width)]

    def tools(ins, outs, sems):
        send_sems, recv_sems, local_sems = sems
        x, y, c, me = _me()
        piece = lambda a, dev: _win(ins[a], items[a][1], dev, width[a])
        mine = lambda: [pltpu.make_async_copy(piece(a, me), outs[a].at[me], local_sems.at[a]) for a in range(n)]

        def remote(sending):
            cps = []
            for k in range(1, N_DEV):
                px, py, pc = (1 - x) if k & 4 else x, (1 - y) if k & 2 else y, (1 - c) if k & 1 else c
                peer = 4 * px + 2 * py + pc
                for a in range(n):
                    src, dst = (piece(a, peer), outs[a].at[me]) if sending else (piece(a, me), outs[a].at[peer])
                    cps.append(pltpu.make_async_remote_copy(
                        src_ref=src, dst_ref=dst, send_sem=send_sems.at[a, k - 1], recv_sem=recv_sems.at[a, k - 1],
                        device_id=(px, py, pc), device_id_type=MESH))
            return cps

        return mine, remote

    def start(ins, outs, sems):
        mine, remote = tools(ins, outs, sems)
        for cp in mine() + remote(True):
            cp.start()

    def middle(ins, outs, sems):
        pass

    def finish(ins, outs, sems):
        mine, remote = tools(ins, outs, sems)
        for cp in remote(False):
            cp.wait_recv()
        for cp in remote(True):
            cp.wait_send()
        for cp in mine():
            cp.wait()

    return Comm([it[0] for it in items], [jax.ShapeDtypeStruct((N_DEV,) + s, it[0].dtype) for it, s in zip(items, shard)],
                [pltpu.SemaphoreType.DMA((n, 7)), pltpu.SemaphoreType.DMA((n, 7)), pltpu.SemaphoreType.DMA((n,))],
                start, middle, finish, deliver)


def adam_update(parts, w, m, v, name):
    R, C = w.shape
    br = next((b for b in (256, 128) if R > b and R % b == 0), R)
    np_ = parts.shape[0]

    def body(p_ref, w_ref, m_ref, v_ref, g_ref, d_ref, nm_ref, nv_ref):
        _adam_body(np_, p_ref, w_ref, m_ref, v_ref, g_ref, d_ref, nm_ref, nv_ref)

    bs = pl.BlockSpec((br, C), lambda i: (i, 0))
    out = jax.ShapeDtypeStruct((R, C), F32)
    return _call(
        body, name=name, out_shape=(out, out, out, out), grid=(R // br,),
        in_specs=[pl.BlockSpec((np_, br, C), lambda i: (0, i, 0)), bs, bs, bs], out_specs=(bs, bs, bs, bs),
        compiler_params=_params(("parallel",)),
    )(parts, w, m, v)


def _adam_body(np_, p_ref, w_ref, m_ref, v_ref, g_ref, d_ref, nm_ref, nv_ref):
    c1 = 1.0 / (1.0 - ADAM_B1 ** ADAM_STEP)
    c2 = 1.0 / (1.0 - ADAM_B2 ** ADAM_STEP)
    g = p_ref[0].astype(F32)
    for p in range(1, np_):
        g = g + p_ref[p].astype(F32)
    nm = ADAM_B1 * m_ref[...] + (1.0 - ADAM_B1) * g
    nv = ADAM_B2 * v_ref[...] + (1.0 - ADAM_B2) * (g * g)
    g_ref[...] = g
    nm_ref[...] = nm
    nv_ref[...] = nv
    d_ref[...] = -ADAM_LR * ((nm * c1) / (jnp.sqrt(nv * c2) + ADAM_EPS) + ADAM_WD * w_ref[...])


def adam_layer(parts, w, m, v, l, prev, name):
    L, R, C = w.shape
    br = next((b for b in (256, 128) if R > b and R % b == 0), R)

    def body(p_ref, w_ref, m_ref, v_ref, *rest):
        _adam_body(N_DEV, p_ref, w_ref, m_ref, v_ref, *rest[-4:])

    bs = pl.BlockSpec((None, br, C), lambda i: (l, i, 0))
    out = jax.ShapeDtypeStruct((L, R, C), F32)
    extra = {} if prev is None else dict(input_output_aliases={4 + q: q for q in range(4)})
    return _call(
        body, name=name, out_shape=(out, out, out, out), grid=(R // br,),
        in_specs=[pl.BlockSpec((N_DEV, br, C), lambda i: (0, i, 0)), bs, bs, bs] + ([] if prev is None else [ANY] * 4),
        out_specs=(bs, bs, bs, bs), compiler_params=_params(("parallel",)), **extra,
    )(parts, w, m, v, *(() if prev is None else prev))


def sum_parts(parts, name):
    n, R, C = parts.shape

    def body(p_ref, o_ref):
        g = p_ref[0]
        for p in range(1, n):
            g = g + p_ref[p]
        o_ref[...] = g

    return _call(body, name=name, out_shape=jax.ShapeDtypeStruct((R, C), parts.dtype),
                 compiler_params=pltpu.CompilerParams(vmem_limit_bytes=VMEM_LIMIT))(parts)


def _pack_rows(arrs):
    rows = []
    for a in arrs:
        f = a.reshape(-1)
        r = -(-f.shape[0] // 1024)
        r8 = -(-r // 8) * 8
        rows.append(jnp.pad(f, (0, r8 * 1024 - f.shape[0])).reshape(r8, 1024))
    packed = jnp.concatenate(rows, 0)
    return jnp.pad(packed, ((0, -packed.shape[0] % 128), (0, 0)))


def _unpack_rows(packed, shapes):
    out, o = [], 0
    for s in shapes:
        nel = math.prod(s)
        r8 = -(-(-(-nel // 1024)) // 8) * 8
        out.append(packed[o:o + r8].reshape(-1)[:nel].reshape(s))
        o += r8
    return out


def kernel(x, norm_mix_g, norm_ffn_g, norm_final_g, rg_w_in, rg_conv_w, rg_conv_b, rg_w_a, rg_b_a, rg_w_x, rg_b_x, rg_lambda, rg_w_out, s5_w_in, s5_a_re, s5_a_im, s5_log_dt, s5_b_re, s5_b_im, s5_c_re, s5_c_im, s5_d, s5_w_glu, s5_w_out, ffn_w_up, ffn_conv_w, ffn_conv_b, ffn_w_down, loss_target, m_norm_mix_g, m_norm_ffn_g, m_norm_final_g, m_rg_w_in, m_rg_conv_w, m_rg_conv_b, m_rg_w_a, m_rg_b_a, m_rg_w_x, m_rg_b_x, m_rg_lambda, m_rg_w_out, m_s5_w_in, m_s5_a_re, m_s5_a_im, m_s5_log_dt, m_s5_b_re, m_s5_b_im, m_s5_c_re, m_s5_c_im, m_s5_d, m_s5_w_glu, m_s5_w_out, m_ffn_w_up, m_ffn_conv_w, m_ffn_conv_b, m_ffn_w_down, v_norm_mix_g, v_norm_ffn_g, v_norm_final_g, v_rg_w_in, v_rg_conv_w, v_rg_conv_b, v_rg_w_a, v_rg_b_a, v_rg_w_x, v_rg_b_x, v_rg_lambda, v_rg_w_out, v_s5_w_in, v_s5_a_re, v_s5_a_im, v_s5_log_dt, v_s5_b_re, v_s5_b_im, v_s5_c_re, v_s5_c_im, v_s5_d, v_s5_w_glu, v_s5_w_out, v_ffn_w_up, v_ffn_conv_w, v_ffn_conv_b, v_ffn_w_down):
    names = ('norm_mix_g', 'norm_ffn_g', 'norm_final_g', 'rg_w_in', 'rg_conv_w', 'rg_conv_b', 'rg_w_a', 'rg_b_a',
             'rg_w_x', 'rg_b_x', 'rg_lambda', 'rg_w_out', 's5_w_in', 's5_a_re', 's5_a_im', 's5_log_dt', 's5_b_re',
             's5_b_im', 's5_c_re', 's5_c_im', 's5_d', 's5_w_glu', 's5_w_out', 'ffn_w_up', 'ffn_conv_w', 'ffn_conv_b',
             'ffn_w_down')
    loc = locals()
    Wl = {k: loc[k] for k in names}
    Ml = {k: loc['m_' + k] for k in names}
    Vl = {k: loc['v_' + k] for k in names}

    depth = norm_mix_g.shape[0]
    mixer_keys = lambda i: ([('rg_w_in', i // 2), ('rg_w_out', i // 2)] if i % 2 == 0 else
                            [('s5_w_in', i // 2), ('s5_w_glu', i // 2), ('s5_w_out', i // 2)])
    ffn_keys = lambda i: [('ffn_w_up', i), ('ffn_w_down', i)]
    shards = {k: Wl[k].astype(BF16) for k in BIG}
    W = {k: Wl[k] for k in REPLICATED}
    W.update({k: [None] * Wl[k].shape[0] for k in BIG})
    parts = {}

    def gather_of(keys, small=False):
        items = []
        for k, l in keys:
            _, r, c = shards[k].shape
            items.append((shards[k], l, BIG[k], (r * N_DEV, c) if BIG[k] == 0 else (r, c * N_DEV)))
        if small:
            items += [(Wl[k], None, Wl[k].ndim - 1, Wl[k].shape[:-1] + (Wl[k].shape[-1] * N_DEV,)) for k in SMALL_SHARDED]

        def deliver(outs):
            for (k, l), arr in zip(keys, outs):
                W[k][l] = arr
            if small:
                W.update(zip(SMALL_SHARDED, outs[len(keys):]))

        return gather_plan(items, deliver)

    def exchange_of(keys, G, extra=()):
        items = [(G[k][l], BIG[k]) for k, l in keys] + [(arr, axis) for _, arr, axis in extra]
        return exchange_plan(items, lambda outs: parts.update(zip(list(keys) + [e[0] for e in extra], outs)))

    class Sched:
        @staticmethod
        def comm(host, G=None):
            kind, _, i = host.rpartition("_")
            i = int(i)
            if host == "rg_fwd_0":
                return gather_of(ffn_keys(0))
            if kind == "ffn_up" and i + 1 < depth:
                return gather_of(ffn_keys(i + 1)[:1])
            if kind == "ffn_fwd" and i + 1 < depth:
                return gather_of(mixer_keys(i + 1) + ffn_keys(i + 1)[1:])
            if kind == "ffn_bwd_act" and i + 1 < depth:
                return exchange_of(mixer_keys(i + 1) + ffn_keys(i + 1), G)
            if host == "rg_bwd_0":
                return exchange_of(ffn_keys(0), G)
            return None

    run_comm(gather_of(mixer_keys(0), small=True), "gather_first")

    se8, gx, G = local_step(x[0], loss_target[0], W, Sched)
    loss = lax.psum(0.5 * jnp.sum(se8) / x.shape[-1], ("x", "y", "c"))

    rep_shapes = [Wl[k].shape for k in REPLICATED]
    gp = _pack_rows([G[k].astype(F32) for k in REPLICATED])
    extra = [(k, G[k], G[k].ndim - 1) for k in SMALL_SHARDED] + [('replicated', gp, 0)]
    run_comm(exchange_of(mixer_keys(0), G, extra), "exchange_last")
    out_g, out_d, out_m, out_v = {}, {}, {}, {}
    for k in BIG:
        res = None
        for l in range(Wl[k].shape[0]):
            res = adam_layer(parts[(k, l)], Wl[k], Ml[k], Vl[k], l, res, f"adam_{k}_{l}")
        out_g[k], out_d[k], out_m[k], out_v[k] = res
    for k in SMALL_SHARDED:
        shp = Wl[k].shape
        r2 = (math.prod(shp[:-1]), shp[-1])
        res = adam_update(parts[k].reshape((N_DEV,) + r2), Wl[k].reshape(r2), Ml[k].reshape(r2), Vl[k].reshape(r2),
                          f"adam_{k}")
        out_g[k], out_d[k], out_m[k], out_v[k] = [t.reshape(shp) for t in res]
    rsum = sum_parts(parts['replicated'], "sum_replicated")
    run_comm(gather_plan([(rsum, None, 0, gp.shape)], lambda outs: parts.update(rep_full=outs[0])), "gather_small_grads")
    res = adam_update(parts['rep_full'][None], _pack_rows([Wl[k] for k in REPLICATED]),
                      _pack_rows([Ml[k] for k in REPLICATED]), _pack_rows([Vl[k] for k in REPLICATED]), "adam_replicated")
    for dst, packed in zip((out_g, out_d, out_m, out_v), res):
        for k, t in zip(REPLICATED, _unpack_rows(packed, rep_shapes)):
            dst[k] = t
    return (loss, gx[None], *[out_g[k] for k in names], *[out_d[k] for k in names], *[out_m[k] for k in names],
            *[out_v[k] for k in names])
```

```python
import functools
import math

import jax
import jax.numpy as jnp
from jax import lax
from jax.experimental import pallas as pl
from jax.experimental.pallas import tpu as pltpu

F32 = jnp.float32
BF16 = jnp.bfloat16
MXU_DTYPE = jnp.bfloat16

NORM_EPS = 1e-6
RG_C = 8.0
RG_HEADS = 8
RG_BW = 128
S5_G = 64
S5_GC = 16
S5_P = 64
S5_SB = 8
S5_NSB = S5_G // S5_SB
S5_NS = 2 * S5_G * S5_P
ADAM_LR = 0.001
ADAM_B1 = 0.9
ADAM_B2 = 0.999
ADAM_EPS = 1e-08
ADAM_WD = 0.01
ADAM_STEP = 10
N_DEV = 8
VMEM_LIMIT = 56 * 1024 * 1024


def _call(body, **kw):
    return pl.pallas_call(body, **kw)


def _params(sem, vmem=VMEM_LIMIT):
    return pltpu.CompilerParams(dimension_semantics=sem, vmem_limit_bytes=vmem)


MESH = pl.DeviceIdType.MESH
ANY = pl.BlockSpec(memory_space=pl.ANY)


class Comm:
    def __init__(self, operands, out_shape, scratch, start, middle, finish, deliver):
        self.operands, self.out_shape, self.scratch = list(operands), list(out_shape), list(scratch)
        self.start, self.middle, self.finish, self.deliver = start, middle, finish, deliver


def run_comm(comm, name):
    ci, co = len(comm.operands), len(comm.out_shape)

    def body(*refs):
        parts = (refs[:ci], refs[ci:ci + co], refs[ci + co:])
        comm.start(*parts)
        comm.middle(*parts)
        comm.finish(*parts)

    comm.deliver(_call(body, name=name, out_shape=tuple(comm.out_shape), in_specs=[ANY] * ci,
                       out_specs=tuple([ANY] * co), scratch_shapes=comm.scratch)(*comm.operands))


def _hosted(body, comm, args, *, name, out_shape, grid, in_specs, out_specs, scratch_shapes, compiler_params):
    if comm is None:
        return _call(body, name=name, out_shape=tuple(out_shape), grid=grid, in_specs=in_specs,
                     out_specs=tuple(out_specs), scratch_shapes=scratch_shapes, compiler_params=compiler_params)(*args)
    n_in, n_out, n_sc = len(in_specs), len(out_shape), len(scratch_shapes)
    ci, co = len(comm.operands), len(comm.out_shape)
    nsteps = math.prod(grid)
    mid = (2 * nsteps) // 3

    def wrapped(*refs):
        ins, refs = refs[:n_in], refs[n_in:]
        cins, refs = refs[:ci], refs[ci:]
        outs, refs = refs[:n_out], refs[n_out:]
        couts, refs = refs[:co], refs[co:]
        sc, csc = refs[:n_sc], refs[n_sc:]
        step = pl.program_id(0)
        for d in range(1, len(grid)):
            step = step * grid[d] + pl.program_id(d)

        @pl.when(step == 0)
        def _():
            comm.start(cins, couts, csc)

        body(*ins, *outs, *sc)

        @pl.when(step == mid)
        def _():
            comm.middle(cins, couts, csc)

        @pl.when(step == nsteps - 1)
        def _():
            comm.finish(cins, couts, csc)

    res = _call(wrapped, name=name, out_shape=(*out_shape, *comm.out_shape), grid=grid,
                in_specs=[*in_specs, *[ANY] * ci], out_specs=(*out_specs, *[ANY] * co),
                scratch_shapes=[*scratch_shapes, *comm.scratch],
                compiler_params=_params(("arbitrary",) * len(grid)))(*args, *comm.operands)
    comm.deliver(res[n_out:])
    return res[:n_out]


_GELU_C = 0.7978845608028654
_GELU_A = 0.044715


def _gelu(x):
    return 0.5 * x * (1.0 + jnp.tanh(_GELU_C * (x + _GELU_A * x * x * x)))


def _gelu_and_grad(x):
    x2 = x * x
    t = jnp.tanh(_GELU_C * (x + _GELU_A * x2 * x))
    g = 0.5 * x * (1.0 + t)
    dg = 0.5 * (1.0 + t) + 0.5 * x * (1.0 - t * t) * _GELU_C * (1.0 + 3.0 * _GELU_A * x2)
    return g, dg


def _sigmoid(x):
    return 1.0 / (1.0 + jnp.exp(-x))


def _neg_expm1(x):
    series = -x * (1.0 + x * (0.5 + x * (1.0 / 6.0 + x * (1.0 / 24.0 + x * (1.0 / 120.0 + x * (1.0 / 720.0))))))
    return jnp.where(x > -0.1, series, 1.0 - jnp.exp(x))


def _rowsum8(x):
    r, c = x.shape
    return x.reshape(r // 8, 8, c).sum(axis=0)


def _dot(a, b):
    return jnp.dot(a.astype(MXU_DTYPE), b.astype(MXU_DTYPE), preferred_element_type=F32)


def _dot_nt(a, b):
    return lax.dot_general(a.astype(MXU_DTYPE), b.astype(MXU_DTYPE), (((1,), (1,)), ((), ())),
                           preferred_element_type=F32)


def _dot_tn(a, b):
    return lax.dot_general(a.astype(MXU_DTYPE), b.astype(MXU_DTYPE), (((0,), (0,)), ((), ())),
                           preferred_element_type=F32)


def _shift_down(x, s, fills, row):
    y = pltpu.roll(x, s, 0)
    for t in range(s):
        y = jnp.where(row == t, fills[s - 1 - t], y)
    return y


def _shift_up(x, s, fills, row):
    n = x.shape[0]
    y = pltpu.roll(x, n - s, 0)
    for t in range(s):
        y = jnp.where(row == n - s + t, fills[t], y)
    return y


def _rms(x):
    r = lax.rsqrt(jnp.mean(x * x, axis=-1, keepdims=True) + NORM_EPS)
    return r, x * r


def _rms_bwd(dhn, xhat, r, g):
    dz = dhn * g
    return r * (dz - xhat * jnp.mean(dz * xhat, axis=-1, keepdims=True))


def norm_matmul(h, g, w, out_dtype, tm, tn, name, comm=None):
    T, D = h.shape
    N = w.shape[1]
    tn = N if tn is None else tn

    def body(h_ref, g_ref, w_ref, o_ref, hn_ref, hn_s):
        @pl.when(pl.program_id(1) == 0)
        def _():
            _, xhat = _rms(h_ref[...])
            v = (xhat * g_ref[...]).astype(MXU_DTYPE)
            hn_s[...] = v
            hn_ref[...] = v

        o_ref[...] = jnp.dot(hn_s[...], w_ref[...].astype(MXU_DTYPE), preferred_element_type=F32).astype(o_ref.dtype)

    return _hosted(
        body, comm, (h, g, w), name=name,
        out_shape=(jax.ShapeDtypeStruct((T, N), out_dtype), jax.ShapeDtypeStruct((T, D), MXU_DTYPE)),
        grid=(T // tm, N // tn),
        in_specs=[pl.BlockSpec((tm, D), lambda i, j: (i, 0)), pl.BlockSpec((1, D), lambda i, j: (0, 0)),
                  pl.BlockSpec((D, tn), lambda i, j: (0, j))],
        out_specs=(pl.BlockSpec((tm, tn), lambda i, j: (i, j)), pl.BlockSpec((tm, D), lambda i, j: (i, 0))),
        scratch_shapes=[pltpu.VMEM((tm, D), MXU_DTYPE)],
        compiler_params=_params(("parallel", "arbitrary")),
    )


def matmul_tn(a, b, bm, bn, bt, name, out_dtype=BF16):
    T, M = a.shape
    N = b.shape[1]
    nk = T // bt

    def body(a_ref, b_ref, o_ref, acc):
        k = pl.program_id(2)

        @pl.when(k == 0)
        def _():
            acc[...] = jnp.zeros_like(acc)

        acc[...] += _dot_tn(a_ref[...], b_ref[...])

        @pl.when(k == nk - 1)
        def _():
            o_ref[...] = acc[...].astype(o_ref.dtype)

    return _call(
        body, name=name,
        out_shape=jax.ShapeDtypeStruct((M, N), out_dtype),
        grid=(M // bm, N // bn, nk),
        in_specs=[pl.BlockSpec((bt, bm), lambda i, j, k: (k, i)), pl.BlockSpec((bt, bn), lambda i, j, k: (k, j))],
        out_specs=pl.BlockSpec((bm, bn), lambda i, j, k: (i, j)),
        scratch_shapes=[pltpu.VMEM((bm, bn), F32)],
        compiler_params=_params(("parallel", "parallel", "arbitrary")),
    )(a, b)


def dx_norm_bwd(dz, w, h, g, dres, tm, name, conv_w=None, conv_x=None, chunk=1024):
    T, N = dz.shape
    D = w.shape[0]
    nt = T // tm
    has_conv = conv_w is not None
    kw = conv_w.shape[0] if has_conv else 0

    def body(*refs):
        if has_conv:
            dz_ref, cw_ref, x_ref, w_ref, h_ref, g_ref, dres_ref, dh_ref, dg_ref, dzp_ref, dcw_ref, carry, zs = refs
        else:
            dz_ref, w_ref, h_ref, g_ref, dres_ref, dh_ref, dg_ref = refs
        i = pl.program_id(0)

        @pl.when(i == 0)
        def _():
            dg_ref[...] = jnp.zeros_like(dg_ref)
            if has_conv:
                carry[...] = jnp.zeros_like(carry)
                dcw_ref[...] = jnp.zeros_like(dcw_ref)

        if has_conv:
            row = lax.broadcasted_iota(jnp.int32, (tm, 1), 0)
            for c0 in range(0, N, chunk):
                sl = slice(c0, c0 + chunk)
                d0 = dz_ref[:, sl].astype(F32)
                xv = x_ref[:, sl].astype(F32)
                fills = [carry[t:t + 1, sl] for t in range(kw - 1)]
                acc = cw_ref[kw - 1:kw, sl] * d0
                dcw_ref[kw - 1, :, sl] += _rowsum8(d0 * xv)
                for s in range(1, kw):
                    ds = _shift_up(d0, s, fills, row)
                    acc = acc + cw_ref[kw - 1 - s:kw - s, sl] * ds
                    dcw_ref[kw - 1 - s, :, sl] += _rowsum8(ds * xv)
                zb = acc.astype(MXU_DTYPE)
                zs[:, sl] = zb
                dzp_ref[:, sl] = zb
            carry[...] = dz_ref[0:16, :].astype(F32)
            z = zs[...]
        else:
            z = dz_ref[...]
        dhn = _dot_nt(z, w_ref[...])
        r, xhat = _rms(h_ref[...])
        dg_ref[...] += _rowsum8(dhn * xhat)
        dh_ref[...] = dres_ref[...] + _rms_bwd(dhn, xhat, r, g_ref[...])

    if has_conv:
        ti = lambda i: nt - 1 - i
    else:
        ti = lambda i: i
    row_spec = lambda c: pl.BlockSpec((tm, c), lambda i: (ti(i), 0))
    full = lambda a: pl.BlockSpec(a.shape, lambda i: (0,) * a.ndim)
    in_specs = [row_spec(N)] + ([full(conv_w), row_spec(N)] if has_conv else []) + [full(w), row_spec(D), full(g), row_spec(D)]
    out_shape = [jax.ShapeDtypeStruct((T, D), F32), jax.ShapeDtypeStruct((8, D), F32)]
    out_specs = [row_spec(D), pl.BlockSpec((8, D), lambda i: (0, 0))]
    scratch = []
    if has_conv:
        out_shape += [jax.ShapeDtypeStruct((T, N), MXU_DTYPE), jax.ShapeDtypeStruct((kw, 8, N), F32)]
        out_specs += [row_spec(N), pl.BlockSpec((kw, 8, N), lambda i: (0, 0, 0))]
        scratch = [pltpu.VMEM((16, N), F32), pltpu.VMEM((tm, N), MXU_DTYPE)]
    args = [dz] + ([conv_w, conv_x] if has_conv else []) + [w, h, g, dres]
    return _call(
        body, name=name, out_shape=tuple(out_shape), grid=(nt,), in_specs=in_specs, out_specs=tuple(out_specs),
        scratch_shapes=scratch, compiler_params=_params(("arbitrary",)),
    )(*args)


def _ffn_conv_chunk(up_ref, cw_ref, cb_ref, carry, row, sl):
    x = up_ref[:, sl].astype(F32)
    fills = [carry[15:16, sl], carry[14:15, sl]]
    x1 = _shift_down(x, 1, fills, row)
    x2 = _shift_down(x, 2, fills, row)
    return cb_ref[:, sl] + cw_ref[2:3, sl] * x + cw_ref[1:2, sl] * x1 + cw_ref[0:1, sl] * x2


def ffn_fwd(up, h, conv_w, conv_b, w_down, tm, name, chunk=512, comm=None):
    T, C = up.shape
    F = C // 2
    D = h.shape[1]

    def body(up_ref, h_ref, cw_ref, cb_ref, wd_ref, o_ref, av_ref, carry, act_s):
        @pl.when(pl.program_id(0) == 0)
        def _():
            carry[...] = jnp.zeros_like(carry)

        row = lax.broadcasted_iota(jnp.int32, (tm, 1), 0)
        for c0 in range(0, F, chunk):
            sa, sv = slice(c0, c0 + chunk), slice(F + c0, F + c0 + chunk)
            a = _ffn_conv_chunk(up_ref, cw_ref, cb_ref, carry, row, sa)
            v = _ffn_conv_chunk(up_ref, cw_ref, cb_ref, carry, row, sv)
            av_ref[:, sa] = a.astype(MXU_DTYPE)
            av_ref[:, sv] = v.astype(MXU_DTYPE)
            act_s[:, sa] = (_gelu(a) * v).astype(MXU_DTYPE)
        carry[...] = up_ref[tm - 16:tm, :].astype(F32)
        o_ref[...] = h_ref[...] + jnp.dot(act_s[...], wd_ref[...].astype(MXU_DTYPE), preferred_element_type=F32)

    full = lambda a: pl.BlockSpec(a.shape, lambda i: (0,) * a.ndim)
    return _hosted(
        body, comm, (up, h, conv_w, conv_b, w_down), name=name,
        out_shape=(jax.ShapeDtypeStruct((T, D), F32), jax.ShapeDtypeStruct((T, C), MXU_DTYPE)),
        grid=(T // tm,),
        in_specs=[pl.BlockSpec((tm, C), lambda i: (i, 0)), pl.BlockSpec((tm, D), lambda i: (i, 0)),
                  full(conv_w), full(conv_b), full(w_down)],
        out_specs=(pl.BlockSpec((tm, D), lambda i: (i, 0)), pl.BlockSpec((tm, C), lambda i: (i, 0))),
        scratch_shapes=[pltpu.VMEM((16, C), F32), pltpu.VMEM((tm, F), MXU_DTYPE)],
        compiler_params=_params(("arbitrary",)),
    )


def ffn_bwd_act(dout, av, w_down, tm, name, chunk=512, comm=None):
    T, C = av.shape
    F = C // 2
    D = dout.shape[1]

    def body(do_ref, av_ref, wd_ref, dup_ref, act_ref, dcb_ref, dact_s):
        @pl.when(pl.program_id(0) == 0)
        def _():
            dcb_ref[...] = jnp.zeros_like(dcb_ref)

        dact_s[...] = _dot_nt(do_ref[...], wd_ref[...])
        for c0 in range(0, F, chunk):
            sa, sv = slice(c0, c0 + chunk), slice(F + c0, F + c0 + chunk)
            v = av_ref[:, sv].astype(F32)
            ga, dga = _gelu_and_grad(av_ref[:, sa].astype(F32))
            act_ref[:, sa] = (ga * v).astype(MXU_DTYPE)
            dact = dact_s[:, sa]
            da = dact * v * dga
            dv = dact * ga
            dup_ref[:, sa] = da.astype(MXU_DTYPE)
            dup_ref[:, sv] = dv.astype(MXU_DTYPE)
            dcb_ref[:, sa] += _rowsum8(da)
            dcb_ref[:, sv] += _rowsum8(dv)

    full = lambda a: pl.BlockSpec(a.shape, lambda i: (0,) * a.ndim)
    return _hosted(
        body, comm, (dout, av, w_down), name=name,
        out_shape=(jax.ShapeDtypeStruct((T, C), MXU_DTYPE), jax.ShapeDtypeStruct((T, F), MXU_DTYPE),
                   jax.ShapeDtypeStruct((8, C), F32)),
        grid=(T // tm,),
        in_specs=[pl.BlockSpec((tm, D), lambda i: (i, 0)), pl.BlockSpec((tm, C), lambda i: (i, 0)), full(w_down)],
        out_specs=(pl.BlockSpec((tm, C), lambda i: (i, 0)), pl.BlockSpec((tm, F), lambda i: (i, 0)),
                   pl.BlockSpec((8, C), lambda i: (0, 0))),
        scratch_shapes=[pltpu.VMEM((tm, F), F32)],
        compiler_params=_params(("arbitrary",)),
    )


def _rg_gates(xr, wa_ref, ba_ref, wx_ref, bx_ref, sp_ref):
    xb = xr.astype(MXU_DTYPE)
    pa, px = [], []
    for hd in range(RG_HEADS):
        sl = slice(hd * RG_BW, (hd + 1) * RG_BW)
        pa.append(jnp.dot(xb[:, sl], wa_ref[hd].astype(MXU_DTYPE), preferred_element_type=F32))
        px.append(jnp.dot(xb[:, sl], wx_ref[hd].astype(MXU_DTYPE), preferred_element_type=F32))
    r = _sigmoid(jnp.concatenate(pa, axis=1) + ba_ref[...])
    ig = _sigmoid(jnp.concatenate(px, axis=1) + bx_ref[...])
    la = -RG_C * r * sp_ref[...]
    a = jnp.exp(la)
    mult = jnp.sqrt(_neg_expm1(2.0 * la))
    return xb, r, ig, a, mult


def _rg_conv(x, fills, cw_ref, cb_ref, row):
    x1 = _shift_down(x, 1, fills, row)
    x2 = _shift_down(x, 2, fills, row)
    x3 = _shift_down(x, 3, fills, row)
    xr = cb_ref[...] + cw_ref[3:4, :] * x + cw_ref[2:3, :] * x1 + cw_ref[1:2, :] * x2 + cw_ref[0:1, :] * x3
    return xr, (x3, x2, x1, x)


def rg_fwd(xg, h, conv_w, conv_b, w_a, b_a, w_x, b_x, sp, w_out, tm, name, comm=None):
    T, D2 = xg.shape
    D = D2 // 2
    nb = tm // 8

    def body(xg_ref, h_ref, cw_ref, cb_ref, wa_ref, ba_ref, wx_ref, bx_ref, sp_ref, wo_ref, o_ref, hs_ref,
             xcarry, hcarry, a_s, b_s):
        @pl.when(pl.program_id(0) == 0)
        def _():
            xcarry[...] = jnp.zeros_like(xcarry)
            hcarry[...] = jnp.zeros_like(hcarry)

        row = lax.broadcasted_iota(jnp.int32, (tm, 1), 0)
        x = xg_ref[:, 0:D]
        fills = [xcarry[7:8, :], xcarry[6:7, :], xcarry[5:6, :]]
        xr, _ = _rg_conv(x, fills, cw_ref, cb_ref, row)
        xcarry[...] = xg_ref[tm - 8:tm, 0:D]
        _, r, ig, a, mult = _rg_gates(xr, wa_ref, ba_ref, wx_ref, bx_ref, sp_ref)
        a_s[...] = a
        b_s[...] = mult * ig * xr
        row8 = lax.broadcasted_iota(jnp.int32, (8, 1), 0)

        def blk(j, c):
            o = pl.multiple_of(j * 8, 8)
            A = a_s[pl.ds(o, 8), :]
            H = b_s[pl.ds(o, 8), :]
            for s in (1, 2, 4):
                m = row8 >= s
                H = H + A * jnp.where(m, pltpu.roll(H, s, 0), 0.0)
                A = A * jnp.where(m, pltpu.roll(A, s, 0), 1.0)
            H = H + A * c
            hs_ref[pl.ds(o, 8), :] = H
            return H[7:8, :]

        c = lax.fori_loop(0, nb, blk, hcarry[0:1, :])
        hcarry[0:1, :] = c
        y = hs_ref[...] * _gelu(xg_ref[:, D:D2])
        o_ref[...] = h_ref[...] + _dot(y, wo_ref[...])

    full = lambda a: pl.BlockSpec(a.shape, lambda i: (0,) * a.ndim)
    args = (xg, h, conv_w, conv_b, w_a, b_a, w_x, b_x, sp, w_out)
    return _hosted(
        body, comm, args, name=name,
        out_shape=(jax.ShapeDtypeStruct((T, D), F32), jax.ShapeDtypeStruct((T, D), F32)),
        grid=(T // tm,),
        in_specs=[pl.BlockSpec((tm, D2), lambda i: (i, 0)), pl.BlockSpec((tm, D), lambda i: (i, 0))]
        + [full(a) for a in args[2:]],
        out_specs=(pl.BlockSpec((tm, D), lambda i: (i, 0)), pl.BlockSpec((tm, D), lambda i: (i, 0))),
        scratch_shapes=[pltpu.VMEM((8, D), F32), pltpu.VMEM((8, D), F32), pltpu.VMEM((tm, D), F32),
                        pltpu.VMEM((tm, D), F32)],
        compiler_params=_params(("arbitrary",)),
    )


def rg_bwd(dout, xg, hs, conv_w, conv_b, w_a, b_a, w_x, b_x, sp, w_out, tm, name, comm=None):
    T, D2 = xg.shape
    D = D2 // 2
    nt = T // tm
    nb = tm // 8
    kw = conv_w.shape[0]

    def body(do_ref, xg_ref, xh_ref, hs_ref, hh_ref, cw_ref, cb_ref, wa_ref, ba_ref, wx_ref, bx_ref, sp_ref, wo_ref,
             dxg_ref, y_ref, dwa_ref, dwx_ref, dba_ref, dbx_ref, dsp_ref, dcb_ref, dcw_ref,
             acarry, lcarry, dcarry, a_s, b_s, l_s):
        i = pl.program_id(0)
        first_tile = i == nt - 1

        @pl.when(i == 0)
        def _():
            for ref in (acarry, lcarry, dcarry, dwa_ref, dwx_ref, dba_ref, dbx_ref, dsp_ref, dcb_ref, dcw_ref):
                ref[...] = jnp.zeros_like(ref)

        row = lax.broadcasted_iota(jnp.int32, (tm, 1), 0)
        keep = jnp.where(first_tile, 0.0, 1.0)
        x = xg_ref[:, 0:D]
        gate = xg_ref[:, D:D2]
        xh = xh_ref[...] * keep
        fills = [xh[7:8, :], xh[6:7, :], xh[5:6, :]]
        xr, taps = _rg_conv(x, fills, cw_ref, cb_ref, row)
        xb, r, ig, a, mult = _rg_gates(xr, wa_ref, ba_ref, wx_ref, bx_ref, sp_ref)
        hs = hs_ref[...]
        hprev = _shift_down(hs, 1, [hh_ref[7:8, :] * keep], row)
        dy = _dot_nt(do_ref[...], wo_ref[...])
        gg, dgg = _gelu_and_grad(gate)
        y_ref[...] = (hs * gg).astype(MXU_DTYPE)
        dxg_ref[:, D:D2] = (dy * hs * dgg).astype(MXU_DTYPE)
        a_s[...] = _shift_up(a, 1, [acarry[0:1, :]], row)
        b_s[...] = dy * gg
        row8 = lax.broadcasted_iota(jnp.int32, (8, 1), 0)

        def blk(jj, c):
            o = pl.multiple_of((nb - 1 - jj) * 8, 8)
            A = a_s[pl.ds(o, 8), :]
            H = b_s[pl.ds(o, 8), :]
            for s in (1, 2, 4):
                m = row8 < 8 - s
                H = H + A * jnp.where(m, pltpu.roll(H, 8 - s, 0), 0.0)
                A = A * jnp.where(m, pltpu.roll(A, 8 - s, 0), 1.0)
            H = H + A * c
            l_s[pl.ds(o, 8), :] = H
            return H[0:1, :]

        c = lax.fori_loop(0, nb, blk, lcarry[0:1, :])
        lcarry[0:1, :] = c
        acarry[0:1, :] = a[0:1, :]
        lam = l_s[...]
        dla = lam * hprev * a - (lam * ig * xr) * (a * a) / mult
        dig = lam * mult * xr
        dxr = lam * mult * ig
        spv = sp_ref[...]
        dsp_ref[...] += _rowsum8(dla * (-RG_C) * r)
        dpa = (dla * (-RG_C) * spv) * r * (1.0 - r)
        dpx = dig * ig * (1.0 - ig)
        dba_ref[...] += _rowsum8(dpa)
        dbx_ref[...] += _rowsum8(dpx)
        dpab = dpa.astype(MXU_DTYPE)
        dpxb = dpx.astype(MXU_DTYPE)
        back = []
        for hd in range(RG_HEADS):
            sl = slice(hd * RG_BW, (hd + 1) * RG_BW)
            dwa_ref[hd] += _dot_tn(xb[:, sl], dpab[:, sl])
            dwx_ref[hd] += _dot_tn(xb[:, sl], dpxb[:, sl])
            back.append(_dot_nt(dpab[:, sl], wa_ref[hd]) + _dot_nt(dpxb[:, sl], wx_ref[hd]))
        dxr = dxr + jnp.concatenate(back, axis=1)
        nfills = [dcarry[0:1, :], dcarry[1:2, :], dcarry[2:3, :]]
        dxp = cw_ref[kw - 1:kw, :] * dxr
        for s in range(1, kw):
            dxp = dxp + cw_ref[kw - 1 - s:kw - s, :] * _shift_up(dxr, s, nfills, row)
        dcarry[...] = dxr[0:8, :]
        dxg_ref[:, 0:D] = dxp.astype(MXU_DTYPE)
        dcb_ref[...] += _rowsum8(dxr)
        for k in range(kw):
            dcw_ref[k] += _rowsum8(dxr * taps[k])

    rt = lambda i: nt - 1 - i
    halo = lambda i: jnp.maximum((nt - 1 - i) * (tm // 8) - 1, 0)
    full = lambda a: pl.BlockSpec(a.shape, lambda i: (0,) * a.ndim)
    params = (conv_w, conv_b, w_a, b_a, w_x, b_x, sp, w_out)
    acc = lambda shape: pl.BlockSpec(shape, lambda i: (0,) * len(shape))
    return _hosted(
        body, comm, (dout, xg, xg, hs, hs, *params), name=name,
        out_shape=(jax.ShapeDtypeStruct((T, D2), MXU_DTYPE), jax.ShapeDtypeStruct((T, D), MXU_DTYPE),
                   jax.ShapeDtypeStruct((RG_HEADS, RG_BW, RG_BW), F32), jax.ShapeDtypeStruct((RG_HEADS, RG_BW, RG_BW), F32),
                   jax.ShapeDtypeStruct((8, D), F32), jax.ShapeDtypeStruct((8, D), F32), jax.ShapeDtypeStruct((8, D), F32),
                   jax.ShapeDtypeStruct((8, D), F32), jax.ShapeDtypeStruct((kw, 8, D), F32)),
        grid=(nt,),
        in_specs=[pl.BlockSpec((tm, D), lambda i: (rt(i), 0)), pl.BlockSpec((tm, D2), lambda i: (rt(i), 0)),
                  pl.BlockSpec((8, D), lambda i: (halo(i), 0)), pl.BlockSpec((tm, D), lambda i: (rt(i), 0)),
                  pl.BlockSpec((8, D), lambda i: (halo(i), 0))] + [full(a) for a in params],
        out_specs=(pl.BlockSpec((tm, D2), lambda i: (rt(i), 0)), pl.BlockSpec((tm, D), lambda i: (rt(i), 0)),
                   acc((RG_HEADS, RG_BW, RG_BW)), acc((RG_HEADS, RG_BW, RG_BW)), acc((8, D)), acc((8, D)), acc((8, D)),
                   acc((8, D)), acc((kw, 8, D))),
        scratch_shapes=[pltpu.VMEM((8, D), F32), pltpu.VMEM((8, D), F32), pltpu.VMEM((8, D), F32),
                        pltpu.VMEM((tm, D), F32), pltpu.VMEM((tm, D), F32), pltpu.VMEM((tm, D), F32)],
        compiler_params=_params(("arbitrary",)),
    )


_SBW = 2 * S5_SB * S5_P
_SBH = S5_SB * S5_P
_SBU = S5_SB * S5_GC


def _regroup(x, seg_len):
    n, c = x.shape
    return jnp.swapaxes(x.reshape(8, seg_len, c), 0, 1).reshape(n, c)


def _ungroup(x, seg_len):
    n, c = x.shape
    return jnp.swapaxes(x.reshape(seg_len, 8, c), 0, 1).reshape(n, c)


def _s5_lanes(sb):
    return (slice(sb * _SBW, sb * _SBW + _SBH), slice(sb * _SBW + _SBH, (sb + 1) * _SBW),
            slice(sb * _SBH, (sb + 1) * _SBH))


def _s5_scan_fwd(S, carry, cin, ap_r, ap_i, aq_r, aq_i, seg_len):
    row8 = lax.broadcasted_iota(jnp.int32, (8, 1), 0)
    for sb in range(S5_NSB):
        lr, li, la = _s5_lanes(sb)
        ar, ai = ap_r[0:1, la], ap_i[0:1, la]
        hr, hi = S[0:8, lr], S[0:8, li]
        for i in range(1, seg_len):
            blk = slice(i * 8, (i + 1) * 8)
            hr, hi = ar * hr - ai * hi + S[blk, lr], ar * hi + ai * hr + S[blk, li]
            S[blk, lr] = hr
            S[blk, li] = hi
        for s, idx in ((1, 0), (2, 1), (4, 3)):
            qr, qi = aq_r[idx:idx + 1, la], aq_i[idx:idx + 1, la]
            m = row8 >= s
            sr = jnp.where(m, pltpu.roll(hr, s, 0), 0.0)
            si = jnp.where(m, pltpu.roll(hi, s, 0), 0.0)
            hr, hi = hr + qr * sr - qi * si, hi + qr * si + qi * sr
        cr, ci = carry[0:1, lr], carry[0:1, li]
        pr, pi = aq_r[:, la], aq_i[:, la]
        hr, hi = hr + pr * cr - pi * ci, hi + pr * ci + pi * cr
        xr = jnp.where(row8 == 0, cr, pltpu.roll(hr, 1, 0))
        xi = jnp.where(row8 == 0, ci, pltpu.roll(hi, 1, 0))
        carry[0:1, lr] = hr[7:8, :]
        carry[0:1, li] = hi[7:8, :]
        if cin is not None:
            cin[:, lr] = xr
            cin[:, li] = xi
        for i in range(seg_len):
            blk = slice(i * 8, (i + 1) * 8)
            pr, pi = ap_r[i:i + 1, la], ap_i[i:i + 1, la]
            S[blk, lr] += pr * xr - pi * xi
            S[blk, li] += pr * xi + pi * xr


def s5_fwd(u, h, wb, wc, ap_r, ap_i, aq_r, aq_i, d, w_glu, w_out, tm, name):
    T, D = u.shape
    nt = T // tm
    seg = tm // 8

    def body(u_ref, h_ref, wb_ref, wc_ref, apr_ref, api_ref, aqr_ref, aqi_ref, d_ref, wg_ref, wo_ref,
             o_ref, yp_ref, gl_ref, st_ref, S, carry):
        @pl.when(pl.program_id(0) == 0)
        def _():
            carry[...] = jnp.zeros_like(carry)

        st_ref[...] = carry[...]
        uv = _regroup(u_ref[...], seg)
        ub = uv.astype(MXU_DTYPE)
        for sb in range(S5_NSB):
            S[:, sb * _SBW:(sb + 1) * _SBW] = jnp.dot(ub[:, sb * _SBU:(sb + 1) * _SBU], wb_ref[sb].astype(MXU_DTYPE),
                                                      preferred_element_type=F32)
        _s5_scan_fwd(S, carry, None, apr_ref, api_ref, aqr_ref, aqi_ref, seg)
        ys = [jnp.dot(S[:, sb * _SBW:(sb + 1) * _SBW].astype(MXU_DTYPE), wc_ref[sb].astype(MXU_DTYPE),
                      preferred_element_type=F32) for sb in range(S5_NSB)]
        yp = jnp.concatenate(ys, axis=1) + d_ref[...] * uv
        yp_ref[...] = _ungroup(yp, seg)
        gl = _dot(_gelu(yp), wg_ref[...])
        gl_ref[...] = _ungroup(gl, seg)
        out = gl[:, 0:D] * _sigmoid(gl[:, D:2 * D])
        o_ref[...] = h_ref[...] + _ungroup(_dot(out, wo_ref[...]), seg)

    full = lambda a: pl.BlockSpec(a.shape, lambda i: (0,) * a.ndim)
    args = (u, h, wb, wc, ap_r, ap_i, aq_r, aq_i, d, w_glu, w_out)
    return _call(
        body, name=name,
        out_shape=(jax.ShapeDtypeStruct((T, D), F32), jax.ShapeDtypeStruct((T, D), F32),
                   jax.ShapeDtypeStruct((T, 2 * D), F32), jax.ShapeDtypeStruct((nt, 8, S5_NS), F32)),
        grid=(nt,),
        in_specs=[pl.BlockSpec((tm, D), lambda i: (i, 0)), pl.BlockSpec((tm, D), lambda i: (i, 0))]
        + [full(a) for a in args[2:]],
        out_specs=(pl.BlockSpec((tm, D), lambda i: (i, 0)), pl.BlockSpec((tm, D), lambda i: (i, 0)),
                   pl.BlockSpec((tm, 2 * D), lambda i: (i, 0)), pl.BlockSpec((None, 8, S5_NS), lambda i: (i, 0, 0))),
        scratch_shapes=[pltpu.VMEM((tm, S5_NS), F32), pltpu.VMEM((8, S5_NS), F32)],
        compiler_params=_params(("arbitrary",)),
    )(*args)


def s5_bwd_glu(dout, gl, ypre, u, w_glu, w_out, tm, name):
    T, D = u.shape

    def body(do_ref, gl_ref, yp_ref, u_ref, wg_ref, wo_ref, dy_ref, oact_ref, dgl_ref, gact_ref, dd_ref):
        @pl.when(pl.program_id(0) == 0)
        def _():
            dd_ref[...] = jnp.zeros_like(dd_ref)

        gl1 = gl_ref[:, 0:D]
        sg = _sigmoid(gl_ref[:, D:2 * D])
        oact_ref[...] = (gl1 * sg).astype(MXU_DTYPE)
        dgo = _dot_nt(do_ref[...], wo_ref[...])
        d1 = (dgo * sg).astype(MXU_DTYPE)
        d2 = (dgo * gl1 * sg * (1.0 - sg)).astype(MXU_DTYPE)
        dgl_ref[:, 0:D] = d1
        dgl_ref[:, D:2 * D] = d2
        dg = _dot_nt(d1, wg_ref[:, 0:D]) + _dot_nt(d2, wg_ref[:, D:2 * D])
        g, gd = _gelu_and_grad(yp_ref[...])
        gact_ref[...] = g.astype(MXU_DTYPE)
        dy = dg * gd
        dy_ref[...] = dy
        dd_ref[...] += _rowsum8(dy * u_ref[...])

    full = lambda a: pl.BlockSpec(a.shape, lambda i: (0,) * a.ndim)
    rs = lambda c: pl.BlockSpec((tm, c), lambda i: (i, 0))
    return _call(
        body, name=name,
        out_shape=(jax.ShapeDtypeStruct((T, D), F32), jax.ShapeDtypeStruct((T, D), MXU_DTYPE),
                   jax.ShapeDtypeStruct((T, 2 * D), MXU_DTYPE), jax.ShapeDtypeStruct((T, D), MXU_DTYPE),
                   jax.ShapeDtypeStruct((8, D), F32)),
        grid=(T // tm,),
        in_specs=[rs(D), rs(2 * D), rs(D), rs(D), full(w_glu), full(w_out)],
        out_specs=(rs(D), rs(D), rs(2 * D), rs(D), pl.BlockSpec((8, D), lambda i: (0, 0))),
        compiler_params=_params(("arbitrary",)),
    )(dout, gl, ypre, u, w_glu, w_out)


def s5_bwd_ssm(dy, u, st, wb, wc, ap_r, ap_i, aq_r, aq_i, d, tm, name, comm=None):
    T, D = u.shape
    nt = T // tm
    seg = tm // 8
    GP = S5_G * S5_P
    flip = lambda t: jnp.concatenate([t[k:k + 1] for k in reversed(range(t.shape[0]))], 0)
    tables = (ap_r, ap_i, aq_r, aq_i, flip(ap_r), flip(ap_i), flip(aq_r), flip(aq_i))

    def body(dy_ref, u_ref, st_ref, wb_ref, wc_ref, apr_ref, api_ref, aqr_ref, aqi_ref, aprr_ref, apir_ref,
             aqrr_ref, aqir_ref, d_ref, du_ref, dwb_ref, dwc_ref, dar_ref, dai_ref, S, L, carry, lcarry, cin):
        @pl.when(pl.program_id(0) == 0)
        def _():
            for ref in (lcarry, dwb_ref, dwc_ref, dar_ref, dai_ref):
                ref[...] = jnp.zeros_like(ref)

        uv = _regroup(u_ref[...], seg)
        ub = uv.astype(MXU_DTYPE)
        dyv = _regroup(dy_ref[...], seg)
        dyb = dyv.astype(MXU_DTYPE)
        carry[...] = st_ref[...]
        for sb in range(S5_NSB):
            S[:, sb * _SBW:(sb + 1) * _SBW] = jnp.dot(ub[:, sb * _SBU:(sb + 1) * _SBU], wb_ref[sb].astype(MXU_DTYPE),
                                                      preferred_element_type=F32)
            L[:, sb * _SBW:(sb + 1) * _SBW] = _dot_nt(dyb[:, sb * _SBU:(sb + 1) * _SBU], wc_ref[sb])
        _s5_scan_fwd(S, carry, cin, apr_ref, api_ref, aqr_ref, aqi_ref, seg)
        row8 = lax.broadcasted_iota(jnp.int32, (8, 1), 0)
        for sb in range(S5_NSB):
            lr, li, la = _s5_lanes(sb)
            ar, ai = apr_ref[0:1, la], api_ref[0:1, la]
            last = slice((seg - 1) * 8, seg * 8)
            gr, gi = L[last, lr], L[last, li]
            for i in reversed(range(seg - 1)):
                blk = slice(i * 8, (i + 1) * 8)
                gr, gi = ar * gr + ai * gi + L[blk, lr], ar * gi - ai * gr + L[blk, li]
                L[blk, lr] = gr
                L[blk, li] = gi
            for s, idx in ((1, 0), (2, 1), (4, 3)):
                qr, qi = aqr_ref[idx:idx + 1, la], aqi_ref[idx:idx + 1, la]
                m = row8 < 8 - s
                sr = jnp.where(m, pltpu.roll(gr, 8 - s, 0), 0.0)
                si = jnp.where(m, pltpu.roll(gi, 8 - s, 0), 0.0)
                gr, gi = gr + qr * sr + qi * si, gi + qr * si - qi * sr
            cr, ci = lcarry[0:1, lr], lcarry[0:1, li]
            pr, pi = aqrr_ref[:, la], aqir_ref[:, la]
            gr, gi = gr + pr * cr + pi * ci, gi + pr * ci - pi * cr
            xr = jnp.where(row8 == 7, cr, pltpu.roll(gr, 7, 0))
            xi = jnp.where(row8 == 7, ci, pltpu.roll(gi, 7, 0))
            lcarry[0:1, lr] = gr[0:1, :]
            lcarry[0:1, li] = gi[0:1, :]
            acc_r = jnp.zeros((8, _SBH), F32)
            acc_i = jnp.zeros((8, _SBH), F32)
            for i in range(seg):
                blk = slice(i * 8, (i + 1) * 8)
                pr, pi = aprr_ref[i:i + 1, la], apir_ref[i:i + 1, la]
                gr = L[blk, lr] + pr * xr + pi * xi
                gi = L[blk, li] + pr * xi - pi * xr
                L[blk, lr] = gr
                L[blk, li] = gi
                if i == 0:
                    hpr, hpi = cin[:, lr], cin[:, li]
                else:
                    hpr, hpi = S[(i - 1) * 8:i * 8, lr], S[(i - 1) * 8:i * 8, li]
                acc_r = acc_r + gr * hpr + gi * hpi
                acc_i = acc_i + gi * hpr - gr * hpi
            dar_ref[:, la] += acc_r
            dai_ref[:, la] += acc_i
        dus = []
        for sb in range(S5_NSB):
            ls = slice(sb * _SBW, (sb + 1) * _SBW)
            us = slice(sb * _SBU, (sb + 1) * _SBU)
            lb = L[:, ls].astype(MXU_DTYPE)
            dwb_ref[sb] += _dot_tn(lb, ub[:, us])
            dwc_ref[sb] += _dot_tn(S[:, ls].astype(MXU_DTYPE), dyb[:, us])
            dus.append(_dot_nt(lb, wb_ref[sb]))
        du_ref[...] = _ungroup(jnp.concatenate(dus, axis=1) + dyv * d_ref[...], seg).astype(MXU_DTYPE)

    rt = lambda i: nt - 1 - i
    full = lambda a: pl.BlockSpec(a.shape, lambda i: (0,) * a.ndim)
    acc = lambda shape: pl.BlockSpec(shape, lambda i: (0,) * len(shape))
    return _hosted(
        body, comm, (dy, u, st, wb, wc, *tables, d), name=name,
        out_shape=(jax.ShapeDtypeStruct((T, D), MXU_DTYPE), jax.ShapeDtypeStruct((S5_NSB, _SBW, _SBU), F32),
                   jax.ShapeDtypeStruct((S5_NSB, _SBW, _SBU), F32), jax.ShapeDtypeStruct((8, GP), F32),
                   jax.ShapeDtypeStruct((8, GP), F32)),
        grid=(nt,),
        in_specs=[pl.BlockSpec((tm, D), lambda i: (rt(i), 0)), pl.BlockSpec((tm, D), lambda i: (rt(i), 0)),
                  pl.BlockSpec((None, 8, S5_NS), lambda i: (rt(i), 0, 0)), full(wb), full(wc)]
        + [full(t) for t in tables] + [full(d)],
        out_specs=(pl.BlockSpec((tm, D), lambda i: (rt(i), 0)), acc((S5_NSB, _SBW, _SBU)), acc((S5_NSB, _SBW, _SBU)),
                   acc((8, GP)), acc((8, GP))),
        scratch_shapes=[pltpu.VMEM((tm, S5_NS), F32), pltpu.VMEM((tm, S5_NS), F32), pltpu.VMEM((8, S5_NS), F32),
                        pltpu.VMEM((8, S5_NS), F32), pltpu.VMEM((8, S5_NS), F32)],
        compiler_params=_params(("arbitrary",)),
    )


def final_loss(h, g, target, tm, name):
    T, D = h.shape

    def body(h_ref, g_ref, t_ref, dh_ref, se_ref, dg_ref):
        @pl.when(pl.program_id(0) == 0)
        def _():
            se_ref[...] = jnp.zeros_like(se_ref)
            dg_ref[...] = jnp.zeros_like(dg_ref)

        r, xhat = _rms(h_ref[...])
        gv = g_ref[...]
        e = xhat * gv - t_ref[...]
        se_ref[...] += _rowsum8(e * e)
        dy = e * (1.0 / D)
        dg_ref[...] += _rowsum8(dy * xhat)
        dh_ref[...] = _rms_bwd(dy, xhat, r, gv)

    rs = pl.BlockSpec((tm, D), lambda i: (i, 0))
    acc = pl.BlockSpec((8, D), lambda i: (0, 0))
    return _call(
        body, name=name,
        out_shape=(jax.ShapeDtypeStruct((T, D), F32), jax.ShapeDtypeStruct((8, D), F32), jax.ShapeDtypeStruct((8, D), F32)),
        grid=(T // tm,), in_specs=[rs, pl.BlockSpec((1, D), lambda i: (0, 0)), rs], out_specs=(rs, acc, acc),
        compiler_params=_params(("arbitrary",)),
    )(h, g, target)


def _s5_discretize(a_re, a_im, log_dt, b_re, b_im):
    dt = jnp.exp(log_dt)[:, None]
    mag = jnp.exp(a_re * dt)
    abr = mag * jnp.cos(a_im * dt)
    abi = mag * jnp.sin(a_im * dt)
    ur, ui = abr - 1.0, abi
    den = a_re * a_re + a_im * a_im
    wr = (ur * a_re + ui * a_im) / den
    wi = (ui * a_re - ur * a_im) / den
    bbr = wr[..., None] * b_re - wi[..., None] * b_im
    bbi = wr[..., None] * b_im + wi[..., None] * b_re
    return abr, abi, bbr, bbi


def _s5_pack(abr, abi, bbr, bbi, c_re, c_im, seg_len):
    eye = jnp.eye(S5_SB, dtype=F32)
    b = jnp.stack([bbr, bbi], 0).reshape(2, S5_NSB, S5_SB, S5_P, S5_GC)
    wb = jnp.einsum('rsgpc,gh->shcrgp', b, eye).reshape(S5_NSB, _SBU, _SBW)
    c = jnp.stack([c_re, -c_im], 0).reshape(2, S5_NSB, S5_SB, S5_GC, S5_P)
    wc = jnp.einsum('rsgcp,gh->srgphc', c, eye).reshape(S5_NSB, _SBW, _SBU)
    def powers(r, i, n):
        pr, pi = [r], [i]
        for _ in range(n - 1):
            pr.append(pr[-1] * r - pi[-1] * i)
            pi.append(pr[-2] * i + pi[-1] * r)
        return jnp.concatenate(pr, 0), jnp.concatenate(pi, 0)

    ap_r, ap_i = powers(abr.reshape(1, -1), abi.reshape(1, -1), seg_len)
    aq_r, aq_i = powers(ap_r[seg_len - 1:seg_len], ap_i[seg_len - 1:seg_len], 8)
    return wb.astype(MXU_DTYPE), wc.astype(MXU_DTYPE), ap_r, ap_i, aq_r, aq_i


def _s5_unpack_grads(dwb_t, dwc, dar8, dai8):
    eye = jnp.eye(S5_SB, dtype=F32)
    t = dwb_t.reshape(S5_NSB, 2, S5_SB, S5_P, S5_SB, S5_GC)
    db = jnp.einsum('srgphc,gh->rsgpc', t, eye).reshape(2, S5_G, S5_P, S5_GC)
    t = dwc.reshape(S5_NSB, 2, S5_SB, S5_P, S5_SB, S5_GC)
    dc = jnp.einsum('srgphc,gh->rsgcp', t, eye).reshape(2, S5_G, S5_GC, S5_P)
    return db[0], db[1], dc[0], -dc[1], dar8.sum(0).reshape(S5_G, S5_P), dai8.sum(0).reshape(S5_G, S5_P)


TM = 256
TM_S5 = 256


def _tn(a, b, name):
    T, M, N = a.shape[0], a.shape[1], b.shape[1]
    bt = 4096 if a.dtype.itemsize + b.dtype.itemsize <= 4 else 2048
    return matmul_tn(a, b, min(M, 1024), min(N, 1024), bt if T % bt == 0 else T, name)


def local_step(x, target, W, sched):
    T, D = x.shape
    depth = W['norm_mix_g'].shape[0]
    row = lambda v: v.reshape(1, -1)
    saved = []
    h = x
    s5c = []
    tr = min(512, T)
    for j in range(W['s5_a_re'].shape[0]):
        prm = (W['s5_a_re'][j], W['s5_a_im'][j], W['s5_log_dt'][j], W['s5_b_re'][j], W['s5_b_im'][j])
        disc, disc_vjp = jax.vjp(_s5_discretize, *prm)
        s5c.append((*_s5_pack(*disc, W['s5_c_re'][j], W['s5_c_im'][j], min(TM_S5, T) // 8), disc_vjp))
    sp_all = jax.nn.softplus(-W['rg_lambda'])
    for i in range(depth):
        j = i // 2
        if i % 2 == 0:
            xg, hn = norm_matmul(h, row(W['norm_mix_g'][i]), W['rg_w_in'][j], F32, tr, None, f"rg_in_{i}")
            h1, hs = rg_fwd(xg, h, W['rg_conv_w'][j], row(W['rg_conv_b'][j]), W['rg_w_a'][j].astype(MXU_DTYPE),
                            row(W['rg_b_a'][j]), W['rg_w_x'][j].astype(MXU_DTYPE), row(W['rg_b_x'][j]), row(sp_all[j]),
                            W['rg_w_out'][j], TM, f"rg_fwd_{i}", comm=sched.comm(f"rg_fwd_{i}"))
            mix = (xg, hn, hs)
        else:
            u, hn = norm_matmul(h, row(W['norm_mix_g'][i]), W['s5_w_in'][j], F32, tr, None, f"s5_in_{i}")
            h1, ypre, gl, st = s5_fwd(u, h, *s5c[j][:-1], row(W['s5_d'][j]), W['s5_w_glu'][j], W['s5_w_out'][j],
                                      min(TM_S5, T), f"s5_fwd_{i}")
            mix = (u, hn, ypre, gl, st)
        up, hn2 = norm_matmul(h1, row(W['norm_ffn_g'][i]), W['ffn_w_up'][i], MXU_DTYPE, tr, None, f"ffn_up_{i}",
                              comm=sched.comm(f"ffn_up_{i}"))
        h2, av = ffn_fwd(up, h1, W['ffn_conv_w'][i], row(W['ffn_conv_b'][i]), W['ffn_w_down'][i], TM, f"ffn_fwd_{i}",
                         comm=sched.comm(f"ffn_fwd_{i}"))
        saved.append((h, mix, h1, hn2, up, av))
        h = h2
    dh, se8, dgf8 = final_loss(h, row(W['norm_final_g']), target, tr, "final_loss")
    G = {k: [None] * len(v) for k, v in W.items() if k != 'norm_final_g'}
    G['norm_final_g'] = dgf8.sum(0)
    for i in reversed(range(depth)):
        j = i // 2
        h0, mix, h1, hn2, up, av = saved[i]
        dup, act, dcb8 = ffn_bwd_act(dh, av, W['ffn_w_down'][i], TM, f"ffn_bwd_act_{i}",
                                     comm=sched.comm(f"ffn_bwd_act_{i}", G))
        G['ffn_w_down'][i] = _tn(act, dh, f"ffn_dwdown_{i}")
        dh1, dg8, dupp, dcw8 = dx_norm_bwd(dup, W['ffn_w_up'][i], h1, row(W['norm_ffn_g'][i]), dh, TM,
                                           f"ffn_bwd_in_{i}", conv_w=W['ffn_conv_w'][i], conv_x=up)
        G['ffn_w_up'][i] = _tn(hn2, dupp, f"ffn_dwup_{i}")
        G['ffn_conv_b'][i] = dcb8.sum(0)
        G['ffn_conv_w'][i] = dcw8.sum(1)
        G['norm_ffn_g'][i] = dg8.sum(0)
        if i % 2 == 0:
            xg, hn, hs = mix
            dxg, y, dwa, dwx, dba8, dbx8, dsp8, dcb8, dcw8 = rg_bwd(
                dh1, xg, hs, W['rg_conv_w'][j], row(W['rg_conv_b'][j]), W['rg_w_a'][j].astype(MXU_DTYPE),
                row(W['rg_b_a'][j]), W['rg_w_x'][j].astype(MXU_DTYPE), row(W['rg_b_x'][j]), row(sp_all[j]),
                W['rg_w_out'][j], TM, f"rg_bwd_{i}", comm=sched.comm(f"rg_bwd_{i}", G))
            G['rg_w_out'][j] = _tn(y, dh1, f"rg_dwout_{i}")
            dh, dg8 = dx_norm_bwd(dxg, W['rg_w_in'][j], h0, row(W['norm_mix_g'][i]), dh1, TM, f"rg_bwd_in_{i}")
            G['rg_w_in'][j] = _tn(hn, dxg, f"rg_dwin_{i}")
            G['rg_w_a'][j], G['rg_w_x'][j] = dwa, dwx
            G['rg_b_a'][j] = dba8.sum(0).reshape(RG_HEADS, RG_BW)
            G['rg_b_x'][j] = dbx8.sum(0).reshape(RG_HEADS, RG_BW)
            G['rg_lambda'][j] = dsp8.sum(0) * (-jax.nn.sigmoid(-W['rg_lambda'][j]))
            G['rg_conv_b'][j] = dcb8.sum(0)
            G['rg_conv_w'][j] = dcw8.sum(1)
        else:
            u, hn, ypre, gl, st = mix
            disc_vjp = s5c[j][-1]
            dy, oact, dgl, gact, dd8 = s5_bwd_glu(dh1, gl, ypre, u, W['s5_w_glu'][j], W['s5_w_out'][j], TM,
                                                  f"s5_bwd_glu_{i}")
            G['s5_w_out'][j] = _tn(oact, dh1, f"s5_dwout_{i}")
            G['s5_w_glu'][j] = _tn(gact, dgl, f"s5_dwglu_{i}")
            du, dwb_t, dwc, dar8, dai8 = s5_bwd_ssm(dy, u, st, *s5c[j][:-1], row(W['s5_d'][j]),
                                                    min(TM_S5, T), f"s5_bwd_ssm_{i}",
                                                    comm=sched.comm(f"s5_bwd_ssm_{i}", G))
            dh, dg8 = dx_norm_bwd(du, W['s5_w_in'][j], h0, row(W['norm_mix_g'][i]), dh1, TM, f"s5_bwd_in_{i}")
            G['s5_w_in'][j] = _tn(hn, du, f"s5_dwin_{i}")
            dbbr, dbbi, dcr, dci, dabr, dabi = _s5_unpack_grads(dwb_t, dwc, dar8, dai8)
            da_re, da_im, dlog_dt, db_re, db_im = disc_vjp((dabr, dabi, dbbr, dbbi))
            G['s5_a_re'][j], G['s5_a_im'][j], G['s5_log_dt'][j] = da_re, da_im, dlog_dt
            G['s5_b_re'][j], G['s5_b_im'][j], G['s5_c_re'][j], G['s5_c_im'][j] = db_re, db_im, dcr, dci
            G['s5_d'][j] = dd8.sum(0)
        G['norm_mix_g'][i] = dg8.sum(0)
    G = {k: (v if (k == 'norm_final_g' or k in BIG) else jnp.stack(v, 0)) for k, v in G.items()}
    return se8, dh, G


BIG = {'rg_w_in': 1, 'rg_w_out': 0, 's5_w_in': 0, 's5_w_glu': 1, 's5_w_out': 0, 'ffn_w_up': 1, 'ffn_w_down': 0}
SMALL_SHARDED = ('rg_conv_w', 'ffn_conv_w', 's5_d')
REPLICATED = ('norm_mix_g', 'norm_ffn_g', 'norm_final_g', 'rg_conv_b', 'rg_w_a', 'rg_b_a', 'rg_w_x', 'rg_b_x',
              'rg_lambda', 's5_a_re', 's5_a_im', 's5_log_dt', 's5_b_re', 's5_b_im', 's5_c_re', 's5_c_im', 'ffn_conv_b')


def _me():
    x, y, c = lax.axis_index("x"), lax.axis_index("y"), lax.axis_index("c")
    return x, y, c, 4 * x + 2 * y + c


def _win(ref, axis, dev, width):
    idx = [slice(None)] * len(ref.shape)
    idx[axis] = pl.ds(pl.multiple_of(dev * width, width), width)
    return ref.at[tuple(idx)]


def gather_plan(items, deliver):
    n = len(items)

    def tools(ins, outs, sems):
        send_sems, recv_sems, local_sems = sems
        x, y, c, me = _me()
        sib = (x, y, 1 - c)
        chips = [(1 - x, y), (x, 1 - y), (1 - x, 1 - y)]
        num = lambda px, py, pc: 4 * px + 2 * py + pc

        def src_of(a):
            return ins[a] if items[a][1] is None else ins[a].at[items[a][1]]

        def block(a, dev):
            return _win(outs[a], items[a][2], dev, src_of(a).shape[items[a][2]])

        def copy(a, k, dev, to, own=False):
            return pltpu.make_async_remote_copy(
                src_ref=src_of(a) if own else block(a, dev), dst_ref=block(a, dev),
                send_sem=send_sems.at[a, k], recv_sem=recv_sems.at[a, k], device_id=to, device_id_type=MESH)

        mine = lambda: [pltpu.make_async_copy(src_of(a), block(a, me), local_sems.at[a]) for a in range(n)]
        own = lambda: [cp for a in range(n) for cp in
                       [copy(a, 0, me, sib, own=True)] + [copy(a, 1 + j, me, (*chip, c), own=True)
                                                          for j, chip in enumerate(chips)]]
        arrived = lambda j, a: copy(a, 1 + j, num(*chips[j], c), (x, y, c))
        passed = lambda j, a: copy(a, 4 + j, num(*chips[j], c), sib)
        from_sib = lambda: ([copy(a, 0, num(x, y, 1 - c), (x, y, c)) for a in range(n)]
                            + [copy(a, 4 + j, num(*chip, 1 - c), (x, y, c)) for j, chip in enumerate(chips)
                               for a in range(n)])
        return mine, own, arrived, passed, from_sib

    def start(ins, outs, sems):
        mine, own, _, _, _ = tools(ins, outs, sems)
        for cp in mine() + own():
            cp.start()

    def middle(ins, outs, sems):
        _, _, arrived, passed, _ = tools(ins, outs, sems)
        for j in range(3):
            for a in range(n):
                arrived(j, a).wait_recv()
                passed(j, a).start()

    def finish(ins, outs, sems):
        mine, own, _, passed, from_sib = tools(ins, outs, sems)
        for cp in from_sib():
            cp.wait_recv()
        for cp in own() + [passed(j, a) for j in range(3) for a in range(n)]:
            cp.wait_send()
        for cp in mine():
            cp.wait()

    return Comm([it[0] for it in items], [jax.ShapeDtypeStruct(it[3], it[0].dtype) for it in items],
                [pltpu.SemaphoreType.DMA((n, 7)), pltpu.SemaphoreType.DMA((n, 7)), pltpu.SemaphoreType.DMA((n,))],
                start, middle, finish, deliver)


def exchange_plan(items, deliver):
    n = len(items)
    width = [arr.shape[axis] // N_DEV for arr, axis in items]
    shard = [arr.shape[:axis] + (w,) + arr.shape[axis + 1:] for (arr, axis), w in zip(items, width)]

    def tools(ins, outs, sems):
        send_sems, recv_sems, local_sems = sems
        x, y, c, me = _me()
        piece = lambda a, dev: _win(ins[a], items[a][1], dev, width[a])
        mine = lambda: [pltpu.make_async_copy(piece(a, me), outs[a].at[me], local_sems.at[a]) for a in range(n)]

        def remote(sending):
            cps = []
            for k in range(1, N_DEV):
                px, py, pc = (1 - x) if k & 4 else x, (1 - y) if k & 2 else y, (1 - c) if k & 1 else c
                peer = 4 * px + 2 * py + pc
                for a in range(n):
                    src, dst = (piece(a, peer), outs[a].at[me]) if sending else (piece(a, me), outs[a].at[peer])
                    cps.append(pltpu.make_async_remote_copy(
                        src_ref=src, dst_ref=dst, send_sem=send_sems.at[a, k - 1], recv_sem=recv_sems.at[a, k - 1],
                        device_id=(px, py, pc), device_id_type=MESH))
            return cps

        return mine, remote

    def start(ins, outs, sems):
        mine, remote = tools(ins, outs, sems)
        for cp in mine() + remote(True):
            cp.start()

    def middle(ins, outs, sems):
        pass

    def finish(ins, outs, sems):
        mine, remote = tools(ins, outs, sems)
        for cp in remote(False):
            cp.wait_recv()
        for cp in remote(True):
            cp.wait_send()
        for cp in mine():
            cp.wait()

    return Comm([it[0] for it in items], [jax.ShapeDtypeStruct((N_DEV,) + s, it[0].dtype) for it, s in zip(items, shard)],
                [pltpu.SemaphoreType.DMA((n, 7)), pltpu.SemaphoreType.DMA((n, 7)), pltpu.SemaphoreType.DMA((n,))],
                start, middle, finish, deliver)


def adam_update(parts, w, m, v, name):
    R, C = w.shape
    br = next((b for b in (256, 128) if R > b and R % b == 0), R)
    np_ = parts.shape[0]

    def body(p_ref, w_ref, m_ref, v_ref, g_ref, d_ref, nm_ref, nv_ref):
        _adam_body(np_, p_ref, w_ref, m_ref, v_ref, g_ref, d_ref, nm_ref, nv_ref)

    bs = pl.BlockSpec((br, C), lambda i: (i, 0))
    out = jax.ShapeDtypeStruct((R, C), F32)
    return _call(
        body, name=name, out_shape=(out, out, out, out), grid=(R // br,),
        in_specs=[pl.BlockSpec((np_, br, C), lambda i: (0, i, 0)), bs, bs, bs], out_specs=(bs, bs, bs, bs),
        compiler_params=_params(("parallel",)),
    )(parts, w, m, v)


def _adam_body(np_, p_ref, w_ref, m_ref, v_ref, g_ref, d_ref, nm_ref, nv_ref):
    c1 = 1.0 / (1.0 - ADAM_B1 ** ADAM_STEP)
    c2 = 1.0 / (1.0 - ADAM_B2 ** ADAM_STEP)
    g = p_ref[0].astype(F32)
    for p in range(1, np_):
        g = g + p_ref[p].astype(F32)
    nm = ADAM_B1 * m_ref[...] + (1.0 - ADAM_B1) * g
    nv = ADAM_B2 * v_ref[...] + (1.0 - ADAM_B2) * (g * g)
    g_ref[...] = g
    nm_ref[...] = nm
    nv_ref[...] = nv
    d_ref[...] = -ADAM_LR * ((nm * c1) / (jnp.sqrt(nv * c2) + ADAM_EPS) + ADAM_WD * w_ref[...])


def adam_layer(parts, w, m, v, l, prev, name):
    L, R, C = w.shape
    br = next((b for b in (256, 128) if R > b and R % b == 0), R)

    def body(p_ref, w_ref, m_ref, v_ref, *rest):
        _adam_body(N_DEV, p_ref, w_ref, m_ref, v_ref, *rest[-4:])

    bs = pl.BlockSpec((None, br, C), lambda i: (l, i, 0))
    out = jax.ShapeDtypeStruct((L, R, C), F32)
    extra = {} if prev is None else dict(input_output_aliases={4 + q: q for q in range(4)})
    return _call(
        body, name=name, out_shape=(out, out, out, out), grid=(R // br,),
        in_specs=[pl.BlockSpec((N_DEV, br, C), lambda i: (0, i, 0)), bs, bs, bs] + ([] if prev is None else [ANY] * 4),
        out_specs=(bs, bs, bs, bs), compiler_params=_params(("parallel",)), **extra,
    )(parts, w, m, v, *(() if prev is None else prev))


def sum_parts(parts, name):
    n, R, C = parts.shape

    def body(p_ref, o_ref):
        g = p_ref[0]
        for p in range(1, n):
            g = g + p_ref[p]
        o_ref[...] = g

    return _call(body, name=name, out_shape=jax.ShapeDtypeStruct((R, C), parts.dtype),
                 compiler_params=pltpu.CompilerParams(vmem_limit_bytes=VMEM_LIMIT))(parts)


def _pack_rows(arrs):
    rows = []
    for a in arrs:
        f = a.reshape(-1)
        r = -(-f.shape[0] // 1024)
        r8 = -(-r // 8) * 8
        rows.append(jnp.pad(f, (0, r8 * 1024 - f.shape[0])).reshape(r8, 1024))
    packed = jnp.concatenate(rows, 0)
    return jnp.pad(packed, ((0, -packed.shape[0] % 128), (0, 0)))


def _unpack_rows(packed, shapes):
    out, o = [], 0
    for s in shapes:
        nel = math.prod(s)
        r8 = -(-(-(-nel // 1024)) // 8) * 8
        out.append(packed[o:o + r8].reshape(-1)[:nel].reshape(s))
        o += r8
    return out


def kernel(x, norm_mix_g, norm_ffn_g, norm_final_g, rg_w_in, rg_conv_w, rg_conv_b, rg_w_a, rg_b_a, rg_w_x, rg_b_x, rg_lambda, rg_w_out, s5_w_in, s5_a_re, s5_a_im, s5_log_dt, s5_b_re, s5_b_im, s5_c_re, s5_c_im, s5_d, s5_w_glu, s5_w_out, ffn_w_up, ffn_conv_w, ffn_conv_b, ffn_w_down, loss_target, m_norm_mix_g, m_norm_ffn_g, m_norm_final_g, m_rg_w_in, m_rg_conv_w, m_rg_conv_b, m_rg_w_a, m_rg_b_a, m_rg_w_x, m_rg_b_x, m_rg_lambda, m_rg_w_out, m_s5_w_in, m_s5_a_re, m_s5_a_im, m_s5_log_dt, m_s5_b_re, m_s5_b_im, m_s5_c_re, m_s5_c_im, m_s5_d, m_s5_w_glu, m_s5_w_out, m_ffn_w_up, m_ffn_conv_w, m_ffn_conv_b, m_ffn_w_down, v_norm_mix_g, v_norm_ffn_g, v_norm_final_g, v_rg_w_in, v_rg_conv_w, v_rg_conv_b, v_rg_w_a, v_rg_b_a, v_rg_w_x, v_rg_b_x, v_rg_lambda, v_rg_w_out, v_s5_w_in, v_s5_a_re, v_s5_a_im, v_s5_log_dt, v_s5_b_re, v_s5_b_im, v_s5_c_re, v_s5_c_im, v_s5_d, v_s5_w_glu, v_s5_w_out, v_ffn_w_up, v_ffn_conv_w, v_ffn_conv_b, v_ffn_w_down):
    names = ('norm_mix_g', 'norm_ffn_g', 'norm_final_g', 'rg_w_in', 'rg_conv_w', 'rg_conv_b', 'rg_w_a', 'rg_b_a',
             'rg_w_x', 'rg_b_x', 'rg_lambda', 'rg_w_out', 's5_w_in', 's5_a_re', 's5_a_im', 's5_log_dt', 's5_b_re',
             's5_b_im', 's5_c_re', 's5_c_im', 's5_d', 's5_w_glu', 's5_w_out', 'ffn_w_up', 'ffn_conv_w', 'ffn_conv_b',
             'ffn_w_down')
    loc = locals()
    Wl = {k: loc[k] for k in names}
    Ml = {k: loc['m_' + k] for k in names}
    Vl = {k: loc['v_' + k] for k in names}

    depth = norm_mix_g.shape[0]
    mixer_keys = lambda i: ([('rg_w_in', i // 2), ('rg_w_out', i // 2)] if i % 2 == 0 else
                            [('s5_w_in', i // 2), ('s5_w_glu', i // 2), ('s5_w_out', i // 2)])
    ffn_keys = lambda i: [('ffn_w_up', i), ('ffn_w_down', i)]
    shards = {k: Wl[k].astype(BF16) for k in BIG}
    W = {k: Wl[k] for k in REPLICATED}
    W.update({k: [None] * Wl[k].shape[0] for k in BIG})
    parts = {}

    def gather_of(keys, small=False):
        items = []
        for k, l in keys:
            _, r, c = shards[k].shape
            items.append((shards[k], l, BIG[k], (r * N_DEV, c) if BIG[k] == 0 else (r, c * N_DEV)))
        if small:
            items += [(Wl[k], None, Wl[k].ndim - 1, Wl[k].shape[:-1] + (Wl[k].shape[-1] * N_DEV,)) for k in SMALL_SHARDED]

        def deliver(outs):
            for (k, l), arr in zip(keys, outs):
                W[k][l] = arr
            if small:
                W.update(zip(SMALL_SHARDED, outs[len(keys):]))

        return gather_plan(items, deliver)

    def exchange_of(keys, G, extra=()):
        items = [(G[k][l], BIG[k]) for k, l in keys] + [(arr, axis) for _, arr, axis in extra]
        return exchange_plan(items, lambda outs: parts.update(zip(list(keys) + [e[0] for e in extra], outs)))

    class Sched:
        @staticmethod
        def comm(host, G=None):
            kind, _, i = host.rpartition("_")
            i = int(i)
            if host == "rg_fwd_0":
                return gather_of(ffn_keys(0))
            if kind == "ffn_up" and i + 1 < depth:
                return gather_of(ffn_keys(i + 1)[:1])
            if kind == "ffn_fwd" and i + 1 < depth:
                return gather_of(mixer_keys(i + 1) + ffn_keys(i + 1)[1:])
            if kind == "ffn_bwd_act" and i + 1 < depth:
                return exchange_of(mixer_keys(i + 1), G)
            if kind in ("rg_bwd", "s5_bwd_ssm"):
                return exchange_of(ffn_keys(i), G)
            return None

    run_comm(gather_of(mixer_keys(0), small=True), "gather_first")

    se8, gx, G = local_step(x[0], loss_target[0], W, Sched)
    loss = lax.psum(0.5 * jnp.sum(se8) / x.shape[-1], ("x", "y", "c"))

    rep_shapes = [Wl[k].shape for k in REPLICATED]
    gp = _pack_rows([G[k].astype(F32) for k in REPLICATED])
    extra = [(k, G[k], G[k].ndim - 1) for k in SMALL_SHARDED] + [('replicated', gp, 0)]
    run_comm(exchange_of(mixer_keys(0), G, extra), "exchange_last")
    out_g, out_d, out_m, out_v = {}, {}, {}, {}
    for k in BIG:
        res = None
        for l in range(Wl[k].shape[0]):
            res = adam_layer(parts[(k, l)], Wl[k], Ml[k], Vl[k], l, res, f"adam_{k}_{l}")
        out_g[k], out_d[k], out_m[k], out_v[k] = res
    for k in SMALL_SHARDED:
        shp = Wl[k].shape
        r2 = (math.prod(shp[:-1]), shp[-1])
        res = adam_update(parts[k].reshape((N_DEV,) + r2), Wl[k].reshape(r2), Ml[k].reshape(r2), Vl[k].reshape(r2),
                          f"adam_{k}")
        out_g[k], out_d[k], out_m[k], out_v[k] = [t.reshape(shp) for t in res]
    rsum = sum_parts(parts['replicated'], "sum_replicated")
    run_comm(gather_plan([(rsum, None, 0, gp.shape)], lambda outs: parts.update(rep_full=outs[0])), "gather_small_grads")
    res = adam_update(parts['rep_full'][None], _pack_rows([Wl[k] for k in REPLICATED]),
                      _pack_rows([Ml[k] for k in REPLICATED]), _pack_rows([Vl[k] for k in REPLICATED]), "adam_replicated")
    for dst, packed in zip((out_g, out_d, out_m, out_v), res):
        for k, t in zip(REPLICATED, _unpack_rows(packed, rep_shapes)):
            dst[k] = t
    return (loss, gx[None], *[out_g[k] for k in names], *[out_d[k] for k in names], *[out_m[k] for k in names],
            *[out_v[k] for k in names])
```

```python
import functools
import math

import jax
import jax.numpy as jnp
from jax import lax
from jax.experimental import pallas as pl
from jax.experimental.pallas import tpu as pltpu

F32 = jnp.float32
BF16 = jnp.bfloat16
MXU_DTYPE = jnp.bfloat16

NORM_EPS = 1e-6
RG_C = 8.0
RG_HEADS = 8
RG_BW = 128
S5_G = 64
S5_GC = 16
S5_P = 64
S5_SB = 8
S5_NSB = S5_G // S5_SB
S5_NS = 2 * S5_G * S5_P
ADAM_LR = 0.001
ADAM_B1 = 0.9
ADAM_B2 = 0.999
ADAM_EPS = 1e-08
ADAM_WD = 0.01
ADAM_STEP = 10
N_DEV = 8
VMEM_LIMIT = 56 * 1024 * 1024


def _call(body, **kw):
    return pl.pallas_call(body, **kw)


def _params(sem, vmem=VMEM_LIMIT):
    return pltpu.CompilerParams(dimension_semantics=sem, vmem_limit_bytes=vmem)


MESH = pl.DeviceIdType.MESH
ANY = pl.BlockSpec(memory_space=pl.ANY)


class Comm:
    def __init__(self, operands, out_shape, scratch, start, middle, finish, deliver):
        self.operands, self.out_shape, self.scratch = list(operands), list(out_shape), list(scratch)
        self.start, self.middle, self.finish, self.deliver = start, middle, finish, deliver


def run_comm(comm, name):
    ci, co = len(comm.operands), len(comm.out_shape)

    def body(*refs):
        parts = (refs[:ci], refs[ci:ci + co], refs[ci + co:])
        comm.start(*parts)
        comm.middle(*parts)
        comm.finish(*parts)

    comm.deliver(_call(body, name=name, out_shape=tuple(comm.out_shape), in_specs=[ANY] * ci,
                       out_specs=tuple([ANY] * co), scratch_shapes=comm.scratch)(*comm.operands))


def _hosted(body, comm, args, *, name, out_shape, grid, in_specs, out_specs, scratch_shapes, compiler_params):
    if comm is None:
        return _call(body, name=name, out_shape=tuple(out_shape), grid=grid, in_specs=in_specs,
                     out_specs=tuple(out_specs), scratch_shapes=scratch_shapes, compiler_params=compiler_params)(*args)
    n_in, n_out, n_sc = len(in_specs), len(out_shape), len(scratch_shapes)
    ci, co = len(comm.operands), len(comm.out_shape)
    nsteps = math.prod(grid)
    mid = (2 * nsteps) // 3

    def wrapped(*refs):
        ins, refs = refs[:n_in], refs[n_in:]
        cins, refs = refs[:ci], refs[ci:]
        outs, refs = refs[:n_out], refs[n_out:]
        couts, refs = refs[:co], refs[co:]
        sc, csc = refs[:n_sc], refs[n_sc:]
        step = pl.program_id(0)
        for d in range(1, len(grid)):
            step = step * grid[d] + pl.program_id(d)

        @pl.when(step == 0)
        def _():
            comm.start(cins, couts, csc)

        body(*ins, *outs, *sc)

        @pl.when(step == mid)
        def _():
            comm.middle(cins, couts, csc)

        @pl.when(step == nsteps - 1)
        def _():
            comm.finish(cins, couts, csc)

    res = _call(wrapped, name=name, out_shape=(*out_shape, *comm.out_shape), grid=grid,
                in_specs=[*in_specs, *[ANY] * ci], out_specs=(*out_specs, *[ANY] * co),
                scratch_shapes=[*scratch_shapes, *comm.scratch],
                compiler_params=_params(("arbitrary",) * len(grid)))(*args, *comm.operands)
    comm.deliver(res[n_out:])
    return res[:n_out]


_GELU_C = 0.7978845608028654
_GELU_A = 0.044715


def _gelu(x):
    return 0.5 * x * (1.0 + jnp.tanh(_GELU_C * (x + _GELU_A * x * x * x)))


def _gelu_and_grad(x):
    x2 = x * x
    t = jnp.tanh(_GELU_C * (x + _GELU_A * x2 * x))
    g = 0.5 * x * (1.0 + t)
    dg = 0.5 * (1.0 + t) + 0.5 * x * (1.0 - t * t) * _GELU_C * (1.0 + 3.0 * _GELU_A * x2)
    return g, dg


def _sigmoid(x):
    return 1.0 / (1.0 + jnp.exp(-x))


def _neg_expm1(x):
    series = -x * (1.0 + x * (0.5 + x * (1.0 / 6.0 + x * (1.0 / 24.0 + x * (1.0 / 120.0 + x * (1.0 / 720.0))))))
    return jnp.where(x > -0.1, series, 1.0 - jnp.exp(x))


def _rowsum8(x):
    r, c = x.shape
    return x.reshape(r // 8, 8, c).sum(axis=0)


def _dot(a, b):
    return jnp.dot(a.astype(MXU_DTYPE), b.astype(MXU_DTYPE), preferred_element_type=F32)


def _dot_nt(a, b):
    return lax.dot_general(a.astype(MXU_DTYPE), b.astype(MXU_DTYPE), (((1,), (1,)), ((), ())),
                           preferred_element_type=F32)


def _dot_tn(a, b):
    return lax.dot_general(a.astype(MXU_DTYPE), b.astype(MXU_DTYPE), (((0,), (0,)), ((), ())),
                           preferred_element_type=F32)


def _shift_down(x, s, fills, row):
    y = pltpu.roll(x, s, 0)
    for t in range(s):
        y = jnp.where(row == t, fills[s - 1 - t], y)
    return y


def _shift_up(x, s, fills, row):
    n = x.shape[0]
    y = pltpu.roll(x, n - s, 0)
    for t in range(s):
        y = jnp.where(row == n - s + t, fills[t], y)
    return y


def _rms(x):
    r = lax.rsqrt(jnp.mean(x * x, axis=-1, keepdims=True) + NORM_EPS)
    return r, x * r


def _rms_bwd(dhn, xhat, r, g):
    dz = dhn * g
    return r * (dz - xhat * jnp.mean(dz * xhat, axis=-1, keepdims=True))


def norm_matmul(h, g, w, out_dtype, tm, tn, name, comm=None):
    T, D = h.shape
    N = w.shape[1]
    tn = N if tn is None else tn

    def body(h_ref, g_ref, w_ref, o_ref, hn_ref, hn_s):
        @pl.when(pl.program_id(1) == 0)
        def _():
            _, xhat = _rms(h_ref[...])
            v = (xhat * g_ref[...]).astype(MXU_DTYPE)
            hn_s[...] = v
            hn_ref[...] = v

        o_ref[...] = jnp.dot(hn_s[...], w_ref[...].astype(MXU_DTYPE), preferred_element_type=F32).astype(o_ref.dtype)

    return _hosted(
        body, comm, (h, g, w), name=name,
        out_shape=(jax.ShapeDtypeStruct((T, N), out_dtype), jax.ShapeDtypeStruct((T, D), MXU_DTYPE)),
        grid=(T // tm, N // tn),
        in_specs=[pl.BlockSpec((tm, D), lambda i, j: (i, 0)), pl.BlockSpec((1, D), lambda i, j: (0, 0)),
                  pl.BlockSpec((D, tn), lambda i, j: (0, j))],
        out_specs=(pl.BlockSpec((tm, tn), lambda i, j: (i, j)), pl.BlockSpec((tm, D), lambda i, j: (i, 0))),
        scratch_shapes=[pltpu.VMEM((tm, D), MXU_DTYPE)],
        compiler_params=_params(("parallel", "arbitrary")),
    )


def matmul_tn(a, b, bm, bn, bt, name, out_dtype=BF16, comm=None):
    T, M = a.shape
    N = b.shape[1]
    nk = T // bt

    def body(a_ref, b_ref, o_ref, acc):
        k = pl.program_id(2)

        @pl.when(k == 0)
        def _():
            acc[...] = jnp.zeros_like(acc)

        acc[...] += _dot_tn(a_ref[...], b_ref[...])

        @pl.when(k == nk - 1)
        def _():
            o_ref[...] = acc[...].astype(o_ref.dtype)

    return _hosted(
        body, comm, (a, b), name=name,
        out_shape=(jax.ShapeDtypeStruct((M, N), out_dtype),),
        grid=(M // bm, N // bn, nk),
        in_specs=[pl.BlockSpec((bt, bm), lambda i, j, k: (k, i)), pl.BlockSpec((bt, bn), lambda i, j, k: (k, j))],
        out_specs=(pl.BlockSpec((bm, bn), lambda i, j, k: (i, j)),),
        scratch_shapes=[pltpu.VMEM((bm, bn), F32)],
        compiler_params=_params(("parallel", "parallel", "arbitrary")),
    )[0]


def dx_norm_bwd(dz, w, h, g, dres, tm, name, conv_w=None, conv_x=None, chunk=256, comm=None):
    T, N = dz.shape
    D = w.shape[0]
    nt = T // tm
    has_conv = conv_w is not None
    kw = conv_w.shape[0] if has_conv else 0

    def body(*refs):
        if has_conv:
            dz_ref, cw_ref, x_ref, w_ref, h_ref, g_ref, dres_ref, dh_ref, dg_ref, dzp_ref, dcw_ref, carry = refs
        else:
            dz_ref, w_ref, h_ref, g_ref, dres_ref, dh_ref, dg_ref = refs
        i = pl.program_id(0)

        @pl.when(i == 0)
        def _():
            dg_ref[...] = jnp.zeros_like(dg_ref)
            if has_conv:
                carry[...] = jnp.zeros_like(carry)
                dcw_ref[...] = jnp.zeros_like(dcw_ref)

        if has_conv:
            row = lax.broadcasted_iota(jnp.int32, (tm, 1), 0)
            for c0 in range(0, N, chunk):
                sl = slice(c0, c0 + chunk)
                d0 = dz_ref[:, sl].astype(F32)
                xv = x_ref[:, sl].astype(F32)
                fills = [carry[t:t + 1, sl] for t in range(kw - 1)]
                acc = cw_ref[kw - 1:kw, sl] * d0
                dcw_ref[kw - 1, :, sl] += _rowsum8(d0 * xv)
                for s in range(1, kw):
                    ds = _shift_up(d0, s, fills, row)
                    acc = acc + cw_ref[kw - 1 - s:kw - s, sl] * ds
                    dcw_ref[kw - 1 - s, :, sl] += _rowsum8(ds * xv)
                zb = acc.astype(MXU_DTYPE)
                dzp_ref[:, sl] = zb
                part = _dot_nt(zb, w_ref[:, sl])
                dhn = part if c0 == 0 else dhn + part
            carry[...] = dz_ref[0:16, :].astype(F32)
        else:
            dhn = _dot_nt(dz_ref[...], w_ref[...])
        r, xhat = _rms(h_ref[...])
        dg_ref[...] += _rowsum8(dhn * xhat)
        dh_ref[...] = dres_ref[...] + _rms_bwd(dhn, xhat, r, g_ref[...])

    if has_conv:
        ti = lambda i: nt - 1 - i
    else:
        ti = lambda i: i
    row_spec = lambda c: pl.BlockSpec((tm, c), lambda i: (ti(i), 0))
    full = lambda a: pl.BlockSpec(a.shape, lambda i: (0,) * a.ndim)
    in_specs = [row_spec(N)] + ([full(conv_w), row_spec(N)] if has_conv else []) + [full(w), row_spec(D), full(g), row_spec(D)]
    out_shape = [jax.ShapeDtypeStruct((T, D), F32), jax.ShapeDtypeStruct((8, D), F32)]
    out_specs = [row_spec(D), pl.BlockSpec((8, D), lambda i: (0, 0))]
    scratch = []
    if has_conv:
        out_shape += [jax.ShapeDtypeStruct((T, N), MXU_DTYPE), jax.ShapeDtypeStruct((kw, 8, N), F32)]
        out_specs += [row_spec(N), pl.BlockSpec((kw, 8, N), lambda i: (0, 0, 0))]
        scratch = [pltpu.VMEM((16, N), F32)]
    args = [dz] + ([conv_w, conv_x] if has_conv else []) + [w, h, g, dres]
    return _hosted(
        body, comm, args, name=name, out_shape=tuple(out_shape), grid=(nt,), in_specs=in_specs,
        out_specs=tuple(out_specs), scratch_shapes=scratch, compiler_params=_params(("arbitrary",)),
    )


def _ffn_conv_chunk(up_ref, cw_ref, cb_ref, carry, row, sl):
    x = up_ref[:, sl].astype(F32)
    fills = [carry[15:16, sl], carry[14:15, sl]]
    x1 = _shift_down(x, 1, fills, row)
    x2 = _shift_down(x, 2, fills, row)
    return cb_ref[:, sl] + cw_ref[2:3, sl] * x + cw_ref[1:2, sl] * x1 + cw_ref[0:1, sl] * x2


def ffn_fwd(up, h, conv_w, conv_b, w_down, tm, name, chunk=256, comm=None):
    T, C = up.shape
    F = C // 2
    D = h.shape[1]

    def body(up_ref, h_ref, cw_ref, cb_ref, wd_ref, o_ref, av_ref, carry):
        @pl.when(pl.program_id(0) == 0)
        def _():
            carry[...] = jnp.zeros_like(carry)

        row = lax.broadcasted_iota(jnp.int32, (tm, 1), 0)
        out = h_ref[...]
        for c0 in range(0, F, chunk):
            sa, sv = slice(c0, c0 + chunk), slice(F + c0, F + c0 + chunk)
            a = _ffn_conv_chunk(up_ref, cw_ref, cb_ref, carry, row, sa)
            v = _ffn_conv_chunk(up_ref, cw_ref, cb_ref, carry, row, sv)
            av_ref[:, sa] = a.astype(MXU_DTYPE)
            av_ref[:, sv] = v.astype(MXU_DTYPE)
            out = out + _dot(_gelu(a) * v, wd_ref[sa, :])
        carry[...] = up_ref[tm - 16:tm, :].astype(F32)
        o_ref[...] = out

    full = lambda a: pl.BlockSpec(a.shape, lambda i: (0,) * a.ndim)
    return _hosted(
        body, comm, (up, h, conv_w, conv_b, w_down), name=name,
        out_shape=(jax.ShapeDtypeStruct((T, D), F32), jax.ShapeDtypeStruct((T, C), MXU_DTYPE)),
        grid=(T // tm,),
        in_specs=[pl.BlockSpec((tm, C), lambda i: (i, 0)), pl.BlockSpec((tm, D), lambda i: (i, 0)),
                  full(conv_w), full(conv_b), full(w_down)],
        out_specs=(pl.BlockSpec((tm, D), lambda i: (i, 0)), pl.BlockSpec((tm, C), lambda i: (i, 0))),
        scratch_shapes=[pltpu.VMEM((16, C), F32)],
        compiler_params=_params(("arbitrary",)),
    )


def ffn_bwd_act(dout, av, w_down, tm, name, chunk=512, comm=None):
    T, C = av.shape
    F = C // 2
    D = dout.shape[1]

    def body(do_ref, av_ref, wd_ref, dup_ref, act_ref, dcb_ref):
        @pl.when(pl.program_id(0) == 0)
        def _():
            dcb_ref[...] = jnp.zeros_like(dcb_ref)

        dob = do_ref[...].astype(MXU_DTYPE)
        for c0 in range(0, F, chunk):
            sa, sv = slice(c0, c0 + chunk), slice(F + c0, F + c0 + chunk)
            dact = _dot_nt(dob, wd_ref[sa, :])
            v = av_ref[:, sv].astype(F32)
            ga, dga = _gelu_and_grad(av_ref[:, sa].astype(F32))
            act_ref[:, sa] = (ga * v).astype(MXU_DTYPE)
            da = dact * v * dga
            dv = dact * ga
            dup_ref[:, sa] = da.astype(MXU_DTYPE)
            dup_ref[:, sv] = dv.astype(MXU_DTYPE)
            dcb_ref[:, sa] += _rowsum8(da)
            dcb_ref[:, sv] += _rowsum8(dv)

    full = lambda a: pl.BlockSpec(a.shape, lambda i: (0,) * a.ndim)
    return _hosted(
        body, comm, (dout, av, w_down), name=name,
        out_shape=(jax.ShapeDtypeStruct((T, C), MXU_DTYPE), jax.ShapeDtypeStruct((T, F), MXU_DTYPE),
                   jax.ShapeDtypeStruct((8, C), F32)),
        grid=(T // tm,),
        in_specs=[pl.BlockSpec((tm, D), lambda i: (i, 0)), pl.BlockSpec((tm, C), lambda i: (i, 0)), full(w_down)],
        out_specs=(pl.BlockSpec((tm, C), lambda i: (i, 0)), pl.BlockSpec((tm, F), lambda i: (i, 0)),
                   pl.BlockSpec((8, C), lambda i: (0, 0))),
        scratch_shapes=[],
        compiler_params=_params(("arbitrary",)),
    )


def _rg_gates(xr, wa_ref, ba_ref, wx_ref, bx_ref, sp_ref):
    xb = xr.astype(MXU_DTYPE)
    pa, px = [], []
    for hd in range(RG_HEADS):
        sl = slice(hd * RG_BW, (hd + 1) * RG_BW)
        pa.append(jnp.dot(xb[:, sl], wa_ref[hd].astype(MXU_DTYPE), preferred_element_type=F32))
        px.append(jnp.dot(xb[:, sl], wx_ref[hd].astype(MXU_DTYPE), preferred_element_type=F32))
    r = _sigmoid(jnp.concatenate(pa, axis=1) + ba_ref[...])
    ig = _sigmoid(jnp.concatenate(px, axis=1) + bx_ref[...])
    la = -RG_C * r * sp_ref[...]
    a = jnp.exp(la)
    mult = jnp.sqrt(_neg_expm1(2.0 * la))
    return xb, r, ig, a, mult


def _rg_conv(x, fills, cw_ref, cb_ref, row):
    x1 = _shift_down(x, 1, fills, row)
    x2 = _shift_down(x, 2, fills, row)
    x3 = _shift_down(x, 3, fills, row)
    xr = cb_ref[...] + cw_ref[3:4, :] * x + cw_ref[2:3, :] * x1 + cw_ref[1:2, :] * x2 + cw_ref[0:1, :] * x3
    return xr, (x3, x2, x1, x)


def rg_fwd(xg, h, conv_w, conv_b, w_a, b_a, w_x, b_x, sp, w_out, tm, name, comm=None):
    T, D2 = xg.shape
    D = D2 // 2
    nb = tm // 8

    def body(xg_ref, h_ref, cw_ref, cb_ref, wa_ref, ba_ref, wx_ref, bx_ref, sp_ref, wo_ref, o_ref, hs_ref,
             xcarry, hcarry, a_s, b_s):
        @pl.when(pl.program_id(0) == 0)
        def _():
            xcarry[...] = jnp.zeros_like(xcarry)
            hcarry[...] = jnp.zeros_like(hcarry)

        row = lax.broadcasted_iota(jnp.int32, (tm, 1), 0)
        x = xg_ref[:, 0:D]
        fills = [xcarry[7:8, :], xcarry[6:7, :], xcarry[5:6, :]]
        xr, _ = _rg_conv(x, fills, cw_ref, cb_ref, row)
        xcarry[...] = xg_ref[tm - 8:tm, 0:D]
        _, r, ig, a, mult = _rg_gates(xr, wa_ref, ba_ref, wx_ref, bx_ref, sp_ref)
        a_s[...] = a
        b_s[...] = mult * ig * xr
        row8 = lax.broadcasted_iota(jnp.int32, (8, 1), 0)

        def blk(j, c):
            o = pl.multiple_of(j * 8, 8)
            A = a_s[pl.ds(o, 8), :]
            H = b_s[pl.ds(o, 8), :]
            for s in (1, 2, 4):
                m = row8 >= s
                H = H + A * jnp.where(m, pltpu.roll(H, s, 0), 0.0)
                A = A * jnp.where(m, pltpu.roll(A, s, 0), 1.0)
            H = H + A * c
            hs_ref[pl.ds(o, 8), :] = H
            return H[7:8, :]

        c = lax.fori_loop(0, nb, blk, hcarry[0:1, :])
        hcarry[0:1, :] = c
        y = hs_ref[...] * _gelu(xg_ref[:, D:D2])
        o_ref[...] = h_ref[...] + _dot(y, wo_ref[...])

    full = lambda a: pl.BlockSpec(a.shape, lambda i: (0,) * a.ndim)
    args = (xg, h, conv_w, conv_b, w_a, b_a, w_x, b_x, sp, w_out)
    return _hosted(
        body, comm, args, name=name,
        out_shape=(jax.ShapeDtypeStruct((T, D), F32), jax.ShapeDtypeStruct((T, D), F32)),
        grid=(T // tm,),
        in_specs=[pl.BlockSpec((tm, D2), lambda i: (i, 0)), pl.BlockSpec((tm, D), lambda i: (i, 0))]
        + [full(a) for a in args[2:]],
        out_specs=(pl.BlockSpec((tm, D), lambda i: (i, 0)), pl.BlockSpec((tm, D), lambda i: (i, 0))),
        scratch_shapes=[pltpu.VMEM((8, D), F32), pltpu.VMEM((8, D), F32), pltpu.VMEM((tm, D), F32),
                        pltpu.VMEM((tm, D), F32)],
        compiler_params=_params(("arbitrary",)),
    )


def rg_bwd(dout, xg, hs, conv_w, conv_b, w_a, b_a, w_x, b_x, sp, w_out, tm, name, comm=None):
    T, D2 = xg.shape
    D = D2 // 2
    nt = T // tm
    nb = tm // 8
    kw = conv_w.shape[0]

    def body(do_ref, xg_ref, xh_ref, hs_ref, hh_ref, cw_ref, cb_ref, wa_ref, ba_ref, wx_ref, bx_ref, sp_ref, wo_ref,
             dxg_ref, y_ref, dwa_ref, dwx_ref, dba_ref, dbx_ref, dsp_ref, dcb_ref, dcw_ref,
             acarry, lcarry, dcarry, a_s, b_s, l_s):
        i = pl.program_id(0)
        first_tile = i == nt - 1

        @pl.when(i == 0)
        def _():
            for ref in (acarry, lcarry, dcarry, dwa_ref, dwx_ref, dba_ref, dbx_ref, dsp_ref, dcb_ref, dcw_ref):
                ref[...] = jnp.zeros_like(ref)

        row = lax.broadcasted_iota(jnp.int32, (tm, 1), 0)
        keep = jnp.where(first_tile, 0.0, 1.0)
        x = xg_ref[:, 0:D]
        gate = xg_ref[:, D:D2]
        xh = xh_ref[...] * keep
        fills = [xh[7:8, :], xh[6:7, :], xh[5:6, :]]
        xr, taps = _rg_conv(x, fills, cw_ref, cb_ref, row)
        xb, r, ig, a, mult = _rg_gates(xr, wa_ref, ba_ref, wx_ref, bx_ref, sp_ref)
        hs = hs_ref[...]
        hprev = _shift_down(hs, 1, [hh_ref[7:8, :] * keep], row)
        dy = _dot_nt(do_ref[...], wo_ref[...])
        gg, dgg = _gelu_and_grad(gate)
        y_ref[...] = (hs * gg).astype(MXU_DTYPE)
        dxg_ref[:, D:D2] = (dy * hs * dgg).astype(MXU_DTYPE)
        a_s[...] = _shift_up(a, 1, [acarry[0:1, :]], row)
        b_s[...] = dy * gg
        row8 = lax.broadcasted_iota(jnp.int32, (8, 1), 0)

        def blk(jj, c):
            o = pl.multiple_of((nb - 1 - jj) * 8, 8)
            A = a_s[pl.ds(o, 8), :]
            H = b_s[pl.ds(o, 8), :]
            for s in (1, 2, 4):
                m = row8 < 8 - s
                H = H + A * jnp.where(m, pltpu.roll(H, 8 - s, 0), 0.0)
                A = A * jnp.where(m, pltpu.roll(A, 8 - s, 0), 1.0)
            H = H + A * c
            l_s[pl.ds(o, 8), :] = H
            return H[0:1, :]

        c = lax.fori_loop(0, nb, blk, lcarry[0:1, :])
        lcarry[0:1, :] = c
        acarry[0:1, :] = a[0:1, :]
        lam = l_s[...]
        dla = lam * hprev * a - (lam * ig * xr) * (a * a) / mult
        dig = lam * mult * xr
        dxr = lam * mult * ig
        spv = sp_ref[...]
        dsp_ref[...] += _rowsum8(dla * (-RG_C) * r)
        dpa = (dla * (-RG_C) * spv) * r * (1.0 - r)
        dpx = dig * ig * (1.0 - ig)
        dba_ref[...] += _rowsum8(dpa)
        dbx_ref[...] += _rowsum8(dpx)
        dpab = dpa.astype(MXU_DTYPE)
        dpxb = dpx.astype(MXU_DTYPE)
        back = []
        for hd in range(RG_HEADS):
            sl = slice(hd * RG_BW, (hd + 1) * RG_BW)
            dwa_ref[hd] += _dot_tn(xb[:, sl], dpab[:, sl])
            dwx_ref[hd] += _dot_tn(xb[:, sl], dpxb[:, sl])
            back.append(_dot_nt(dpab[:, sl], wa_ref[hd]) + _dot_nt(dpxb[:, sl], wx_ref[hd]))
        dxr = dxr + jnp.concatenate(back, axis=1)
        nfills = [dcarry[0:1, :], dcarry[1:2, :], dcarry[2:3, :]]
        dxp = cw_ref[kw - 1:kw, :] * dxr
        for s in range(1, kw):
            dxp = dxp + cw_ref[kw - 1 - s:kw - s, :] * _shift_up(dxr, s, nfills, row)
        dcarry[...] = dxr[0:8, :]
        dxg_ref[:, 0:D] = dxp.astype(MXU_DTYPE)
        dcb_ref[...] += _rowsum8(dxr)
        for k in range(kw):
            dcw_ref[k] += _rowsum8(dxr * taps[k])

    rt = lambda i: nt - 1 - i
    halo = lambda i: jnp.maximum((nt - 1 - i) * (tm // 8) - 1, 0)
    full = lambda a: pl.BlockSpec(a.shape, lambda i: (0,) * a.ndim)
    params = (conv_w, conv_b, w_a, b_a, w_x, b_x, sp, w_out)
    acc = lambda shape: pl.BlockSpec(shape, lambda i: (0,) * len(shape))
    return _hosted(
        body, comm, (dout, xg, xg, hs, hs, *params), name=name,
        out_shape=(jax.ShapeDtypeStruct((T, D2), MXU_DTYPE), jax.ShapeDtypeStruct((T, D), MXU_DTYPE),
                   jax.ShapeDtypeStruct((RG_HEADS, RG_BW, RG_BW), F32), jax.ShapeDtypeStruct((RG_HEADS, RG_BW, RG_BW), F32),
                   jax.ShapeDtypeStruct((8, D), F32), jax.ShapeDtypeStruct((8, D), F32), jax.ShapeDtypeStruct((8, D), F32),
                   jax.ShapeDtypeStruct((8, D), F32), jax.ShapeDtypeStruct((kw, 8, D), F32)),
        grid=(nt,),
        in_specs=[pl.BlockSpec((tm, D), lambda i: (rt(i), 0)), pl.BlockSpec((tm, D2), lambda i: (rt(i), 0)),
                  pl.BlockSpec((8, D), lambda i: (halo(i), 0)), pl.BlockSpec((tm, D), lambda i: (rt(i), 0)),
                  pl.BlockSpec((8, D), lambda i: (halo(i), 0))] + [full(a) for a in params],
        out_specs=(pl.BlockSpec((tm, D2), lambda i: (rt(i), 0)), pl.BlockSpec((tm, D), lambda i: (rt(i), 0)),
                   acc((RG_HEADS, RG_BW, RG_BW)), acc((RG_HEADS, RG_BW, RG_BW)), acc((8, D)), acc((8, D)), acc((8, D)),
                   acc((8, D)), acc((kw, 8, D))),
        scratch_shapes=[pltpu.VMEM((8, D), F32), pltpu.VMEM((8, D), F32), pltpu.VMEM((8, D), F32),
                        pltpu.VMEM((tm, D), F32), pltpu.VMEM((tm, D), F32), pltpu.VMEM((tm, D), F32)],
        compiler_params=_params(("arbitrary",)),
    )


_SBW = 2 * S5_SB * S5_P
_SBH = S5_SB * S5_P
_SBU = S5_SB * S5_GC


def _regroup(x, seg_len):
    n, c = x.shape
    return jnp.swapaxes(x.reshape(8, seg_len, c), 0, 1).reshape(n, c)


def _ungroup(x, seg_len):
    n, c = x.shape
    return jnp.swapaxes(x.reshape(seg_len, 8, c), 0, 1).reshape(n, c)


def _s5_lanes(sb):
    return (slice(sb * _SBW, sb * _SBW + _SBH), slice(sb * _SBW + _SBH, (sb + 1) * _SBW),
            slice(sb * _SBH, (sb + 1) * _SBH))


def _s5_scan_fwd(S, carry, cin, ap_r, ap_i, aq_r, aq_i, seg_len, sb):
    row8 = lax.broadcasted_iota(jnp.int32, (8, 1), 0)
    lr, li, la = _s5_lanes(sb)
    ar, ai = ap_r[0:1, la], ap_i[0:1, la]
    hr, hi = S[0:8, lr], S[0:8, li]
    for i in range(1, seg_len):
        blk = slice(i * 8, (i + 1) * 8)
        hr, hi = ar * hr - ai * hi + S[blk, lr], ar * hi + ai * hr + S[blk, li]
        S[blk, lr] = hr
        S[blk, li] = hi
    for s, idx in ((1, 0), (2, 1), (4, 3)):
        qr, qi = aq_r[idx:idx + 1, la], aq_i[idx:idx + 1, la]
        m = row8 >= s
        sr = jnp.where(m, pltpu.roll(hr, s, 0), 0.0)
        si = jnp.where(m, pltpu.roll(hi, s, 0), 0.0)
        hr, hi = hr + qr * sr - qi * si, hi + qr * si + qi * sr
    cr, ci = carry[0:1, lr], carry[0:1, li]
    pr, pi = aq_r[:, la], aq_i[:, la]
    hr, hi = hr + pr * cr - pi * ci, hi + pr * ci + pi * cr
    xr = jnp.where(row8 == 0, cr, pltpu.roll(hr, 1, 0))
    xi = jnp.where(row8 == 0, ci, pltpu.roll(hi, 1, 0))
    carry[0:1, lr] = hr[7:8, :]
    carry[0:1, li] = hi[7:8, :]
    if cin is not None:
        cin[:, lr] = xr
        cin[:, li] = xi
    for i in range(seg_len):
        blk = slice(i * 8, (i + 1) * 8)
        pr, pi = ap_r[i:i + 1, la], ap_i[i:i + 1, la]
        S[blk, lr] += pr * xr - pi * xi
        S[blk, li] += pr * xi + pi * xr


def s5_fwd(u, h, wb, wc, ap_r, ap_i, aq_r, aq_i, d, w_glu, w_out, tm, name):
    T, D = u.shape
    nt = T // tm
    seg = tm // 8

    def body(u_ref, h_ref, wb_ref, wc_ref, apr_ref, api_ref, aqr_ref, aqi_ref, d_ref, wg_ref, wo_ref,
             o_ref, yp_ref, gl_ref, st_ref, S, carry):
        @pl.when(pl.program_id(0) == 0)
        def _():
            carry[...] = jnp.zeros_like(carry)

        st_ref[...] = carry[...]
        uv = _regroup(u_ref[...], seg)
        ub = uv.astype(MXU_DTYPE)
        for sb in range(S5_NSB):
            S[:, sb * _SBW:(sb + 1) * _SBW] = jnp.dot(ub[:, sb * _SBU:(sb + 1) * _SBU], wb_ref[sb].astype(MXU_DTYPE),
                                                      preferred_element_type=F32)
        for sb in range(S5_NSB):
            _s5_scan_fwd(S, carry, None, apr_ref, api_ref, aqr_ref, aqi_ref, seg, sb)
        ys = [jnp.dot(S[:, sb * _SBW:(sb + 1) * _SBW].astype(MXU_DTYPE), wc_ref[sb].astype(MXU_DTYPE),
                      preferred_element_type=F32) for sb in range(S5_NSB)]
        yp = jnp.concatenate(ys, axis=1) + d_ref[...] * uv
        yp_ref[...] = _ungroup(yp, seg)
        gl = _dot(_gelu(yp), wg_ref[...])
        gl_ref[...] = _ungroup(gl, seg)
        out = gl[:, 0:D] * _sigmoid(gl[:, D:2 * D])
        o_ref[...] = h_ref[...] + _ungroup(_dot(out, wo_ref[...]), seg)

    full = lambda a: pl.BlockSpec(a.shape, lambda i: (0,) * a.ndim)
    args = (u, h, wb, wc, ap_r, ap_i, aq_r, aq_i, d, w_glu, w_out)
    return _call(
        body, name=name,
        out_shape=(jax.ShapeDtypeStruct((T, D), F32), jax.ShapeDtypeStruct((T, D), F32),
                   jax.ShapeDtypeStruct((T, 2 * D), F32), jax.ShapeDtypeStruct((nt, 8, S5_NS), F32)),
        grid=(nt,),
        in_specs=[pl.BlockSpec((tm, D), lambda i: (i, 0)), pl.BlockSpec((tm, D), lambda i: (i, 0))]
        + [full(a) for a in args[2:]],
        out_specs=(pl.BlockSpec((tm, D), lambda i: (i, 0)), pl.BlockSpec((tm, D), lambda i: (i, 0)),
                   pl.BlockSpec((tm, 2 * D), lambda i: (i, 0)), pl.BlockSpec((None, 8, S5_NS), lambda i: (i, 0, 0))),
        scratch_shapes=[pltpu.VMEM((tm, S5_NS), F32), pltpu.VMEM((8, S5_NS), F32)],
        compiler_params=_params(("arbitrary",)),
    )(*args)


def s5_bwd_glu(dout, gl, ypre, u, w_glu, w_out, tm, name):
    T, D = u.shape

    def body(do_ref, gl_ref, yp_ref, u_ref, wg_ref, wo_ref, dy_ref, oact_ref, dgl_ref, gact_ref, dd_ref):
        @pl.when(pl.program_id(0) == 0)
        def _():
            dd_ref[...] = jnp.zeros_like(dd_ref)

        gl1 = gl_ref[:, 0:D]
        sg = _sigmoid(gl_ref[:, D:2 * D])
        oact_ref[...] = (gl1 * sg).astype(MXU_DTYPE)
        dgo = _dot_nt(do_ref[...], wo_ref[...])
        d1 = (dgo * sg).astype(MXU_DTYPE)
        d2 = (dgo * gl1 * sg * (1.0 - sg)).astype(MXU_DTYPE)
        dgl_ref[:, 0:D] = d1
        dgl_ref[:, D:2 * D] = d2
        dg = _dot_nt(d1, wg_ref[:, 0:D]) + _dot_nt(d2, wg_ref[:, D:2 * D])
        g, gd = _gelu_and_grad(yp_ref[...])
        gact_ref[...] = g.astype(MXU_DTYPE)
        dy = dg * gd
        dy_ref[...] = dy
        dd_ref[...] += _rowsum8(dy * u_ref[...])

    full = lambda a: pl.BlockSpec(a.shape, lambda i: (0,) * a.ndim)
    rs = lambda c: pl.BlockSpec((tm, c), lambda i: (i, 0))
    return _call(
        body, name=name,
        out_shape=(jax.ShapeDtypeStruct((T, D), F32), jax.ShapeDtypeStruct((T, D), MXU_DTYPE),
                   jax.ShapeDtypeStruct((T, 2 * D), MXU_DTYPE), jax.ShapeDtypeStruct((T, D), MXU_DTYPE),
                   jax.ShapeDtypeStruct((8, D), F32)),
        grid=(T // tm,),
        in_specs=[rs(D), rs(2 * D), rs(D), rs(D), full(w_glu), full(w_out)],
        out_specs=(rs(D), rs(D), rs(2 * D), rs(D), pl.BlockSpec((8, D), lambda i: (0, 0))),
        compiler_params=_params(("arbitrary",)),
    )(dout, gl, ypre, u, w_glu, w_out)


def s5_bwd_ssm(dy, u, st, wb, wc, tables, d, tm, name, comm=None):
    T, D = u.shape
    nt = T // tm
    seg = tm // 8
    GP = S5_G * S5_P
    assert len(tables) == 8

    def body(dy_ref, u_ref, st_ref, wb_ref, wc_ref, apr_ref, api_ref, aqr_ref, aqi_ref, aprr_ref, apir_ref,
             aqrr_ref, aqir_ref, d_ref, du_ref, dwb_ref, dwc_ref, dar_ref, dai_ref, S, L, carry, lcarry, cin):
        @pl.when(pl.program_id(0) == 0)
        def _():
            for ref in (lcarry, dwb_ref, dwc_ref, dar_ref, dai_ref):
                ref[...] = jnp.zeros_like(ref)

        uv = _regroup(u_ref[...], seg)
        ub = uv.astype(MXU_DTYPE)
        dyv = _regroup(dy_ref[...], seg)
        dyb = dyv.astype(MXU_DTYPE)
        carry[...] = st_ref[...]
        row8 = lax.broadcasted_iota(jnp.int32, (8, 1), 0)
        dus = []
        for sb in range(S5_NSB):
            ls = slice(sb * _SBW, (sb + 1) * _SBW)
            us = slice(sb * _SBU, (sb + 1) * _SBU)
            S[:, ls] = jnp.dot(ub[:, us], wb_ref[sb].astype(MXU_DTYPE), preferred_element_type=F32)
            L[:, ls] = _dot_nt(dyb[:, us], wc_ref[sb])
            _s5_scan_fwd(S, carry, cin, apr_ref, api_ref, aqr_ref, aqi_ref, seg, sb)
            lr, li, la = _s5_lanes(sb)
            ar, ai = apr_ref[0:1, la], api_ref[0:1, la]
            last = slice((seg - 1) * 8, seg * 8)
            gr, gi = L[last, lr], L[last, li]
            for i in reversed(range(seg - 1)):
                blk = slice(i * 8, (i + 1) * 8)
                gr, gi = ar * gr + ai * gi + L[blk, lr], ar * gi - ai * gr + L[blk, li]
                L[blk, lr] = gr
                L[blk, li] = gi
            for s, idx in ((1, 0), (2, 1), (4, 3)):
                qr, qi = aqr_ref[idx:idx + 1, la], aqi_ref[idx:idx + 1, la]
                m = row8 < 8 - s
                sr = jnp.where(m, pltpu.roll(gr, 8 - s, 0), 0.0)
                si = jnp.where(m, pltpu.roll(gi, 8 - s, 0), 0.0)
                gr, gi = gr + qr * sr + qi * si, gi + qr * si - qi * sr
            cr, ci = lcarry[0:1, lr], lcarry[0:1, li]
            pr, pi = aqrr_ref[:, la], aqir_ref[:, la]
            gr, gi = gr + pr * cr + pi * ci, gi + pr * ci - pi * cr
            xr = jnp.where(row8 == 7, cr, pltpu.roll(gr, 7, 0))
            xi = jnp.where(row8 == 7, ci, pltpu.roll(gi, 7, 0))
            lcarry[0:1, lr] = gr[0:1, :]
            lcarry[0:1, li] = gi[0:1, :]
            acc_r = jnp.zeros((8, _SBH), F32)
            acc_i = jnp.zeros((8, _SBH), F32)
            for i in range(seg):
                blk = slice(i * 8, (i + 1) * 8)
                pr, pi = aprr_ref[i:i + 1, la], apir_ref[i:i + 1, la]
                gr = L[blk, lr] + pr * xr + pi * xi
                gi = L[blk, li] + pr * xi - pi * xr
                L[blk, lr] = gr
                L[blk, li] = gi
                if i == 0:
                    hpr, hpi = cin[:, lr], cin[:, li]
                else:
                    hpr, hpi = S[(i - 1) * 8:i * 8, lr], S[(i - 1) * 8:i * 8, li]
                acc_r = acc_r + gr * hpr + gi * hpi
                acc_i = acc_i + gi * hpr - gr * hpi
            dar_ref[:, la] += acc_r
            dai_ref[:, la] += acc_i
            lb = L[:, ls].astype(MXU_DTYPE)
            dwb_ref[sb] += _dot_tn(lb, ub[:, us])
            dwc_ref[sb] += _dot_tn(S[:, ls].astype(MXU_DTYPE), dyb[:, us])
            dus.append(_dot_nt(lb, wb_ref[sb]))
        du_ref[...] = _ungroup(jnp.concatenate(dus, axis=1) + dyv * d_ref[...], seg).astype(MXU_DTYPE)

    rt = lambda i: nt - 1 - i
    full = lambda a: pl.BlockSpec(a.shape, lambda i: (0,) * a.ndim)
    acc = lambda shape: pl.BlockSpec(shape, lambda i: (0,) * len(shape))
    return _hosted(
        body, comm, (dy, u, st, wb, wc, *tables, d), name=name,
        out_shape=(jax.ShapeDtypeStruct((T, D), MXU_DTYPE), jax.ShapeDtypeStruct((S5_NSB, _SBW, _SBU), F32),
                   jax.ShapeDtypeStruct((S5_NSB, _SBW, _SBU), F32), jax.ShapeDtypeStruct((8, GP), F32),
                   jax.ShapeDtypeStruct((8, GP), F32)),
        grid=(nt,),
        in_specs=[pl.BlockSpec((tm, D), lambda i: (rt(i), 0)), pl.BlockSpec((tm, D), lambda i: (rt(i), 0)),
                  pl.BlockSpec((None, 8, S5_NS), lambda i: (rt(i), 0, 0)), full(wb), full(wc)]
        + [full(t) for t in tables] + [full(d)],
        out_specs=(pl.BlockSpec((tm, D), lambda i: (rt(i), 0)), acc((S5_NSB, _SBW, _SBU)), acc((S5_NSB, _SBW, _SBU)),
                   acc((8, GP)), acc((8, GP))),
        scratch_shapes=[pltpu.VMEM((tm, S5_NS), F32), pltpu.VMEM((tm, S5_NS), F32), pltpu.VMEM((8, S5_NS), F32),
                        pltpu.VMEM((8, S5_NS), F32), pltpu.VMEM((8, S5_NS), F32)],
        compiler_params=_params(("arbitrary",)),
    )


def final_loss(h, g, target, tm, name):
    T, D = h.shape

    def body(h_ref, g_ref, t_ref, dh_ref, se_ref, dg_ref):
        @pl.when(pl.program_id(0) == 0)
        def _():
            se_ref[...] = jnp.zeros_like(se_ref)
            dg_ref[...] = jnp.zeros_like(dg_ref)

        r, xhat = _rms(h_ref[...])
        gv = g_ref[...]
        e = xhat * gv - t_ref[...]
        se_ref[...] += _rowsum8(e * e)
        dy = e * (1.0 / D)
        dg_ref[...] += _rowsum8(dy * xhat)
        dh_ref[...] = _rms_bwd(dy, xhat, r, gv)

    rs = pl.BlockSpec((tm, D), lambda i: (i, 0))
    acc = pl.BlockSpec((8, D), lambda i: (0, 0))
    return _call(
        body, name=name,
        out_shape=(jax.ShapeDtypeStruct((T, D), F32), jax.ShapeDtypeStruct((8, D), F32), jax.ShapeDtypeStruct((8, D), F32)),
        grid=(T // tm,), in_specs=[rs, pl.BlockSpec((1, D), lambda i: (0, 0)), rs], out_specs=(rs, acc, acc),
        compiler_params=_params(("arbitrary",)),
    )(h, g, target)


def _s5_discretize(a_re, a_im, log_dt, b_re, b_im):
    dt = jnp.exp(log_dt)[:, None]
    mag = jnp.exp(a_re * dt)
    abr = mag * jnp.cos(a_im * dt)
    abi = mag * jnp.sin(a_im * dt)
    ur, ui = abr - 1.0, abi
    den = a_re * a_re + a_im * a_im
    wr = (ur * a_re + ui * a_im) / den
    wi = (ui * a_re - ur * a_im) / den
    bbr = wr[..., None] * b_re - wi[..., None] * b_im
    bbi = wr[..., None] * b_im + wi[..., None] * b_re
    return abr, abi, bbr, bbi


def _s5_pack(abr, abi, bbr, bbi, c_re, c_im, seg_len):
    eye = jnp.eye(S5_SB, dtype=F32)
    b = jnp.stack([bbr, bbi], 0).reshape(2, S5_NSB, S5_SB, S5_P, S5_GC)
    wb = jnp.einsum('rsgpc,gh->shcrgp', b, eye).reshape(S5_NSB, _SBU, _SBW)
    c = jnp.stack([c_re, -c_im], 0).reshape(2, S5_NSB, S5_SB, S5_GC, S5_P)
    wc = jnp.einsum('rsgcp,gh->srgphc', c, eye).reshape(S5_NSB, _SBW, _SBU)
    def powers(r, i, n):
        fr, fi, br, bi = r, i, r, i
        m = 1
        while m < n:
            tr, ti = fr[m - 1:m], fi[m - 1:m]
            fr, fi = (jnp.concatenate([fr, fr * tr - fi * ti], 0), jnp.concatenate([fi, fr * ti + fi * tr], 0))
            br, bi = (jnp.concatenate([br * tr - bi * ti, br], 0), jnp.concatenate([br * ti + bi * tr, bi], 0))
            m *= 2
        return fr, fi, br, bi

    assert seg_len & (seg_len - 1) == 0
    ap = powers(abr.reshape(1, -1), abi.reshape(1, -1), seg_len)
    aq = powers(ap[0][seg_len - 1:seg_len], ap[1][seg_len - 1:seg_len], 8)
    return wb.astype(MXU_DTYPE), wc.astype(MXU_DTYPE), ap[0], ap[1], aq[0], aq[1], ap[2], ap[3], aq[2], aq[3]


def _s5_unpack_grads(dwb_t, dwc, dar8, dai8):
    eye = jnp.eye(S5_SB, dtype=F32)
    t = dwb_t.reshape(S5_NSB, 2, S5_SB, S5_P, S5_SB, S5_GC)
    db = jnp.einsum('srgphc,gh->rsgpc', t, eye).reshape(2, S5_G, S5_P, S5_GC)
    t = dwc.reshape(S5_NSB, 2, S5_SB, S5_P, S5_SB, S5_GC)
    dc = jnp.einsum('srgphc,gh->rsgcp', t, eye).reshape(2, S5_G, S5_GC, S5_P)
    return db[0], db[1], dc[0], -dc[1], dar8.sum(0).reshape(S5_G, S5_P), dai8.sum(0).reshape(S5_G, S5_P)


TM = 256
TM_S5 = 256


def _tn(a, b, name, comm=None):
    T, M, N = a.shape[0], a.shape[1], b.shape[1]
    bt = 4096 if a.dtype.itemsize + b.dtype.itemsize <= 4 else 2048
    return matmul_tn(a, b, min(M, 1024), min(N, 1024), bt if T % bt == 0 else T, name, comm=comm)


def local_step(x, target, W, sched):
    T, D = x.shape
    depth = W['norm_mix_g'].shape[0]
    row = lambda v: v.reshape(1, -1)
    saved = []
    h = x
    s5c = []
    tr = min(512, T)
    for j in range(W['s5_a_re'].shape[0]):
        prm = (W['s5_a_re'][j], W['s5_a_im'][j], W['s5_log_dt'][j], W['s5_b_re'][j], W['s5_b_im'][j])
        disc, disc_vjp = jax.vjp(_s5_discretize, *prm)
        s5c.append((*_s5_pack(*disc, W['s5_c_re'][j], W['s5_c_im'][j], min(TM_S5, T) // 8), disc_vjp))
    sp_all = jax.nn.softplus(-W['rg_lambda'])
    for i in range(depth):
        j = i // 2
        if i % 2 == 0:
            xg, hn = norm_matmul(h, row(W['norm_mix_g'][i]), W['rg_w_in'][j], F32, tr, None, f"rg_in_{i}")
            h1, hs = rg_fwd(xg, h, W['rg_conv_w'][j], row(W['rg_conv_b'][j]), W['rg_w_a'][j].astype(MXU_DTYPE),
                            row(W['rg_b_a'][j]), W['rg_w_x'][j].astype(MXU_DTYPE), row(W['rg_b_x'][j]), row(sp_all[j]),
                            W['rg_w_out'][j], TM, f"rg_fwd_{i}", comm=sched.comm(f"rg_fwd_{i}"))
            mix = (xg, hn, hs)
        else:
            u, hn = norm_matmul(h, row(W['norm_mix_g'][i]), W['s5_w_in'][j], F32, tr, None, f"s5_in_{i}")
            h1, ypre, gl, st = s5_fwd(u, h, *s5c[j][:6], row(W['s5_d'][j]), W['s5_w_glu'][j], W['s5_w_out'][j],
                                      min(TM_S5, T), f"s5_fwd_{i}")
            mix = (u, hn, ypre, gl, st)
        up, hn2 = norm_matmul(h1, row(W['norm_ffn_g'][i]), W['ffn_w_up'][i], MXU_DTYPE, tr, None, f"ffn_up_{i}",
                              comm=sched.comm(f"ffn_up_{i}"))
        h2, av = ffn_fwd(up, h1, W['ffn_conv_w'][i], row(W['ffn_conv_b'][i]), W['ffn_w_down'][i], TM, f"ffn_fwd_{i}",
                         comm=sched.comm(f"ffn_fwd_{i}"))
        saved.append((h, mix, h1, hn2, up, av))
        h = h2
    dh, se8, dgf8 = final_loss(h, row(W['norm_final_g']), target, tr, "final_loss")
    G = {k: [None] * len(v) for k, v in W.items() if k != 'norm_final_g'}
    G['norm_final_g'] = dgf8.sum(0)
    for i in reversed(range(depth)):
        j = i // 2
        h0, mix, h1, hn2, up, av = saved[i]
        dup, act, dcb8 = ffn_bwd_act(dh, av, W['ffn_w_down'][i], TM, f"ffn_bwd_act_{i}",
                                     comm=sched.comm(f"ffn_bwd_act_{i}", G))
        G['ffn_w_down'][i] = _tn(act, dh, f"ffn_dwdown_{i}")
        dh1, dg8, dupp, dcw8 = dx_norm_bwd(dup, W['ffn_w_up'][i], h1, row(W['norm_ffn_g'][i]), dh, TM,
                                           f"ffn_bwd_in_{i}", conv_w=W['ffn_conv_w'][i], conv_x=up)
        G['ffn_w_up'][i] = _tn(hn2, dupp, f"ffn_dwup_{i}")
        G['ffn_conv_b'][i] = dcb8.sum(0)
        G['ffn_conv_w'][i] = dcw8.sum(1)
        G['norm_ffn_g'][i] = dg8.sum(0)
        if i % 2 == 0:
            xg, hn, hs = mix
            dxg, y, dwa, dwx, dba8, dbx8, dsp8, dcb8, dcw8 = rg_bwd(
                dh1, xg, hs, W['rg_conv_w'][j], row(W['rg_conv_b'][j]), W['rg_w_a'][j].astype(MXU_DTYPE),
                row(W['rg_b_a'][j]), W['rg_w_x'][j].astype(MXU_DTYPE), row(W['rg_b_x'][j]), row(sp_all[j]),
                W['rg_w_out'][j], TM, f"rg_bwd_{i}", comm=sched.comm(f"rg_bwd_{i}", G))
            G['rg_w_out'][j] = _tn(y, dh1, f"rg_dwout_{i}")
            G['rg_w_a'][j], G['rg_w_x'][j] = dwa, dwx
            G['rg_b_a'][j] = dba8.sum(0).reshape(RG_HEADS, RG_BW)
            G['rg_b_x'][j] = dbx8.sum(0).reshape(RG_HEADS, RG_BW)
            G['rg_lambda'][j] = dsp8.sum(0) * (-jax.nn.sigmoid(-W['rg_lambda'][j]))
            G['rg_conv_b'][j] = dcb8.sum(0)
            G['rg_conv_w'][j] = dcw8.sum(1)
            dh, dg8 = dx_norm_bwd(dxg, W['rg_w_in'][j], h0, row(W['norm_mix_g'][i]), dh1, TM, f"rg_bwd_in_{i}",
                                  comm=sched.comm(f"rg_bwd_in_{i}", G))
            G['rg_w_in'][j] = _tn(hn, dxg, f"rg_dwin_{i}", comm=sched.comm(f"rg_dwin_{i}", G))
        else:
            u, hn, ypre, gl, st = mix
            disc_vjp = s5c[j][-1]
            dy, oact, dgl, gact, dd8 = s5_bwd_glu(dh1, gl, ypre, u, W['s5_w_glu'][j], W['s5_w_out'][j], TM,
                                                  f"s5_bwd_glu_{i}")
            G['s5_w_out'][j] = _tn(oact, dh1, f"s5_dwout_{i}")
            G['s5_w_glu'][j] = _tn(gact, dgl, f"s5_dwglu_{i}")
            du, dwb_t, dwc, dar8, dai8 = s5_bwd_ssm(dy, u, st, *s5c[j][:2], s5c[j][2:10], row(W['s5_d'][j]),
                                                    min(TM_S5, T), f"s5_bwd_ssm_{i}",
                                                    comm=sched.comm(f"s5_bwd_ssm_{i}", G))
            dh, dg8 = dx_norm_bwd(du, W['s5_w_in'][j], h0, row(W['norm_mix_g'][i]), dh1, TM, f"s5_bwd_in_{i}")
            G['s5_w_in'][j] = _tn(hn, du, f"s5_dwin_{i}")
            dbbr, dbbi, dcr, dci, dabr, dabi = _s5_unpack_grads(dwb_t, dwc, dar8, dai8)
            da_re, da_im, dlog_dt, db_re, db_im = disc_vjp((dabr, dabi, dbbr, dbbi))
            G['s5_a_re'][j], G['s5_a_im'][j], G['s5_log_dt'][j] = da_re, da_im, dlog_dt
            G['s5_b_re'][j], G['s5_b_im'][j], G['s5_c_re'][j], G['s5_c_im'][j] = db_re, db_im, dcr, dci
            G['s5_d'][j] = dd8.sum(0)
        G['norm_mix_g'][i] = dg8.sum(0)
    G = {k: (v if (k == 'norm_final_g' or k in BIG) else jnp.stack(v, 0)) for k, v in G.items()}
    return se8, dh, G


BIG = {'rg_w_in': 1, 'rg_w_out': 0, 's5_w_in': 0, 's5_w_glu': 1, 's5_w_out': 0, 'ffn_w_up': 1, 'ffn_w_down': 0}
SMALL_SHARDED = ('rg_conv_w', 'ffn_conv_w', 's5_d')
REPLICATED = ('norm_mix_g', 'norm_ffn_g', 'norm_final_g', 'rg_conv_b', 'rg_w_a', 'rg_b_a', 'rg_w_x', 'rg_b_x',
              'rg_lambda', 's5_a_re', 's5_a_im', 's5_log_dt', 's5_b_re', 's5_b_im', 's5_c_re', 's5_c_im', 'ffn_conv_b')
REP_LATE = ('norm_mix_g',)
REP_MAIN = tuple(k for k in REPLICATED if k not in REP_LATE)


def _me():
    x, y, c = lax.axis_index("x"), lax.axis_index("y"), lax.axis_index("c")
    return x, y, c, 4 * x + 2 * y + c


def _win(ref, axis, dev, width):
    idx = [slice(None)] * len(ref.shape)
    idx[axis] = pl.ds(pl.multiple_of(dev * width, width), width)
    return ref.at[tuple(idx)]


def gather_plan(items, deliver):
    n = len(items)

    def tools(ins, outs, sems):
        send_sems, recv_sems, local_sems = sems
        x, y, c, me = _me()
        sib = (x, y, 1 - c)
        chips = [(1 - x, y), (x, 1 - y), (1 - x, 1 - y)]
        num = lambda px, py, pc: 4 * px + 2 * py + pc

        def src_of(a):
            return ins[a] if items[a][1] is None else ins[a].at[items[a][1]]

        def block(a, dev):
            return _win(outs[a], items[a][2], dev, src_of(a).shape[items[a][2]])

        def copy(a, k, dev, to, own=False):
            return pltpu.make_async_remote_copy(
                src_ref=src_of(a) if own else block(a, dev), dst_ref=block(a, dev),
                send_sem=send_sems.at[a, k], recv_sem=recv_sems.at[a, k], device_id=to, device_id_type=MESH)

        mine = lambda: [pltpu.make_async_copy(src_of(a), block(a, me), local_sems.at[a]) for a in range(n)]
        own = lambda: [cp for a in range(n) for cp in
                       [copy(a, 0, me, sib, own=True)] + [copy(a, 1 + j, me, (*chip, c), own=True)
                                                          for j, chip in enumerate(chips)]]
        arrived = lambda j, a: copy(a, 1 + j, num(*chips[j], c), (x, y, c))
        passed = lambda j, a: copy(a, 4 + j, num(*chips[j], c), sib)
        from_sib = lambda: ([copy(a, 0, num(x, y, 1 - c), (x, y, c)) for a in range(n)]
                            + [copy(a, 4 + j, num(*chip, 1 - c), (x, y, c)) for j, chip in enumerate(chips)
                               for a in range(n)])
        return mine, own, arrived, passed, from_sib

    def start(ins, outs, sems):
        mine, own, _, _, _ = tools(ins, outs, sems)
        for cp in mine() + own():
            cp.start()

    def middle(ins, outs, sems):
        _, _, arrived, passed, _ = tools(ins, outs, sems)
        for j in range(3):
            for a in range(n):
                arrived(j, a).wait_recv()
                passed(j, a).start()

    def finish(ins, outs, sems):
        mine, own, _, passed, from_sib = tools(ins, outs, sems)
        for cp in from_sib():
            cp.wait_recv()
        for cp in own() + [passed(j, a) for j in range(3) for a in range(n)]:
            cp.wait_send()
        for cp in mine():
            cp.wait()

    return Comm([it[0] for it in items], [jax.ShapeDtypeStruct(it[3], it[0].dtype) for it in items],
                [pltpu.SemaphoreType.DMA((n, 7)), pltpu.SemaphoreType.DMA((n, 7)), pltpu.SemaphoreType.DMA((n,))],
                start, middle, finish, deliver)


def exchange_plan(items, deliver):
    n = len(items)
    width = [arr.shape[axis] // N_DEV for arr, axis in items]
    shard = [arr.shape[:axis] + (w,) + arr.shape[axis + 1:] for (arr, axis), w in zip(items, width)]

    def tools(ins, outs, sems):
        send_sems, recv_sems, local_sems = sems
        x, y, c, me = _me()
        piece = lambda a, dev: _win(ins[a], items[a][1], dev, width[a])
        mine = lambda: [pltpu.make_async_copy(piece(a, me), outs[a].at[me], local_sems.at[a]) for a in range(n)]

        def remote(sending):
            cps = []
            for k in range(1, N_DEV):
                px, py, pc = (1 - x) if k & 4 else x, (1 - y) if k & 2 else y, (1 - c) if k & 1 else c
                peer = 4 * px + 2 * py + pc
                for a in range(n):
                    src, dst = (piece(a, peer), outs[a].at[me]) if sending else (piece(a, me), outs[a].at[peer])
                    cps.append(pltpu.make_async_remote_copy(
                        src_ref=src, dst_ref=dst, send_sem=send_sems.at[a, k - 1], recv_sem=recv_sems.at[a, k - 1],
                        device_id=(px, py, pc), device_id_type=MESH))
            return cps

        return mine, remote

    def start(ins, outs, sems):
        mine, remote = tools(ins, outs, sems)
        for cp in mine() + remote(True):
            cp.start()

    def middle(ins, outs, sems):
        pass

    def finish(ins, outs, sems):
        mine, remote = tools(ins, outs, sems)
        for cp in remote(False):
            cp.wait_recv()
        for cp in remote(True):
            cp.wait_send()
        for cp in mine():
            cp.wait()

    return Comm([it[0] for it in items], [jax.ShapeDtypeStruct((N_DEV,) + s, it[0].dtype) for it, s in zip(items, shard)],
                [pltpu.SemaphoreType.DMA((n, 7)), pltpu.SemaphoreType.DMA((n, 7)), pltpu.SemaphoreType.DMA((n,))],
                start, middle, finish, deliver)


def adam_update(parts, w, m, v, name):
    R, C = w.shape
    br = next((b for b in (256, 128) if R > b and R % b == 0), R)
    np_ = parts.shape[0]

    def body(p_ref, w_ref, m_ref, v_ref, g_ref, d_ref, nm_ref, nv_ref):
        _adam_body(np_, p_ref, w_ref, m_ref, v_ref, g_ref, d_ref, nm_ref, nv_ref)

    bs = pl.BlockSpec((br, C), lambda i: (i, 0))
    out = jax.ShapeDtypeStruct((R, C), F32)
    return _call(
        body, name=name, out_shape=(out, out, out, out), grid=(R // br,),
        in_specs=[pl.BlockSpec((np_, br, C), lambda i: (0, i, 0)), bs, bs, bs], out_specs=(bs, bs, bs, bs),
        compiler_params=_params(("parallel",)),
    )(parts, w, m, v)


def _adam_body(np_, p_ref, w_ref, m_ref, v_ref, g_ref, d_ref, nm_ref, nv_ref):
    c1 = 1.0 / (1.0 - ADAM_B1 ** ADAM_STEP)
    c2 = 1.0 / (1.0 - ADAM_B2 ** ADAM_STEP)
    g = p_ref[0].astype(F32)
    for p in range(1, np_):
        g = g + p_ref[p].astype(F32)
    nm = ADAM_B1 * m_ref[...] + (1.0 - ADAM_B1) * g
    nv = ADAM_B2 * v_ref[...] + (1.0 - ADAM_B2) * (g * g)
    g_ref[...] = g
    nm_ref[...] = nm
    nv_ref[...] = nv
    d_ref[...] = -ADAM_LR * ((nm * c1) / (jnp.sqrt(nv * c2) + ADAM_EPS) + ADAM_WD * w_ref[...])


def adam_layer(parts, w, m, v, l, prev, name):
    L, R, C = w.shape
    br = next((b for b in (256, 128) if R > b and R % b == 0), R)

    def body(p_ref, w_ref, m_ref, v_ref, *rest):
        _adam_body(N_DEV, p_ref, w_ref, m_ref, v_ref, *rest[-4:])

    bs = pl.BlockSpec((None, br, C), lambda i: (l, i, 0))
    out = jax.ShapeDtypeStruct((L, R, C), F32)
    extra = {} if prev is None else dict(input_output_aliases={4 + q: q for q in range(4)})
    return _call(
        body, name=name, out_shape=(out, out, out, out), grid=(R // br,),
        in_specs=[pl.BlockSpec((N_DEV, br, C), lambda i: (0, i, 0)), bs, bs, bs] + ([] if prev is None else [ANY] * 4),
        out_specs=(bs, bs, bs, bs), compiler_params=_params(("parallel",)), **extra,
    )(parts, w, m, v, *(() if prev is None else prev))


def sum_parts(parts, name):
    n, R, C = parts.shape

    def body(p_ref, o_ref):
        g = p_ref[0]
        for p in range(1, n):
            g = g + p_ref[p]
        o_ref[...] = g

    return _call(body, name=name, out_shape=jax.ShapeDtypeStruct((R, C), parts.dtype),
                 compiler_params=pltpu.CompilerParams(vmem_limit_bytes=VMEM_LIMIT))(parts)


def _pack_rows(arrs):
    rows = []
    for a in arrs:
        f = a.reshape(-1)
        r = -(-f.shape[0] // 1024)
        r8 = -(-r // 8) * 8
        rows.append(jnp.pad(f, (0, r8 * 1024 - f.shape[0])).reshape(r8, 1024))
    packed = jnp.concatenate(rows, 0)
    return jnp.pad(packed, ((0, -packed.shape[0] % 128), (0, 0)))


def _unpack_rows(packed, shapes):
    out, o = [], 0
    for s in shapes:
        nel = math.prod(s)
        r8 = -(-(-(-nel // 1024)) // 8) * 8
        out.append(packed[o:o + r8].reshape(-1)[:nel].reshape(s))
        o += r8
    return out


def kernel(x, norm_mix_g, norm_ffn_g, norm_final_g, rg_w_in, rg_conv_w, rg_conv_b, rg_w_a, rg_b_a, rg_w_x, rg_b_x, rg_lambda, rg_w_out, s5_w_in, s5_a_re, s5_a_im, s5_log_dt, s5_b_re, s5_b_im, s5_c_re, s5_c_im, s5_d, s5_w_glu, s5_w_out, ffn_w_up, ffn_conv_w, ffn_conv_b, ffn_w_down, loss_target, m_norm_mix_g, m_norm_ffn_g, m_norm_final_g, m_rg_w_in, m_rg_conv_w, m_rg_conv_b, m_rg_w_a, m_rg_b_a, m_rg_w_x, m_rg_b_x, m_rg_lambda, m_rg_w_out, m_s5_w_in, m_s5_a_re, m_s5_a_im, m_s5_log_dt, m_s5_b_re, m_s5_b_im, m_s5_c_re, m_s5_c_im, m_s5_d, m_s5_w_glu, m_s5_w_out, m_ffn_w_up, m_ffn_conv_w, m_ffn_conv_b, m_ffn_w_down, v_norm_mix_g, v_norm_ffn_g, v_norm_final_g, v_rg_w_in, v_rg_conv_w, v_rg_conv_b, v_rg_w_a, v_rg_b_a, v_rg_w_x, v_rg_b_x, v_rg_lambda, v_rg_w_out, v_s5_w_in, v_s5_a_re, v_s5_a_im, v_s5_log_dt, v_s5_b_re, v_s5_b_im, v_s5_c_re, v_s5_c_im, v_s5_d, v_s5_w_glu, v_s5_w_out, v_ffn_w_up, v_ffn_conv_w, v_ffn_conv_b, v_ffn_w_down):
    names = ('norm_mix_g', 'norm_ffn_g', 'norm_final_g', 'rg_w_in', 'rg_conv_w', 'rg_conv_b', 'rg_w_a', 'rg_b_a',
             'rg_w_x', 'rg_b_x', 'rg_lambda', 'rg_w_out', 's5_w_in', 's5_a_re', 's5_a_im', 's5_log_dt', 's5_b_re',
             's5_b_im', 's5_c_re', 's5_c_im', 's5_d', 's5_w_glu', 's5_w_out', 'ffn_w_up', 'ffn_conv_w', 'ffn_conv_b',
             'ffn_w_down')
    loc = locals()
    Wl = {k: loc[k] for k in names}
    Ml = {k: loc['m_' + k] for k in names}
    Vl = {k: loc['v_' + k] for k in names}

    depth = norm_mix_g.shape[0]
    mixer_keys = lambda i: ([('rg_w_in', i // 2), ('rg_w_out', i // 2)] if i % 2 == 0 else
                            [('s5_w_in', i // 2), ('s5_w_glu', i // 2), ('s5_w_out', i // 2)])
    ffn_keys = lambda i: [('ffn_w_up', i), ('ffn_w_down', i)]
    shards = {k: Wl[k].astype(BF16) for k in BIG}
    W = {k: Wl[k] for k in REPLICATED}
    W.update({k: [None] * Wl[k].shape[0] for k in BIG})
    parts = {}

    def gather_of(keys, small=False):
        items = []
        for k, l in keys:
            _, r, c = shards[k].shape
            items.append((shards[k], l, BIG[k], (r * N_DEV, c) if BIG[k] == 0 else (r, c * N_DEV)))
        if small:
            items += [(Wl[k], None, Wl[k].ndim - 1, Wl[k].shape[:-1] + (Wl[k].shape[-1] * N_DEV,)) for k in SMALL_SHARDED]

        def deliver(outs):
            for (k, l), arr in zip(keys, outs):
                W[k][l] = arr
            if small:
                W.update(zip(SMALL_SHARDED, outs[len(keys):]))

        return gather_plan(items, deliver)

    def exchange_of(keys, G, extra=()):
        items = [(G[k][l], BIG[k]) for k, l in keys] + [(arr, axis) for _, arr, axis in extra]
        return exchange_plan(items, lambda outs: parts.update(zip(list(keys) + [e[0] for e in extra], outs)))

    class Sched:
        @staticmethod
        def comm(host, G=None):
            kind, _, i = host.rpartition("_")
            i = int(i)
            if host == "rg_fwd_0":
                return gather_of(ffn_keys(0))
            if kind == "ffn_up" and i + 1 < depth:
                return gather_of(ffn_keys(i + 1)[:1])
            if kind == "ffn_fwd" and i + 1 < depth:
                return gather_of(mixer_keys(i + 1) + ffn_keys(i + 1)[1:])
            if kind == "ffn_bwd_act" and i + 1 < depth:
                return exchange_of(mixer_keys(i + 1), G)
            if kind in ("rg_bwd", "s5_bwd_ssm"):
                return exchange_of(ffn_keys(i), G)
            if host == "rg_bwd_in_0":
                stack = lambda k: G[k] if k == 'norm_final_g' else jnp.stack(G[k], 0)
                extra = [(k, stack(k), Wl[k].ndim - 1) for k in SMALL_SHARDED]
                extra.append(('rep_main', _pack_rows([stack(k).astype(F32) for k in REP_MAIN]), 0))
                return exchange_of(mixer_keys(0)[1:], G, extra)
            if host == "rg_dwin_0":
                rsum = sum_parts(parts['rep_main'], "sum_replicated")
                return gather_plan([(rsum, None, 0, (rsum.shape[0] * N_DEV, rsum.shape[1]))],
                                   lambda outs: parts.update(rep_main_full=outs[0]))
            return None

    run_comm(gather_of(mixer_keys(0), small=True), "gather_first")

    se8, gx, G = local_step(x[0], loss_target[0], W, Sched)
    loss = lax.psum(0.5 * jnp.sum(se8) / x.shape[-1], ("x", "y", "c"))

    gp_late = _pack_rows([G[k].astype(F32) for k in REP_LATE])
    run_comm(exchange_of(mixer_keys(0)[:1], G, [('rep_late', gp_late, 0)]), "exchange_last")
    out_g, out_d, out_m, out_v = {}, {}, {}, {}
    for k in BIG:
        res = None
        for l in range(Wl[k].shape[0]):
            res = adam_layer(parts[(k, l)], Wl[k], Ml[k], Vl[k], l, res, f"adam_{k}_{l}")
        out_g[k], out_d[k], out_m[k], out_v[k] = res
    for k in SMALL_SHARDED:
        shp = Wl[k].shape
        r2 = (math.prod(shp[:-1]), shp[-1])
        res = adam_update(parts[k].reshape((N_DEV,) + r2), Wl[k].reshape(r2), Ml[k].reshape(r2), Vl[k].reshape(r2),
                          f"adam_{k}")
        out_g[k], out_d[k], out_m[k], out_v[k] = [t.reshape(shp) for t in res]
    rsum = sum_parts(parts['rep_late'], "sum_replicated_late")
    run_comm(gather_plan([(rsum, None, 0, gp_late.shape)], lambda outs: parts.update(rep_late_full=outs[0])),
             "gather_small_grads")
    for keys, full, tag in ((REP_MAIN, parts['rep_main_full'], "main"), (REP_LATE, parts['rep_late_full'], "late")):
        res = adam_update(full[None], _pack_rows([Wl[k] for k in keys]), _pack_rows([Ml[k] for k in keys]),
                          _pack_rows([Vl[k] for k in keys]), f"adam_replicated_{tag}")
        for dst, packed in zip((out_g, out_d, out_m, out_v), res):
            dst.update(zip(keys, _unpack_rows(packed, [Wl[k].shape for k in keys])))
    return (loss, gx[None], *[out_g[k] for k in names], *[out_d[k] for k in names], *[out_m[k] for k in names],
            *[out_v[k] for k in names])
```

```python
import functools
import math

import jax
import jax.numpy as jnp
from jax import lax
from jax.experimental import pallas as pl
from jax.experimental.pallas import tpu as pltpu

F32 = jnp.float32
BF16 = jnp.bfloat16
MXU_DTYPE = jnp.bfloat16

NORM_EPS = 1e-6
RG_C = 8.0
RG_HEADS = 8
RG_BW = 128
S5_G = 64
S5_GC = 16
S5_P = 64
S5_SB = 8
S5_NSB = S5_G // S5_SB
S5_NS = 2 * S5_G * S5_P
ADAM_LR = 0.001
ADAM_B1 = 0.9
ADAM_B2 = 0.999
ADAM_EPS = 1e-08
ADAM_WD = 0.01
ADAM_STEP = 10
N_DEV = 8
VMEM_LIMIT = 56 * 1024 * 1024


def _call(body, **kw):
    return pl.pallas_call(body, **kw)


def _params(sem, vmem=VMEM_LIMIT):
    return pltpu.CompilerParams(dimension_semantics=sem, vmem_limit_bytes=vmem)


MESH = pl.DeviceIdType.MESH
ANY = pl.BlockSpec(memory_space=pl.ANY)


class Comm:
    def __init__(self, operands, out_shape, scratch, start, middle, finish, deliver):
        self.operands, self.out_shape, self.scratch = list(operands), list(out_shape), list(scratch)
        self.start, self.middle, self.finish, self.deliver = start, middle, finish, deliver


def run_comm(comm, name):
    ci, co = len(comm.operands), len(comm.out_shape)

    def body(*refs):
        parts = (refs[:ci], refs[ci:ci + co], refs[ci + co:])
        comm.start(*parts)
        comm.middle(*parts)
        comm.finish(*parts)

    comm.deliver(_call(body, name=name, out_shape=tuple(comm.out_shape), in_specs=[ANY] * ci,
                       out_specs=tuple([ANY] * co), scratch_shapes=comm.scratch)(*comm.operands))


def _hosted(body, comm, args, *, name, out_shape, grid, in_specs, out_specs, scratch_shapes, compiler_params):
    if comm is None:
        return _call(body, name=name, out_shape=tuple(out_shape), grid=grid, in_specs=in_specs,
                     out_specs=tuple(out_specs), scratch_shapes=scratch_shapes, compiler_params=compiler_params)(*args)
    n_in, n_out, n_sc = len(in_specs), len(out_shape), len(scratch_shapes)
    ci, co = len(comm.operands), len(comm.out_shape)
    nsteps = math.prod(grid)
    mid = (2 * nsteps) // 3

    def wrapped(*refs):
        ins, refs = refs[:n_in], refs[n_in:]
        cins, refs = refs[:ci], refs[ci:]
        outs, refs = refs[:n_out], refs[n_out:]
        couts, refs = refs[:co], refs[co:]
        sc, csc = refs[:n_sc], refs[n_sc:]
        step = pl.program_id(0)
        for d in range(1, len(grid)):
            step = step * grid[d] + pl.program_id(d)

        @pl.when(step == 0)
        def _():
            comm.start(cins, couts, csc)

        body(*ins, *outs, *sc)

        @pl.when(step == mid)
        def _():
            comm.middle(cins, couts, csc)

        @pl.when(step == nsteps - 1)
        def _():
            comm.finish(cins, couts, csc)

    res = _call(wrapped, name=name, out_shape=(*out_shape, *comm.out_shape), grid=grid,
                in_specs=[*in_specs, *[ANY] * ci], out_specs=(*out_specs, *[ANY] * co),
                scratch_shapes=[*scratch_shapes, *comm.scratch],
                compiler_params=_params(("arbitrary",) * len(grid)))(*args, *comm.operands)
    comm.deliver(res[n_out:])
    return res[:n_out]


_GELU_C = 0.7978845608028654
_GELU_A = 0.044715


def _gelu(x):
    return 0.5 * x * (1.0 + jnp.tanh(_GELU_C * (x + _GELU_A * x * x * x)))


def _gelu_and_grad(x):
    x2 = x * x
    t = jnp.tanh(_GELU_C * (x + _GELU_A * x2 * x))
    g = 0.5 * x * (1.0 + t)
    dg = 0.5 * (1.0 + t) + 0.5 * x * (1.0 - t * t) * _GELU_C * (1.0 + 3.0 * _GELU_A * x2)
    return g, dg


def _sigmoid(x):
    return 1.0 / (1.0 + jnp.exp(-x))


def _neg_expm1(x):
    series = -x * (1.0 + x * (0.5 + x * (1.0 / 6.0 + x * (1.0 / 24.0 + x * (1.0 / 120.0 + x * (1.0 / 720.0))))))
    return jnp.where(x > -0.1, series, 1.0 - jnp.exp(x))


def _rowsum8(x):
    r, c = x.shape
    return x.reshape(r // 8, 8, c).sum(axis=0)


def _dot(a, b):
    return jnp.dot(a.astype(MXU_DTYPE), b.astype(MXU_DTYPE), preferred_element_type=F32)


def _dot_nt(a, b):
    return lax.dot_general(a.astype(MXU_DTYPE), b.astype(MXU_DTYPE), (((1,), (1,)), ((), ())),
                           preferred_element_type=F32)


def _dot_tn(a, b):
    return lax.dot_general(a.astype(MXU_DTYPE), b.astype(MXU_DTYPE), (((0,), (0,)), ((), ())),
                           preferred_element_type=F32)


def _shift_down(x, s, fills, row):
    y = pltpu.roll(x, s, 0)
    for t in range(s):
        y = jnp.where(row == t, fills[s - 1 - t], y)
    return y


def _shift_up(x, s, fills, row):
    n = x.shape[0]
    y = pltpu.roll(x, n - s, 0)
    for t in range(s):
        y = jnp.where(row == n - s + t, fills[t], y)
    return y


def _rms(x):
    r = lax.rsqrt(jnp.mean(x * x, axis=-1, keepdims=True) + NORM_EPS)
    return r, x * r


def _rms_bwd(dhn, xhat, r, g):
    dz = dhn * g
    return r * (dz - xhat * jnp.mean(dz * xhat, axis=-1, keepdims=True))


def norm_matmul(h, g, w, out_dtype, tm, tn, name, comm=None):
    T, D = h.shape
    N = w.shape[1]
    tn = N if tn is None else tn

    def body(h_ref, g_ref, w_ref, o_ref, hn_ref, hn_s):
        @pl.when(pl.program_id(1) == 0)
        def _():
            _, xhat = _rms(h_ref[...])
            v = (xhat * g_ref[...]).astype(MXU_DTYPE)
            hn_s[...] = v
            hn_ref[...] = v

        o_ref[...] = jnp.dot(hn_s[...], w_ref[...].astype(MXU_DTYPE), preferred_element_type=F32).astype(o_ref.dtype)

    return _hosted(
        body, comm, (h, g, w), name=name,
        out_shape=(jax.ShapeDtypeStruct((T, N), out_dtype), jax.ShapeDtypeStruct((T, D), MXU_DTYPE)),
        grid=(T // tm, N // tn),
        in_specs=[pl.BlockSpec((tm, D), lambda i, j: (i, 0)), pl.BlockSpec((1, D), lambda i, j: (0, 0)),
                  pl.BlockSpec((D, tn), lambda i, j: (0, j))],
        out_specs=(pl.BlockSpec((tm, tn), lambda i, j: (i, j)), pl.BlockSpec((tm, D), lambda i, j: (i, 0))),
        scratch_shapes=[pltpu.VMEM((tm, D), MXU_DTYPE)],
        compiler_params=_params(("parallel", "arbitrary")),
    )


def matmul_tn(a, b, bm, bn, bt, name, out_dtype=BF16, comm=None):
    T, M = a.shape
    N = b.shape[1]
    nk = T // bt

    def body(a_ref, b_ref, o_ref, acc):
        k = pl.program_id(2)

        @pl.when(k == 0)
        def _():
            acc[...] = jnp.zeros_like(acc)

        acc[...] += _dot_tn(a_ref[...], b_ref[...])

        @pl.when(k == nk - 1)
        def _():
            o_ref[...] = acc[...].astype(o_ref.dtype)

    return _hosted(
        body, comm, (a, b), name=name,
        out_shape=(jax.ShapeDtypeStruct((M, N), out_dtype),),
        grid=(M // bm, N // bn, nk),
        in_specs=[pl.BlockSpec((bt, bm), lambda i, j, k: (k, i)), pl.BlockSpec((bt, bn), lambda i, j, k: (k, j))],
        out_specs=(pl.BlockSpec((bm, bn), lambda i, j, k: (i, j)),),
        scratch_shapes=[pltpu.VMEM((bm, bn), F32)],
        compiler_params=_params(("parallel", "parallel", "arbitrary")),
    )[0]


def dx_norm_bwd(dz, w, h, g, dres, tm, name, conv_w=None, conv_x=None, chunk=256, comm=None):
    T, N = dz.shape
    D = w.shape[0]
    nt = T // tm
    has_conv = conv_w is not None
    kw = conv_w.shape[0] if has_conv else 0

    def body(*refs):
        if has_conv:
            dz_ref, cw_ref, x_ref, w_ref, h_ref, g_ref, dres_ref, dh_ref, dg_ref, dzp_ref, dcw_ref, carry = refs
        else:
            dz_ref, w_ref, h_ref, g_ref, dres_ref, dh_ref, dg_ref = refs
        i = pl.program_id(0)

        @pl.when(i == 0)
        def _():
            dg_ref[...] = jnp.zeros_like(dg_ref)
            if has_conv:
                carry[...] = jnp.zeros_like(carry)
                dcw_ref[...] = jnp.zeros_like(dcw_ref)

        if has_conv:
            row = lax.broadcasted_iota(jnp.int32, (tm, 1), 0)
            for c0 in range(0, N, chunk):
                sl = slice(c0, c0 + chunk)
                d0 = dz_ref[:, sl].astype(F32)
                xv = x_ref[:, sl].astype(F32)
                fills = [carry[t:t + 1, sl] for t in range(kw - 1)]
                acc = cw_ref[kw - 1:kw, sl] * d0
                dcw_ref[kw - 1, :, sl] += _rowsum8(d0 * xv)
                for s in range(1, kw):
                    ds = _shift_up(d0, s, fills, row)
                    acc = acc + cw_ref[kw - 1 - s:kw - s, sl] * ds
                    dcw_ref[kw - 1 - s, :, sl] += _rowsum8(ds * xv)
                zb = acc.astype(MXU_DTYPE)
                dzp_ref[:, sl] = zb
                part = _dot_nt(zb, w_ref[:, sl])
                dhn = part if c0 == 0 else dhn + part
            carry[...] = dz_ref[0:16, :].astype(F32)
        else:
            dhn = _dot_nt(dz_ref[...], w_ref[...])
        r, xhat = _rms(h_ref[...])
        dg_ref[...] += _rowsum8(dhn * xhat)
        dh_ref[...] = dres_ref[...] + _rms_bwd(dhn, xhat, r, g_ref[...])

    if has_conv:
        ti = lambda i: nt - 1 - i
    else:
        ti = lambda i: i
    row_spec = lambda c: pl.BlockSpec((tm, c), lambda i: (ti(i), 0))
    full = lambda a: pl.BlockSpec(a.shape, lambda i: (0,) * a.ndim)
    in_specs = [row_spec(N)] + ([full(conv_w), row_spec(N)] if has_conv else []) + [full(w), row_spec(D), full(g), row_spec(D)]
    out_shape = [jax.ShapeDtypeStruct((T, D), F32), jax.ShapeDtypeStruct((8, D), F32)]
    out_specs = [row_spec(D), pl.BlockSpec((8, D), lambda i: (0, 0))]
    scratch = []
    if has_conv:
        out_shape += [jax.ShapeDtypeStruct((T, N), MXU_DTYPE), jax.ShapeDtypeStruct((kw, 8, N), F32)]
        out_specs += [row_spec(N), pl.BlockSpec((kw, 8, N), lambda i: (0, 0, 0))]
        scratch = [pltpu.VMEM((16, N), F32)]
    args = [dz] + ([conv_w, conv_x] if has_conv else []) + [w, h, g, dres]
    return _hosted(
        body, comm, args, name=name, out_shape=tuple(out_shape), grid=(nt,), in_specs=in_specs,
        out_specs=tuple(out_specs), scratch_shapes=scratch, compiler_params=_params(("arbitrary",)),
    )


def _ffn_conv_chunk(up_ref, cw_ref, cb_ref, carry, row, sl):
    x = up_ref[:, sl].astype(F32)
    fills = [carry[15:16, sl], carry[14:15, sl]]
    x1 = _shift_down(x, 1, fills, row)
    x2 = _shift_down(x, 2, fills, row)
    return cb_ref[:, sl] + cw_ref[2:3, sl] * x + cw_ref[1:2, sl] * x1 + cw_ref[0:1, sl] * x2


def ffn_fwd(up, h, conv_w, conv_b, w_down, tm, name, chunk=256, comm=None):
    T, C = up.shape
    F = C // 2
    D = h.shape[1]

    def body(up_ref, h_ref, cw_ref, cb_ref, wd_ref, o_ref, av_ref, carry):
        @pl.when(pl.program_id(0) == 0)
        def _():
            carry[...] = jnp.zeros_like(carry)

        row = lax.broadcasted_iota(jnp.int32, (tm, 1), 0)
        out = h_ref[...]
        for c0 in range(0, F, chunk):
            sa, sv = slice(c0, c0 + chunk), slice(F + c0, F + c0 + chunk)
            a = _ffn_conv_chunk(up_ref, cw_ref, cb_ref, carry, row, sa)
            v = _ffn_conv_chunk(up_ref, cw_ref, cb_ref, carry, row, sv)
            av_ref[:, sa] = a.astype(MXU_DTYPE)
            av_ref[:, sv] = v.astype(MXU_DTYPE)
            out = out + _dot(_gelu(a) * v, wd_ref[sa, :])
        carry[...] = up_ref[tm - 16:tm, :].astype(F32)
        o_ref[...] = out

    full = lambda a: pl.BlockSpec(a.shape, lambda i: (0,) * a.ndim)
    return _hosted(
        body, comm, (up, h, conv_w, conv_b, w_down), name=name,
        out_shape=(jax.ShapeDtypeStruct((T, D), F32), jax.ShapeDtypeStruct((T, C), MXU_DTYPE)),
        grid=(T // tm,),
        in_specs=[pl.BlockSpec((tm, C), lambda i: (i, 0)), pl.BlockSpec((tm, D), lambda i: (i, 0)),
                  full(conv_w), full(conv_b), full(w_down)],
        out_specs=(pl.BlockSpec((tm, D), lambda i: (i, 0)), pl.BlockSpec((tm, C), lambda i: (i, 0))),
        scratch_shapes=[pltpu.VMEM((16, C), F32)],
        compiler_params=_params(("arbitrary",)),
    )


def ffn_bwd_act(dout, av, w_down, tm, name, chunk=512, comm=None):
    T, C = av.shape
    F = C // 2
    D = dout.shape[1]

    def body(do_ref, av_ref, wd_ref, dup_ref, act_ref, dcb_ref):
        @pl.when(pl.program_id(0) == 0)
        def _():
            dcb_ref[...] = jnp.zeros_like(dcb_ref)

        dob = do_ref[...].astype(MXU_DTYPE)
        for c0 in range(0, F, chunk):
            sa, sv = slice(c0, c0 + chunk), slice(F + c0, F + c0 + chunk)
            dact = _dot_nt(dob, wd_ref[sa, :])
            v = av_ref[:, sv].astype(F32)
            ga, dga = _gelu_and_grad(av_ref[:, sa].astype(F32))
            act_ref[:, sa] = (ga * v).astype(MXU_DTYPE)
            da = dact * v * dga
            dv = dact * ga
            dup_ref[:, sa] = da.astype(MXU_DTYPE)
            dup_ref[:, sv] = dv.astype(MXU_DTYPE)
            dcb_ref[:, sa] += _rowsum8(da)
            dcb_ref[:, sv] += _rowsum8(dv)

    full = lambda a: pl.BlockSpec(a.shape, lambda i: (0,) * a.ndim)
    return _hosted(
        body, comm, (dout, av, w_down), name=name,
        out_shape=(jax.ShapeDtypeStruct((T, C), MXU_DTYPE), jax.ShapeDtypeStruct((T, F), MXU_DTYPE),
                   jax.ShapeDtypeStruct((8, C), F32)),
        grid=(T // tm,),
        in_specs=[pl.BlockSpec((tm, D), lambda i: (i, 0)), pl.BlockSpec((tm, C), lambda i: (i, 0)), full(w_down)],
        out_specs=(pl.BlockSpec((tm, C), lambda i: (i, 0)), pl.BlockSpec((tm, F), lambda i: (i, 0)),
                   pl.BlockSpec((8, C), lambda i: (0, 0))),
        scratch_shapes=[],
        compiler_params=_params(("arbitrary",)),
    )


def _rg_gates(xr, wa_ref, ba_ref, wx_ref, bx_ref, sp_ref):
    xb = xr.astype(MXU_DTYPE)
    pa, px = [], []
    for hd in range(RG_HEADS):
        sl = slice(hd * RG_BW, (hd + 1) * RG_BW)
        pa.append(jnp.dot(xb[:, sl], wa_ref[hd].astype(MXU_DTYPE), preferred_element_type=F32))
        px.append(jnp.dot(xb[:, sl], wx_ref[hd].astype(MXU_DTYPE), preferred_element_type=F32))
    r = _sigmoid(jnp.concatenate(pa, axis=1) + ba_ref[...])
    ig = _sigmoid(jnp.concatenate(px, axis=1) + bx_ref[...])
    la = -RG_C * r * sp_ref[...]
    a = jnp.exp(la)
    mult = jnp.sqrt(_neg_expm1(2.0 * la))
    return xb, r, ig, a, mult


def _rg_conv(x, fills, cw_ref, cb_ref, row):
    x1 = _shift_down(x, 1, fills, row)
    x2 = _shift_down(x, 2, fills, row)
    x3 = _shift_down(x, 3, fills, row)
    xr = cb_ref[...] + cw_ref[3:4, :] * x + cw_ref[2:3, :] * x1 + cw_ref[1:2, :] * x2 + cw_ref[0:1, :] * x3
    return xr, (x3, x2, x1, x)


def rg_fwd(xg, h, conv_w, conv_b, w_a, b_a, w_x, b_x, sp, w_out, tm, name, comm=None):
    T, D2 = xg.shape
    D = D2 // 2
    nb = tm // 8

    def body(xg_ref, h_ref, cw_ref, cb_ref, wa_ref, ba_ref, wx_ref, bx_ref, sp_ref, wo_ref, o_ref, hs_ref,
             xcarry, hcarry, a_s, b_s):
        @pl.when(pl.program_id(0) == 0)
        def _():
            xcarry[...] = jnp.zeros_like(xcarry)
            hcarry[...] = jnp.zeros_like(hcarry)

        row = lax.broadcasted_iota(jnp.int32, (tm, 1), 0)
        x = xg_ref[:, 0:D]
        fills = [xcarry[7:8, :], xcarry[6:7, :], xcarry[5:6, :]]
        xr, _ = _rg_conv(x, fills, cw_ref, cb_ref, row)
        xcarry[...] = xg_ref[tm - 8:tm, 0:D]
        _, r, ig, a, mult = _rg_gates(xr, wa_ref, ba_ref, wx_ref, bx_ref, sp_ref)
        a_s[...] = a
        b_s[...] = mult * ig * xr
        row8 = lax.broadcasted_iota(jnp.int32, (8, 1), 0)

        def blk(j, c):
            o = pl.multiple_of(j * 8, 8)
            A = a_s[pl.ds(o, 8), :]
            H = b_s[pl.ds(o, 8), :]
            for s in (1, 2, 4):
                m = row8 >= s
                H = H + A * jnp.where(m, pltpu.roll(H, s, 0), 0.0)
                A = A * jnp.where(m, pltpu.roll(A, s, 0), 1.0)
            H = H + A * c
            hs_ref[pl.ds(o, 8), :] = H
            return H[7:8, :]

        c = lax.fori_loop(0, nb, blk, hcarry[0:1, :])
        hcarry[0:1, :] = c
        y = hs_ref[...] * _gelu(xg_ref[:, D:D2])
        o_ref[...] = h_ref[...] + _dot(y, wo_ref[...])

    full = lambda a: pl.BlockSpec(a.shape, lambda i: (0,) * a.ndim)
    args = (xg, h, conv_w, conv_b, w_a, b_a, w_x, b_x, sp, w_out)
    return _hosted(
        body, comm, args, name=name,
        out_shape=(jax.ShapeDtypeStruct((T, D), F32), jax.ShapeDtypeStruct((T, D), F32)),
        grid=(T // tm,),
        in_specs=[pl.BlockSpec((tm, D2), lambda i: (i, 0)), pl.BlockSpec((tm, D), lambda i: (i, 0))]
        + [full(a) for a in args[2:]],
        out_specs=(pl.BlockSpec((tm, D), lambda i: (i, 0)), pl.BlockSpec((tm, D), lambda i: (i, 0))),
        scratch_shapes=[pltpu.VMEM((8, D), F32), pltpu.VMEM((8, D), F32), pltpu.VMEM((tm, D), F32),
                        pltpu.VMEM((tm, D), F32)],
        compiler_params=_params(("arbitrary",)),
    )


def rg_bwd(dout, xg, hs, conv_w, conv_b, w_a, b_a, w_x, b_x, sp, w_out, tm, name, comm=None):
    T, D2 = xg.shape
    D = D2 // 2
    nt = T // tm
    nb = tm // 8
    kw = conv_w.shape[0]

    def body(do_ref, xg_ref, xh_ref, hs_ref, hh_ref, cw_ref, cb_ref, wa_ref, ba_ref, wx_ref, bx_ref, sp_ref, wo_ref,
             dxg_ref, y_ref, dwa_ref, dwx_ref, dba_ref, dbx_ref, dsp_ref, dcb_ref, dcw_ref,
             acarry, lcarry, dcarry, a_s, b_s, l_s):
        i = pl.program_id(0)
        first_tile = i == nt - 1

        @pl.when(i == 0)
        def _():
            for ref in (acarry, lcarry, dcarry, dwa_ref, dwx_ref, dba_ref, dbx_ref, dsp_ref, dcb_ref, dcw_ref):
                ref[...] = jnp.zeros_like(ref)

        row = lax.broadcasted_iota(jnp.int32, (tm, 1), 0)
        keep = jnp.where(first_tile, 0.0, 1.0)
        x = xg_ref[:, 0:D]
        gate = xg_ref[:, D:D2]
        xh = xh_ref[...] * keep
        fills = [xh[7:8, :], xh[6:7, :], xh[5:6, :]]
        xr, taps = _rg_conv(x, fills, cw_ref, cb_ref, row)
        xb, r, ig, a, mult = _rg_gates(xr, wa_ref, ba_ref, wx_ref, bx_ref, sp_ref)
        hs = hs_ref[...]
        hprev = _shift_down(hs, 1, [hh_ref[7:8, :] * keep], row)
        dy = _dot_nt(do_ref[...], wo_ref[...])
        gg, dgg = _gelu_and_grad(gate)
        y_ref[...] = (hs * gg).astype(MXU_DTYPE)
        dxg_ref[:, D:D2] = (dy * hs * dgg).astype(MXU_DTYPE)
        a_s[...] = _shift_up(a, 1, [acarry[0:1, :]], row)
        b_s[...] = dy * gg
        row8 = lax.broadcasted_iota(jnp.int32, (8, 1), 0)

        def blk(jj, c):
            o = pl.multiple_of((nb - 1 - jj) * 8, 8)
            A = a_s[pl.ds(o, 8), :]
            H = b_s[pl.ds(o, 8), :]
            for s in (1, 2, 4):
                m = row8 < 8 - s
                H = H + A * jnp.where(m, pltpu.roll(H, 8 - s, 0), 0.0)
                A = A * jnp.where(m, pltpu.roll(A, 8 - s, 0), 1.0)
            H = H + A * c
            l_s[pl.ds(o, 8), :] = H
            return H[0:1, :]

        c = lax.fori_loop(0, nb, blk, lcarry[0:1, :])
        lcarry[0:1, :] = c
        acarry[0:1, :] = a[0:1, :]
        lam = l_s[...]
        dla = lam * hprev * a - (lam * ig * xr) * (a * a) / mult
        dig = lam * mult * xr
        dxr = lam * mult * ig
        spv = sp_ref[...]
        dsp_ref[...] += _rowsum8(dla * (-RG_C) * r)
        dpa = (dla * (-RG_C) * spv) * r * (1.0 - r)
        dpx = dig * ig * (1.0 - ig)
        dba_ref[...] += _rowsum8(dpa)
        dbx_ref[...] += _rowsum8(dpx)
        dpab = dpa.astype(MXU_DTYPE)
        dpxb = dpx.astype(MXU_DTYPE)
        back = []
        for hd in range(RG_HEADS):
            sl = slice(hd * RG_BW, (hd + 1) * RG_BW)
            dwa_ref[hd] += _dot_tn(xb[:, sl], dpab[:, sl])
            dwx_ref[hd] += _dot_tn(xb[:, sl], dpxb[:, sl])
            back.append(_dot_nt(dpab[:, sl], wa_ref[hd]) + _dot_nt(dpxb[:, sl], wx_ref[hd]))
        dxr = dxr + jnp.concatenate(back, axis=1)
        nfills = [dcarry[0:1, :], dcarry[1:2, :], dcarry[2:3, :]]
        dxp = cw_ref[kw - 1:kw, :] * dxr
        for s in range(1, kw):
            dxp = dxp + cw_ref[kw - 1 - s:kw - s, :] * _shift_up(dxr, s, nfills, row)
        dcarry[...] = dxr[0:8, :]
        dxg_ref[:, 0:D] = dxp.astype(MXU_DTYPE)
        dcb_ref[...] += _rowsum8(dxr)
        for k in range(kw):
            dcw_ref[k] += _rowsum8(dxr * taps[k])

    rt = lambda i: nt - 1 - i
    halo = lambda i: jnp.maximum((nt - 1 - i) * (tm // 8) - 1, 0)
    full = lambda a: pl.BlockSpec(a.shape, lambda i: (0,) * a.ndim)
    params = (conv_w, conv_b, w_a, b_a, w_x, b_x, sp, w_out)
    acc = lambda shape: pl.BlockSpec(shape, lambda i: (0,) * len(shape))
    return _hosted(
        body, comm, (dout, xg, xg, hs, hs, *params), name=name,
        out_shape=(jax.ShapeDtypeStruct((T, D2), MXU_DTYPE), jax.ShapeDtypeStruct((T, D), MXU_DTYPE),
                   jax.ShapeDtypeStruct((RG_HEADS, RG_BW, RG_BW), F32), jax.ShapeDtypeStruct((RG_HEADS, RG_BW, RG_BW), F32),
                   jax.ShapeDtypeStruct((8, D), F32), jax.ShapeDtypeStruct((8, D), F32), jax.ShapeDtypeStruct((8, D), F32),
                   jax.ShapeDtypeStruct((8, D), F32), jax.ShapeDtypeStruct((kw, 8, D), F32)),
        grid=(nt,),
        in_specs=[pl.BlockSpec((tm, D), lambda i: (rt(i), 0)), pl.BlockSpec((tm, D2), lambda i: (rt(i), 0)),
                  pl.BlockSpec((8, D), lambda i: (halo(i), 0)), pl.BlockSpec((tm, D), lambda i: (rt(i), 0)),
                  pl.BlockSpec((8, D), lambda i: (halo(i), 0))] + [full(a) for a in params],
        out_specs=(pl.BlockSpec((tm, D2), lambda i: (rt(i), 0)), pl.BlockSpec((tm, D), lambda i: (rt(i), 0)),
                   acc((RG_HEADS, RG_BW, RG_BW)), acc((RG_HEADS, RG_BW, RG_BW)), acc((8, D)), acc((8, D)), acc((8, D)),
                   acc((8, D)), acc((kw, 8, D))),
        scratch_shapes=[pltpu.VMEM((8, D), F32), pltpu.VMEM((8, D), F32), pltpu.VMEM((8, D), F32),
                        pltpu.VMEM((tm, D), F32), pltpu.VMEM((tm, D), F32), pltpu.VMEM((tm, D), F32)],
        compiler_params=_params(("arbitrary",)),
    )


_SBW = 2 * S5_SB * S5_P
_SBH = S5_SB * S5_P
_SBU = S5_SB * S5_GC


def _regroup(x, seg_len):
    n, c = x.shape
    return jnp.swapaxes(x.reshape(8, seg_len, c), 0, 1).reshape(n, c)


def _ungroup(x, seg_len):
    n, c = x.shape
    return jnp.swapaxes(x.reshape(seg_len, 8, c), 0, 1).reshape(n, c)


def _s5_lanes(sb):
    return (slice(sb * _SBW, sb * _SBW + _SBH), slice(sb * _SBW + _SBH, (sb + 1) * _SBW),
            slice(sb * _SBH, (sb + 1) * _SBH))


def _s5_scan_fwd(S, carry, cin, ap_r, ap_i, aq_r, aq_i, seg_len, sb):
    row8 = lax.broadcasted_iota(jnp.int32, (8, 1), 0)
    lr, li, la = _s5_lanes(sb)
    ar, ai = ap_r[0:1, la], ap_i[0:1, la]
    hr, hi = S[0:8, lr], S[0:8, li]
    for i in range(1, seg_len):
        blk = slice(i * 8, (i + 1) * 8)
        hr, hi = ar * hr - ai * hi + S[blk, lr], ar * hi + ai * hr + S[blk, li]
        S[blk, lr] = hr
        S[blk, li] = hi
    for s, idx in ((1, 0), (2, 1), (4, 3)):
        qr, qi = aq_r[idx:idx + 1, la], aq_i[idx:idx + 1, la]
        m = row8 >= s
        sr = jnp.where(m, pltpu.roll(hr, s, 0), 0.0)
        si = jnp.where(m, pltpu.roll(hi, s, 0), 0.0)
        hr, hi = hr + qr * sr - qi * si, hi + qr * si + qi * sr
    cr, ci = carry[0:1, lr], carry[0:1, li]
    pr, pi = aq_r[:, la], aq_i[:, la]
    hr, hi = hr + pr * cr - pi * ci, hi + pr * ci + pi * cr
    xr = jnp.where(row8 == 0, cr, pltpu.roll(hr, 1, 0))
    xi = jnp.where(row8 == 0, ci, pltpu.roll(hi, 1, 0))
    carry[0:1, lr] = hr[7:8, :]
    carry[0:1, li] = hi[7:8, :]
    if cin is not None:
        cin[:, lr] = xr
        cin[:, li] = xi
    for i in range(seg_len):
        blk = slice(i * 8, (i + 1) * 8)
        pr, pi = ap_r[i:i + 1, la], ap_i[i:i + 1, la]
        S[blk, lr] += pr * xr - pi * xi
        S[blk, li] += pr * xi + pi * xr


def s5_fwd(u, h, wb, wc, ap_r, ap_i, aq_r, aq_i, d, w_glu, w_out, tm, name):
    T, D = u.shape
    nt = T // tm
    seg = tm // 8

    def body(u_ref, h_ref, wb_ref, wc_ref, apr_ref, api_ref, aqr_ref, aqi_ref, d_ref, wg_ref, wo_ref,
             o_ref, yp_ref, gl_ref, st_ref, S, carry):
        @pl.when(pl.program_id(0) == 0)
        def _():
            carry[...] = jnp.zeros_like(carry)

        st_ref[...] = carry[...]
        uv = _regroup(u_ref[...], seg)
        ub = uv.astype(MXU_DTYPE)
        for sb in range(S5_NSB):
            S[:, sb * _SBW:(sb + 1) * _SBW] = jnp.dot(ub[:, sb * _SBU:(sb + 1) * _SBU], wb_ref[sb].astype(MXU_DTYPE),
                                                      preferred_element_type=F32)
        for sb in range(S5_NSB):
            _s5_scan_fwd(S, carry, None, apr_ref, api_ref, aqr_ref, aqi_ref, seg, sb)
        ys = [jnp.dot(S[:, sb * _SBW:(sb + 1) * _SBW].astype(MXU_DTYPE), wc_ref[sb].astype(MXU_DTYPE),
                      preferred_element_type=F32) for sb in range(S5_NSB)]
        yp = jnp.concatenate(ys, axis=1) + d_ref[...] * uv
        yp_ref[...] = _ungroup(yp, seg)
        gl = _dot(_gelu(yp), wg_ref[...])
        gl_ref[...] = _ungroup(gl, seg)
        out = gl[:, 0:D] * _sigmoid(gl[:, D:2 * D])
        o_ref[...] = h_ref[...] + _ungroup(_dot(out, wo_ref[...]), seg)

    full = lambda a: pl.BlockSpec(a.shape, lambda i: (0,) * a.ndim)
    args = (u, h, wb, wc, ap_r, ap_i, aq_r, aq_i, d, w_glu, w_out)
    return _call(
        body, name=name,
        out_shape=(jax.ShapeDtypeStruct((T, D), F32), jax.ShapeDtypeStruct((T, D), F32),
                   jax.ShapeDtypeStruct((T, 2 * D), F32), jax.ShapeDtypeStruct((nt, 8, S5_NS), F32)),
        grid=(nt,),
        in_specs=[pl.BlockSpec((tm, D), lambda i: (i, 0)), pl.BlockSpec((tm, D), lambda i: (i, 0))]
        + [full(a) for a in args[2:]],
        out_specs=(pl.BlockSpec((tm, D), lambda i: (i, 0)), pl.BlockSpec((tm, D), lambda i: (i, 0)),
                   pl.BlockSpec((tm, 2 * D), lambda i: (i, 0)), pl.BlockSpec((None, 8, S5_NS), lambda i: (i, 0, 0))),
        scratch_shapes=[pltpu.VMEM((tm, S5_NS), F32), pltpu.VMEM((8, S5_NS), F32)],
        compiler_params=_params(("arbitrary",)),
    )(*args)


def s5_bwd_glu(dout, gl, ypre, u, w_glu, w_out, tm, name):
    T, D = u.shape

    def body(do_ref, gl_ref, yp_ref, u_ref, wg_ref, wo_ref, dy_ref, oact_ref, dgl_ref, gact_ref, dd_ref):
        @pl.when(pl.program_id(0) == 0)
        def _():
            dd_ref[...] = jnp.zeros_like(dd_ref)

        gl1 = gl_ref[:, 0:D]
        sg = _sigmoid(gl_ref[:, D:2 * D])
        oact_ref[...] = (gl1 * sg).astype(MXU_DTYPE)
        dgo = _dot_nt(do_ref[...], wo_ref[...])
        d1 = (dgo * sg).astype(MXU_DTYPE)
        d2 = (dgo * gl1 * sg * (1.0 - sg)).astype(MXU_DTYPE)
        dgl_ref[:, 0:D] = d1
        dgl_ref[:, D:2 * D] = d2
        dg = _dot_nt(d1, wg_ref[:, 0:D]) + _dot_nt(d2, wg_ref[:, D:2 * D])
        g, gd = _gelu_and_grad(yp_ref[...])
        gact_ref[...] = g.astype(MXU_DTYPE)
        dy = dg * gd
        dy_ref[...] = dy
        dd_ref[...] += _rowsum8(dy * u_ref[...])

    full = lambda a: pl.BlockSpec(a.shape, lambda i: (0,) * a.ndim)
    rs = lambda c: pl.BlockSpec((tm, c), lambda i: (i, 0))
    return _call(
        body, name=name,
        out_shape=(jax.ShapeDtypeStruct((T, D), F32), jax.ShapeDtypeStruct((T, D), MXU_DTYPE),
                   jax.ShapeDtypeStruct((T, 2 * D), MXU_DTYPE), jax.ShapeDtypeStruct((T, D), MXU_DTYPE),
                   jax.ShapeDtypeStruct((8, D), F32)),
        grid=(T // tm,),
        in_specs=[rs(D), rs(2 * D), rs(D), rs(D), full(w_glu), full(w_out)],
        out_specs=(rs(D), rs(D), rs(2 * D), rs(D), pl.BlockSpec((8, D), lambda i: (0, 0))),
        compiler_params=_params(("arbitrary",)),
    )(dout, gl, ypre, u, w_glu, w_out)


def s5_bwd_ssm(dy, u, st, wb, wc, tables, d, tm, name, comm=None):
    T, D = u.shape
    nt = T // tm
    seg = tm // 8
    GP = S5_G * S5_P
    assert len(tables) == 8

    def body(dy_ref, u_ref, st_ref, wb_ref, wc_ref, apr_ref, api_ref, aqr_ref, aqi_ref, aprr_ref, apir_ref,
             aqrr_ref, aqir_ref, d_ref, du_ref, dwb_ref, dwc_ref, dar_ref, dai_ref, S, L, carry, lcarry, cin):
        @pl.when(pl.program_id(0) == 0)
        def _():
            for ref in (lcarry, dwb_ref, dwc_ref, dar_ref, dai_ref):
                ref[...] = jnp.zeros_like(ref)

        uv = _regroup(u_ref[...], seg)
        ub = uv.astype(MXU_DTYPE)
        dyv = _regroup(dy_ref[...], seg)
        dyb = dyv.astype(MXU_DTYPE)
        carry[...] = st_ref[...]
        row8 = lax.broadcasted_iota(jnp.int32, (8, 1), 0)
        dus = []
        for sb in range(S5_NSB):
            ls = slice(sb * _SBW, (sb + 1) * _SBW)
            us = slice(sb * _SBU, (sb + 1) * _SBU)
            S[:, ls] = jnp.dot(ub[:, us], wb_ref[sb].astype(MXU_DTYPE), preferred_element_type=F32)
            L[:, ls] = _dot_nt(dyb[:, us], wc_ref[sb])
            _s5_scan_fwd(S, carry, cin, apr_ref, api_ref, aqr_ref, aqi_ref, seg, sb)
            lr, li, la = _s5_lanes(sb)
            ar, ai = apr_ref[0:1, la], api_ref[0:1, la]
            last = slice((seg - 1) * 8, seg * 8)
            gr, gi = L[last, lr], L[last, li]
            for i in reversed(range(seg - 1)):
                blk = slice(i * 8, (i + 1) * 8)
                gr, gi = ar * gr + ai * gi + L[blk, lr], ar * gi - ai * gr + L[blk, li]
                L[blk, lr] = gr
                L[blk, li] = gi
            for s, idx in ((1, 0), (2, 1), (4, 3)):
                qr, qi = aqr_ref[idx:idx + 1, la], aqi_ref[idx:idx + 1, la]
                m = row8 < 8 - s
                sr = jnp.where(m, pltpu.roll(gr, 8 - s, 0), 0.0)
                si = jnp.where(m, pltpu.roll(gi, 8 - s, 0), 0.0)
                gr, gi = gr + qr * sr + qi * si, gi + qr * si - qi * sr
            cr, ci = lcarry[0:1, lr], lcarry[0:1, li]
            pr, pi = aqrr_ref[:, la], aqir_ref[:, la]
            gr, gi = gr + pr * cr + pi * ci, gi + pr * ci - pi * cr
            xr = jnp.where(row8 == 7, cr, pltpu.roll(gr, 7, 0))
            xi = jnp.where(row8 == 7, ci, pltpu.roll(gi, 7, 0))
            lcarry[0:1, lr] = gr[0:1, :]
            lcarry[0:1, li] = gi[0:1, :]
            acc_r = jnp.zeros((8, _SBH), F32)
            acc_i = jnp.zeros((8, _SBH), F32)
            for i in range(seg):
                blk = slice(i * 8, (i + 1) * 8)
                pr, pi = aprr_ref[i:i + 1, la], apir_ref[i:i + 1, la]
                gr = L[blk, lr] + pr * xr + pi * xi
                gi = L[blk, li] + pr * xi - pi * xr
                L[blk, lr] = gr
                L[blk, li] = gi
                if i == 0:
                    hpr, hpi = cin[:, lr], cin[:, li]
                else:
                    hpr, hpi = S[(i - 1) * 8:i * 8, lr], S[(i - 1) * 8:i * 8, li]
                acc_r = acc_r + gr * hpr + gi * hpi
                acc_i = acc_i + gi * hpr - gr * hpi
            dar_ref[:, la] += acc_r
            dai_ref[:, la] += acc_i
            lb = L[:, ls].astype(MXU_DTYPE)
            dwb_ref[sb] += _dot_tn(lb, ub[:, us])
            dwc_ref[sb] += _dot_tn(S[:, ls].astype(MXU_DTYPE), dyb[:, us])
            dus.append(_dot_nt(lb, wb_ref[sb]))
        du_ref[...] = _ungroup(jnp.concatenate(dus, axis=1) + dyv * d_ref[...], seg).astype(MXU_DTYPE)

    rt = lambda i: nt - 1 - i
    full = lambda a: pl.BlockSpec(a.shape, lambda i: (0,) * a.ndim)
    acc = lambda shape: pl.BlockSpec(shape, lambda i: (0,) * len(shape))
    return _hosted(
        body, comm, (dy, u, st, wb, wc, *tables, d), name=name,
        out_shape=(jax.ShapeDtypeStruct((T, D), MXU_DTYPE), jax.ShapeDtypeStruct((S5_NSB, _SBW, _SBU), F32),
                   jax.ShapeDtypeStruct((S5_NSB, _SBW, _SBU), F32), jax.ShapeDtypeStruct((8, GP), F32),
                   jax.ShapeDtypeStruct((8, GP), F32)),
        grid=(nt,),
        in_specs=[pl.BlockSpec((tm, D), lambda i: (rt(i), 0)), pl.BlockSpec((tm, D), lambda i: (rt(i), 0)),
                  pl.BlockSpec((None, 8, S5_NS), lambda i: (rt(i), 0, 0)), full(wb), full(wc)]
        + [full(t) for t in tables] + [full(d)],
        out_specs=(pl.BlockSpec((tm, D), lambda i: (rt(i), 0)), acc((S5_NSB, _SBW, _SBU)), acc((S5_NSB, _SBW, _SBU)),
                   acc((8, GP)), acc((8, GP))),
        scratch_shapes=[pltpu.VMEM((tm, S5_NS), F32), pltpu.VMEM((tm, S5_NS), F32), pltpu.VMEM((8, S5_NS), F32),
                        pltpu.VMEM((8, S5_NS), F32), pltpu.VMEM((8, S5_NS), F32)],
        compiler_params=_params(("arbitrary",)),
    )


def final_loss(h, g, target, tm, name):
    T, D = h.shape

    def body(h_ref, g_ref, t_ref, dh_ref, se_ref, dg_ref):
        @pl.when(pl.program_id(0) == 0)
        def _():
            se_ref[...] = jnp.zeros_like(se_ref)
            dg_ref[...] = jnp.zeros_like(dg_ref)

        r, xhat = _rms(h_ref[...])
        gv = g_ref[...]
        e = xhat * gv - t_ref[...]
        se_ref[...] += _rowsum8(e * e)
        dy = e * (1.0 / D)
        dg_ref[...] += _rowsum8(dy * xhat)
        dh_ref[...] = _rms_bwd(dy, xhat, r, gv)

    rs = pl.BlockSpec((tm, D), lambda i: (i, 0))
    acc = pl.BlockSpec((8, D), lambda i: (0, 0))
    return _call(
        body, name=name,
        out_shape=(jax.ShapeDtypeStruct((T, D), F32), jax.ShapeDtypeStruct((8, D), F32), jax.ShapeDtypeStruct((8, D), F32)),
        grid=(T // tm,), in_specs=[rs, pl.BlockSpec((1, D), lambda i: (0, 0)), rs], out_specs=(rs, acc, acc),
        compiler_params=_params(("arbitrary",)),
    )(h, g, target)


def _s5_discretize(a_re, a_im, log_dt, b_re, b_im):
    dt = jnp.exp(log_dt)[:, None]
    mag = jnp.exp(a_re * dt)
    abr = mag * jnp.cos(a_im * dt)
    abi = mag * jnp.sin(a_im * dt)
    ur, ui = abr - 1.0, abi
    den = a_re * a_re + a_im * a_im
    wr = (ur * a_re + ui * a_im) / den
    wi = (ui * a_re - ur * a_im) / den
    bbr = wr[..., None] * b_re - wi[..., None] * b_im
    bbi = wr[..., None] * b_im + wi[..., None] * b_re
    return abr, abi, bbr, bbi


def _s5_pack(abr, abi, bbr, bbi, c_re, c_im, seg_len):
    eye = jnp.eye(S5_SB, dtype=F32)
    b = jnp.stack([bbr, bbi], 0).reshape(2, S5_NSB, S5_SB, S5_P, S5_GC)
    wb = jnp.einsum('rsgpc,gh->shcrgp', b, eye).reshape(S5_NSB, _SBU, _SBW)
    c = jnp.stack([c_re, -c_im], 0).reshape(2, S5_NSB, S5_SB, S5_GC, S5_P)
    wc = jnp.einsum('rsgcp,gh->srgphc', c, eye).reshape(S5_NSB, _SBW, _SBU)
    def powers(r, i, n):
        fr, fi, br, bi = r, i, r, i
        m = 1
        while m < n:
            tr, ti = fr[m - 1:m], fi[m - 1:m]
            fr, fi = (jnp.concatenate([fr, fr * tr - fi * ti], 0), jnp.concatenate([fi, fr * ti + fi * tr], 0))
            br, bi = (jnp.concatenate([br * tr - bi * ti, br], 0), jnp.concatenate([br * ti + bi * tr, bi], 0))
            m *= 2
        return fr, fi, br, bi

    assert seg_len & (seg_len - 1) == 0
    ap = powers(abr.reshape(1, -1), abi.reshape(1, -1), seg_len)
    aq = powers(ap[0][seg_len - 1:seg_len], ap[1][seg_len - 1:seg_len], 8)
    return wb.astype(MXU_DTYPE), wc.astype(MXU_DTYPE), ap[0], ap[1], aq[0], aq[1], ap[2], ap[3], aq[2], aq[3]


def _s5_unpack_grads(dwb_t, dwc, dar8, dai8):
    eye = jnp.eye(S5_SB, dtype=F32)
    t = dwb_t.reshape(S5_NSB, 2, S5_SB, S5_P, S5_SB, S5_GC)
    db = jnp.einsum('srgphc,gh->rsgpc', t, eye).reshape(2, S5_G, S5_P, S5_GC)
    t = dwc.reshape(S5_NSB, 2, S5_SB, S5_P, S5_SB, S5_GC)
    dc = jnp.einsum('srgphc,gh->rsgcp', t, eye).reshape(2, S5_G, S5_GC, S5_P)
    return db[0], db[1], dc[0], -dc[1], dar8.sum(0).reshape(S5_G, S5_P), dai8.sum(0).reshape(S5_G, S5_P)


TM = 256
TM_FFN = 512
TM_S5 = 256


def _tn(a, b, name, comm=None):
    T, M, N = a.shape[0], a.shape[1], b.shape[1]
    bt = 4096 if a.dtype.itemsize + b.dtype.itemsize <= 4 else 2048
    return matmul_tn(a, b, min(M, 1024), min(N, 1024), bt if T % bt == 0 else T, name, comm=comm)


def local_step(x, target, W, sched):
    T, D = x.shape
    depth = W['norm_mix_g'].shape[0]
    row = lambda v: v.reshape(1, -1)
    saved = []
    h = x
    s5c = []
    tr = min(512, T)
    for j in range(W['s5_a_re'].shape[0]):
        prm = (W['s5_a_re'][j], W['s5_a_im'][j], W['s5_log_dt'][j], W['s5_b_re'][j], W['s5_b_im'][j])
        disc, disc_vjp = jax.vjp(_s5_discretize, *prm)
        s5c.append((*_s5_pack(*disc, W['s5_c_re'][j], W['s5_c_im'][j], min(TM_S5, T) // 8), disc_vjp))
    sp_all = jax.nn.softplus(-W['rg_lambda'])
    for i in range(depth):
        j = i // 2
        if i % 2 == 0:
            xg, hn = norm_matmul(h, row(W['norm_mix_g'][i]), W['rg_w_in'][j], F32, tr, None, f"rg_in_{i}")
            h1, hs = rg_fwd(xg, h, W['rg_conv_w'][j], row(W['rg_conv_b'][j]), W['rg_w_a'][j].astype(MXU_DTYPE),
                            row(W['rg_b_a'][j]), W['rg_w_x'][j].astype(MXU_DTYPE), row(W['rg_b_x'][j]), row(sp_all[j]),
                            W['rg_w_out'][j], TM, f"rg_fwd_{i}", comm=sched.comm(f"rg_fwd_{i}"))
            mix = (xg, hn, hs)
        else:
            u, hn = norm_matmul(h, row(W['norm_mix_g'][i]), W['s5_w_in'][j], F32, tr, None, f"s5_in_{i}")
            h1, ypre, gl, st = s5_fwd(u, h, *s5c[j][:6], row(W['s5_d'][j]), W['s5_w_glu'][j], W['s5_w_out'][j],
                                      min(TM_S5, T), f"s5_fwd_{i}")
            mix = (u, hn, ypre, gl, st)
        up, hn2 = norm_matmul(h1, row(W['norm_ffn_g'][i]), W['ffn_w_up'][i], MXU_DTYPE, tr, None, f"ffn_up_{i}",
                              comm=sched.comm(f"ffn_up_{i}"))
        h2, av = ffn_fwd(up, h1, W['ffn_conv_w'][i], row(W['ffn_conv_b'][i]), W['ffn_w_down'][i], min(TM_FFN, T), f"ffn_fwd_{i}",
                         comm=sched.comm(f"ffn_fwd_{i}"))
        saved.append((h, mix, h1, hn2, up, av))
        h = h2
    dh, se8, dgf8 = final_loss(h, row(W['norm_final_g']), target, tr, "final_loss")
    G = {k: [None] * len(v) for k, v in W.items() if k != 'norm_final_g'}
    G['norm_final_g'] = dgf8.sum(0)
    for i in reversed(range(depth)):
        j = i // 2
        h0, mix, h1, hn2, up, av = saved[i]
        dup, act, dcb8 = ffn_bwd_act(dh, av, W['ffn_w_down'][i], min(TM_FFN, T), f"ffn_bwd_act_{i}",
                                     comm=sched.comm(f"ffn_bwd_act_{i}", G))
        G['ffn_w_down'][i] = _tn(act, dh, f"ffn_dwdown_{i}")
        dh1, dg8, dupp, dcw8 = dx_norm_bwd(dup, W['ffn_w_up'][i], h1, row(W['norm_ffn_g'][i]), dh, TM,
                                           f"ffn_bwd_in_{i}", conv_w=W['ffn_conv_w'][i], conv_x=up)
        G['ffn_w_up'][i] = _tn(hn2, dupp, f"ffn_dwup_{i}")
        G['ffn_conv_b'][i] = dcb8.sum(0)
        G['ffn_conv_w'][i] = dcw8.sum(1)
        G['norm_ffn_g'][i] = dg8.sum(0)
        if i % 2 == 0:
            xg, hn, hs = mix
            dxg, y, dwa, dwx, dba8, dbx8, dsp8, dcb8, dcw8 = rg_bwd(
                dh1, xg, hs, W['rg_conv_w'][j], row(W['rg_conv_b'][j]), W['rg_w_a'][j].astype(MXU_DTYPE),
                row(W['rg_b_a'][j]), W['rg_w_x'][j].astype(MXU_DTYPE), row(W['rg_b_x'][j]), row(sp_all[j]),
                W['rg_w_out'][j], TM, f"rg_bwd_{i}", comm=sched.comm(f"rg_bwd_{i}", G))
            G['rg_w_out'][j] = _tn(y, dh1, f"rg_dwout_{i}")
            G['rg_w_a'][j], G['rg_w_x'][j] = dwa, dwx
            G['rg_b_a'][j] = dba8.sum(0).reshape(RG_HEADS, RG_BW)
            G['rg_b_x'][j] = dbx8.sum(0).reshape(RG_HEADS, RG_BW)
            G['rg_lambda'][j] = dsp8.sum(0) * (-jax.nn.sigmoid(-W['rg_lambda'][j]))
            G['rg_conv_b'][j] = dcb8.sum(0)
            G['rg_conv_w'][j] = dcw8.sum(1)
            dh, dg8 = dx_norm_bwd(dxg, W['rg_w_in'][j], h0, row(W['norm_mix_g'][i]), dh1, TM, f"rg_bwd_in_{i}",
                                  comm=sched.comm(f"rg_bwd_in_{i}", G))
            G['rg_w_in'][j] = _tn(hn, dxg, f"rg_dwin_{i}", comm=sched.comm(f"rg_dwin_{i}", G))
        else:
            u, hn, ypre, gl, st = mix
            disc_vjp = s5c[j][-1]
            dy, oact, dgl, gact, dd8 = s5_bwd_glu(dh1, gl, ypre, u, W['s5_w_glu'][j], W['s5_w_out'][j], TM,
                                                  f"s5_bwd_glu_{i}")
            G['s5_w_out'][j] = _tn(oact, dh1, f"s5_dwout_{i}")
            G['s5_w_glu'][j] = _tn(gact, dgl, f"s5_dwglu_{i}")
            du, dwb_t, dwc, dar8, dai8 = s5_bwd_ssm(dy, u, st, *s5c[j][:2], s5c[j][2:10], row(W['s5_d'][j]),
                                                    min(TM_S5, T), f"s5_bwd_ssm_{i}",
                                                    comm=sched.comm(f"s5_bwd_ssm_{i}", G))
            dh, dg8 = dx_norm_bwd(du, W['s5_w_in'][j], h0, row(W['norm_mix_g'][i]), dh1, TM, f"s5_bwd_in_{i}")
            G['s5_w_in'][j] = _tn(hn, du, f"s5_dwin_{i}")
            dbbr, dbbi, dcr, dci, dabr, dabi = _s5_unpack_grads(dwb_t, dwc, dar8, dai8)
            da_re, da_im, dlog_dt, db_re, db_im = disc_vjp((dabr, dabi, dbbr, dbbi))
            G['s5_a_re'][j], G['s5_a_im'][j], G['s5_log_dt'][j] = da_re, da_im, dlog_dt
            G['s5_b_re'][j], G['s5_b_im'][j], G['s5_c_re'][j], G['s5_c_im'][j] = db_re, db_im, dcr, dci
            G['s5_d'][j] = dd8.sum(0)
        G['norm_mix_g'][i] = dg8.sum(0)
    G = {k: (v if (k == 'norm_final_g' or k in BIG) else jnp.stack(v, 0)) for k, v in G.items()}
    return se8, dh, G


BIG = {'rg_w_in': 1, 'rg_w_out': 0, 's5_w_in': 0, 's5_w_glu': 1, 's5_w_out': 0, 'ffn_w_up': 1, 'ffn_w_down': 0}
SMALL_SHARDED = ('rg_conv_w', 'ffn_conv_w', 's5_d')
REPLICATED = ('norm_mix_g', 'norm_ffn_g', 'norm_final_g', 'rg_conv_b', 'rg_w_a', 'rg_b_a', 'rg_w_x', 'rg_b_x',
              'rg_lambda', 's5_a_re', 's5_a_im', 's5_log_dt', 's5_b_re', 's5_b_im', 's5_c_re', 's5_c_im', 'ffn_conv_b')
REP_LATE = ('norm_mix_g',)
REP_MAIN = tuple(k for k in REPLICATED if k not in REP_LATE)


def _me():
    x, y, c = lax.axis_index("x"), lax.axis_index("y"), lax.axis_index("c")
    return x, y, c, 4 * x + 2 * y + c


def _win(ref, axis, dev, width):
    idx = [slice(None)] * len(ref.shape)
    idx[axis] = pl.ds(pl.multiple_of(dev * width, width), width)
    return ref.at[tuple(idx)]


def gather_plan(items, deliver):
    n = len(items)

    def tools(ins, outs, sems):
        send_sems, recv_sems, local_sems = sems
        x, y, c, me = _me()
        sib = (x, y, 1 - c)
        chips = [(1 - x, y), (x, 1 - y), (1 - x, 1 - y)]
        num = lambda px, py, pc: 4 * px + 2 * py + pc

        def src_of(a):
            return ins[a] if items[a][1] is None else ins[a].at[items[a][1]]

        def block(a, dev):
            return _win(outs[a], items[a][2], dev, src_of(a).shape[items[a][2]])

        def copy(a, k, dev, to, own=False):
            return pltpu.make_async_remote_copy(
                src_ref=src_of(a) if own else block(a, dev), dst_ref=block(a, dev),
                send_sem=send_sems.at[a, k], recv_sem=recv_sems.at[a, k], device_id=to, device_id_type=MESH)

        mine = lambda: [pltpu.make_async_copy(src_of(a), block(a, me), local_sems.at[a]) for a in range(n)]
        own = lambda: [cp for a in range(n) for cp in
                       [copy(a, 0, me, sib, own=True)] + [copy(a, 1 + j, me, (*chip, c), own=True)
                                                          for j, chip in enumerate(chips)]]
        arrived = lambda j, a: copy(a, 1 + j, num(*chips[j], c), (x, y, c))
        passed = lambda j, a: copy(a, 4 + j, num(*chips[j], c), sib)
        from_sib = lambda: ([copy(a, 0, num(x, y, 1 - c), (x, y, c)) for a in range(n)]
                            + [copy(a, 4 + j, num(*chip, 1 - c), (x, y, c)) for j, chip in enumerate(chips)
                               for a in range(n)])
        return mine, own, arrived, passed, from_sib

    def start(ins, outs, sems):
        mine, own, _, _, _ = tools(ins, outs, sems)
        for cp in mine() + own():
            cp.start()

    def middle(ins, outs, sems):
        _, _, arrived, passed, _ = tools(ins, outs, sems)
        for j in range(3):
            for a in range(n):
                arrived(j, a).wait_recv()
                passed(j, a).start()

    def finish(ins, outs, sems):
        mine, own, _, passed, from_sib = tools(ins, outs, sems)
        for cp in from_sib():
            cp.wait_recv()
        for cp in own() + [passed(j, a) for j in range(3) for a in range(n)]:
            cp.wait_send()
        for cp in mine():
            cp.wait()

    return Comm([it[0] for it in items], [jax.ShapeDtypeStruct(it[3], it[0].dtype) for it in items],
                [pltpu.SemaphoreType.DMA((n, 7)), pltpu.SemaphoreType.DMA((n, 7)), pltpu.SemaphoreType.DMA((n,))],
                start, middle, finish, deliver)


def exchange_plan(items, deliver):
    n = len(items)
    width = [arr.shape[axis] // N_DEV for arr, axis in items]
    shard = [arr.shape[:axis] + (w,) + arr.shape[axis + 1:] for (arr, axis), w in zip(items, width)]

    def tools(ins, outs, sems):
        send_sems, recv_sems, local_sems = sems
        x, y, c, me = _me()
        piece = lambda a, dev: _win(ins[a], items[a][1], dev, width[a])
        mine = lambda: [pltpu.make_async_copy(piece(a, me), outs[a].at[me], local_sems.at[a]) for a in range(n)]

        def remote(sending):
            cps = []
            for k in range(1, N_DEV):
                px, py, pc = (1 - x) if k & 4 else x, (1 - y) if k & 2 else y, (1 - c) if k & 1 else c
                peer = 4 * px + 2 * py + pc
                for a in range(n):
                    src, dst = (piece(a, peer), outs[a].at[me]) if sending else (piece(a, me), outs[a].at[peer])
                    cps.append(pltpu.make_async_remote_copy(
                        src_ref=src, dst_ref=dst, send_sem=send_sems.at[a, k - 1], recv_sem=recv_sems.at[a, k - 1],
                        device_id=(px, py, pc), device_id_type=MESH))
            return cps

        return mine, remote

    def start(ins, outs, sems):
        mine, remote = tools(ins, outs, sems)
        for cp in mine() + remote(True):
            cp.start()

    def middle(ins, outs, sems):
        pass

    def finish(ins, outs, sems):
        mine, remote = tools(ins, outs, sems)
        for cp in remote(False):
            cp.wait_recv()
        for cp in remote(True):
            cp.wait_send()
        for cp in mine():
            cp.wait()

    return Comm([it[0] for it in items], [jax.ShapeDtypeStruct((N_DEV,) + s, it[0].dtype) for it, s in zip(items, shard)],
                [pltpu.SemaphoreType.DMA((n, 7)), pltpu.SemaphoreType.DMA((n, 7)), pltpu.SemaphoreType.DMA((n,))],
                start, middle, finish, deliver)


def adam_update(parts, w, m, v, name):
    R, C = w.shape
    br = next((b for b in (256, 128) if R > b and R % b == 0), R)
    np_ = parts.shape[0]

    def body(p_ref, w_ref, m_ref, v_ref, g_ref, d_ref, nm_ref, nv_ref):
        _adam_body(np_, p_ref, w_ref, m_ref, v_ref, g_ref, d_ref, nm_ref, nv_ref)

    bs = pl.BlockSpec((br, C), lambda i: (i, 0))
    out = jax.ShapeDtypeStruct((R, C), F32)
    return _call(
        body, name=name, out_shape=(out, out, out, out), grid=(R // br,),
        in_specs=[pl.BlockSpec((np_, br, C), lambda i: (0, i, 0)), bs, bs, bs], out_specs=(bs, bs, bs, bs),
        compiler_params=_params(("parallel",)),
    )(parts, w, m, v)


def _adam_body(np_, p_ref, w_ref, m_ref, v_ref, g_ref, d_ref, nm_ref, nv_ref):
    c1 = 1.0 / (1.0 - ADAM_B1 ** ADAM_STEP)
    c2 = 1.0 / (1.0 - ADAM_B2 ** ADAM_STEP)
    g = p_ref[0].astype(F32)
    for p in range(1, np_):
        g = g + p_ref[p].astype(F32)
    nm = ADAM_B1 * m_ref[...] + (1.0 - ADAM_B1) * g
    nv = ADAM_B2 * v_ref[...] + (1.0 - ADAM_B2) * (g * g)
    g_ref[...] = g
    nm_ref[...] = nm
    nv_ref[...] = nv
    d_ref[...] = -ADAM_LR * ((nm * c1) / (jnp.sqrt(nv * c2) + ADAM_EPS) + ADAM_WD * w_ref[...])


def adam_layer(parts, w, m, v, l, prev, name):
    L, R, C = w.shape
    br = next((b for b in (256, 128) if R > b and R % b == 0), R)

    def body(p_ref, w_ref, m_ref, v_ref, *rest):
        _adam_body(N_DEV, p_ref, w_ref, m_ref, v_ref, *rest[-4:])

    bs = pl.BlockSpec((None, br, C), lambda i: (l, i, 0))
    out = jax.ShapeDtypeStruct((L, R, C), F32)
    extra = {} if prev is None else dict(input_output_aliases={4 + q: q for q in range(4)})
    return _call(
        body, name=name, out_shape=(out, out, out, out), grid=(R // br,),
        in_specs=[pl.BlockSpec((N_DEV, br, C), lambda i: (0, i, 0)), bs, bs, bs] + ([] if prev is None else [ANY] * 4),
        out_specs=(bs, bs, bs, bs), compiler_params=_params(("parallel",)), **extra,
    )(parts, w, m, v, *(() if prev is None else prev))


def sum_parts(parts, name):
    n, R, C = parts.shape

    def body(p_ref, o_ref):
        g = p_ref[0]
        for p in range(1, n):
            g = g + p_ref[p]
        o_ref[...] = g

    return _call(body, name=name, out_shape=jax.ShapeDtypeStruct((R, C), parts.dtype),
                 compiler_params=pltpu.CompilerParams(vmem_limit_bytes=VMEM_LIMIT))(parts)


def _pack_rows(arrs):
    rows = []
    for a in arrs:
        f = a.reshape(-1)
        r = -(-f.shape[0] // 1024)
        r8 = -(-r // 8) * 8
        rows.append(jnp.pad(f, (0, r8 * 1024 - f.shape[0])).reshape(r8, 1024))
    packed = jnp.concatenate(rows, 0)
    return jnp.pad(packed, ((0, -packed.shape[0] % 128), (0, 0)))


def _unpack_rows(packed, shapes):
    out, o = [], 0
    for s in shapes:
        nel = math.prod(s)
        r8 = -(-(-(-nel // 1024)) // 8) * 8
        out.append(packed[o:o + r8].reshape(-1)[:nel].reshape(s))
        o += r8
    return out


def kernel(x, norm_mix_g, norm_ffn_g, norm_final_g, rg_w_in, rg_conv_w, rg_conv_b, rg_w_a, rg_b_a, rg_w_x, rg_b_x, rg_lambda, rg_w_out, s5_w_in, s5_a_re, s5_a_im, s5_log_dt, s5_b_re, s5_b_im, s5_c_re, s5_c_im, s5_d, s5_w_glu, s5_w_out, ffn_w_up, ffn_conv_w, ffn_conv_b, ffn_w_down, loss_target, m_norm_mix_g, m_norm_ffn_g, m_norm_final_g, m_rg_w_in, m_rg_conv_w, m_rg_conv_b, m_rg_w_a, m_rg_b_a, m_rg_w_x, m_rg_b_x, m_rg_lambda, m_rg_w_out, m_s5_w_in, m_s5_a_re, m_s5_a_im, m_s5_log_dt, m_s5_b_re, m_s5_b_im, m_s5_c_re, m_s5_c_im, m_s5_d, m_s5_w_glu, m_s5_w_out, m_ffn_w_up, m_ffn_conv_w, m_ffn_conv_b, m_ffn_w_down, v_norm_mix_g, v_norm_ffn_g, v_norm_final_g, v_rg_w_in, v_rg_conv_w, v_rg_conv_b, v_rg_w_a, v_rg_b_a, v_rg_w_x, v_rg_b_x, v_rg_lambda, v_rg_w_out, v_s5_w_in, v_s5_a_re, v_s5_a_im, v_s5_log_dt, v_s5_b_re, v_s5_b_im, v_s5_c_re, v_s5_c_im, v_s5_d, v_s5_w_glu, v_s5_w_out, v_ffn_w_up, v_ffn_conv_w, v_ffn_conv_b, v_ffn_w_down):
    names = ('norm_mix_g', 'norm_ffn_g', 'norm_final_g', 'rg_w_in', 'rg_conv_w', 'rg_conv_b', 'rg_w_a', 'rg_b_a',
             'rg_w_x', 'rg_b_x', 'rg_lambda', 'rg_w_out', 's5_w_in', 's5_a_re', 's5_a_im', 's5_log_dt', 's5_b_re',
             's5_b_im', 's5_c_re', 's5_c_im', 's5_d', 's5_w_glu', 's5_w_out', 'ffn_w_up', 'ffn_conv_w', 'ffn_conv_b',
             'ffn_w_down')
    loc = locals()
    Wl = {k: loc[k] for k in names}
    Ml = {k: loc['m_' + k] for k in names}
    Vl = {k: loc['v_' + k] for k in names}

    depth = norm_mix_g.shape[0]
    mixer_keys = lambda i: ([('rg_w_in', i // 2), ('rg_w_out', i // 2)] if i % 2 == 0 else
                            [('s5_w_in', i // 2), ('s5_w_glu', i // 2), ('s5_w_out', i // 2)])
    ffn_keys = lambda i: [('ffn_w_up', i), ('ffn_w_down', i)]
    shards = {k: Wl[k].astype(BF16) for k in BIG}
    W = {k: Wl[k] for k in REPLICATED}
    W.update({k: [None] * Wl[k].shape[0] for k in BIG})
    parts = {}

    def gather_of(keys, small=False):
        items = []
        for k, l in keys:
            _, r, c = shards[k].shape
            items.append((shards[k], l, BIG[k], (r * N_DEV, c) if BIG[k] == 0 else (r, c * N_DEV)))
        if small:
            items += [(Wl[k], None, Wl[k].ndim - 1, Wl[k].shape[:-1] + (Wl[k].shape[-1] * N_DEV,)) for k in SMALL_SHARDED]

        def deliver(outs):
            for (k, l), arr in zip(keys, outs):
                W[k][l] = arr
            if small:
                W.update(zip(SMALL_SHARDED, outs[len(keys):]))

        return gather_plan(items, deliver)

    def exchange_of(keys, G, extra=()):
        items = [(G[k][l], BIG[k]) for k, l in keys] + [(arr, axis) for _, arr, axis in extra]
        return exchange_plan(items, lambda outs: parts.update(zip(list(keys) + [e[0] for e in extra], outs)))

    class Sched:
        @staticmethod
        def comm(host, G=None):
            kind, _, i = host.rpartition("_")
            i = int(i)
            if host == "rg_fwd_0":
                return gather_of(ffn_keys(0))
            if kind == "ffn_up" and i + 1 < depth:
                return gather_of(ffn_keys(i + 1)[:1])
            if kind == "ffn_fwd" and i + 1 < depth:
                return gather_of(mixer_keys(i + 1) + ffn_keys(i + 1)[1:])
            if kind == "ffn_bwd_act" and i + 1 < depth:
                return exchange_of(mixer_keys(i + 1), G)
            if kind in ("rg_bwd", "s5_bwd_ssm"):
                return exchange_of(ffn_keys(i), G)
            if host == "rg_bwd_in_0":
                stack = lambda k: G[k] if k == 'norm_final_g' else jnp.stack(G[k], 0)
                extra = [(k, stack(k), Wl[k].ndim - 1) for k in SMALL_SHARDED]
                extra.append(('rep_main', _pack_rows([stack(k).astype(F32) for k in REP_MAIN]), 0))
                return exchange_of(mixer_keys(0)[1:], G, extra)
            if host == "rg_dwin_0":
                rsum = sum_parts(parts['rep_main'], "sum_replicated")
                return gather_plan([(rsum, None, 0, (rsum.shape[0] * N_DEV, rsum.shape[1]))],
                                   lambda outs: parts.update(rep_main_full=outs[0]))
            return None

    run_comm(gather_of(mixer_keys(0), small=True), "gather_first")

    se8, gx, G = local_step(x[0], loss_target[0], W, Sched)
    loss = lax.psum(0.5 * jnp.sum(se8) / x.shape[-1], ("x", "y", "c"))

    gp_late = _pack_rows([G[k].astype(F32) for k in REP_LATE])
    run_comm(exchange_of(mixer_keys(0)[:1], G, [('rep_late', gp_late, 0)]), "exchange_last")
    out_g, out_d, out_m, out_v = {}, {}, {}, {}
    for k in BIG:
        res = None
        for l in range(Wl[k].shape[0]):
            res = adam_layer(parts[(k, l)], Wl[k], Ml[k], Vl[k], l, res, f"adam_{k}_{l}")
        out_g[k], out_d[k], out_m[k], out_v[k] = res
    for k in SMALL_SHARDED:
        shp = Wl[k].shape
        r2 = (math.prod(shp[:-1]), shp[-1])
        res = adam_update(parts[k].reshape((N_DEV,) + r2), Wl[k].reshape(r2), Ml[k].reshape(r2), Vl[k].reshape(r2),
                          f"adam_{k}")
        out_g[k], out_d[k], out_m[k], out_v[k] = [t.reshape(shp) for t in res]
    rsum = sum_parts(parts['rep_late'], "sum_replicated_late")
    run_comm(gather_plan([(rsum, None, 0, gp_late.shape)], lambda outs: parts.update(rep_late_full=outs[0])),
             "gather_small_grads")
    for keys, full, tag in ((REP_MAIN, parts['rep_main_full'], "main"), (REP_LATE, parts['rep_late_full'], "late")):
        res = adam_update(full[None], _pack_rows([Wl[k] for k in keys]), _pack_rows([Ml[k] for k in keys]),
                          _pack_rows([Vl[k] for k in keys]), f"adam_replicated_{tag}")
        for dst, packed in zip((out_g, out_d, out_m, out_v), res):
            dst.update(zip(keys, _unpack_rows(packed, [Wl[k].shape for k in keys])))
    return (loss, gx[None], *[out_g[k] for k in names], *[out_d[k] for k in names], *[out_m[k] for k in names],
            *[out_v[k] for k in names])
```

```python
import functools
import math

import jax
import jax.numpy as jnp
from jax import lax
from jax.experimental import pallas as pl
from jax.experimental.pallas import tpu as pltpu

F32 = jnp.float32
BF16 = jnp.bfloat16
MXU_DTYPE = jnp.bfloat16

NORM_EPS = 1e-6
RG_C = 8.0
RG_HEADS = 8
RG_BW = 128
S5_G = 64
S5_GC = 16
S5_P = 64
S5_SB = 8
S5_NSB = S5_G // S5_SB
S5_NS = 2 * S5_G * S5_P
ADAM_LR = 0.001
ADAM_B1 = 0.9
ADAM_B2 = 0.999
ADAM_EPS = 1e-08
ADAM_WD = 0.01
ADAM_STEP = 10
N_DEV = 8
VMEM_LIMIT = 56 * 1024 * 1024


def _call(body, **kw):
    return pl.pallas_call(body, **kw)


def _params(sem, vmem=VMEM_LIMIT):
    return pltpu.CompilerParams(dimension_semantics=sem, vmem_limit_bytes=vmem)


MESH = pl.DeviceIdType.MESH
ANY = pl.BlockSpec(memory_space=pl.ANY)


class Comm:
    def __init__(self, operands, out_shape, scratch, start, middle, finish, deliver):
        self.operands, self.out_shape, self.scratch = list(operands), list(out_shape), list(scratch)
        self.start, self.middle, self.finish, self.deliver = start, middle, finish, deliver


def run_comm(comm, name):
    ci, co = len(comm.operands), len(comm.out_shape)

    def body(*refs):
        parts = (refs[:ci], refs[ci:ci + co], refs[ci + co:])
        comm.start(*parts)
        comm.middle(*parts)
        comm.finish(*parts)

    comm.deliver(_call(body, name=name, out_shape=tuple(comm.out_shape), in_specs=[ANY] * ci,
                       out_specs=tuple([ANY] * co), scratch_shapes=comm.scratch)(*comm.operands))


def _hosted(body, comm, args, *, name, out_shape, grid, in_specs, out_specs, scratch_shapes, compiler_params):
    if comm is None:
        return _call(body, name=name, out_shape=tuple(out_shape), grid=grid, in_specs=in_specs,
                     out_specs=tuple(out_specs), scratch_shapes=scratch_shapes, compiler_params=compiler_params)(*args)
    n_in, n_out, n_sc = len(in_specs), len(out_shape), len(scratch_shapes)
    ci, co = len(comm.operands), len(comm.out_shape)
    nsteps = math.prod(grid)
    mid = (2 * nsteps) // 3

    def wrapped(*refs):
        ins, refs = refs[:n_in], refs[n_in:]
        cins, refs = refs[:ci], refs[ci:]
        outs, refs = refs[:n_out], refs[n_out:]
        couts, refs = refs[:co], refs[co:]
        sc, csc = refs[:n_sc], refs[n_sc:]
        step = pl.program_id(0)
        for d in range(1, len(grid)):
            step = step * grid[d] + pl.program_id(d)

        @pl.when(step == 0)
        def _():
            comm.start(cins, couts, csc)

        body(*ins, *outs, *sc)

        @pl.when(step == mid)
        def _():
            comm.middle(cins, couts, csc)

        @pl.when(step == nsteps - 1)
        def _():
            comm.finish(cins, couts, csc)

    res = _call(wrapped, name=name, out_shape=(*out_shape, *comm.out_shape), grid=grid,
                in_specs=[*in_specs, *[ANY] * ci], out_specs=(*out_specs, *[ANY] * co),
                scratch_shapes=[*scratch_shapes, *comm.scratch],
                compiler_params=_params(("arbitrary",) * len(grid)))(*args, *comm.operands)
    comm.deliver(res[n_out:])
    return res[:n_out]


_GELU_C = 0.7978845608028654
_GELU_A = 0.044715


def _gelu(x):
    return 0.5 * x * (1.0 + jnp.tanh(_GELU_C * (x + _GELU_A * x * x * x)))


def _gelu_and_grad(x):
    x2 = x * x
    t = jnp.tanh(_GELU_C * (x + _GELU_A * x2 * x))
    g = 0.5 * x * (1.0 + t)
    dg = 0.5 * (1.0 + t) + 0.5 * x * (1.0 - t * t) * _GELU_C * (1.0 + 3.0 * _GELU_A * x2)
    return g, dg


def _sigmoid(x):
    return 1.0 / (1.0 + jnp.exp(-x))


def _neg_expm1(x):
    series = -x * (1.0 + x * (0.5 + x * (1.0 / 6.0 + x * (1.0 / 24.0 + x * (1.0 / 120.0 + x * (1.0 / 720.0))))))
    return jnp.where(x > -0.1, series, 1.0 - jnp.exp(x))


def _rowsum8(x):
    r, c = x.shape
    return x.reshape(r // 8, 8, c).sum(axis=0)


def _dot(a, b):
    return jnp.dot(a.astype(MXU_DTYPE), b.astype(MXU_DTYPE), preferred_element_type=F32)


def _dot_nt(a, b):
    return lax.dot_general(a.astype(MXU_DTYPE), b.astype(MXU_DTYPE), (((1,), (1,)), ((), ())),
                           preferred_element_type=F32)


def _dot_tn(a, b):
    return lax.dot_general(a.astype(MXU_DTYPE), b.astype(MXU_DTYPE), (((0,), (0,)), ((), ())),
                           preferred_element_type=F32)


def _shift_down(x, s, fills, row):
    y = pltpu.roll(x, s, 0)
    for t in range(s):
        y = jnp.where(row == t, fills[s - 1 - t], y)
    return y


def _shift_up(x, s, fills, row):
    n = x.shape[0]
    y = pltpu.roll(x, n - s, 0)
    for t in range(s):
        y = jnp.where(row == n - s + t, fills[t], y)
    return y


def _rms(x):
    r = lax.rsqrt(jnp.mean(x * x, axis=-1, keepdims=True) + NORM_EPS)
    return r, x * r


def _rms_bwd(dhn, xhat, r, g):
    dz = dhn * g
    return r * (dz - xhat * jnp.mean(dz * xhat, axis=-1, keepdims=True))


def norm_matmul(h, g, w, out_dtype, tm, tn, name, comm=None):
    T, D = h.shape
    N = w.shape[1]
    tn = N if tn is None else tn

    def body(h_ref, g_ref, w_ref, o_ref, hn_ref, hn_s):
        @pl.when(pl.program_id(1) == 0)
        def _():
            _, xhat = _rms(h_ref[...])
            v = (xhat * g_ref[...]).astype(MXU_DTYPE)
            hn_s[...] = v
            hn_ref[...] = v

        o_ref[...] = jnp.dot(hn_s[...], w_ref[...].astype(MXU_DTYPE), preferred_element_type=F32).astype(o_ref.dtype)

    return _hosted(
        body, comm, (h, g, w), name=name,
        out_shape=(jax.ShapeDtypeStruct((T, N), out_dtype), jax.ShapeDtypeStruct((T, D), MXU_DTYPE)),
        grid=(T // tm, N // tn),
        in_specs=[pl.BlockSpec((tm, D), lambda i, j: (i, 0)), pl.BlockSpec((1, D), lambda i, j: (0, 0)),
                  pl.BlockSpec((D, tn), lambda i, j: (0, j))],
        out_specs=(pl.BlockSpec((tm, tn), lambda i, j: (i, j)), pl.BlockSpec((tm, D), lambda i, j: (i, 0))),
        scratch_shapes=[pltpu.VMEM((tm, D), MXU_DTYPE)],
        compiler_params=_params(("parallel", "arbitrary")),
    )


def matmul_tn(a, b, bm, bn, bt, name, out_dtype=BF16, comm=None):
    T, M = a.shape
    N = b.shape[1]
    nk = T // bt

    def body(a_ref, b_ref, o_ref, acc):
        k = pl.program_id(2)

        @pl.when(k == 0)
        def _():
            acc[...] = jnp.zeros_like(acc)

        acc[...] += _dot_tn(a_ref[...], b_ref[...])

        @pl.when(k == nk - 1)
        def _():
            o_ref[...] = acc[...].astype(o_ref.dtype)

    return _hosted(
        body, comm, (a, b), name=name,
        out_shape=(jax.ShapeDtypeStruct((M, N), out_dtype),),
        grid=(M // bm, N // bn, nk),
        in_specs=[pl.BlockSpec((bt, bm), lambda i, j, k: (k, i)), pl.BlockSpec((bt, bn), lambda i, j, k: (k, j))],
        out_specs=(pl.BlockSpec((bm, bn), lambda i, j, k: (i, j)),),
        scratch_shapes=[pltpu.VMEM((bm, bn), F32)],
        compiler_params=_params(("parallel", "parallel", "arbitrary")),
    )[0]


def dx_norm_bwd(dz, w, h, g, dres, tm, name, conv_w=None, conv_x=None, chunk=256, comm=None):
    T, N = dz.shape
    D = w.shape[0]
    nt = T // tm
    has_conv = conv_w is not None
    kw = conv_w.shape[0] if has_conv else 0

    def body(*refs):
        if has_conv:
            dz_ref, cw_ref, x_ref, w_ref, h_ref, g_ref, dres_ref, dh_ref, dg_ref, dzp_ref, dcw_ref, carry = refs
        else:
            dz_ref, w_ref, h_ref, g_ref, dres_ref, dh_ref, dg_ref = refs
        i = pl.program_id(0)

        @pl.when(i == 0)
        def _():
            dg_ref[...] = jnp.zeros_like(dg_ref)
            if has_conv:
                carry[...] = jnp.zeros_like(carry)
                dcw_ref[...] = jnp.zeros_like(dcw_ref)

        if has_conv:
            row = lax.broadcasted_iota(jnp.int32, (tm, 1), 0)
            for c0 in range(0, N, chunk):
                sl = slice(c0, c0 + chunk)
                d0 = dz_ref[:, sl].astype(F32)
                xv = x_ref[:, sl].astype(F32)
                fills = [carry[t:t + 1, sl] for t in range(kw - 1)]
                acc = cw_ref[kw - 1:kw, sl] * d0
                dcw_ref[kw - 1, :, sl] += _rowsum8(d0 * xv)
                for s in range(1, kw):
                    ds = _shift_up(d0, s, fills, row)
                    acc = acc + cw_ref[kw - 1 - s:kw - s, sl] * ds
                    dcw_ref[kw - 1 - s, :, sl] += _rowsum8(ds * xv)
                zb = acc.astype(MXU_DTYPE)
                dzp_ref[:, sl] = zb
                part = _dot_nt(zb, w_ref[:, sl])
                dhn = part if c0 == 0 else dhn + part
            carry[...] = dz_ref[0:16, :].astype(F32)
        else:
            dhn = _dot_nt(dz_ref[...], w_ref[...])
        r, xhat = _rms(h_ref[...])
        dg_ref[...] += _rowsum8(dhn * xhat)
        dh_ref[...] = dres_ref[...] + _rms_bwd(dhn, xhat, r, g_ref[...])

    if has_conv:
        ti = lambda i: nt - 1 - i
    else:
        ti = lambda i: i
    row_spec = lambda c: pl.BlockSpec((tm, c), lambda i: (ti(i), 0))
    full = lambda a: pl.BlockSpec(a.shape, lambda i: (0,) * a.ndim)
    in_specs = [row_spec(N)] + ([full(conv_w), row_spec(N)] if has_conv else []) + [full(w), row_spec(D), full(g), row_spec(D)]
    out_shape = [jax.ShapeDtypeStruct((T, D), F32), jax.ShapeDtypeStruct((8, D), F32)]
    out_specs = [row_spec(D), pl.BlockSpec((8, D), lambda i: (0, 0))]
    scratch = []
    if has_conv:
        out_shape += [jax.ShapeDtypeStruct((T, N), MXU_DTYPE), jax.ShapeDtypeStruct((kw, 8, N), F32)]
        out_specs += [row_spec(N), pl.BlockSpec((kw, 8, N), lambda i: (0, 0, 0))]
        scratch = [pltpu.VMEM((16, N), F32)]
    args = [dz] + ([conv_w, conv_x] if has_conv else []) + [w, h, g, dres]
    return _hosted(
        body, comm, args, name=name, out_shape=tuple(out_shape), grid=(nt,), in_specs=in_specs,
        out_specs=tuple(out_specs), scratch_shapes=scratch, compiler_params=_params(("arbitrary",)),
    )


def _ffn_conv_chunk(up_ref, cw_ref, cb_ref, carry, row, sl):
    x = up_ref[:, sl].astype(F32)
    fills = [carry[15:16, sl], carry[14:15, sl]]
    x1 = _shift_down(x, 1, fills, row)
    x2 = _shift_down(x, 2, fills, row)
    return cb_ref[:, sl] + cw_ref[2:3, sl] * x + cw_ref[1:2, sl] * x1 + cw_ref[0:1, sl] * x2


def ffn_fwd(up, h, conv_w, conv_b, w_down, tm, name, chunk=256, comm=None):
    T, C = up.shape
    F = C // 2
    D = h.shape[1]

    def body(up_ref, h_ref, cw_ref, cb_ref, wd_ref, o_ref, av_ref, carry):
        @pl.when(pl.program_id(0) == 0)
        def _():
            carry[...] = jnp.zeros_like(carry)

        row = lax.broadcasted_iota(jnp.int32, (tm, 1), 0)
        out = h_ref[...]
        for c0 in range(0, F, chunk):
            sa, sv = slice(c0, c0 + chunk), slice(F + c0, F + c0 + chunk)
            a = _ffn_conv_chunk(up_ref, cw_ref, cb_ref, carry, row, sa)
            v = _ffn_conv_chunk(up_ref, cw_ref, cb_ref, carry, row, sv)
            av_ref[:, sa] = a.astype(MXU_DTYPE)
            av_ref[:, sv] = v.astype(MXU_DTYPE)
            out = out + _dot(_gelu(a) * v, wd_ref[sa, :])
        carry[...] = up_ref[tm - 16:tm, :].astype(F32)
        o_ref[...] = out

    full = lambda a: pl.BlockSpec(a.shape, lambda i: (0,) * a.ndim)
    return _hosted(
        body, comm, (up, h, conv_w, conv_b, w_down), name=name,
        out_shape=(jax.ShapeDtypeStruct((T, D), F32), jax.ShapeDtypeStruct((T, C), MXU_DTYPE)),
        grid=(T // tm,),
        in_specs=[pl.BlockSpec((tm, C), lambda i: (i, 0)), pl.BlockSpec((tm, D), lambda i: (i, 0)),
                  full(conv_w), full(conv_b), full(w_down)],
        out_specs=(pl.BlockSpec((tm, D), lambda i: (i, 0)), pl.BlockSpec((tm, C), lambda i: (i, 0))),
        scratch_shapes=[pltpu.VMEM((16, C), F32)],
        compiler_params=_params(("arbitrary",)),
    )


def ffn_bwd_act(dout, av, w_down, tm, name, chunk=512, comm=None):
    T, C = av.shape
    F = C // 2
    D = dout.shape[1]

    def body(do_ref, av_ref, wd_ref, dup_ref, act_ref, dcb_ref):
        @pl.when(pl.program_id(0) == 0)
        def _():
            dcb_ref[...] = jnp.zeros_like(dcb_ref)

        dob = do_ref[...].astype(MXU_DTYPE)
        for c0 in range(0, F, chunk):
            sa, sv = slice(c0, c0 + chunk), slice(F + c0, F + c0 + chunk)
            dact = _dot_nt(dob, wd_ref[sa, :])
            v = av_ref[:, sv].astype(F32)
            ga, dga = _gelu_and_grad(av_ref[:, sa].astype(F32))
            act_ref[:, sa] = (ga * v).astype(MXU_DTYPE)
            da = dact * v * dga
            dv = dact * ga
            dup_ref[:, sa] = da.astype(MXU_DTYPE)
            dup_ref[:, sv] = dv.astype(MXU_DTYPE)
            dcb_ref[:, sa] += _rowsum8(da)
            dcb_ref[:, sv] += _rowsum8(dv)

    full = lambda a: pl.BlockSpec(a.shape, lambda i: (0,) * a.ndim)
    return _hosted(
        body, comm, (dout, av, w_down), name=name,
        out_shape=(jax.ShapeDtypeStruct((T, C), MXU_DTYPE), jax.ShapeDtypeStruct((T, F), MXU_DTYPE),
                   jax.ShapeDtypeStruct((8, C), F32)),
        grid=(T // tm,),
        in_specs=[pl.BlockSpec((tm, D), lambda i: (i, 0)), pl.BlockSpec((tm, C), lambda i: (i, 0)), full(w_down)],
        out_specs=(pl.BlockSpec((tm, C), lambda i: (i, 0)), pl.BlockSpec((tm, F), lambda i: (i, 0)),
                   pl.BlockSpec((8, C), lambda i: (0, 0))),
        scratch_shapes=[],
        compiler_params=_params(("arbitrary",)),
    )


def _rg_gates(xr, wa_ref, ba_ref, wx_ref, bx_ref, sp_ref):
    xb = xr.astype(MXU_DTYPE)
    pa, px = [], []
    for hd in range(RG_HEADS):
        sl = slice(hd * RG_BW, (hd + 1) * RG_BW)
        pa.append(jnp.dot(xb[:, sl], wa_ref[hd].astype(MXU_DTYPE), preferred_element_type=F32))
        px.append(jnp.dot(xb[:, sl], wx_ref[hd].astype(MXU_DTYPE), preferred_element_type=F32))
    r = _sigmoid(jnp.concatenate(pa, axis=1) + ba_ref[...])
    ig = _sigmoid(jnp.concatenate(px, axis=1) + bx_ref[...])
    la = -RG_C * r * sp_ref[...]
    a = jnp.exp(la)
    mult = jnp.sqrt(_neg_expm1(2.0 * la))
    return xb, r, ig, a, mult


def _rg_conv(x, fills, cw_ref, cb_ref, row):
    x1 = _shift_down(x, 1, fills, row)
    x2 = _shift_down(x, 2, fills, row)
    x3 = _shift_down(x, 3, fills, row)
    xr = cb_ref[...] + cw_ref[3:4, :] * x + cw_ref[2:3, :] * x1 + cw_ref[1:2, :] * x2 + cw_ref[0:1, :] * x3
    return xr, (x3, x2, x1, x)


def rg_fwd(xg, h, conv_w, conv_b, w_a, b_a, w_x, b_x, sp, w_out, tm, name, comm=None):
    T, D2 = xg.shape
    D = D2 // 2
    nb = tm // 8

    def body(xg_ref, h_ref, cw_ref, cb_ref, wa_ref, ba_ref, wx_ref, bx_ref, sp_ref, wo_ref, o_ref, hs_ref,
             xcarry, hcarry, a_s, b_s):
        @pl.when(pl.program_id(0) == 0)
        def _():
            xcarry[...] = jnp.zeros_like(xcarry)
            hcarry[...] = jnp.zeros_like(hcarry)

        row = lax.broadcasted_iota(jnp.int32, (tm, 1), 0)
        x = xg_ref[:, 0:D]
        fills = [xcarry[7:8, :], xcarry[6:7, :], xcarry[5:6, :]]
        xr, _ = _rg_conv(x, fills, cw_ref, cb_ref, row)
        xcarry[...] = xg_ref[tm - 8:tm, 0:D]
        _, r, ig, a, mult = _rg_gates(xr, wa_ref, ba_ref, wx_ref, bx_ref, sp_ref)
        a_s[...] = a
        b_s[...] = mult * ig * xr
        row8 = lax.broadcasted_iota(jnp.int32, (8, 1), 0)

        def blk(j, c):
            o = pl.multiple_of(j * 8, 8)
            A = a_s[pl.ds(o, 8), :]
            H = b_s[pl.ds(o, 8), :]
            for s in (1, 2, 4):
                m = row8 >= s
                H = H + A * jnp.where(m, pltpu.roll(H, s, 0), 0.0)
                A = A * jnp.where(m, pltpu.roll(A, s, 0), 1.0)
            H = H + A * c
            hs_ref[pl.ds(o, 8), :] = H
            return H[7:8, :]

        c = lax.fori_loop(0, nb, blk, hcarry[0:1, :])
        hcarry[0:1, :] = c
        y = hs_ref[...] * _gelu(xg_ref[:, D:D2])
        o_ref[...] = h_ref[...] + _dot(y, wo_ref[...])

    full = lambda a: pl.BlockSpec(a.shape, lambda i: (0,) * a.ndim)
    args = (xg, h, conv_w, conv_b, w_a, b_a, w_x, b_x, sp, w_out)
    return _hosted(
        body, comm, args, name=name,
        out_shape=(jax.ShapeDtypeStruct((T, D), F32), jax.ShapeDtypeStruct((T, D), F32)),
        grid=(T // tm,),
        in_specs=[pl.BlockSpec((tm, D2), lambda i: (i, 0)), pl.BlockSpec((tm, D), lambda i: (i, 0))]
        + [full(a) for a in args[2:]],
        out_specs=(pl.BlockSpec((tm, D), lambda i: (i, 0)), pl.BlockSpec((tm, D), lambda i: (i, 0))),
        scratch_shapes=[pltpu.VMEM((8, D), F32), pltpu.VMEM((8, D), F32), pltpu.VMEM((tm, D), F32),
                        pltpu.VMEM((tm, D), F32)],
        compiler_params=_params(("arbitrary",)),
    )


def rg_bwd(dout, xg, hs, conv_w, conv_b, w_a, b_a, w_x, b_x, sp, w_out, tm, name, comm=None):
    T, D2 = xg.shape
    D = D2 // 2
    nt = T // tm
    nb = tm // 8
    kw = conv_w.shape[0]

    def body(do_ref, xg_ref, xh_ref, hs_ref, hh_ref, cw_ref, cb_ref, wa_ref, ba_ref, wx_ref, bx_ref, sp_ref, wo_ref,
             dxg_ref, y_ref, dwa_ref, dwx_ref, dba_ref, dbx_ref, dsp_ref, dcb_ref, dcw_ref,
             acarry, lcarry, dcarry, a_s, b_s, l_s):
        i = pl.program_id(0)
        first_tile = i == nt - 1

        @pl.when(i == 0)
        def _():
            for ref in (acarry, lcarry, dcarry, dwa_ref, dwx_ref, dba_ref, dbx_ref, dsp_ref, dcb_ref, dcw_ref):
                ref[...] = jnp.zeros_like(ref)

        row = lax.broadcasted_iota(jnp.int32, (tm, 1), 0)
        keep = jnp.where(first_tile, 0.0, 1.0)
        x = xg_ref[:, 0:D]
        gate = xg_ref[:, D:D2]
        xh = xh_ref[...] * keep
        fills = [xh[7:8, :], xh[6:7, :], xh[5:6, :]]
        xr, taps = _rg_conv(x, fills, cw_ref, cb_ref, row)
        xb, r, ig, a, mult = _rg_gates(xr, wa_ref, ba_ref, wx_ref, bx_ref, sp_ref)
        hs = hs_ref[...]
        hprev = _shift_down(hs, 1, [hh_ref[7:8, :] * keep], row)
        dy = _dot_nt(do_ref[...], wo_ref[...])
        gg, dgg = _gelu_and_grad(gate)
        y_ref[...] = (hs * gg).astype(MXU_DTYPE)
        dxg_ref[:, D:D2] = (dy * hs * dgg).astype(MXU_DTYPE)
        a_s[...] = _shift_up(a, 1, [acarry[0:1, :]], row)
        b_s[...] = dy * gg
        row8 = lax.broadcasted_iota(jnp.int32, (8, 1), 0)

        def blk(jj, c):
            o = pl.multiple_of((nb - 1 - jj) * 8, 8)
            A = a_s[pl.ds(o, 8), :]
            H = b_s[pl.ds(o, 8), :]
            for s in (1, 2, 4):
                m = row8 < 8 - s
                H = H + A * jnp.where(m, pltpu.roll(H, 8 - s, 0), 0.0)
                A = A * jnp.where(m, pltpu.roll(A, 8 - s, 0), 1.0)
            H = H + A * c
            l_s[pl.ds(o, 8), :] = H
            return H[0:1, :]

        c = lax.fori_loop(0, nb, blk, lcarry[0:1, :])
        lcarry[0:1, :] = c
        acarry[0:1, :] = a[0:1, :]
        lam = l_s[...]
        dla = lam * hprev * a - (lam * ig * xr) * (a * a) / mult
        dig = lam * mult * xr
        dxr = lam * mult * ig
        spv = sp_ref[...]
        dsp_ref[...] += _rowsum8(dla * (-RG_C) * r)
        dpa = (dla * (-RG_C) * spv) * r * (1.0 - r)
        dpx = dig * ig * (1.0 - ig)
        dba_ref[...] += _rowsum8(dpa)
        dbx_ref[...] += _rowsum8(dpx)
        dpab = dpa.astype(MXU_DTYPE)
        dpxb = dpx.astype(MXU_DTYPE)
        back = []
        for hd in range(RG_HEADS):
            sl = slice(hd * RG_BW, (hd + 1) * RG_BW)
            dwa_ref[hd] += _dot_tn(xb[:, sl], dpab[:, sl])
            dwx_ref[hd] += _dot_tn(xb[:, sl], dpxb[:, sl])
            back.append(_dot_nt(dpab[:, sl], wa_ref[hd]) + _dot_nt(dpxb[:, sl], wx_ref[hd]))
        dxr = dxr + jnp.concatenate(back, axis=1)
        nfills = [dcarry[0:1, :], dcarry[1:2, :], dcarry[2:3, :]]
        dxp = cw_ref[kw - 1:kw, :] * dxr
        for s in range(1, kw):
            dxp = dxp + cw_ref[kw - 1 - s:kw - s, :] * _shift_up(dxr, s, nfills, row)
        dcarry[...] = dxr[0:8, :]
        dxg_ref[:, 0:D] = dxp.astype(MXU_DTYPE)
        dcb_ref[...] += _rowsum8(dxr)
        for k in range(kw):
            dcw_ref[k] += _rowsum8(dxr * taps[k])

    rt = lambda i: nt - 1 - i
    halo = lambda i: jnp.maximum((nt - 1 - i) * (tm // 8) - 1, 0)
    full = lambda a: pl.BlockSpec(a.shape, lambda i: (0,) * a.ndim)
    params = (conv_w, conv_b, w_a, b_a, w_x, b_x, sp, w_out)
    acc = lambda shape: pl.BlockSpec(shape, lambda i: (0,) * len(shape))
    return _hosted(
        body, comm, (dout, xg, xg, hs, hs, *params), name=name,
        out_shape=(jax.ShapeDtypeStruct((T, D2), MXU_DTYPE), jax.ShapeDtypeStruct((T, D), MXU_DTYPE),
                   jax.ShapeDtypeStruct((RG_HEADS, RG_BW, RG_BW), F32), jax.ShapeDtypeStruct((RG_HEADS, RG_BW, RG_BW), F32),
                   jax.ShapeDtypeStruct((8, D), F32), jax.ShapeDtypeStruct((8, D), F32), jax.ShapeDtypeStruct((8, D), F32),
                   jax.ShapeDtypeStruct((8, D), F32), jax.ShapeDtypeStruct((kw, 8, D), F32)),
        grid=(nt,),
        in_specs=[pl.BlockSpec((tm, D), lambda i: (rt(i), 0)), pl.BlockSpec((tm, D2), lambda i: (rt(i), 0)),
                  pl.BlockSpec((8, D), lambda i: (halo(i), 0)), pl.BlockSpec((tm, D), lambda i: (rt(i), 0)),
                  pl.BlockSpec((8, D), lambda i: (halo(i), 0))] + [full(a) for a in params],
        out_specs=(pl.BlockSpec((tm, D2), lambda i: (rt(i), 0)), pl.BlockSpec((tm, D), lambda i: (rt(i), 0)),
                   acc((RG_HEADS, RG_BW, RG_BW)), acc((RG_HEADS, RG_BW, RG_BW)), acc((8, D)), acc((8, D)), acc((8, D)),
                   acc((8, D)), acc((kw, 8, D))),
        scratch_shapes=[pltpu.VMEM((8, D), F32), pltpu.VMEM((8, D), F32), pltpu.VMEM((8, D), F32),
                        pltpu.VMEM((tm, D), F32), pltpu.VMEM((tm, D), F32), pltpu.VMEM((tm, D), F32)],
        compiler_params=_params(("arbitrary",)),
    )


_SBW = 2 * S5_SB * S5_P
_SBH = S5_SB * S5_P
_SBU = S5_SB * S5_GC


def _regroup(x, seg_len):
    n, c = x.shape
    return jnp.swapaxes(x.reshape(8, seg_len, c), 0, 1).reshape(n, c)


def _ungroup(x, seg_len):
    n, c = x.shape
    return jnp.swapaxes(x.reshape(seg_len, 8, c), 0, 1).reshape(n, c)


def _s5_lanes(sb):
    return (slice(sb * _SBW, sb * _SBW + _SBH), slice(sb * _SBW + _SBH, (sb + 1) * _SBW),
            slice(sb * _SBH, (sb + 1) * _SBH))


def _s5_scan_fwd(S, carry, cin, ap_r, ap_i, aq_r, aq_i, seg_len, sb):
    row8 = lax.broadcasted_iota(jnp.int32, (8, 1), 0)
    lr, li, la = _s5_lanes(sb)
    ar, ai = ap_r[0:1, la], ap_i[0:1, la]
    hr, hi = S[0:8, lr], S[0:8, li]
    for i in range(1, seg_len):
        blk = slice(i * 8, (i + 1) * 8)
        hr, hi = ar * hr - ai * hi + S[blk, lr], ar * hi + ai * hr + S[blk, li]
        S[blk, lr] = hr
        S[blk, li] = hi
    for s, idx in ((1, 0), (2, 1), (4, 3)):
        qr, qi = aq_r[idx:idx + 1, la], aq_i[idx:idx + 1, la]
        m = row8 >= s
        sr = jnp.where(m, pltpu.roll(hr, s, 0), 0.0)
        si = jnp.where(m, pltpu.roll(hi, s, 0), 0.0)
        hr, hi = hr + qr * sr - qi * si, hi + qr * si + qi * sr
    cr, ci = carry[0:1, lr], carry[0:1, li]
    pr, pi = aq_r[:, la], aq_i[:, la]
    hr, hi = hr + pr * cr - pi * ci, hi + pr * ci + pi * cr
    xr = jnp.where(row8 == 0, cr, pltpu.roll(hr, 1, 0))
    xi = jnp.where(row8 == 0, ci, pltpu.roll(hi, 1, 0))
    carry[0:1, lr] = hr[7:8, :]
    carry[0:1, li] = hi[7:8, :]
    if cin is not None:
        cin[:, lr] = xr
        cin[:, li] = xi
    for i in range(seg_len):
        blk = slice(i * 8, (i + 1) * 8)
        pr, pi = ap_r[i:i + 1, la], ap_i[i:i + 1, la]
        S[blk, lr] += pr * xr - pi * xi
        S[blk, li] += pr * xi + pi * xr


def s5_fwd(u, h, wb, wc, ap_r, ap_i, aq_r, aq_i, d, w_glu, w_out, tm, name):
    T, D = u.shape
    nt = T // tm
    seg = tm // 8

    def body(u_ref, h_ref, wb_ref, wc_ref, apr_ref, api_ref, aqr_ref, aqi_ref, d_ref, wg_ref, wo_ref,
             o_ref, yp_ref, gl_ref, st_ref, S, carry):
        @pl.when(pl.program_id(0) == 0)
        def _():
            carry[...] = jnp.zeros_like(carry)

        st_ref[...] = carry[...]
        uv = _regroup(u_ref[...], seg)
        ub = uv.astype(MXU_DTYPE)
        for sb in range(S5_NSB):
            S[:, sb * _SBW:(sb + 1) * _SBW] = jnp.dot(ub[:, sb * _SBU:(sb + 1) * _SBU], wb_ref[sb].astype(MXU_DTYPE),
                                                      preferred_element_type=F32)
        for sb in range(S5_NSB):
            _s5_scan_fwd(S, carry, None, apr_ref, api_ref, aqr_ref, aqi_ref, seg, sb)
        ys = [jnp.dot(S[:, sb * _SBW:(sb + 1) * _SBW].astype(MXU_DTYPE), wc_ref[sb].astype(MXU_DTYPE),
                      preferred_element_type=F32) for sb in range(S5_NSB)]
        yp = jnp.concatenate(ys, axis=1) + d_ref[...] * uv
        yp_ref[...] = _ungroup(yp, seg)
        gl = _dot(_gelu(yp), wg_ref[...])
        gl_ref[...] = _ungroup(gl, seg)
        out = gl[:, 0:D] * _sigmoid(gl[:, D:2 * D])
        o_ref[...] = h_ref[...] + _ungroup(_dot(out, wo_ref[...]), seg)

    full = lambda a: pl.BlockSpec(a.shape, lambda i: (0,) * a.ndim)
    args = (u, h, wb, wc, ap_r, ap_i, aq_r, aq_i, d, w_glu, w_out)
    return _call(
        body, name=name,
        out_shape=(jax.ShapeDtypeStruct((T, D), F32), jax.ShapeDtypeStruct((T, D), F32),
                   jax.ShapeDtypeStruct((T, 2 * D), F32), jax.ShapeDtypeStruct((nt, 8, S5_NS), F32)),
        grid=(nt,),
        in_specs=[pl.BlockSpec((tm, D), lambda i: (i, 0)), pl.BlockSpec((tm, D), lambda i: (i, 0))]
        + [full(a) for a in args[2:]],
        out_specs=(pl.BlockSpec((tm, D), lambda i: (i, 0)), pl.BlockSpec((tm, D), lambda i: (i, 0)),
                   pl.BlockSpec((tm, 2 * D), lambda i: (i, 0)), pl.BlockSpec((None, 8, S5_NS), lambda i: (i, 0, 0))),
        scratch_shapes=[pltpu.VMEM((tm, S5_NS), F32), pltpu.VMEM((8, S5_NS), F32)],
        compiler_params=_params(("arbitrary",)),
    )(*args)


def s5_bwd_glu(dout, gl, ypre, u, w_glu, w_out, tm, name):
    T, D = u.shape

    def body(do_ref, gl_ref, yp_ref, u_ref, wg_ref, wo_ref, dy_ref, oact_ref, dgl_ref, gact_ref, dd_ref):
        @pl.when(pl.program_id(0) == 0)
        def _():
            dd_ref[...] = jnp.zeros_like(dd_ref)

        gl1 = gl_ref[:, 0:D]
        sg = _sigmoid(gl_ref[:, D:2 * D])
        oact_ref[...] = (gl1 * sg).astype(MXU_DTYPE)
        dgo = _dot_nt(do_ref[...], wo_ref[...])
        d1 = (dgo * sg).astype(MXU_DTYPE)
        d2 = (dgo * gl1 * sg * (1.0 - sg)).astype(MXU_DTYPE)
        dgl_ref[:, 0:D] = d1
        dgl_ref[:, D:2 * D] = d2
        dg = _dot_nt(d1, wg_ref[:, 0:D]) + _dot_nt(d2, wg_ref[:, D:2 * D])
        g, gd = _gelu_and_grad(yp_ref[...])
        gact_ref[...] = g.astype(MXU_DTYPE)
        dy = dg * gd
        dy_ref[...] = dy
        dd_ref[...] += _rowsum8(dy * u_ref[...])

    full = lambda a: pl.BlockSpec(a.shape, lambda i: (0,) * a.ndim)
    rs = lambda c: pl.BlockSpec((tm, c), lambda i: (i, 0))
    return _call(
        body, name=name,
        out_shape=(jax.ShapeDtypeStruct((T, D), F32), jax.ShapeDtypeStruct((T, D), MXU_DTYPE),
                   jax.ShapeDtypeStruct((T, 2 * D), MXU_DTYPE), jax.ShapeDtypeStruct((T, D), MXU_DTYPE),
                   jax.ShapeDtypeStruct((8, D), F32)),
        grid=(T // tm,),
        in_specs=[rs(D), rs(2 * D), rs(D), rs(D), full(w_glu), full(w_out)],
        out_specs=(rs(D), rs(D), rs(2 * D), rs(D), pl.BlockSpec((8, D), lambda i: (0, 0))),
        compiler_params=_params(("arbitrary",)),
    )(dout, gl, ypre, u, w_glu, w_out)


def s5_bwd_ssm(dy, u, st, wb, wc, tables, d, tm, name, comm=None):
    T, D = u.shape
    nt = T // tm
    seg = tm // 8
    GP = S5_G * S5_P
    assert len(tables) == 8

    def body(dy_ref, u_ref, st_ref, wb_ref, wc_ref, apr_ref, api_ref, aqr_ref, aqi_ref, aprr_ref, apir_ref,
             aqrr_ref, aqir_ref, d_ref, du_ref, dwb_ref, dwc_ref, dar_ref, dai_ref, S, L, carry, lcarry, cin):
        @pl.when(pl.program_id(0) == 0)
        def _():
            for ref in (lcarry, dwb_ref, dwc_ref, dar_ref, dai_ref):
                ref[...] = jnp.zeros_like(ref)

        uv = _regroup(u_ref[...], seg)
        ub = uv.astype(MXU_DTYPE)
        dyv = _regroup(dy_ref[...], seg)
        dyb = dyv.astype(MXU_DTYPE)
        carry[...] = st_ref[...]
        row8 = lax.broadcasted_iota(jnp.int32, (8, 1), 0)
        dus = []
        for sb in range(S5_NSB):
            ls = slice(sb * _SBW, (sb + 1) * _SBW)
            us = slice(sb * _SBU, (sb + 1) * _SBU)
            S[:, ls] = jnp.dot(ub[:, us], wb_ref[sb].astype(MXU_DTYPE), preferred_element_type=F32)
            L[:, ls] = _dot_nt(dyb[:, us], wc_ref[sb])
            _s5_scan_fwd(S, carry, cin, apr_ref, api_ref, aqr_ref, aqi_ref, seg, sb)
            lr, li, la = _s5_lanes(sb)
            ar, ai = apr_ref[0:1, la], api_ref[0:1, la]
            last = slice((seg - 1) * 8, seg * 8)
            gr, gi = L[last, lr], L[last, li]
            for i in reversed(range(seg - 1)):
                blk = slice(i * 8, (i + 1) * 8)
                gr, gi = ar * gr + ai * gi + L[blk, lr], ar * gi - ai * gr + L[blk, li]
                L[blk, lr] = gr
                L[blk, li] = gi
            for s, idx in ((1, 0), (2, 1), (4, 3)):
                qr, qi = aqr_ref[idx:idx + 1, la], aqi_ref[idx:idx + 1, la]
                m = row8 < 8 - s
                sr = jnp.where(m, pltpu.roll(gr, 8 - s, 0), 0.0)
                si = jnp.where(m, pltpu.roll(gi, 8 - s, 0), 0.0)
                gr, gi = gr + qr * sr + qi * si, gi + qr * si - qi * sr
            cr, ci = lcarry[0:1, lr], lcarry[0:1, li]
            pr, pi = aqrr_ref[:, la], aqir_ref[:, la]
            gr, gi = gr + pr * cr + pi * ci, gi + pr * ci - pi * cr
            xr = jnp.where(row8 == 7, cr, pltpu.roll(gr, 7, 0))
            xi = jnp.where(row8 == 7, ci, pltpu.roll(gi, 7, 0))
            lcarry[0:1, lr] = gr[0:1, :]
            lcarry[0:1, li] = gi[0:1, :]
            acc_r = jnp.zeros((8, _SBH), F32)
            acc_i = jnp.zeros((8, _SBH), F32)
            for i in range(seg):
                blk = slice(i * 8, (i + 1) * 8)
                pr, pi = aprr_ref[i:i + 1, la], apir_ref[i:i + 1, la]
                gr = L[blk, lr] + pr * xr + pi * xi
                gi = L[blk, li] + pr * xi - pi * xr
                L[blk, lr] = gr
                L[blk, li] = gi
                if i == 0:
                    hpr, hpi = cin[:, lr], cin[:, li]
                else:
                    hpr, hpi = S[(i - 1) * 8:i * 8, lr], S[(i - 1) * 8:i * 8, li]
                acc_r = acc_r + gr * hpr + gi * hpi
                acc_i = acc_i + gi * hpr - gr * hpi
            dar_ref[:, la] += acc_r
            dai_ref[:, la] += acc_i
            lb = L[:, ls].astype(MXU_DTYPE)
            dwb_ref[sb] += _dot_tn(lb, ub[:, us])
            dwc_ref[sb] += _dot_tn(S[:, ls].astype(MXU_DTYPE), dyb[:, us])
            dus.append(_dot_nt(lb, wb_ref[sb]))
        du_ref[...] = _ungroup(jnp.concatenate(dus, axis=1) + dyv * d_ref[...], seg).astype(MXU_DTYPE)

    rt = lambda i: nt - 1 - i
    full = lambda a: pl.BlockSpec(a.shape, lambda i: (0,) * a.ndim)
    acc = lambda shape: pl.BlockSpec(shape, lambda i: (0,) * len(shape))
    return _hosted(
        body, comm, (dy, u, st, wb, wc, *tables, d), name=name,
        out_shape=(jax.ShapeDtypeStruct((T, D), MXU_DTYPE), jax.ShapeDtypeStruct((S5_NSB, _SBW, _SBU), F32),
                   jax.ShapeDtypeStruct((S5_NSB, _SBW, _SBU), F32), jax.ShapeDtypeStruct((8, GP), F32),
                   jax.ShapeDtypeStruct((8, GP), F32)),
        grid=(nt,),
        in_specs=[pl.BlockSpec((tm, D), lambda i: (rt(i), 0)), pl.BlockSpec((tm, D), lambda i: (rt(i), 0)),
                  pl.BlockSpec((None, 8, S5_NS), lambda i: (rt(i), 0, 0)), full(wb), full(wc)]
        + [full(t) for t in tables] + [full(d)],
        out_specs=(pl.BlockSpec((tm, D), lambda i: (rt(i), 0)), acc((S5_NSB, _SBW, _SBU)), acc((S5_NSB, _SBW, _SBU)),
                   acc((8, GP)), acc((8, GP))),
        scratch_shapes=[pltpu.VMEM((tm, S5_NS), F32), pltpu.VMEM((tm, S5_NS), F32), pltpu.VMEM((8, S5_NS), F32),
                        pltpu.VMEM((8, S5_NS), F32), pltpu.VMEM((8, S5_NS), F32)],
        compiler_params=_params(("arbitrary",)),
    )


def final_loss(h, g, target, tm, name):
    T, D = h.shape

    def body(h_ref, g_ref, t_ref, dh_ref, se_ref, dg_ref):
        @pl.when(pl.program_id(0) == 0)
        def _():
            se_ref[...] = jnp.zeros_like(se_ref)
            dg_ref[...] = jnp.zeros_like(dg_ref)

        r, xhat = _rms(h_ref[...])
        gv = g_ref[...]
        e = xhat * gv - t_ref[...]
        se_ref[...] += _rowsum8(e * e)
        dy = e * (1.0 / D)
        dg_ref[...] += _rowsum8(dy * xhat)
        dh_ref[...] = _rms_bwd(dy, xhat, r, gv)

    rs = pl.BlockSpec((tm, D), lambda i: (i, 0))
    acc = pl.BlockSpec((8, D), lambda i: (0, 0))
    return _call(
        body, name=name,
        out_shape=(jax.ShapeDtypeStruct((T, D), F32), jax.ShapeDtypeStruct((8, D), F32), jax.ShapeDtypeStruct((8, D), F32)),
        grid=(T // tm,), in_specs=[rs, pl.BlockSpec((1, D), lambda i: (0, 0)), rs], out_specs=(rs, acc, acc),
        compiler_params=_params(("arbitrary",)),
    )(h, g, target)


def _s5_discretize(a_re, a_im, log_dt, b_re, b_im):
    dt = jnp.exp(log_dt)[:, None]
    mag = jnp.exp(a_re * dt)
    abr = mag * jnp.cos(a_im * dt)
    abi = mag * jnp.sin(a_im * dt)
    ur, ui = abr - 1.0, abi
    den = a_re * a_re + a_im * a_im
    wr = (ur * a_re + ui * a_im) / den
    wi = (ui * a_re - ur * a_im) / den
    bbr = wr[..., None] * b_re - wi[..., None] * b_im
    bbi = wr[..., None] * b_im + wi[..., None] * b_re
    return abr, abi, bbr, bbi


def _s5_pack(abr, abi, bbr, bbi, c_re, c_im, seg_len):
    eye = jnp.eye(S5_SB, dtype=F32)
    b = jnp.stack([bbr, bbi], 0).reshape(2, S5_NSB, S5_SB, S5_P, S5_GC)
    wb = jnp.einsum('rsgpc,gh->shcrgp', b, eye).reshape(S5_NSB, _SBU, _SBW)
    c = jnp.stack([c_re, -c_im], 0).reshape(2, S5_NSB, S5_SB, S5_GC, S5_P)
    wc = jnp.einsum('rsgcp,gh->srgphc', c, eye).reshape(S5_NSB, _SBW, _SBU)
    def powers(r, i, n):
        fr, fi, br, bi = r, i, r, i
        m = 1
        while m < n:
            tr, ti = fr[m - 1:m], fi[m - 1:m]
            fr, fi = (jnp.concatenate([fr, fr * tr - fi * ti], 0), jnp.concatenate([fi, fr * ti + fi * tr], 0))
            br, bi = (jnp.concatenate([br * tr - bi * ti, br], 0), jnp.concatenate([br * ti + bi * tr, bi], 0))
            m *= 2
        return fr, fi, br, bi

    assert seg_len & (seg_len - 1) == 0
    ap = powers(abr.reshape(1, -1), abi.reshape(1, -1), seg_len)
    aq = powers(ap[0][seg_len - 1:seg_len], ap[1][seg_len - 1:seg_len], 8)
    return wb.astype(MXU_DTYPE), wc.astype(MXU_DTYPE), ap[0], ap[1], aq[0], aq[1], ap[2], ap[3], aq[2], aq[3]


def _s5_unpack_grads(dwb_t, dwc, dar8, dai8):
    eye = jnp.eye(S5_SB, dtype=F32)
    t = dwb_t.reshape(S5_NSB, 2, S5_SB, S5_P, S5_SB, S5_GC)
    db = jnp.einsum('srgphc,gh->rsgpc', t, eye).reshape(2, S5_G, S5_P, S5_GC)
    t = dwc.reshape(S5_NSB, 2, S5_SB, S5_P, S5_SB, S5_GC)
    dc = jnp.einsum('srgphc,gh->rsgcp', t, eye).reshape(2, S5_G, S5_GC, S5_P)
    return db[0], db[1], dc[0], -dc[1], dar8.sum(0).reshape(S5_G, S5_P), dai8.sum(0).reshape(S5_G, S5_P)


TM = 256
TM_FFN = 512
TM_S5 = 256


def _tn(a, b, name, comm=None):
    T, M, N = a.shape[0], a.shape[1], b.shape[1]
    bt = 4096 if a.dtype.itemsize + b.dtype.itemsize <= 4 else 2048
    return matmul_tn(a, b, min(M, 1024), min(N, 1024), bt if T % bt == 0 else T, name, comm=comm)


def local_step(x, target, W, sched):
    T, D = x.shape
    depth = W['norm_mix_g'].shape[0]
    row = lambda v: v.reshape(1, -1)
    saved = []
    h = x
    s5c = []
    tr = min(512, T)
    for j in range(W['s5_a_re'].shape[0]):
        prm = (W['s5_a_re'][j], W['s5_a_im'][j], W['s5_log_dt'][j], W['s5_b_re'][j], W['s5_b_im'][j])
        disc, disc_vjp = jax.vjp(_s5_discretize, *prm)
        s5c.append((*_s5_pack(*disc, W['s5_c_re'][j], W['s5_c_im'][j], min(TM_S5, T) // 8), disc_vjp))
    sp_all = jax.nn.softplus(-W['rg_lambda'])
    for i in range(depth):
        j = i // 2
        if i % 2 == 0:
            xg, hn = norm_matmul(h, row(W['norm_mix_g'][i]), W['rg_w_in'][j], F32, tr, None, f"rg_in_{i}",
                                  comm=sched.comm(f"rg_in_{i}"))
            h1, hs = rg_fwd(xg, h, W['rg_conv_w'][j], row(W['rg_conv_b'][j]), W['rg_w_a'][j].astype(MXU_DTYPE),
                            row(W['rg_b_a'][j]), W['rg_w_x'][j].astype(MXU_DTYPE), row(W['rg_b_x'][j]), row(sp_all[j]),
                            W['rg_w_out'][j], min(TM_FFN, T), f"rg_fwd_{i}", comm=sched.comm(f"rg_fwd_{i}"))
            mix = (xg, hn, hs)
        else:
            u, hn = norm_matmul(h, row(W['norm_mix_g'][i]), W['s5_w_in'][j], F32, tr, None, f"s5_in_{i}")
            h1, ypre, gl, st = s5_fwd(u, h, *s5c[j][:6], row(W['s5_d'][j]), W['s5_w_glu'][j], W['s5_w_out'][j],
                                      min(TM_S5, T), f"s5_fwd_{i}")
            mix = (u, hn, ypre, gl, st)
        up, hn2 = norm_matmul(h1, row(W['norm_ffn_g'][i]), W['ffn_w_up'][i], MXU_DTYPE, tr, None, f"ffn_up_{i}",
                              comm=sched.comm(f"ffn_up_{i}"))
        h2, av = ffn_fwd(up, h1, W['ffn_conv_w'][i], row(W['ffn_conv_b'][i]), W['ffn_w_down'][i], min(TM_FFN, T), f"ffn_fwd_{i}",
                         comm=sched.comm(f"ffn_fwd_{i}"))
        saved.append((h, mix, h1, hn2, up, av))
        h = h2
    dh, se8, dgf8 = final_loss(h, row(W['norm_final_g']), target, tr, "final_loss")
    G = {k: [None] * len(v) for k, v in W.items() if k != 'norm_final_g'}
    G['norm_final_g'] = dgf8.sum(0)
    for i in reversed(range(depth)):
        j = i // 2
        h0, mix, h1, hn2, up, av = saved[i]
        dup, act, dcb8 = ffn_bwd_act(dh, av, W['ffn_w_down'][i], min(TM_FFN, T), f"ffn_bwd_act_{i}",
                                     comm=sched.comm(f"ffn_bwd_act_{i}", G))
        G['ffn_w_down'][i] = _tn(act, dh, f"ffn_dwdown_{i}")
        dh1, dg8, dupp, dcw8 = dx_norm_bwd(dup, W['ffn_w_up'][i], h1, row(W['norm_ffn_g'][i]), dh, TM,
                                           f"ffn_bwd_in_{i}", conv_w=W['ffn_conv_w'][i], conv_x=up)
        G['ffn_w_up'][i] = _tn(hn2, dupp, f"ffn_dwup_{i}")
        G['ffn_conv_b'][i] = dcb8.sum(0)
        G['ffn_conv_w'][i] = dcw8.sum(1)
        G['norm_ffn_g'][i] = dg8.sum(0)
        if i % 2 == 0:
            xg, hn, hs = mix
            dxg, y, dwa, dwx, dba8, dbx8, dsp8, dcb8, dcw8 = rg_bwd(
                dh1, xg, hs, W['rg_conv_w'][j], row(W['rg_conv_b'][j]), W['rg_w_a'][j].astype(MXU_DTYPE),
                row(W['rg_b_a'][j]), W['rg_w_x'][j].astype(MXU_DTYPE), row(W['rg_b_x'][j]), row(sp_all[j]),
                W['rg_w_out'][j], min(TM_FFN, T), f"rg_bwd_{i}", comm=sched.comm(f"rg_bwd_{i}", G))
            G['rg_w_out'][j] = _tn(y, dh1, f"rg_dwout_{i}")
            G['rg_w_a'][j], G['rg_w_x'][j] = dwa, dwx
            G['rg_b_a'][j] = dba8.sum(0).reshape(RG_HEADS, RG_BW)
            G['rg_b_x'][j] = dbx8.sum(0).reshape(RG_HEADS, RG_BW)
            G['rg_lambda'][j] = dsp8.sum(0) * (-jax.nn.sigmoid(-W['rg_lambda'][j]))
            G['rg_conv_b'][j] = dcb8.sum(0)
            G['rg_conv_w'][j] = dcw8.sum(1)
            dh, dg8 = dx_norm_bwd(dxg, W['rg_w_in'][j], h0, row(W['norm_mix_g'][i]), dh1, TM, f"rg_bwd_in_{i}",
                                  comm=sched.comm(f"rg_bwd_in_{i}", G))
            G['rg_w_in'][j] = _tn(hn, dxg, f"rg_dwin_{i}", comm=sched.comm(f"rg_dwin_{i}", G))
        else:
            u, hn, ypre, gl, st = mix
            disc_vjp = s5c[j][-1]
            dy, oact, dgl, gact, dd8 = s5_bwd_glu(dh1, gl, ypre, u, W['s5_w_glu'][j], W['s5_w_out'][j], TM,
                                                  f"s5_bwd_glu_{i}")
            G['s5_w_out'][j] = _tn(oact, dh1, f"s5_dwout_{i}")
            G['s5_w_glu'][j] = _tn(gact, dgl, f"s5_dwglu_{i}")
            du, dwb_t, dwc, dar8, dai8 = s5_bwd_ssm(dy, u, st, *s5c[j][:2], s5c[j][2:10], row(W['s5_d'][j]),
                                                    min(TM_S5, T), f"s5_bwd_ssm_{i}",
                                                    comm=sched.comm(f"s5_bwd_ssm_{i}", G))
            dh, dg8 = dx_norm_bwd(du, W['s5_w_in'][j], h0, row(W['norm_mix_g'][i]), dh1, TM, f"s5_bwd_in_{i}")
            G['s5_w_in'][j] = _tn(hn, du, f"s5_dwin_{i}")
            dbbr, dbbi, dcr, dci, dabr, dabi = _s5_unpack_grads(dwb_t, dwc, dar8, dai8)
            da_re, da_im, dlog_dt, db_re, db_im = disc_vjp((dabr, dabi, dbbr, dbbi))
            G['s5_a_re'][j], G['s5_a_im'][j], G['s5_log_dt'][j] = da_re, da_im, dlog_dt
            G['s5_b_re'][j], G['s5_b_im'][j], G['s5_c_re'][j], G['s5_c_im'][j] = db_re, db_im, dcr, dci
            G['s5_d'][j] = dd8.sum(0)
        G['norm_mix_g'][i] = dg8.sum(0)
    G = {k: (v if (k == 'norm_final_g' or k in BIG) else jnp.stack(v, 0)) for k, v in G.items()}
    return se8, dh, G


BIG = {'rg_w_in': 1, 'rg_w_out': 0, 's5_w_in': 0, 's5_w_glu': 1, 's5_w_out': 0, 'ffn_w_up': 1, 'ffn_w_down': 0}
SMALL_SHARDED = ('rg_conv_w', 'ffn_conv_w', 's5_d')
REPLICATED = ('norm_mix_g', 'norm_ffn_g', 'norm_final_g', 'rg_conv_b', 'rg_w_a', 'rg_b_a', 'rg_w_x', 'rg_b_x',
              'rg_lambda', 's5_a_re', 's5_a_im', 's5_log_dt', 's5_b_re', 's5_b_im', 's5_c_re', 's5_c_im', 'ffn_conv_b')
REP_LATE = ('norm_mix_g',)
REP_MAIN = tuple(k for k in REPLICATED if k not in REP_LATE)


def _me():
    x, y, c = lax.axis_index("x"), lax.axis_index("y"), lax.axis_index("c")
    return x, y, c, 4 * x + 2 * y + c


def _win(ref, axis, dev, width):
    idx = [slice(None)] * len(ref.shape)
    idx[axis] = pl.ds(pl.multiple_of(dev * width, width), width)
    return ref.at[tuple(idx)]


def gather_plan(items, deliver):
    n = len(items)

    def tools(ins, outs, sems):
        send_sems, recv_sems, local_sems = sems
        x, y, c, me = _me()
        sib = (x, y, 1 - c)
        chips = [(1 - x, y), (x, 1 - y), (1 - x, 1 - y)]
        num = lambda px, py, pc: 4 * px + 2 * py + pc

        def src_of(a):
            return ins[a] if items[a][1] is None else ins[a].at[items[a][1]]

        def block(a, dev):
            return _win(outs[a], items[a][2], dev, src_of(a).shape[items[a][2]])

        def copy(a, k, dev, to, own=False):
            return pltpu.make_async_remote_copy(
                src_ref=src_of(a) if own else block(a, dev), dst_ref=block(a, dev),
                send_sem=send_sems.at[a, k], recv_sem=recv_sems.at[a, k], device_id=to, device_id_type=MESH)

        mine = lambda: [pltpu.make_async_copy(src_of(a), block(a, me), local_sems.at[a]) for a in range(n)]
        own = lambda: [cp for a in range(n) for cp in
                       [copy(a, 0, me, sib, own=True)] + [copy(a, 1 + j, me, (*chip, c), own=True)
                                                          for j, chip in enumerate(chips)]]
        arrived = lambda j, a: copy(a, 1 + j, num(*chips[j], c), (x, y, c))
        passed = lambda j, a: copy(a, 4 + j, num(*chips[j], c), sib)
        from_sib = lambda: ([copy(a, 0, num(x, y, 1 - c), (x, y, c)) for a in range(n)]
                            + [copy(a, 4 + j, num(*chip, 1 - c), (x, y, c)) for j, chip in enumerate(chips)
                               for a in range(n)])
        return mine, own, arrived, passed, from_sib

    def start(ins, outs, sems):
        mine, own, _, _, _ = tools(ins, outs, sems)
        for cp in mine() + own():
            cp.start()

    def middle(ins, outs, sems):
        _, _, arrived, passed, _ = tools(ins, outs, sems)
        for j in range(3):
            for a in range(n):
                arrived(j, a).wait_recv()
                passed(j, a).start()

    def finish(ins, outs, sems):
        mine, own, _, passed, from_sib = tools(ins, outs, sems)
        for cp in from_sib():
            cp.wait_recv()
        for cp in own() + [passed(j, a) for j in range(3) for a in range(n)]:
            cp.wait_send()
        for cp in mine():
            cp.wait()

    return Comm([it[0] for it in items], [jax.ShapeDtypeStruct(it[3], it[0].dtype) for it in items],
                [pltpu.SemaphoreType.DMA((n, 7)), pltpu.SemaphoreType.DMA((n, 7)), pltpu.SemaphoreType.DMA((n,))],
                start, middle, finish, deliver)


def exchange_plan(items, deliver):
    n = len(items)
    width = [arr.shape[axis] // N_DEV for arr, axis in items]
    shard = [arr.shape[:axis] + (w,) + arr.shape[axis + 1:] for (arr, axis), w in zip(items, width)]

    def tools(ins, outs, sems):
        send_sems, recv_sems, local_sems = sems
        x, y, c, me = _me()
        piece = lambda a, dev: _win(ins[a], items[a][1], dev, width[a])
        mine = lambda: [pltpu.make_async_copy(piece(a, me), outs[a].at[me], local_sems.at[a]) for a in range(n)]

        def remote(sending):
            cps = []
            for k in range(1, N_DEV):
                px, py, pc = (1 - x) if k & 4 else x, (1 - y) if k & 2 else y, (1 - c) if k & 1 else c
                peer = 4 * px + 2 * py + pc
                for a in range(n):
                    src, dst = (piece(a, peer), outs[a].at[me]) if sending else (piece(a, me), outs[a].at[peer])
                    cps.append(pltpu.make_async_remote_copy(
                        src_ref=src, dst_ref=dst, send_sem=send_sems.at[a, k - 1], recv_sem=recv_sems.at[a, k - 1],
                        device_id=(px, py, pc), device_id_type=MESH))
            return cps

        return mine, remote

    def start(ins, outs, sems):
        mine, remote = tools(ins, outs, sems)
        for cp in mine() + remote(True):
            cp.start()

    def middle(ins, outs, sems):
        pass

    def finish(ins, outs, sems):
        mine, remote = tools(ins, outs, sems)
        for cp in remote(False):
            cp.wait_recv()
        for cp in remote(True):
            cp.wait_send()
        for cp in mine():
            cp.wait()

    return Comm([it[0] for it in items], [jax.ShapeDtypeStruct((N_DEV,) + s, it[0].dtype) for it, s in zip(items, shard)],
                [pltpu.SemaphoreType.DMA((n, 7)), pltpu.SemaphoreType.DMA((n, 7)), pltpu.SemaphoreType.DMA((n,))],
                start, middle, finish, deliver)


def adam_update(parts, w, m, v, name):
    R, C = w.shape
    br = next((b for b in (256, 128) if R > b and R % b == 0), R)
    np_ = parts.shape[0]

    def body(p_ref, w_ref, m_ref, v_ref, g_ref, d_ref, nm_ref, nv_ref):
        _adam_body(np_, p_ref, w_ref, m_ref, v_ref, g_ref, d_ref, nm_ref, nv_ref)

    bs = pl.BlockSpec((br, C), lambda i: (i, 0))
    out = jax.ShapeDtypeStruct((R, C), F32)
    return _call(
        body, name=name, out_shape=(out, out, out, out), grid=(R // br,),
        in_specs=[pl.BlockSpec((np_, br, C), lambda i: (0, i, 0)), bs, bs, bs], out_specs=(bs, bs, bs, bs),
        compiler_params=_params(("parallel",)),
    )(parts, w, m, v)


def _adam_body(np_, p_ref, w_ref, m_ref, v_ref, g_ref, d_ref, nm_ref, nv_ref):
    c1 = 1.0 / (1.0 - ADAM_B1 ** ADAM_STEP)
    c2 = 1.0 / (1.0 - ADAM_B2 ** ADAM_STEP)
    g = p_ref[0].astype(F32)
    for p in range(1, np_):
        g = g + p_ref[p].astype(F32)
    nm = ADAM_B1 * m_ref[...] + (1.0 - ADAM_B1) * g
    nv = ADAM_B2 * v_ref[...] + (1.0 - ADAM_B2) * (g * g)
    g_ref[...] = g
    nm_ref[...] = nm
    nv_ref[...] = nv
    d_ref[...] = -ADAM_LR * ((nm * c1) / (jnp.sqrt(nv * c2) + ADAM_EPS) + ADAM_WD * w_ref[...])


def adam_layer(parts, w, m, v, l, prev, name):
    L, R, C = w.shape
    br = next((b for b in (256, 128) if R > b and R % b == 0), R)

    def body(p_ref, w_ref, m_ref, v_ref, *rest):
        _adam_body(N_DEV, p_ref, w_ref, m_ref, v_ref, *rest[-4:])

    bs = pl.BlockSpec((None, br, C), lambda i: (l, i, 0))
    out = jax.ShapeDtypeStruct((L, R, C), F32)
    extra = {} if prev is None else dict(input_output_aliases={4 + q: q for q in range(4)})
    return _call(
        body, name=name, out_shape=(out, out, out, out), grid=(R // br,),
        in_specs=[pl.BlockSpec((N_DEV, br, C), lambda i: (0, i, 0)), bs, bs, bs] + ([] if prev is None else [ANY] * 4),
        out_specs=(bs, bs, bs, bs), compiler_params=_params(("parallel",)), **extra,
    )(parts, w, m, v, *(() if prev is None else prev))


def sum_parts(parts, name):
    n, R, C = parts.shape

    def body(p_ref, o_ref):
        g = p_ref[0]
        for p in range(1, n):
            g = g + p_ref[p]
        o_ref[...] = g

    return _call(body, name=name, out_shape=jax.ShapeDtypeStruct((R, C), parts.dtype),
                 compiler_params=pltpu.CompilerParams(vmem_limit_bytes=VMEM_LIMIT))(parts)


def _pack_rows(arrs):
    rows = []
    for a in arrs:
        f = a.reshape(-1)
        r = -(-f.shape[0] // 1024)
        r8 = -(-r // 8) * 8
        rows.append(jnp.pad(f, (0, r8 * 1024 - f.shape[0])).reshape(r8, 1024))
    packed = jnp.concatenate(rows, 0)
    return jnp.pad(packed, ((0, -packed.shape[0] % 128), (0, 0)))


def _unpack_rows(packed, shapes):
    out, o = [], 0
    for s in shapes:
        nel = math.prod(s)
        r8 = -(-(-(-nel // 1024)) // 8) * 8
        out.append(packed[o:o + r8].reshape(-1)[:nel].reshape(s))
        o += r8
    return out


def kernel(x, norm_mix_g, norm_ffn_g, norm_final_g, rg_w_in, rg_conv_w, rg_conv_b, rg_w_a, rg_b_a, rg_w_x, rg_b_x, rg_lambda, rg_w_out, s5_w_in, s5_a_re, s5_a_im, s5_log_dt, s5_b_re, s5_b_im, s5_c_re, s5_c_im, s5_d, s5_w_glu, s5_w_out, ffn_w_up, ffn_conv_w, ffn_conv_b, ffn_w_down, loss_target, m_norm_mix_g, m_norm_ffn_g, m_norm_final_g, m_rg_w_in, m_rg_conv_w, m_rg_conv_b, m_rg_w_a, m_rg_b_a, m_rg_w_x, m_rg_b_x, m_rg_lambda, m_rg_w_out, m_s5_w_in, m_s5_a_re, m_s5_a_im, m_s5_log_dt, m_s5_b_re, m_s5_b_im, m_s5_c_re, m_s5_c_im, m_s5_d, m_s5_w_glu, m_s5_w_out, m_ffn_w_up, m_ffn_conv_w, m_ffn_conv_b, m_ffn_w_down, v_norm_mix_g, v_norm_ffn_g, v_norm_final_g, v_rg_w_in, v_rg_conv_w, v_rg_conv_b, v_rg_w_a, v_rg_b_a, v_rg_w_x, v_rg_b_x, v_rg_lambda, v_rg_w_out, v_s5_w_in, v_s5_a_re, v_s5_a_im, v_s5_log_dt, v_s5_b_re, v_s5_b_im, v_s5_c_re, v_s5_c_im, v_s5_d, v_s5_w_glu, v_s5_w_out, v_ffn_w_up, v_ffn_conv_w, v_ffn_conv_b, v_ffn_w_down):
    names = ('norm_mix_g', 'norm_ffn_g', 'norm_final_g', 'rg_w_in', 'rg_conv_w', 'rg_conv_b', 'rg_w_a', 'rg_b_a',
             'rg_w_x', 'rg_b_x', 'rg_lambda', 'rg_w_out', 's5_w_in', 's5_a_re', 's5_a_im', 's5_log_dt', 's5_b_re',
             's5_b_im', 's5_c_re', 's5_c_im', 's5_d', 's5_w_glu', 's5_w_out', 'ffn_w_up', 'ffn_conv_w', 'ffn_conv_b',
             'ffn_w_down')
    loc = locals()
    Wl = {k: loc[k] for k in names}
    Ml = {k: loc['m_' + k] for k in names}
    Vl = {k: loc['v_' + k] for k in names}

    depth = norm_mix_g.shape[0]
    mixer_keys = lambda i: ([('rg_w_in', i // 2), ('rg_w_out', i // 2)] if i % 2 == 0 else
                            [('s5_w_in', i // 2), ('s5_w_glu', i // 2), ('s5_w_out', i // 2)])
    ffn_keys = lambda i: [('ffn_w_up', i), ('ffn_w_down', i)]
    shards = {k: Wl[k].astype(BF16) for k in BIG}
    W = {k: Wl[k] for k in REPLICATED}
    W.update({k: [None] * Wl[k].shape[0] for k in BIG})
    parts = {}

    def gather_of(keys, small=False):
        items = []
        for k, l in keys:
            _, r, c = shards[k].shape
            items.append((shards[k], l, BIG[k], (r * N_DEV, c) if BIG[k] == 0 else (r, c * N_DEV)))
        if small:
            items += [(Wl[k], None, Wl[k].ndim - 1, Wl[k].shape[:-1] + (Wl[k].shape[-1] * N_DEV,)) for k in SMALL_SHARDED]

        def deliver(outs):
            for (k, l), arr in zip(keys, outs):
                W[k][l] = arr
            if small:
                W.update(zip(SMALL_SHARDED, outs[len(keys):]))

        return gather_plan(items, deliver)

    def exchange_of(keys, G, extra=()):
        items = [(G[k][l], BIG[k]) for k, l in keys] + [(arr, axis) for _, arr, axis in extra]
        return exchange_plan(items, lambda outs: parts.update(zip(list(keys) + [e[0] for e in extra], outs)))

    class Sched:
        @staticmethod
        def comm(host, G=None):
            kind, _, i = host.rpartition("_")
            i = int(i)
            if host == "rg_in_0":
                return gather_of(mixer_keys(0)[1:])
            if host == "rg_fwd_0":
                return gather_of(ffn_keys(0))
            if kind == "ffn_up" and i + 1 < depth:
                return gather_of(ffn_keys(i + 1)[:1])
            if kind == "ffn_fwd" and i + 1 < depth:
                return gather_of(mixer_keys(i + 1) + ffn_keys(i + 1)[1:])
            if kind == "ffn_bwd_act" and i + 1 < depth:
                return exchange_of(mixer_keys(i + 1), G)
            if kind in ("rg_bwd", "s5_bwd_ssm"):
                return exchange_of(ffn_keys(i), G)
            if host == "rg_bwd_in_0":
                stack = lambda k: G[k] if k == 'norm_final_g' else jnp.stack(G[k], 0)
                extra = [(k, stack(k), Wl[k].ndim - 1) for k in SMALL_SHARDED]
                extra.append(('rep_main', _pack_rows([stack(k).astype(F32) for k in REP_MAIN]), 0))
                return exchange_of(mixer_keys(0)[1:], G, extra)
            if host == "rg_dwin_0":
                rsum = sum_parts(parts['rep_main'], "sum_replicated")
                return gather_plan([(rsum, None, 0, (rsum.shape[0] * N_DEV, rsum.shape[1]))],
                                   lambda outs: parts.update(rep_main_full=outs[0]))
            return None

    run_comm(gather_of(mixer_keys(0)[:1], small=True), "gather_first")

    se8, gx, G = local_step(x[0], loss_target[0], W, Sched)
    loss = lax.psum(0.5 * jnp.sum(se8) / x.shape[-1], ("x", "y", "c"))

    gp_late = _pack_rows([G[k].astype(F32) for k in REP_LATE])
    run_comm(exchange_of(mixer_keys(0)[:1], G, [('rep_late', gp_late, 0)]), "exchange_last")
    out_g, out_d, out_m, out_v = {}, {}, {}, {}
    for k in BIG:
        res = None
        for l in range(Wl[k].shape[0]):
            res = adam_layer(parts[(k, l)], Wl[k], Ml[k], Vl[k], l, res, f"adam_{k}_{l}")
        out_g[k], out_d[k], out_m[k], out_v[k] = res
    for k in SMALL_SHARDED:
        shp = Wl[k].shape
        r2 = (math.prod(shp[:-1]), shp[-1])
        res = adam_update(parts[k].reshape((N_DEV,) + r2), Wl[k].reshape(r2), Ml[k].reshape(r2), Vl[k].reshape(r2),
                          f"adam_{k}")
        out_g[k], out_d[k], out_m[k], out_v[k] = [t.reshape(shp) for t in res]
    rsum = sum_parts(parts['rep_late'], "sum_replicated_late")
    run_comm(gather_plan([(rsum, None, 0, gp_late.shape)], lambda outs: parts.update(rep_late_full=outs[0])),
             "gather_small_grads")
    for keys, full, tag in ((REP_MAIN, parts['rep_main_full'], "main"), (REP_LATE, parts['rep_late_full'], "late")):
        res = adam_update(full[None], _pack_rows([Wl[k] for k in keys]), _pack_rows([Ml[k] for k in keys]),
                          _pack_rows([Vl[k] for k in keys]), f"adam_replicated_{tag}")
        for dst, packed in zip((out_g, out_d, out_m, out_v), res):
            dst.update(zip(keys, _unpack_rows(packed, [Wl[k].shape for k in keys])))
    return (loss, gx[None], *[out_g[k] for k in names], *[out_d[k] for k in names], *[out_m[k] for k in names],
            *[out_v[k] for k in names])
```

```python
import functools
import math

import jax
import jax.numpy as jnp
from jax import lax
from jax.experimental import pallas as pl
from jax.experimental.pallas import tpu as pltpu

F32 = jnp.float32
BF16 = jnp.bfloat16
MXU_DTYPE = jnp.bfloat16

NORM_EPS = 1e-6
RG_C = 8.0
RG_HEADS = 8
RG_BW = 128
S5_G = 64
S5_GC = 16
S5_P = 64
S5_SB = 8
S5_NSB = S5_G // S5_SB
S5_NS = 2 * S5_G * S5_P
ADAM_LR = 0.001
ADAM_B1 = 0.9
ADAM_B2 = 0.999
ADAM_EPS = 1e-08
ADAM_WD = 0.01
ADAM_STEP = 10
N_DEV = 8
VMEM_LIMIT = 56 * 1024 * 1024


def _call(body, **kw):
    return pl.pallas_call(body, **kw)


def _params(sem, vmem=VMEM_LIMIT):
    return pltpu.CompilerParams(dimension_semantics=sem, vmem_limit_bytes=vmem)


MESH = pl.DeviceIdType.MESH
ANY = pl.BlockSpec(memory_space=pl.ANY)


class Comm:
    def __init__(self, operands, out_shape, scratch, start, middle, finish, deliver):
        self.operands, self.out_shape, self.scratch = list(operands), list(out_shape), list(scratch)
        self.start, self.middle, self.finish, self.deliver = start, middle, finish, deliver


def run_comm(comm, name):
    ci, co = len(comm.operands), len(comm.out_shape)

    def body(*refs):
        parts = (refs[:ci], refs[ci:ci + co], refs[ci + co:])
        comm.start(*parts)
        comm.middle(*parts)
        comm.finish(*parts)

    comm.deliver(_call(body, name=name, out_shape=tuple(comm.out_shape), in_specs=[ANY] * ci,
                       out_specs=tuple([ANY] * co), scratch_shapes=comm.scratch)(*comm.operands))


def _hosted(body, comm, args, *, name, out_shape, grid, in_specs, out_specs, scratch_shapes, compiler_params):
    if comm is None:
        return _call(body, name=name, out_shape=tuple(out_shape), grid=grid, in_specs=in_specs,
                     out_specs=tuple(out_specs), scratch_shapes=scratch_shapes, compiler_params=compiler_params)(*args)
    n_in, n_out, n_sc = len(in_specs), len(out_shape), len(scratch_shapes)
    ci, co = len(comm.operands), len(comm.out_shape)
    nsteps = math.prod(grid)
    mid = (2 * nsteps) // 3

    def wrapped(*refs):
        ins, refs = refs[:n_in], refs[n_in:]
        cins, refs = refs[:ci], refs[ci:]
        outs, refs = refs[:n_out], refs[n_out:]
        couts, refs = refs[:co], refs[co:]
        sc, csc = refs[:n_sc], refs[n_sc:]
        step = pl.program_id(0)
        for d in range(1, len(grid)):
            step = step * grid[d] + pl.program_id(d)

        @pl.when(step == 0)
        def _():
            comm.start(cins, couts, csc)

        body(*ins, *outs, *sc)

        @pl.when(step == mid)
        def _():
            comm.middle(cins, couts, csc)

        @pl.when(step == nsteps - 1)
        def _():
            comm.finish(cins, couts, csc)

    res = _call(wrapped, name=name, out_shape=(*out_shape, *comm.out_shape), grid=grid,
                in_specs=[*in_specs, *[ANY] * ci], out_specs=(*out_specs, *[ANY] * co),
                scratch_shapes=[*scratch_shapes, *comm.scratch],
                compiler_params=_params(("arbitrary",) * len(grid)))(*args, *comm.operands)
    comm.deliver(res[n_out:])
    return res[:n_out]


_GELU_C = 0.7978845608028654
_GELU_A = 0.044715


def _gelu(x):
    return 0.5 * x * (1.0 + jnp.tanh(_GELU_C * (x + _GELU_A * x * x * x)))


def _gelu_and_grad(x):
    x2 = x * x
    t = jnp.tanh(_GELU_C * (x + _GELU_A * x2 * x))
    g = 0.5 * x * (1.0 + t)
    dg = 0.5 * (1.0 + t) + 0.5 * x * (1.0 - t * t) * _GELU_C * (1.0 + 3.0 * _GELU_A * x2)
    return g, dg


def _sigmoid(x):
    return 1.0 / (1.0 + jnp.exp(-x))


def _neg_expm1(x):
    series = -x * (1.0 + x * (0.5 + x * (1.0 / 6.0 + x * (1.0 / 24.0 + x * (1.0 / 120.0 + x * (1.0 / 720.0))))))
    return jnp.where(x > -0.1, series, 1.0 - jnp.exp(x))


def _rowsum8(x):
    r, c = x.shape
    return x.reshape(r // 8, 8, c).sum(axis=0)


def _dot(a, b):
    return jnp.dot(a.astype(MXU_DTYPE), b.astype(MXU_DTYPE), preferred_element_type=F32)


def _dot_nt(a, b):
    return lax.dot_general(a.astype(MXU_DTYPE), b.astype(MXU_DTYPE), (((1,), (1,)), ((), ())),
                           preferred_element_type=F32)


def _dot_tn(a, b):
    return lax.dot_general(a.astype(MXU_DTYPE), b.astype(MXU_DTYPE), (((0,), (0,)), ((), ())),
                           preferred_element_type=F32)


def _shift_down(x, s, fills, row):
    y = pltpu.roll(x, s, 0)
    for t in range(s):
        y = jnp.where(row == t, fills[s - 1 - t], y)
    return y


def _shift_up(x, s, fills, row):
    n = x.shape[0]
    y = pltpu.roll(x, n - s, 0)
    for t in range(s):
        y = jnp.where(row == n - s + t, fills[t], y)
    return y


def _rms(x):
    r = lax.rsqrt(jnp.mean(x * x, axis=-1, keepdims=True) + NORM_EPS)
    return r, x * r


def _rms_bwd(dhn, xhat, r, g):
    dz = dhn * g
    return r * (dz - xhat * jnp.mean(dz * xhat, axis=-1, keepdims=True))


def norm_matmul(h, g, w, out_dtype, tm, tn, name, comm=None):
    T, D = h.shape
    N = w.shape[1]
    tn = N if tn is None else tn

    def body(h_ref, g_ref, w_ref, o_ref, hn_ref, hn_s):
        @pl.when(pl.program_id(1) == 0)
        def _():
            _, xhat = _rms(h_ref[...])
            v = (xhat * g_ref[...]).astype(MXU_DTYPE)
            hn_s[...] = v
            hn_ref[...] = v

        o_ref[...] = jnp.dot(hn_s[...], w_ref[...].astype(MXU_DTYPE), preferred_element_type=F32).astype(o_ref.dtype)

    return _hosted(
        body, comm, (h, g, w), name=name,
        out_shape=(jax.ShapeDtypeStruct((T, N), out_dtype), jax.ShapeDtypeStruct((T, D), MXU_DTYPE)),
        grid=(T // tm, N // tn),
        in_specs=[pl.BlockSpec((tm, D), lambda i, j: (i, 0)), pl.BlockSpec((1, D), lambda i, j: (0, 0)),
                  pl.BlockSpec((D, tn), lambda i, j: (0, j))],
        out_specs=(pl.BlockSpec((tm, tn), lambda i, j: (i, j)), pl.BlockSpec((tm, D), lambda i, j: (i, 0))),
        scratch_shapes=[pltpu.VMEM((tm, D), MXU_DTYPE)],
        compiler_params=_params(("parallel", "arbitrary")),
    )


def matmul_tn(a, b, bm, bn, bt, name, out_dtype=BF16, comm=None):
    T, M = a.shape
    N = b.shape[1]
    nk = T // bt

    def body(a_ref, b_ref, o_ref, acc):
        k = pl.program_id(2)

        @pl.when(k == 0)
        def _():
            acc[...] = jnp.zeros_like(acc)

        acc[...] += _dot_tn(a_ref[...], b_ref[...])

        @pl.when(k == nk - 1)
        def _():
            o_ref[...] = acc[...].astype(o_ref.dtype)

    return _hosted(
        body, comm, (a, b), name=name,
        out_shape=(jax.ShapeDtypeStruct((M, N), out_dtype),),
        grid=(M // bm, N // bn, nk),
        in_specs=[pl.BlockSpec((bt, bm), lambda i, j, k: (k, i)), pl.BlockSpec((bt, bn), lambda i, j, k: (k, j))],
        out_specs=(pl.BlockSpec((bm, bn), lambda i, j, k: (i, j)),),
        scratch_shapes=[pltpu.VMEM((bm, bn), F32)],
        compiler_params=_params(("parallel", "parallel", "arbitrary")),
    )[0]


def dx_norm_bwd(dz, w, h, g, dres, tm, name, conv_w=None, conv_x=None, chunk=256, comm=None):
    T, N = dz.shape
    D = w.shape[0]
    nt = T // tm
    has_conv = conv_w is not None
    kw = conv_w.shape[0] if has_conv else 0

    def body(*refs):
        if has_conv:
            dz_ref, cw_ref, x_ref, w_ref, h_ref, g_ref, dres_ref, dh_ref, dg_ref, dzp_ref, dcw_ref, carry = refs
        else:
            dz_ref, w_ref, h_ref, g_ref, dres_ref, dh_ref, dg_ref = refs
        i = pl.program_id(0)

        @pl.when(i == 0)
        def _():
            dg_ref[...] = jnp.zeros_like(dg_ref)
            if has_conv:
                carry[...] = jnp.zeros_like(carry)
                dcw_ref[...] = jnp.zeros_like(dcw_ref)

        if has_conv:
            row = lax.broadcasted_iota(jnp.int32, (tm, 1), 0)
            for c0 in range(0, N, chunk):
                sl = slice(c0, c0 + chunk)
                d0 = dz_ref[:, sl].astype(F32)
                xv = x_ref[:, sl].astype(F32)
                fills = [carry[t:t + 1, sl] for t in range(kw - 1)]
                acc = cw_ref[kw - 1:kw, sl] * d0
                dcw_ref[kw - 1, :, sl] += _rowsum8(d0 * xv)
                for s in range(1, kw):
                    ds = _shift_up(d0, s, fills, row)
                    acc = acc + cw_ref[kw - 1 - s:kw - s, sl] * ds
                    dcw_ref[kw - 1 - s, :, sl] += _rowsum8(ds * xv)
                zb = acc.astype(MXU_DTYPE)
                dzp_ref[:, sl] = zb
                part = _dot_nt(zb, w_ref[:, sl])
                dhn = part if c0 == 0 else dhn + part
            carry[...] = dz_ref[0:16, :].astype(F32)
        else:
            dhn = _dot_nt(dz_ref[...], w_ref[...])
        r, xhat = _rms(h_ref[...])
        dg_ref[...] += _rowsum8(dhn * xhat)
        dh_ref[...] = dres_ref[...] + _rms_bwd(dhn, xhat, r, g_ref[...])

    if has_conv:
        ti = lambda i: nt - 1 - i
    else:
        ti = lambda i: i
    row_spec = lambda c: pl.BlockSpec((tm, c), lambda i: (ti(i), 0))
    full = lambda a: pl.BlockSpec(a.shape, lambda i: (0,) * a.ndim)
    in_specs = [row_spec(N)] + ([full(conv_w), row_spec(N)] if has_conv else []) + [full(w), row_spec(D), full(g), row_spec(D)]
    out_shape = [jax.ShapeDtypeStruct((T, D), F32), jax.ShapeDtypeStruct((8, D), F32)]
    out_specs = [row_spec(D), pl.BlockSpec((8, D), lambda i: (0, 0))]
    scratch = []
    if has_conv:
        out_shape += [jax.ShapeDtypeStruct((T, N), MXU_DTYPE), jax.ShapeDtypeStruct((kw, 8, N), F32)]
        out_specs += [row_spec(N), pl.BlockSpec((kw, 8, N), lambda i: (0, 0, 0))]
        scratch = [pltpu.VMEM((16, N), F32)]
    args = [dz] + ([conv_w, conv_x] if has_conv else []) + [w, h, g, dres]
    return _hosted(
        body, comm, args, name=name, out_shape=tuple(out_shape), grid=(nt,), in_specs=in_specs,
        out_specs=tuple(out_specs), scratch_shapes=scratch, compiler_params=_params(("arbitrary",)),
    )


def _ffn_conv_chunk(up_ref, cw_ref, cb_ref, carry, row, sl):
    x = up_ref[:, sl].astype(F32)
    fills = [carry[15:16, sl], carry[14:15, sl]]
    x1 = _shift_down(x, 1, fills, row)
    x2 = _shift_down(x, 2, fills, row)
    return cb_ref[:, sl] + cw_ref[2:3, sl] * x + cw_ref[1:2, sl] * x1 + cw_ref[0:1, sl] * x2


def ffn_fwd(up, h, conv_w, conv_b, w_down, tm, name, chunk=256, comm=None):
    T, C = up.shape
    F = C // 2
    D = h.shape[1]

    def body(up_ref, h_ref, cw_ref, cb_ref, wd_ref, o_ref, av_ref, carry):
        @pl.when(pl.program_id(0) == 0)
        def _():
            carry[...] = jnp.zeros_like(carry)

        row = lax.broadcasted_iota(jnp.int32, (tm, 1), 0)
        out = h_ref[...]
        for c0 in range(0, F, chunk):
            sa, sv = slice(c0, c0 + chunk), slice(F + c0, F + c0 + chunk)
            a = _ffn_conv_chunk(up_ref, cw_ref, cb_ref, carry, row, sa)
            v = _ffn_conv_chunk(up_ref, cw_ref, cb_ref, carry, row, sv)
            av_ref[:, sa] = a.astype(MXU_DTYPE)
            av_ref[:, sv] = v.astype(MXU_DTYPE)
            out = out + _dot(_gelu(a) * v, wd_ref[sa, :])
        carry[...] = up_ref[tm - 16:tm, :].astype(F32)
        o_ref[...] = out

    full = lambda a: pl.BlockSpec(a.shape, lambda i: (0,) * a.ndim)
    return _hosted(
        body, comm, (up, h, conv_w, conv_b, w_down), name=name,
        out_shape=(jax.ShapeDtypeStruct((T, D), F32), jax.ShapeDtypeStruct((T, C), MXU_DTYPE)),
        grid=(T // tm,),
        in_specs=[pl.BlockSpec((tm, C), lambda i: (i, 0)), pl.BlockSpec((tm, D), lambda i: (i, 0)),
                  full(conv_w), full(conv_b), full(w_down)],
        out_specs=(pl.BlockSpec((tm, D), lambda i: (i, 0)), pl.BlockSpec((tm, C), lambda i: (i, 0))),
        scratch_shapes=[pltpu.VMEM((16, C), F32)],
        compiler_params=_params(("arbitrary",)),
    )


def ffn_bwd_act(dout, av, w_down, tm, name, chunk=512, comm=None):
    T, C = av.shape
    F = C // 2
    D = dout.shape[1]

    def body(do_ref, av_ref, wd_ref, dup_ref, act_ref, dcb_ref):
        @pl.when(pl.program_id(0) == 0)
        def _():
            dcb_ref[...] = jnp.zeros_like(dcb_ref)

        dob = do_ref[...].astype(MXU_DTYPE)
        for c0 in range(0, F, chunk):
            sa, sv = slice(c0, c0 + chunk), slice(F + c0, F + c0 + chunk)
            dact = _dot_nt(dob, wd_ref[sa, :])
            v = av_ref[:, sv].astype(F32)
            ga, dga = _gelu_and_grad(av_ref[:, sa].astype(F32))
            act_ref[:, sa] = (ga * v).astype(MXU_DTYPE)
            da = dact * v * dga
            dv = dact * ga
            dup_ref[:, sa] = da.astype(MXU_DTYPE)
            dup_ref[:, sv] = dv.astype(MXU_DTYPE)
            dcb_ref[:, sa] += _rowsum8(da)
            dcb_ref[:, sv] += _rowsum8(dv)

    full = lambda a: pl.BlockSpec(a.shape, lambda i: (0,) * a.ndim)
    return _hosted(
        body, comm, (dout, av, w_down), name=name,
        out_shape=(jax.ShapeDtypeStruct((T, C), MXU_DTYPE), jax.ShapeDtypeStruct((T, F), MXU_DTYPE),
                   jax.ShapeDtypeStruct((8, C), F32)),
        grid=(T // tm,),
        in_specs=[pl.BlockSpec((tm, D), lambda i: (i, 0)), pl.BlockSpec((tm, C), lambda i: (i, 0)), full(w_down)],
        out_specs=(pl.BlockSpec((tm, C), lambda i: (i, 0)), pl.BlockSpec((tm, F), lambda i: (i, 0)),
                   pl.BlockSpec((8, C), lambda i: (0, 0))),
        scratch_shapes=[],
        compiler_params=_params(("arbitrary",)),
    )


def ffn_bwd(dout, av, up, w_down, conv_w, w_up, h, g, tm, name, chunk=256, comm=None):
    T, C = av.shape
    F = C // 2
    D = dout.shape[1]
    nt = T // tm
    kw = conv_w.shape[0]

    def body(do_ref, av_ref, x_ref, wd_ref, cw_ref, wu_ref, h_ref, g_ref,
             dh_ref, dg_ref, dzp_ref, act_ref, dcb_ref, dcw_ref, carry):
        @pl.when(pl.program_id(0) == 0)
        def _():
            for ref in (dg_ref, dcb_ref, dcw_ref, carry):
                ref[...] = jnp.zeros_like(ref)

        dov = do_ref[...]
        dob = dov.astype(MXU_DTYPE)
        row = lax.broadcasted_iota(jnp.int32, (tm, 1), 0)
        dhn = None
        for c0 in range(0, F, chunk):
            sa, sv = slice(c0, c0 + chunk), slice(F + c0, F + c0 + chunk)
            dact = _dot_nt(dob, wd_ref[sa, :])
            v = av_ref[:, sv].astype(F32)
            ga, dga = _gelu_and_grad(av_ref[:, sa].astype(F32))
            act_ref[:, sa] = (ga * v).astype(MXU_DTYPE)
            for d0, sl in ((dact * v * dga, sa), (dact * ga, sv)):
                dcb_ref[:, sl] += _rowsum8(d0)
                xv = x_ref[:, sl].astype(F32)
                fills = [carry[t:t + 1, sl] for t in range(kw - 1)]
                acc = cw_ref[kw - 1:kw, sl] * d0
                dcw_ref[kw - 1, :, sl] += _rowsum8(d0 * xv)
                for s in range(1, kw):
                    ds = _shift_up(d0, s, fills, row)
                    acc = acc + cw_ref[kw - 1 - s:kw - s, sl] * ds
                    dcw_ref[kw - 1 - s, :, sl] += _rowsum8(ds * xv)
                carry[:, sl] = d0[0:16, :]
                zb = acc.astype(MXU_DTYPE)
                dzp_ref[:, sl] = zb
                part = _dot_nt(zb, wu_ref[:, sl])
                dhn = part if dhn is None else dhn + part
        r, xhat = _rms(h_ref[...])
        dg_ref[...] += _rowsum8(dhn * xhat)
        dh_ref[...] = dov + _rms_bwd(dhn, xhat, r, g_ref[...])

    rt = lambda c: pl.BlockSpec((tm, c), lambda i: (nt - 1 - i, 0))
    full = lambda a: pl.BlockSpec(a.shape, lambda i: (0,) * a.ndim)
    once = lambda a: pl.BlockSpec(a.shape, lambda i: (0,) * a.ndim, pipeline_mode=pl.Buffered(1))
    acc = lambda shape: pl.BlockSpec(shape, lambda i: (0,) * len(shape))
    return _hosted(
        body, comm, (dout, av, up, w_down, conv_w, w_up, h, g), name=name,
        out_shape=(jax.ShapeDtypeStruct((T, D), F32), jax.ShapeDtypeStruct((8, D), F32),
                   jax.ShapeDtypeStruct((T, C), MXU_DTYPE), jax.ShapeDtypeStruct((T, F), MXU_DTYPE),
                   jax.ShapeDtypeStruct((8, C), F32), jax.ShapeDtypeStruct((kw, 8, C), F32)),
        grid=(nt,),
        in_specs=[rt(D), rt(C), rt(C), once(w_down), full(conv_w), once(w_up), rt(D), full(g)],
        out_specs=(rt(D), acc((8, D)), rt(C), rt(F), acc((8, C)), acc((kw, 8, C))),
        scratch_shapes=[pltpu.VMEM((16, C), F32)],
        compiler_params=_params(("arbitrary",)),
    )


def _rg_gates(xr, wa_ref, ba_ref, wx_ref, bx_ref, sp_ref):
    xb = xr.astype(MXU_DTYPE)
    pa, px = [], []
    for hd in range(RG_HEADS):
        sl = slice(hd * RG_BW, (hd + 1) * RG_BW)
        pa.append(jnp.dot(xb[:, sl], wa_ref[hd].astype(MXU_DTYPE), preferred_element_type=F32))
        px.append(jnp.dot(xb[:, sl], wx_ref[hd].astype(MXU_DTYPE), preferred_element_type=F32))
    r = _sigmoid(jnp.concatenate(pa, axis=1) + ba_ref[...])
    ig = _sigmoid(jnp.concatenate(px, axis=1) + bx_ref[...])
    la = -RG_C * r * sp_ref[...]
    a = jnp.exp(la)
    mult = jnp.sqrt(_neg_expm1(2.0 * la))
    return xb, r, ig, a, mult


def _rg_conv(x, fills, cw_ref, cb_ref, row):
    x1 = _shift_down(x, 1, fills, row)
    x2 = _shift_down(x, 2, fills, row)
    x3 = _shift_down(x, 3, fills, row)
    xr = cb_ref[...] + cw_ref[3:4, :] * x + cw_ref[2:3, :] * x1 + cw_ref[1:2, :] * x2 + cw_ref[0:1, :] * x3
    return xr, (x3, x2, x1, x)


def rg_fwd(xg, h, conv_w, conv_b, w_a, b_a, w_x, b_x, sp, w_out, tm, name, comm=None):
    T, D2 = xg.shape
    D = D2 // 2
    nb = tm // 8

    def body(xg_ref, h_ref, cw_ref, cb_ref, wa_ref, ba_ref, wx_ref, bx_ref, sp_ref, wo_ref, o_ref, hs_ref,
             xcarry, hcarry, a_s, b_s):
        @pl.when(pl.program_id(0) == 0)
        def _():
            xcarry[...] = jnp.zeros_like(xcarry)
            hcarry[...] = jnp.zeros_like(hcarry)

        row = lax.broadcasted_iota(jnp.int32, (tm, 1), 0)
        x = xg_ref[:, 0:D]
        fills = [xcarry[7:8, :], xcarry[6:7, :], xcarry[5:6, :]]
        xr, _ = _rg_conv(x, fills, cw_ref, cb_ref, row)
        xcarry[...] = xg_ref[tm - 8:tm, 0:D]
        _, r, ig, a, mult = _rg_gates(xr, wa_ref, ba_ref, wx_ref, bx_ref, sp_ref)
        a_s[...] = a
        b_s[...] = mult * ig * xr
        row8 = lax.broadcasted_iota(jnp.int32, (8, 1), 0)

        def blk(j, c):
            o = pl.multiple_of(j * 8, 8)
            A = a_s[pl.ds(o, 8), :]
            H = b_s[pl.ds(o, 8), :]
            for s in (1, 2, 4):
                m = row8 >= s
                H = H + A * jnp.where(m, pltpu.roll(H, s, 0), 0.0)
                A = A * jnp.where(m, pltpu.roll(A, s, 0), 1.0)
            H = H + A * c
            hs_ref[pl.ds(o, 8), :] = H
            return H[7:8, :]

        c = lax.fori_loop(0, nb, blk, hcarry[0:1, :])
        hcarry[0:1, :] = c
        y = hs_ref[...] * _gelu(xg_ref[:, D:D2])
        o_ref[...] = h_ref[...] + _dot(y, wo_ref[...])

    full = lambda a: pl.BlockSpec(a.shape, lambda i: (0,) * a.ndim)
    args = (xg, h, conv_w, conv_b, w_a, b_a, w_x, b_x, sp, w_out)
    return _hosted(
        body, comm, args, name=name,
        out_shape=(jax.ShapeDtypeStruct((T, D), F32), jax.ShapeDtypeStruct((T, D), F32)),
        grid=(T // tm,),
        in_specs=[pl.BlockSpec((tm, D2), lambda i: (i, 0)), pl.BlockSpec((tm, D), lambda i: (i, 0))]
        + [full(a) for a in args[2:]],
        out_specs=(pl.BlockSpec((tm, D), lambda i: (i, 0)), pl.BlockSpec((tm, D), lambda i: (i, 0))),
        scratch_shapes=[pltpu.VMEM((8, D), F32), pltpu.VMEM((8, D), F32), pltpu.VMEM((tm, D), F32),
                        pltpu.VMEM((tm, D), F32)],
        compiler_params=_params(("arbitrary",)),
    )


def rg_bwd(dout, xg, hs, conv_w, conv_b, w_a, b_a, w_x, b_x, sp, w_out, tm, name, comm=None):
    T, D2 = xg.shape
    D = D2 // 2
    nt = T // tm
    nb = tm // 8
    kw = conv_w.shape[0]

    def body(do_ref, xg_ref, xh_ref, hs_ref, hh_ref, cw_ref, cb_ref, wa_ref, ba_ref, wx_ref, bx_ref, sp_ref, wo_ref,
             dxg_ref, y_ref, dwa_ref, dwx_ref, dba_ref, dbx_ref, dsp_ref, dcb_ref, dcw_ref,
             acarry, lcarry, dcarry, a_s, b_s, l_s):
        i = pl.program_id(0)
        first_tile = i == nt - 1

        @pl.when(i == 0)
        def _():
            for ref in (acarry, lcarry, dcarry, dwa_ref, dwx_ref, dba_ref, dbx_ref, dsp_ref, dcb_ref, dcw_ref):
                ref[...] = jnp.zeros_like(ref)

        row = lax.broadcasted_iota(jnp.int32, (tm, 1), 0)
        keep = jnp.where(first_tile, 0.0, 1.0)
        x = xg_ref[:, 0:D]
        gate = xg_ref[:, D:D2]
        xh = xh_ref[...] * keep
        fills = [xh[7:8, :], xh[6:7, :], xh[5:6, :]]
        xr, taps = _rg_conv(x, fills, cw_ref, cb_ref, row)
        xb, r, ig, a, mult = _rg_gates(xr, wa_ref, ba_ref, wx_ref, bx_ref, sp_ref)
        hs = hs_ref[...]
        hprev = _shift_down(hs, 1, [hh_ref[7:8, :] * keep], row)
        dy = _dot_nt(do_ref[...], wo_ref[...])
        gg, dgg = _gelu_and_grad(gate)
        y_ref[...] = (hs * gg).astype(MXU_DTYPE)
        dxg_ref[:, D:D2] = (dy * hs * dgg).astype(MXU_DTYPE)
        a_s[...] = _shift_up(a, 1, [acarry[0:1, :]], row)
        b_s[...] = dy * gg
        row8 = lax.broadcasted_iota(jnp.int32, (8, 1), 0)

        def blk(jj, c):
            o = pl.multiple_of((nb - 1 - jj) * 8, 8)
            A = a_s[pl.ds(o, 8), :]
            H = b_s[pl.ds(o, 8), :]
            for s in (1, 2, 4):
                m = row8 < 8 - s
                H = H + A * jnp.where(m, pltpu.roll(H, 8 - s, 0), 0.0)
                A = A * jnp.where(m, pltpu.roll(A, 8 - s, 0), 1.0)
            H = H + A * c
            l_s[pl.ds(o, 8), :] = H
            return H[0:1, :]

        c = lax.fori_loop(0, nb, blk, lcarry[0:1, :])
        lcarry[0:1, :] = c
        acarry[0:1, :] = a[0:1, :]
        lam = l_s[...]
        dla = lam * hprev * a - (lam * ig * xr) * (a * a) / mult
        dig = lam * mult * xr
        dxr = lam * mult * ig
        spv = sp_ref[...]
        dsp_ref[...] += _rowsum8(dla * (-RG_C) * r)
        dpa = (dla * (-RG_C) * spv) * r * (1.0 - r)
        dpx = dig * ig * (1.0 - ig)
        dba_ref[...] += _rowsum8(dpa)
        dbx_ref[...] += _rowsum8(dpx)
        dpab = dpa.astype(MXU_DTYPE)
        dpxb = dpx.astype(MXU_DTYPE)
        back = []
        for hd in range(RG_HEADS):
            sl = slice(hd * RG_BW, (hd + 1) * RG_BW)
            dwa_ref[hd] += _dot_tn(xb[:, sl], dpab[:, sl])
            dwx_ref[hd] += _dot_tn(xb[:, sl], dpxb[:, sl])
            back.append(_dot_nt(dpab[:, sl], wa_ref[hd]) + _dot_nt(dpxb[:, sl], wx_ref[hd]))
        dxr = dxr + jnp.concatenate(back, axis=1)
        nfills = [dcarry[0:1, :], dcarry[1:2, :], dcarry[2:3, :]]
        dxp = cw_ref[kw - 1:kw, :] * dxr
        for s in range(1, kw):
            dxp = dxp + cw_ref[kw - 1 - s:kw - s, :] * _shift_up(dxr, s, nfills, row)
        dcarry[...] = dxr[0:8, :]
        dxg_ref[:, 0:D] = dxp.astype(MXU_DTYPE)
        dcb_ref[...] += _rowsum8(dxr)
        for k in range(kw):
            dcw_ref[k] += _rowsum8(dxr * taps[k])

    rt = lambda i: nt - 1 - i
    halo = lambda i: jnp.maximum((nt - 1 - i) * (tm // 8) - 1, 0)
    full = lambda a: pl.BlockSpec(a.shape, lambda i: (0,) * a.ndim)
    params = (conv_w, conv_b, w_a, b_a, w_x, b_x, sp, w_out)
    acc = lambda shape: pl.BlockSpec(shape, lambda i: (0,) * len(shape))
    return _hosted(
        body, comm, (dout, xg, xg, hs, hs, *params), name=name,
        out_shape=(jax.ShapeDtypeStruct((T, D2), MXU_DTYPE), jax.ShapeDtypeStruct((T, D), MXU_DTYPE),
                   jax.ShapeDtypeStruct((RG_HEADS, RG_BW, RG_BW), F32), jax.ShapeDtypeStruct((RG_HEADS, RG_BW, RG_BW), F32),
                   jax.ShapeDtypeStruct((8, D), F32), jax.ShapeDtypeStruct((8, D), F32), jax.ShapeDtypeStruct((8, D), F32),
                   jax.ShapeDtypeStruct((8, D), F32), jax.ShapeDtypeStruct((kw, 8, D), F32)),
        grid=(nt,),
        in_specs=[pl.BlockSpec((tm, D), lambda i: (rt(i), 0)), pl.BlockSpec((tm, D2), lambda i: (rt(i), 0)),
                  pl.BlockSpec((8, D), lambda i: (halo(i), 0)), pl.BlockSpec((tm, D), lambda i: (rt(i), 0)),
                  pl.BlockSpec((8, D), lambda i: (halo(i), 0))] + [full(a) for a in params],
        out_specs=(pl.BlockSpec((tm, D2), lambda i: (rt(i), 0)), pl.BlockSpec((tm, D), lambda i: (rt(i), 0)),
                   acc((RG_HEADS, RG_BW, RG_BW)), acc((RG_HEADS, RG_BW, RG_BW)), acc((8, D)), acc((8, D)), acc((8, D)),
                   acc((8, D)), acc((kw, 8, D))),
        scratch_shapes=[pltpu.VMEM((8, D), F32), pltpu.VMEM((8, D), F32), pltpu.VMEM((8, D), F32),
                        pltpu.VMEM((tm, D), F32), pltpu.VMEM((tm, D), F32), pltpu.VMEM((tm, D), F32)],
        compiler_params=_params(("arbitrary",)),
    )


_SBW = 2 * S5_SB * S5_P
_SBH = S5_SB * S5_P
_SBU = S5_SB * S5_GC


def _regroup(x, seg_len):
    n, c = x.shape
    return jnp.swapaxes(x.reshape(8, seg_len, c), 0, 1).reshape(n, c)


def _ungroup(x, seg_len):
    n, c = x.shape
    return jnp.swapaxes(x.reshape(seg_len, 8, c), 0, 1).reshape(n, c)


def _s5_lanes(sb):
    return (slice(sb * _SBW, sb * _SBW + _SBH), slice(sb * _SBW + _SBH, (sb + 1) * _SBW),
            slice(sb * _SBH, (sb + 1) * _SBH))


def _s5_scan_fwd(S, carry, cin, ap_r, ap_i, aq_r, aq_i, seg_len, sb):
    row8 = lax.broadcasted_iota(jnp.int32, (8, 1), 0)
    lr, li, la = _s5_lanes(sb)
    ar, ai = ap_r[0:1, la], ap_i[0:1, la]
    hr, hi = S[0:8, lr], S[0:8, li]
    for i in range(1, seg_len):
        blk = slice(i * 8, (i + 1) * 8)
        hr, hi = ar * hr - ai * hi + S[blk, lr], ar * hi + ai * hr + S[blk, li]
        S[blk, lr] = hr
        S[blk, li] = hi
    for s, idx in ((1, 0), (2, 1), (4, 3)):
        qr, qi = aq_r[idx:idx + 1, la], aq_i[idx:idx + 1, la]
        m = row8 >= s
        sr = jnp.where(m, pltpu.roll(hr, s, 0), 0.0)
        si = jnp.where(m, pltpu.roll(hi, s, 0), 0.0)
        hr, hi = hr + qr * sr - qi * si, hi + qr * si + qi * sr
    cr, ci = carry[0:1, lr], carry[0:1, li]
    pr, pi = aq_r[:, la], aq_i[:, la]
    hr, hi = hr + pr * cr - pi * ci, hi + pr * ci + pi * cr
    xr = jnp.where(row8 == 0, cr, pltpu.roll(hr, 1, 0))
    xi = jnp.where(row8 == 0, ci, pltpu.roll(hi, 1, 0))
    carry[0:1, lr] = hr[7:8, :]
    carry[0:1, li] = hi[7:8, :]
    if cin is not None:
        cin[:, lr] = xr
        cin[:, li] = xi
    for i in range(seg_len):
        blk = slice(i * 8, (i + 1) * 8)
        pr, pi = ap_r[i:i + 1, la], ap_i[i:i + 1, la]
        S[blk, lr] += pr * xr - pi * xi
        S[blk, li] += pr * xi + pi * xr


def s5_fwd(u, h, wb, wc, ap_r, ap_i, aq_r, aq_i, d, w_glu, w_out, tm, name):
    T, D = u.shape
    nt = T // tm
    seg = tm // 8

    def body(u_ref, h_ref, wb_ref, wc_ref, apr_ref, api_ref, aqr_ref, aqi_ref, d_ref, wg_ref, wo_ref,
             o_ref, yp_ref, gl_ref, st_ref, S, carry):
        @pl.when(pl.program_id(0) == 0)
        def _():
            carry[...] = jnp.zeros_like(carry)

        st_ref[...] = carry[...]
        uv = _regroup(u_ref[...], seg)
        ub = uv.astype(MXU_DTYPE)
        for sb in range(S5_NSB):
            S[:, sb * _SBW:(sb + 1) * _SBW] = jnp.dot(ub[:, sb * _SBU:(sb + 1) * _SBU], wb_ref[sb].astype(MXU_DTYPE),
                                                      preferred_element_type=F32)
        for sb in range(S5_NSB):
            _s5_scan_fwd(S, carry, None, apr_ref, api_ref, aqr_ref, aqi_ref, seg, sb)
        ys = [jnp.dot(S[:, sb * _SBW:(sb + 1) * _SBW].astype(MXU_DTYPE), wc_ref[sb].astype(MXU_DTYPE),
                      preferred_element_type=F32) for sb in range(S5_NSB)]
        yp = jnp.concatenate(ys, axis=1) + d_ref[...] * uv
        yp_ref[...] = _ungroup(yp, seg)
        gl = _dot(_gelu(yp), wg_ref[...])
        gl_ref[...] = _ungroup(gl, seg)
        out = gl[:, 0:D] * _sigmoid(gl[:, D:2 * D])
        o_ref[...] = h_ref[...] + _ungroup(_dot(out, wo_ref[...]), seg)

    full = lambda a: pl.BlockSpec(a.shape, lambda i: (0,) * a.ndim)
    args = (u, h, wb, wc, ap_r, ap_i, aq_r, aq_i, d, w_glu, w_out)
    return _call(
        body, name=name,
        out_shape=(jax.ShapeDtypeStruct((T, D), F32), jax.ShapeDtypeStruct((T, D), F32),
                   jax.ShapeDtypeStruct((T, 2 * D), F32), jax.ShapeDtypeStruct((nt, 8, S5_NS), F32)),
        grid=(nt,),
        in_specs=[pl.BlockSpec((tm, D), lambda i: (i, 0)), pl.BlockSpec((tm, D), lambda i: (i, 0))]
        + [full(a) for a in args[2:]],
        out_specs=(pl.BlockSpec((tm, D), lambda i: (i, 0)), pl.BlockSpec((tm, D), lambda i: (i, 0)),
                   pl.BlockSpec((tm, 2 * D), lambda i: (i, 0)), pl.BlockSpec((None, 8, S5_NS), lambda i: (i, 0, 0))),
        scratch_shapes=[pltpu.VMEM((tm, S5_NS), F32), pltpu.VMEM((8, S5_NS), F32)],
        compiler_params=_params(("arbitrary",)),
    )(*args)


def s5_bwd_glu(dout, gl, ypre, u, w_glu, w_out, tm, name):
    T, D = u.shape

    def body(do_ref, gl_ref, yp_ref, u_ref, wg_ref, wo_ref, dy_ref, oact_ref, dgl_ref, gact_ref, dd_ref):
        @pl.when(pl.program_id(0) == 0)
        def _():
            dd_ref[...] = jnp.zeros_like(dd_ref)

        gl1 = gl_ref[:, 0:D]
        sg = _sigmoid(gl_ref[:, D:2 * D])
        oact_ref[...] = (gl1 * sg).astype(MXU_DTYPE)
        dgo = _dot_nt(do_ref[...], wo_ref[...])
        d1 = (dgo * sg).astype(MXU_DTYPE)
        d2 = (dgo * gl1 * sg * (1.0 - sg)).astype(MXU_DTYPE)
        dgl_ref[:, 0:D] = d1
        dgl_ref[:, D:2 * D] = d2
        dg = _dot_nt(d1, wg_ref[:, 0:D]) + _dot_nt(d2, wg_ref[:, D:2 * D])
        g, gd = _gelu_and_grad(yp_ref[...])
        gact_ref[...] = g.astype(MXU_DTYPE)
        dy = dg * gd
        dy_ref[...] = dy
        dd_ref[...] += _rowsum8(dy * u_ref[...])

    full = lambda a: pl.BlockSpec(a.shape, lambda i: (0,) * a.ndim)
    rs = lambda c: pl.BlockSpec((tm, c), lambda i: (i, 0))
    return _call(
        body, name=name,
        out_shape=(jax.ShapeDtypeStruct((T, D), F32), jax.ShapeDtypeStruct((T, D), MXU_DTYPE),
                   jax.ShapeDtypeStruct((T, 2 * D), MXU_DTYPE), jax.ShapeDtypeStruct((T, D), MXU_DTYPE),
                   jax.ShapeDtypeStruct((8, D), F32)),
        grid=(T // tm,),
        in_specs=[rs(D), rs(2 * D), rs(D), rs(D), full(w_glu), full(w_out)],
        out_specs=(rs(D), rs(D), rs(2 * D), rs(D), pl.BlockSpec((8, D), lambda i: (0, 0))),
        compiler_params=_params(("arbitrary",)),
    )(dout, gl, ypre, u, w_glu, w_out)


def s5_bwd_ssm(dy, u, st, wb, wc, tables, d, tm, name, comm=None):
    T, D = u.shape
    nt = T // tm
    seg = tm // 8
    GP = S5_G * S5_P
    assert len(tables) == 8

    def body(dy_ref, u_ref, st_ref, wb_ref, wc_ref, apr_ref, api_ref, aqr_ref, aqi_ref, aprr_ref, apir_ref,
             aqrr_ref, aqir_ref, d_ref, du_ref, dwb_ref, dwc_ref, dar_ref, dai_ref, S, L, carry, lcarry, cin):
        @pl.when(pl.program_id(0) == 0)
        def _():
            for ref in (lcarry, dwb_ref, dwc_ref, dar_ref, dai_ref):
                ref[...] = jnp.zeros_like(ref)

        uv = _regroup(u_ref[...], seg)
        ub = uv.astype(MXU_DTYPE)
        dyv = _regroup(dy_ref[...], seg)
        dyb = dyv.astype(MXU_DTYPE)
        carry[...] = st_ref[...]
        row8 = lax.broadcasted_iota(jnp.int32, (8, 1), 0)
        dus = []
        for sb in range(S5_NSB):
            ls = slice(sb * _SBW, (sb + 1) * _SBW)
            us = slice(sb * _SBU, (sb + 1) * _SBU)
            S[:, ls] = jnp.dot(ub[:, us], wb_ref[sb].astype(MXU_DTYPE), preferred_element_type=F32)
            L[:, ls] = _dot_nt(dyb[:, us], wc_ref[sb])
            _s5_scan_fwd(S, carry, cin, apr_ref, api_ref, aqr_ref, aqi_ref, seg, sb)
            lr, li, la = _s5_lanes(sb)
            ar, ai = apr_ref[0:1, la], api_ref[0:1, la]
            last = slice((seg - 1) * 8, seg * 8)
            gr, gi = L[last, lr], L[last, li]
            for i in reversed(range(seg - 1)):
                blk = slice(i * 8, (i + 1) * 8)
                gr, gi = ar * gr + ai * gi + L[blk, lr], ar * gi - ai * gr + L[blk, li]
                L[blk, lr] = gr
                L[blk, li] = gi
            for s, idx in ((1, 0), (2, 1), (4, 3)):
                qr, qi = aqr_ref[idx:idx + 1, la], aqi_ref[idx:idx + 1, la]
                m = row8 < 8 - s
                sr = jnp.where(m, pltpu.roll(gr, 8 - s, 0), 0.0)
                si = jnp.where(m, pltpu.roll(gi, 8 - s, 0), 0.0)
                gr, gi = gr + qr * sr + qi * si, gi + qr * si - qi * sr
            cr, ci = lcarry[0:1, lr], lcarry[0:1, li]
            pr, pi = aqrr_ref[:, la], aqir_ref[:, la]
            gr, gi = gr + pr * cr + pi * ci, gi + pr * ci - pi * cr
            xr = jnp.where(row8 == 7, cr, pltpu.roll(gr, 7, 0))
            xi = jnp.where(row8 == 7, ci, pltpu.roll(gi, 7, 0))
            lcarry[0:1, lr] = gr[0:1, :]
            lcarry[0:1, li] = gi[0:1, :]
            acc_r = jnp.zeros((8, _SBH), F32)
            acc_i = jnp.zeros((8, _SBH), F32)
            for i in range(seg):
                blk = slice(i * 8, (i + 1) * 8)
                pr, pi = aprr_ref[i:i + 1, la], apir_ref[i:i + 1, la]
                gr = L[blk, lr] + pr * xr + pi * xi
                gi = L[blk, li] + pr * xi - pi * xr
                L[blk, lr] = gr
                L[blk, li] = gi
                if i == 0:
                    hpr, hpi = cin[:, lr], cin[:, li]
                else:
                    hpr, hpi = S[(i - 1) * 8:i * 8, lr], S[(i - 1) * 8:i * 8, li]
                acc_r = acc_r + gr * hpr + gi * hpi
                acc_i = acc_i + gi * hpr - gr * hpi
            dar_ref[:, la] += acc_r
            dai_ref[:, la] += acc_i
            lb = L[:, ls].astype(MXU_DTYPE)
            dwb_ref[sb] += _dot_tn(lb, ub[:, us])
            dwc_ref[sb] += _dot_tn(S[:, ls].astype(MXU_DTYPE), dyb[:, us])
            dus.append(_dot_nt(lb, wb_ref[sb]))
        du_ref[...] = _ungroup(jnp.concatenate(dus, axis=1) + dyv * d_ref[...], seg).astype(MXU_DTYPE)

    rt = lambda i: nt - 1 - i
    full = lambda a: pl.BlockSpec(a.shape, lambda i: (0,) * a.ndim)
    acc = lambda shape: pl.BlockSpec(shape, lambda i: (0,) * len(shape))
    return _hosted(
        body, comm, (dy, u, st, wb, wc, *tables, d), name=name,
        out_shape=(jax.ShapeDtypeStruct((T, D), MXU_DTYPE), jax.ShapeDtypeStruct((S5_NSB, _SBW, _SBU), F32),
                   jax.ShapeDtypeStruct((S5_NSB, _SBW, _SBU), F32), jax.ShapeDtypeStruct((8, GP), F32),
                   jax.ShapeDtypeStruct((8, GP), F32)),
        grid=(nt,),
        in_specs=[pl.BlockSpec((tm, D), lambda i: (rt(i), 0)), pl.BlockSpec((tm, D), lambda i: (rt(i), 0)),
                  pl.BlockSpec((None, 8, S5_NS), lambda i: (rt(i), 0, 0)), full(wb), full(wc)]
        + [full(t) for t in tables] + [full(d)],
        out_specs=(pl.BlockSpec((tm, D), lambda i: (rt(i), 0)), acc((S5_NSB, _SBW, _SBU)), acc((S5_NSB, _SBW, _SBU)),
                   acc((8, GP)), acc((8, GP))),
        scratch_shapes=[pltpu.VMEM((tm, S5_NS), F32), pltpu.VMEM((tm, S5_NS), F32), pltpu.VMEM((8, S5_NS), F32),
                        pltpu.VMEM((8, S5_NS), F32), pltpu.VMEM((8, S5_NS), F32)],
        compiler_params=_params(("arbitrary",)),
    )


def final_loss(h, g, target, tm, name):
    T, D = h.shape

    def body(h_ref, g_ref, t_ref, dh_ref, se_ref, dg_ref):
        @pl.when(pl.program_id(0) == 0)
        def _():
            se_ref[...] = jnp.zeros_like(se_ref)
            dg_ref[...] = jnp.zeros_like(dg_ref)

        r, xhat = _rms(h_ref[...])
        gv = g_ref[...]
        e = xhat * gv - t_ref[...]
        se_ref[...] += _rowsum8(e * e)
        dy = e * (1.0 / D)
        dg_ref[...] += _rowsum8(dy * xhat)
        dh_ref[...] = _rms_bwd(dy, xhat, r, gv)

    rs = pl.BlockSpec((tm, D), lambda i: (i, 0))
    acc = pl.BlockSpec((8, D), lambda i: (0, 0))
    return _call(
        body, name=name,
        out_shape=(jax.ShapeDtypeStruct((T, D), F32), jax.ShapeDtypeStruct((8, D), F32), jax.ShapeDtypeStruct((8, D), F32)),
        grid=(T // tm,), in_specs=[rs, pl.BlockSpec((1, D), lambda i: (0, 0)), rs], out_specs=(rs, acc, acc),
        compiler_params=_params(("arbitrary",)),
    )(h, g, target)


def _s5_discretize(a_re, a_im, log_dt, b_re, b_im):
    dt = jnp.exp(log_dt)[:, None]
    mag = jnp.exp(a_re * dt)
    abr = mag * jnp.cos(a_im * dt)
    abi = mag * jnp.sin(a_im * dt)
    ur, ui = abr - 1.0, abi
    den = a_re * a_re + a_im * a_im
    wr = (ur * a_re + ui * a_im) / den
    wi = (ui * a_re - ur * a_im) / den
    bbr = wr[..., None] * b_re - wi[..., None] * b_im
    bbi = wr[..., None] * b_im + wi[..., None] * b_re
    return abr, abi, bbr, bbi


def _s5_pack(abr, abi, bbr, bbi, c_re, c_im, seg_len):
    eye = jnp.eye(S5_SB, dtype=F32)
    b = jnp.stack([bbr, bbi], 0).reshape(2, S5_NSB, S5_SB, S5_P, S5_GC)
    wb = jnp.einsum('rsgpc,gh->shcrgp', b, eye).reshape(S5_NSB, _SBU, _SBW)
    c = jnp.stack([c_re, -c_im], 0).reshape(2, S5_NSB, S5_SB, S5_GC, S5_P)
    wc = jnp.einsum('rsgcp,gh->srgphc', c, eye).reshape(S5_NSB, _SBW, _SBU)
    def powers(r, i, n):
        fr, fi, br, bi = r, i, r, i
        m = 1
        while m < n:
            tr, ti = fr[m - 1:m], fi[m - 1:m]
            fr, fi = (jnp.concatenate([fr, fr * tr - fi * ti], 0), jnp.concatenate([fi, fr * ti + fi * tr], 0))
            br, bi = (jnp.concatenate([br * tr - bi * ti, br], 0), jnp.concatenate([br * ti + bi * tr, bi], 0))
            m *= 2
        return fr, fi, br, bi

    assert seg_len & (seg_len - 1) == 0
    ap = powers(abr.reshape(1, -1), abi.reshape(1, -1), seg_len)
    aq = powers(ap[0][seg_len - 1:seg_len], ap[1][seg_len - 1:seg_len], 8)
    return wb.astype(MXU_DTYPE), wc.astype(MXU_DTYPE), ap[0], ap[1], aq[0], aq[1], ap[2], ap[3], aq[2], aq[3]


def _s5_unpack_grads(dwb_t, dwc, dar8, dai8):
    eye = jnp.eye(S5_SB, dtype=F32)
    t = dwb_t.reshape(S5_NSB, 2, S5_SB, S5_P, S5_SB, S5_GC)
    db = jnp.einsum('srgphc,gh->rsgpc', t, eye).reshape(2, S5_G, S5_P, S5_GC)
    t = dwc.reshape(S5_NSB, 2, S5_SB, S5_P, S5_SB, S5_GC)
    dc = jnp.einsum('srgphc,gh->rsgcp', t, eye).reshape(2, S5_G, S5_GC, S5_P)
    return db[0], db[1], dc[0], -dc[1], dar8.sum(0).reshape(S5_G, S5_P), dai8.sum(0).reshape(S5_G, S5_P)


TM = 256
TM_FFN = 512
TM_S5 = 256


def _tn(a, b, name, comm=None):
    T, M, N = a.shape[0], a.shape[1], b.shape[1]
    bt = 4096 if a.dtype.itemsize + b.dtype.itemsize <= 4 else 2048
    return matmul_tn(a, b, min(M, 1024), min(N, 1024), bt if T % bt == 0 else T, name, comm=comm)


def local_step(x, target, W, sched):
    T, D = x.shape
    depth = W['norm_mix_g'].shape[0]
    row = lambda v: v.reshape(1, -1)
    saved = []
    h = x
    s5c = []
    tr = min(512, T)
    for j in range(W['s5_a_re'].shape[0]):
        prm = (W['s5_a_re'][j], W['s5_a_im'][j], W['s5_log_dt'][j], W['s5_b_re'][j], W['s5_b_im'][j])
        disc, disc_vjp = jax.vjp(_s5_discretize, *prm)
        s5c.append((*_s5_pack(*disc, W['s5_c_re'][j], W['s5_c_im'][j], min(TM_S5, T) // 8), disc_vjp))
    sp_all = jax.nn.softplus(-W['rg_lambda'])
    for i in range(depth):
        j = i // 2
        if i % 2 == 0:
            xg, hn = norm_matmul(h, row(W['norm_mix_g'][i]), W['rg_w_in'][j], F32, tr, None, f"rg_in_{i}")
            h1, hs = rg_fwd(xg, h, W['rg_conv_w'][j], row(W['rg_conv_b'][j]), W['rg_w_a'][j].astype(MXU_DTYPE),
                            row(W['rg_b_a'][j]), W['rg_w_x'][j].astype(MXU_DTYPE), row(W['rg_b_x'][j]), row(sp_all[j]),
                            W['rg_w_out'][j], TM, f"rg_fwd_{i}", comm=sched.comm(f"rg_fwd_{i}"))
            mix = (xg, hn, hs)
        else:
            u, hn = norm_matmul(h, row(W['norm_mix_g'][i]), W['s5_w_in'][j], F32, tr, None, f"s5_in_{i}")
            h1, ypre, gl, st = s5_fwd(u, h, *s5c[j][:6], row(W['s5_d'][j]), W['s5_w_glu'][j], W['s5_w_out'][j],
                                      min(TM_S5, T), f"s5_fwd_{i}")
            mix = (u, hn, ypre, gl, st)
        up, hn2 = norm_matmul(h1, row(W['norm_ffn_g'][i]), W['ffn_w_up'][i], MXU_DTYPE, tr, None, f"ffn_up_{i}",
                              comm=sched.comm(f"ffn_up_{i}"))
        h2, av = ffn_fwd(up, h1, W['ffn_conv_w'][i], row(W['ffn_conv_b'][i]), W['ffn_w_down'][i], min(TM_FFN, T), f"ffn_fwd_{i}",
                         comm=sched.comm(f"ffn_fwd_{i}"))
        saved.append((h, mix, h1, hn2, up, av))
        h = h2
    dh, se8, dgf8 = final_loss(h, row(W['norm_final_g']), target, tr, "final_loss")
    G = {k: [None] * len(v) for k, v in W.items() if k != 'norm_final_g'}
    G['norm_final_g'] = dgf8.sum(0)
    for i in reversed(range(depth)):
        j = i // 2
        h0, mix, h1, hn2, up, av = saved[i]
        dh1, dg8, dupp, act, dcb8, dcw8 = ffn_bwd(dh, av, up, W['ffn_w_down'][i], W['ffn_conv_w'][i], W['ffn_w_up'][i],
                                                  h1, row(W['norm_ffn_g'][i]), TM, f"ffn_bwd_{i}", chunk=1024,
                                                  comm=sched.comm(f"ffn_bwd_act_{i}", G))
        G['ffn_w_down'][i] = _tn(act, dh, f"ffn_dwdown_{i}")
        G['ffn_w_up'][i] = _tn(hn2, dupp, f"ffn_dwup_{i}")
        G['ffn_conv_b'][i] = dcb8.sum(0)
        G['ffn_conv_w'][i] = dcw8.sum(1)
        G['norm_ffn_g'][i] = dg8.sum(0)
        if i % 2 == 0:
            xg, hn, hs = mix
            dxg, y, dwa, dwx, dba8, dbx8, dsp8, dcb8, dcw8 = rg_bwd(
                dh1, xg, hs, W['rg_conv_w'][j], row(W['rg_conv_b'][j]), W['rg_w_a'][j].astype(MXU_DTYPE),
                row(W['rg_b_a'][j]), W['rg_w_x'][j].astype(MXU_DTYPE), row(W['rg_b_x'][j]), row(sp_all[j]),
                W['rg_w_out'][j], TM, f"rg_bwd_{i}", comm=sched.comm(f"rg_bwd_{i}", G))
            G['rg_w_out'][j] = _tn(y, dh1, f"rg_dwout_{i}")
            G['rg_w_a'][j], G['rg_w_x'][j] = dwa, dwx
            G['rg_b_a'][j] = dba8.sum(0).reshape(RG_HEADS, RG_BW)
            G['rg_b_x'][j] = dbx8.sum(0).reshape(RG_HEADS, RG_BW)
            G['rg_lambda'][j] = dsp8.sum(0) * (-jax.nn.sigmoid(-W['rg_lambda'][j]))
            G['rg_conv_b'][j] = dcb8.sum(0)
            G['rg_conv_w'][j] = dcw8.sum(1)
            dh, dg8 = dx_norm_bwd(dxg, W['rg_w_in'][j], h0, row(W['norm_mix_g'][i]), dh1, TM, f"rg_bwd_in_{i}",
                                  comm=sched.comm(f"rg_bwd_in_{i}", G))
            G['rg_w_in'][j] = _tn(hn, dxg, f"rg_dwin_{i}", comm=sched.comm(f"rg_dwin_{i}", G))
        else:
            u, hn, ypre, gl, st = mix
            disc_vjp = s5c[j][-1]
            dy, oact, dgl, gact, dd8 = s5_bwd_glu(dh1, gl, ypre, u, W['s5_w_glu'][j], W['s5_w_out'][j], TM,
                                                  f"s5_bwd_glu_{i}")
            G['s5_w_out'][j] = _tn(oact, dh1, f"s5_dwout_{i}")
            G['s5_w_glu'][j] = _tn(gact, dgl, f"s5_dwglu_{i}")
            du, dwb_t, dwc, dar8, dai8 = s5_bwd_ssm(dy, u, st, *s5c[j][:2], s5c[j][2:10], row(W['s5_d'][j]),
                                                    min(TM_S5, T), f"s5_bwd_ssm_{i}",
                                                    comm=sched.comm(f"s5_bwd_ssm_{i}", G))
            dh, dg8 = dx_norm_bwd(du, W['s5_w_in'][j], h0, row(W['norm_mix_g'][i]), dh1, TM, f"s5_bwd_in_{i}")
            G['s5_w_in'][j] = _tn(hn, du, f"s5_dwin_{i}")
            dbbr, dbbi, dcr, dci, dabr, dabi = _s5_unpack_grads(dwb_t, dwc, dar8, dai8)
            da_re, da_im, dlog_dt, db_re, db_im = disc_vjp((dabr, dabi, dbbr, dbbi))
            G['s5_a_re'][j], G['s5_a_im'][j], G['s5_log_dt'][j] = da_re, da_im, dlog_dt
            G['s5_b_re'][j], G['s5_b_im'][j], G['s5_c_re'][j], G['s5_c_im'][j] = db_re, db_im, dcr, dci
            G['s5_d'][j] = dd8.sum(0)
        G['norm_mix_g'][i] = dg8.sum(0)
    G = {k: (v if (k == 'norm_final_g' or k in BIG) else jnp.stack(v, 0)) for k, v in G.items()}
    return se8, dh, G


BIG = {'rg_w_in': 1, 'rg_w_out': 0, 's5_w_in': 0, 's5_w_glu': 1, 's5_w_out': 0, 'ffn_w_up': 1, 'ffn_w_down': 0}
SMALL_SHARDED = ('rg_conv_w', 'ffn_conv_w', 's5_d')
REPLICATED = ('norm_mix_g', 'norm_ffn_g', 'norm_final_g', 'rg_conv_b', 'rg_w_a', 'rg_b_a', 'rg_w_x', 'rg_b_x',
              'rg_lambda', 's5_a_re', 's5_a_im', 's5_log_dt', 's5_b_re', 's5_b_im', 's5_c_re', 's5_c_im', 'ffn_conv_b')
REP_LATE = ('norm_mix_g',)
REP_MAIN = tuple(k for k in REPLICATED if k not in REP_LATE)


def _me():
    x, y, c = lax.axis_index("x"), lax.axis_index("y"), lax.axis_index("c")
    return x, y, c, 4 * x + 2 * y + c


def _win(ref, axis, dev, width):
    idx = [slice(None)] * len(ref.shape)
    idx[axis] = pl.ds(pl.multiple_of(dev * width, width), width)
    return ref.at[tuple(idx)]


def gather_plan(items, deliver):
    n = len(items)

    def tools(ins, outs, sems):
        send_sems, recv_sems, local_sems = sems
        x, y, c, me = _me()
        sib = (x, y, 1 - c)
        chips = [(1 - x, y), (x, 1 - y), (1 - x, 1 - y)]
        num = lambda px, py, pc: 4 * px + 2 * py + pc

        def src_of(a):
            return ins[a] if items[a][1] is None else ins[a].at[items[a][1]]

        def block(a, dev):
            return _win(outs[a], items[a][2], dev, src_of(a).shape[items[a][2]])

        def copy(a, k, dev, to, own=False):
            return pltpu.make_async_remote_copy(
                src_ref=src_of(a) if own else block(a, dev), dst_ref=block(a, dev),
                send_sem=send_sems.at[a, k], recv_sem=recv_sems.at[a, k], device_id=to, device_id_type=MESH)

        mine = lambda: [pltpu.make_async_copy(src_of(a), block(a, me), local_sems.at[a]) for a in range(n)]
        own = lambda: [cp for a in range(n) for cp in
                       [copy(a, 0, me, sib, own=True)] + [copy(a, 1 + j, me, (*chip, c), own=True)
                                                          for j, chip in enumerate(chips)]]
        arrived = lambda j, a: copy(a, 1 + j, num(*chips[j], c), (x, y, c))
        passed = lambda j, a: copy(a, 4 + j, num(*chips[j], c), sib)
        from_sib = lambda: ([copy(a, 0, num(x, y, 1 - c), (x, y, c)) for a in range(n)]
                            + [copy(a, 4 + j, num(*chip, 1 - c), (x, y, c)) for j, chip in enumerate(chips)
                               for a in range(n)])
        return mine, own, arrived, passed, from_sib

    def start(ins, outs, sems):
        mine, own, _, _, _ = tools(ins, outs, sems)
        for cp in mine() + own():
            cp.start()

    def middle(ins, outs, sems):
        _, _, arrived, passed, _ = tools(ins, outs, sems)
        for j in range(3):
            for a in range(n):
                arrived(j, a).wait_recv()
                passed(j, a).start()

    def finish(ins, outs, sems):
        mine, own, _, passed, from_sib = tools(ins, outs, sems)
        for cp in from_sib():
            cp.wait_recv()
        for cp in own() + [passed(j, a) for j in range(3) for a in range(n)]:
            cp.wait_send()
        for cp in mine():
            cp.wait()

    return Comm([it[0] for it in items], [jax.ShapeDtypeStruct(it[3], it[0].dtype) for it in items],
                [pltpu.SemaphoreType.DMA((n, 7)), pltpu.SemaphoreType.DMA((n, 7)), pltpu.SemaphoreType.DMA((n,))],
                start, middle, finish, deliver)


def exchange_plan(items, deliver):
    n = len(items)
    width = [arr.shape[axis] // N_DEV for arr, axis in items]
    shard = [arr.shape[:axis] + (w,) + arr.shape[axis + 1:] for (arr, axis), w in zip(items, width)]

    def tools(ins, outs, sems):
        send_sems, recv_sems, local_sems = sems
        x, y, c, me = _me()
        piece = lambda a, dev: _win(ins[a], items[a][1], dev, width[a])
        mine = lambda: [pltpu.make_async_copy(piece(a, me), outs[a].at[me], local_sems.at[a]) for a in range(n)]

        def remote(sending):
            cps = []
            for k in range(1, N_DEV):
                px, py, pc = (1 - x) if k & 4 else x, (1 - y) if k & 2 else y, (1 - c) if k & 1 else c
                peer = 4 * px + 2 * py + pc
                for a in range(n):
                    src, dst = (piece(a, peer), outs[a].at[me]) if sending else (piece(a, me), outs[a].at[peer])
                    cps.append(pltpu.make_async_remote_copy(
                        src_ref=src, dst_ref=dst, send_sem=send_sems.at[a, k - 1], recv_sem=recv_sems.at[a, k - 1],
                        device_id=(px, py, pc), device_id_type=MESH))
            return cps

        return mine, remote

    def start(ins, outs, sems):
        mine, remote = tools(ins, outs, sems)
        for cp in mine() + remote(True):
            cp.start()

    def middle(ins, outs, sems):
        pass

    def finish(ins, outs, sems):
        mine, remote = tools(ins, outs, sems)
        for cp in remote(False):
            cp.wait_recv()
        for cp in remote(True):
            cp.wait_send()
        for cp in mine():
            cp.wait()

    return Comm([it[0] for it in items], [jax.ShapeDtypeStruct((N_DEV,) + s, it[0].dtype) for it, s in zip(items, shard)],
                [pltpu.SemaphoreType.DMA((n, 7)), pltpu.SemaphoreType.DMA((n, 7)), pltpu.SemaphoreType.DMA((n,))],
                start, middle, finish, deliver)


def adam_update(parts, w, m, v, name):
    R, C = w.shape
    br = next((b for b in (256, 128) if R > b and R % b == 0), R)
    np_ = parts.shape[0]

    def body(p_ref, w_ref, m_ref, v_ref, g_ref, d_ref, nm_ref, nv_ref):
        _adam_body(np_, p_ref, w_ref, m_ref, v_ref, g_ref, d_ref, nm_ref, nv_ref)

    bs = pl.BlockSpec((br, C), lambda i: (i, 0))
    out = jax.ShapeDtypeStruct((R, C), F32)
    return _call(
        body, name=name, out_shape=(out, out, out, out), grid=(R // br,),
        in_specs=[pl.BlockSpec((np_, br, C), lambda i: (0, i, 0)), bs, bs, bs], out_specs=(bs, bs, bs, bs),
        compiler_params=_params(("parallel",)),
    )(parts, w, m, v)


def _adam_body(np_, p_ref, w_ref, m_ref, v_ref, g_ref, d_ref, nm_ref, nv_ref):
    c1 = 1.0 / (1.0 - ADAM_B1 ** ADAM_STEP)
    c2 = 1.0 / (1.0 - ADAM_B2 ** ADAM_STEP)
    g = p_ref[0].astype(F32)
    for p in range(1, np_):
        g = g + p_ref[p].astype(F32)
    nm = ADAM_B1 * m_ref[...] + (1.0 - ADAM_B1) * g
    nv = ADAM_B2 * v_ref[...] + (1.0 - ADAM_B2) * (g * g)
    g_ref[...] = g
    nm_ref[...] = nm
    nv_ref[...] = nv
    d_ref[...] = -ADAM_LR * ((nm * c1) / (jnp.sqrt(nv * c2) + ADAM_EPS) + ADAM_WD * w_ref[...])


def adam_layer(parts, w, m, v, l, prev, name):
    L, R, C = w.shape
    br = next((b for b in (256, 128) if R > b and R % b == 0), R)

    def body(p_ref, w_ref, m_ref, v_ref, *rest):
        _adam_body(N_DEV, p_ref, w_ref, m_ref, v_ref, *rest[-4:])

    bs = pl.BlockSpec((None, br, C), lambda i: (l, i, 0))
    out = jax.ShapeDtypeStruct((L, R, C), F32)
    extra = {} if prev is None else dict(input_output_aliases={4 + q: q for q in range(4)})
    return _call(
        body, name=name, out_shape=(out, out, out, out), grid=(R // br,),
        in_specs=[pl.BlockSpec((N_DEV, br, C), lambda i: (0, i, 0)), bs, bs, bs] + ([] if prev is None else [ANY] * 4),
        out_specs=(bs, bs, bs, bs), compiler_params=_params(("parallel",)), **extra,
    )(parts, w, m, v, *(() if prev is None else prev))


def sum_parts(parts, name):
    n, R, C = parts.shape

    def body(p_ref, o_ref):
        g = p_ref[0]
        for p in range(1, n):
            g = g + p_ref[p]
        o_ref[...] = g

    return _call(body, name=name, out_shape=jax.ShapeDtypeStruct((R, C), parts.dtype),
                 compiler_params=pltpu.CompilerParams(vmem_limit_bytes=VMEM_LIMIT))(parts)


def _pack_rows(arrs):
    rows = []
    for a in arrs:
        f = a.reshape(-1)
        r = -(-f.shape[0] // 1024)
        r8 = -(-r // 8) * 8
        rows.append(jnp.pad(f, (0, r8 * 1024 - f.shape[0])).reshape(r8, 1024))
    packed = jnp.concatenate(rows, 0)
    return jnp.pad(packed, ((0, -packed.shape[0] % 128), (0, 0)))


def _unpack_rows(packed, shapes):
    out, o = [], 0
    for s in shapes:
        nel = math.prod(s)
        r8 = -(-(-(-nel // 1024)) // 8) * 8
        out.append(packed[o:o + r8].reshape(-1)[:nel].reshape(s))
        o += r8
    return out


def kernel(x, norm_mix_g, norm_ffn_g, norm_final_g, rg_w_in, rg_conv_w, rg_conv_b, rg_w_a, rg_b_a, rg_w_x, rg_b_x, rg_lambda, rg_w_out, s5_w_in, s5_a_re, s5_a_im, s5_log_dt, s5_b_re, s5_b_im, s5_c_re, s5_c_im, s5_d, s5_w_glu, s5_w_out, ffn_w_up, ffn_conv_w, ffn_conv_b, ffn_w_down, loss_target, m_norm_mix_g, m_norm_ffn_g, m_norm_final_g, m_rg_w_in, m_rg_conv_w, m_rg_conv_b, m_rg_w_a, m_rg_b_a, m_rg_w_x, m_rg_b_x, m_rg_lambda, m_rg_w_out, m_s5_w_in, m_s5_a_re, m_s5_a_im, m_s5_log_dt, m_s5_b_re, m_s5_b_im, m_s5_c_re, m_s5_c_im, m_s5_d, m_s5_w_glu, m_s5_w_out, m_ffn_w_up, m_ffn_conv_w, m_ffn_conv_b, m_ffn_w_down, v_norm_mix_g, v_norm_ffn_g, v_norm_final_g, v_rg_w_in, v_rg_conv_w, v_rg_conv_b, v_rg_w_a, v_rg_b_a, v_rg_w_x, v_rg_b_x, v_rg_lambda, v_rg_w_out, v_s5_w_in, v_s5_a_re, v_s5_a_im, v_s5_log_dt, v_s5_b_re, v_s5_b_im, v_s5_c_re, v_s5_c_im, v_s5_d, v_s5_w_glu, v_s5_w_out, v_ffn_w_up, v_ffn_conv_w, v_ffn_conv_b, v_ffn_w_down):
    names = ('norm_mix_g', 'norm_ffn_g', 'norm_final_g', 'rg_w_in', 'rg_conv_w', 'rg_conv_b', 'rg_w_a', 'rg_b_a',
             'rg_w_x', 'rg_b_x', 'rg_lambda', 'rg_w_out', 's5_w_in', 's5_a_re', 's5_a_im', 's5_log_dt', 's5_b_re',
             's5_b_im', 's5_c_re', 's5_c_im', 's5_d', 's5_w_glu', 's5_w_out', 'ffn_w_up', 'ffn_conv_w', 'ffn_conv_b',
             'ffn_w_down')
    loc = locals()
    Wl = {k: loc[k] for k in names}
    Ml = {k: loc['m_' + k] for k in names}
    Vl = {k: loc['v_' + k] for k in names}

    depth = norm_mix_g.shape[0]
    mixer_keys = lambda i: ([('rg_w_in', i // 2), ('rg_w_out', i // 2)] if i % 2 == 0 else
                            [('s5_w_in', i // 2), ('s5_w_glu', i // 2), ('s5_w_out', i // 2)])
    ffn_keys = lambda i: [('ffn_w_up', i), ('ffn_w_down', i)]
    shards = {k: Wl[k].astype(BF16) for k in BIG}
    W = {k: Wl[k] for k in REPLICATED}
    W.update({k: [None] * Wl[k].shape[0] for k in BIG})
    parts = {}

    def gather_of(keys, small=False):
        items = []
        for k, l in keys:
            _, r, c = shards[k].shape
            items.append((shards[k], l, BIG[k], (r * N_DEV, c) if BIG[k] == 0 else (r, c * N_DEV)))
        if small:
            items += [(Wl[k], None, Wl[k].ndim - 1, Wl[k].shape[:-1] + (Wl[k].shape[-1] * N_DEV,)) for k in SMALL_SHARDED]

        def deliver(outs):
            for (k, l), arr in zip(keys, outs):
                W[k][l] = arr
            if small:
                W.update(zip(SMALL_SHARDED, outs[len(keys):]))

        return gather_plan(items, deliver)

    def exchange_of(keys, G, extra=()):
        items = [(G[k][l], BIG[k]) for k, l in keys] + [(arr, axis) for _, arr, axis in extra]
        return exchange_plan(items, lambda outs: parts.update(zip(list(keys) + [e[0] for e in extra], outs)))

    class Sched:
        @staticmethod
        def comm(host, G=None):
            kind, _, i = host.rpartition("_")
            i = int(i)
            if host == "rg_fwd_0":
                return gather_of(ffn_keys(0))
            if kind == "ffn_up" and i + 1 < depth:
                return gather_of(ffn_keys(i + 1)[:1])
            if kind == "ffn_fwd" and i + 1 < depth:
                return gather_of(mixer_keys(i + 1) + ffn_keys(i + 1)[1:])
            if kind == "ffn_bwd_act" and i + 1 < depth:
                return exchange_of(mixer_keys(i + 1), G)
            if kind in ("rg_bwd", "s5_bwd_ssm"):
                return exchange_of(ffn_keys(i), G)
            if host == "rg_bwd_in_0":
                stack = lambda k: G[k] if k == 'norm_final_g' else jnp.stack(G[k], 0)
                extra = [(k, stack(k), Wl[k].ndim - 1) for k in SMALL_SHARDED]
                extra.append(('rep_main', _pack_rows([stack(k).astype(F32) for k in REP_MAIN]), 0))
                return exchange_of(mixer_keys(0)[1:], G, extra)
            if host == "rg_dwin_0":
                rsum = sum_parts(parts['rep_main'], "sum_replicated")
                return gather_plan([(rsum, None, 0, (rsum.shape[0] * N_DEV, rsum.shape[1]))],
                                   lambda outs: parts.update(rep_main_full=outs[0]))
            return None

    run_comm(gather_of(mixer_keys(0), small=True), "gather_first")

    se8, gx, G = local_step(x[0], loss_target[0], W, Sched)
    loss = lax.psum(0.5 * jnp.sum(se8) / x.shape[-1], ("x", "y", "c"))

    gp_late = _pack_rows([G[k].astype(F32) for k in REP_LATE])
    run_comm(exchange_of(mixer_keys(0)[:1], G, [('rep_late', gp_late, 0)]), "exchange_last")
    out_g, out_d, out_m, out_v = {}, {}, {}, {}
    for k in BIG:
        res = None
        for l in range(Wl[k].shape[0]):
            res = adam_layer(parts[(k, l)], Wl[k], Ml[k], Vl[k], l, res, f"adam_{k}_{l}")
        out_g[k], out_d[k], out_m[k], out_v[k] = res
    for k in SMALL_SHARDED:
        shp = Wl[k].shape
        r2 = (math.prod(shp[:-1]), shp[-1])
        res = adam_update(parts[k].reshape((N_DEV,) + r2), Wl[k].reshape(r2), Ml[k].reshape(r2), Vl[k].reshape(r2),
                          f"adam_{k}")
        out_g[k], out_d[k], out_m[k], out_v[k] = [t.reshape(shp) for t in res]
    rsum = sum_parts(parts['rep_late'], "sum_replicated_late")
    run_comm(gather_plan([(rsum, None, 0, gp_late.shape)], lambda outs: parts.update(rep_late_full=outs[0])),
             "gather_small_grads")
    for keys, full, tag in ((REP_MAIN, parts['rep_main_full'], "main"), (REP_LATE, parts['rep_late_full'], "late")):
        res = adam_update(full[None], _pack_rows([Wl[k] for k in keys]), _pack_rows([Ml[k] for k in keys]),
                          _pack_rows([Vl[k] for k in keys]), f"adam_replicated_{tag}")
        for dst, packed in zip((out_g, out_d, out_m, out_v), res):
            dst.update(zip(keys, _unpack_rows(packed, [Wl[k].shape for k in keys])))
    return (loss, gx[None], *[out_g[k] for k in names], *[out_d[k] for k in names], *[out_m[k] for k in names],
            *[out_v[k] for k in names])
```

```python
import functools
import math

import jax
import jax.numpy as jnp
from jax import lax
from jax.experimental import pallas as pl
from jax.experimental.pallas import tpu as pltpu

F32 = jnp.float32
BF16 = jnp.bfloat16
MXU_DTYPE = jnp.bfloat16

NORM_EPS = 1e-6
RG_C = 8.0
RG_HEADS = 8
RG_BW = 128
S5_G = 64
S5_GC = 16
S5_P = 64
S5_SB = 8
S5_NSB = S5_G // S5_SB
S5_NS = 2 * S5_G * S5_P
ADAM_LR = 0.001
ADAM_B1 = 0.9
ADAM_B2 = 0.999
ADAM_EPS = 1e-08
ADAM_WD = 0.01
ADAM_STEP = 10
N_DEV = 8
VMEM_LIMIT = 56 * 1024 * 1024


def _call(body, **kw):
    return pl.pallas_call(body, **kw)


def _params(sem, vmem=VMEM_LIMIT):
    return pltpu.CompilerParams(dimension_semantics=sem, vmem_limit_bytes=vmem)


MESH = pl.DeviceIdType.MESH
ANY = pl.BlockSpec(memory_space=pl.ANY)


class Comm:
    def __init__(self, operands, out_shape, scratch, start, middle, finish, deliver):
        self.operands, self.out_shape, self.scratch = list(operands), list(out_shape), list(scratch)
        self.start, self.middle, self.finish, self.deliver = start, middle, finish, deliver


def run_comm(comm, name):
    ci, co = len(comm.operands), len(comm.out_shape)

    def body(*refs):
        parts = (refs[:ci], refs[ci:ci + co], refs[ci + co:])
        comm.start(*parts)
        comm.middle(*parts)
        comm.finish(*parts)

    comm.deliver(_call(body, name=name, out_shape=tuple(comm.out_shape), in_specs=[ANY] * ci,
                       out_specs=tuple([ANY] * co), scratch_shapes=comm.scratch)(*comm.operands))


def _hosted(body, comm, args, *, name, out_shape, grid, in_specs, out_specs, scratch_shapes, compiler_params):
    if comm is None:
        return _call(body, name=name, out_shape=tuple(out_shape), grid=grid, in_specs=in_specs,
                     out_specs=tuple(out_specs), scratch_shapes=scratch_shapes, compiler_params=compiler_params)(*args)
    n_in, n_out, n_sc = len(in_specs), len(out_shape), len(scratch_shapes)
    ci, co = len(comm.operands), len(comm.out_shape)
    nsteps = math.prod(grid)
    mid = (2 * nsteps) // 3

    def wrapped(*refs):
        ins, refs = refs[:n_in], refs[n_in:]
        cins, refs = refs[:ci], refs[ci:]
        outs, refs = refs[:n_out], refs[n_out:]
        couts, refs = refs[:co], refs[co:]
        sc, csc = refs[:n_sc], refs[n_sc:]
        step = pl.program_id(0)
        for d in range(1, len(grid)):
            step = step * grid[d] + pl.program_id(d)

        @pl.when(step == 0)
        def _():
            comm.start(cins, couts, csc)

        body(*ins, *outs, *sc)

        @pl.when(step == mid)
        def _():
            comm.middle(cins, couts, csc)

        @pl.when(step == nsteps - 1)
        def _():
            comm.finish(cins, couts, csc)

    res = _call(wrapped, name=name, out_shape=(*out_shape, *comm.out_shape), grid=grid,
                in_specs=[*in_specs, *[ANY] * ci], out_specs=(*out_specs, *[ANY] * co),
                scratch_shapes=[*scratch_shapes, *comm.scratch],
                compiler_params=_params(("arbitrary",) * len(grid)))(*args, *comm.operands)
    comm.deliver(res[n_out:])
    return res[:n_out]


_GELU_C = 0.7978845608028654
_GELU_A = 0.044715


def _gelu(x):
    return 0.5 * x * (1.0 + jnp.tanh(_GELU_C * (x + _GELU_A * x * x * x)))


def _gelu_and_grad(x):
    x2 = x * x
    t = jnp.tanh(_GELU_C * (x + _GELU_A * x2 * x))
    g = 0.5 * x * (1.0 + t)
    dg = 0.5 * (1.0 + t) + 0.5 * x * (1.0 - t * t) * _GELU_C * (1.0 + 3.0 * _GELU_A * x2)
    return g, dg


def _sigmoid(x):
    return 1.0 / (1.0 + jnp.exp(-x))


def _neg_expm1(x):
    series = -x * (1.0 + x * (0.5 + x * (1.0 / 6.0 + x * (1.0 / 24.0 + x * (1.0 / 120.0 + x * (1.0 / 720.0))))))
    return jnp.where(x > -0.1, series, 1.0 - jnp.exp(x))


def _rowsum8(x):
    r, c = x.shape
    return x.reshape(r // 8, 8, c).sum(axis=0)


def _dot(a, b):
    return jnp.dot(a.astype(MXU_DTYPE), b.astype(MXU_DTYPE), preferred_element_type=F32)


def _dot_nt(a, b):
    return lax.dot_general(a.astype(MXU_DTYPE), b.astype(MXU_DTYPE), (((1,), (1,)), ((), ())),
                           preferred_element_type=F32)


def _dot_tn(a, b):
    return lax.dot_general(a.astype(MXU_DTYPE), b.astype(MXU_DTYPE), (((0,), (0,)), ((), ())),
                           preferred_element_type=F32)


def _shift_down(x, s, fills, row):
    y = pltpu.roll(x, s, 0)
    for t in range(s):
        y = jnp.where(row == t, fills[s - 1 - t], y)
    return y


def _shift_up(x, s, fills, row):
    n = x.shape[0]
    y = pltpu.roll(x, n - s, 0)
    for t in range(s):
        y = jnp.where(row == n - s + t, fills[t], y)
    return y


def _rms(x):
    r = lax.rsqrt(jnp.mean(x * x, axis=-1, keepdims=True) + NORM_EPS)
    return r, x * r


def _rms_bwd(dhn, xhat, r, g):
    dz = dhn * g
    return r * (dz - xhat * jnp.mean(dz * xhat, axis=-1, keepdims=True))


def norm_matmul(h, g, w, out_dtype, tm, tn, name, comm=None):
    T, D = h.shape
    N = w.shape[1]
    tn = N if tn is None else tn

    def body(h_ref, g_ref, w_ref, o_ref, hn_ref, hn_s):
        @pl.when(pl.program_id(1) == 0)
        def _():
            _, xhat = _rms(h_ref[...])
            v = (xhat * g_ref[...]).astype(MXU_DTYPE)
            hn_s[...] = v
            hn_ref[...] = v

        o_ref[...] = jnp.dot(hn_s[...], w_ref[...].astype(MXU_DTYPE), preferred_element_type=F32).astype(o_ref.dtype)

    return _hosted(
        body, comm, (h, g, w), name=name,
        out_shape=(jax.ShapeDtypeStruct((T, N), out_dtype), jax.ShapeDtypeStruct((T, D), MXU_DTYPE)),
        grid=(T // tm, N // tn),
        in_specs=[pl.BlockSpec((tm, D), lambda i, j: (i, 0)), pl.BlockSpec((1, D), lambda i, j: (0, 0)),
                  pl.BlockSpec((D, tn), lambda i, j: (0, j))],
        out_specs=(pl.BlockSpec((tm, tn), lambda i, j: (i, j)), pl.BlockSpec((tm, D), lambda i, j: (i, 0))),
        scratch_shapes=[pltpu.VMEM((tm, D), MXU_DTYPE)],
        compiler_params=_params(("parallel", "arbitrary")),
    )


def matmul_tn(a, b, bm, bn, bt, name, out_dtype=BF16, comm=None):
    T, M = a.shape
    N = b.shape[1]
    nk = T // bt

    def body(a_ref, b_ref, o_ref, acc):
        k = pl.program_id(2)

        @pl.when(k == 0)
        def _():
            acc[...] = jnp.zeros_like(acc)

        acc[...] += _dot_tn(a_ref[...], b_ref[...])

        @pl.when(k == nk - 1)
        def _():
            o_ref[...] = acc[...].astype(o_ref.dtype)

    return _hosted(
        body, comm, (a, b), name=name,
        out_shape=(jax.ShapeDtypeStruct((M, N), out_dtype),),
        grid=(M // bm, N // bn, nk),
        in_specs=[pl.BlockSpec((bt, bm), lambda i, j, k: (k, i)), pl.BlockSpec((bt, bn), lambda i, j, k: (k, j))],
        out_specs=(pl.BlockSpec((bm, bn), lambda i, j, k: (i, j)),),
        scratch_shapes=[pltpu.VMEM((bm, bn), F32)],
        compiler_params=_params(("parallel", "parallel", "arbitrary")),
    )[0]


def dx_norm_bwd(dz, w, h, g, dres, tm, name, conv_w=None, conv_x=None, chunk=256, comm=None):
    T, N = dz.shape
    D = w.shape[0]
    nt = T // tm
    has_conv = conv_w is not None
    kw = conv_w.shape[0] if has_conv else 0

    def body(*refs):
        if has_conv:
            dz_ref, cw_ref, x_ref, w_ref, h_ref, g_ref, dres_ref, dh_ref, dg_ref, dzp_ref, dcw_ref, carry = refs
        else:
            dz_ref, w_ref, h_ref, g_ref, dres_ref, dh_ref, dg_ref = refs
        i = pl.program_id(0)

        @pl.when(i == 0)
        def _():
            dg_ref[...] = jnp.zeros_like(dg_ref)
            if has_conv:
                carry[...] = jnp.zeros_like(carry)
                dcw_ref[...] = jnp.zeros_like(dcw_ref)

        if has_conv:
            row = lax.broadcasted_iota(jnp.int32, (tm, 1), 0)
            for c0 in range(0, N, chunk):
                sl = slice(c0, c0 + chunk)
                d0 = dz_ref[:, sl].astype(F32)
                xv = x_ref[:, sl].astype(F32)
                fills = [carry[t:t + 1, sl] for t in range(kw - 1)]
                acc = cw_ref[kw - 1:kw, sl] * d0
                dcw_ref[kw - 1, :, sl] += _rowsum8(d0 * xv)
                for s in range(1, kw):
                    ds = _shift_up(d0, s, fills, row)
                    acc = acc + cw_ref[kw - 1 - s:kw - s, sl] * ds
                    dcw_ref[kw - 1 - s, :, sl] += _rowsum8(ds * xv)
                zb = acc.astype(MXU_DTYPE)
                dzp_ref[:, sl] = zb
                part = _dot_nt(zb, w_ref[:, sl])
                dhn = part if c0 == 0 else dhn + part
            carry[...] = dz_ref[0:16, :].astype(F32)
        else:
            dhn = _dot_nt(dz_ref[...], w_ref[...])
        r, xhat = _rms(h_ref[...])
        dg_ref[...] += _rowsum8(dhn * xhat)
        dh_ref[...] = dres_ref[...] + _rms_bwd(dhn, xhat, r, g_ref[...])

    if has_conv:
        ti = lambda i: nt - 1 - i
    else:
        ti = lambda i: i
    row_spec = lambda c: pl.BlockSpec((tm, c), lambda i: (ti(i), 0))
    full = lambda a: pl.BlockSpec(a.shape, lambda i: (0,) * a.ndim)
    in_specs = [row_spec(N)] + ([full(conv_w), row_spec(N)] if has_conv else []) + [full(w), row_spec(D), full(g), row_spec(D)]
    out_shape = [jax.ShapeDtypeStruct((T, D), F32), jax.ShapeDtypeStruct((8, D), F32)]
    out_specs = [row_spec(D), pl.BlockSpec((8, D), lambda i: (0, 0))]
    scratch = []
    if has_conv:
        out_shape += [jax.ShapeDtypeStruct((T, N), MXU_DTYPE), jax.ShapeDtypeStruct((kw, 8, N), F32)]
        out_specs += [row_spec(N), pl.BlockSpec((kw, 8, N), lambda i: (0, 0, 0))]
        scratch = [pltpu.VMEM((16, N), F32)]
    args = [dz] + ([conv_w, conv_x] if has_conv else []) + [w, h, g, dres]
    return _hosted(
        body, comm, args, name=name, out_shape=tuple(out_shape), grid=(nt,), in_specs=in_specs,
        out_specs=tuple(out_specs), scratch_shapes=scratch, compiler_params=_params(("arbitrary",)),
    )


def _ffn_conv_chunk(up_ref, cw_ref, cb_ref, carry, row, sl):
    x = up_ref[:, sl].astype(F32)
    fills = [carry[15:16, sl], carry[14:15, sl]]
    x1 = _shift_down(x, 1, fills, row)
    x2 = _shift_down(x, 2, fills, row)
    return cb_ref[:, sl] + cw_ref[2:3, sl] * x + cw_ref[1:2, sl] * x1 + cw_ref[0:1, sl] * x2


def ffn_block_fwd(h, g, w_up, conv_w, conv_b, w_down, tm, name, chunk=1024, comm=None):
    T, D = h.shape
    C = w_up.shape[1]
    F = C // 2

    def body(h_ref, g_ref, wu_ref, cw_ref, cb_ref, wd_ref, o_ref, hn_ref, up_ref, av_ref, carry):
        @pl.when(pl.program_id(0) == 0)
        def _():
            carry[...] = jnp.zeros_like(carry)

        hv = h_ref[...]
        _, xhat = _rms(hv)
        hn = (xhat * g_ref[...]).astype(MXU_DTYPE)
        hn_ref[...] = hn
        row = lax.broadcasted_iota(jnp.int32, (tm, 1), 0)

        def conv(sl):
            xb = jnp.dot(hn, wu_ref[:, sl].astype(MXU_DTYPE), preferred_element_type=F32).astype(MXU_DTYPE)
            up_ref[:, sl] = xb
            x = xb.astype(F32)
            fills = [carry[15:16, sl], carry[14:15, sl]]
            x1 = _shift_down(x, 1, fills, row)
            x2 = _shift_down(x, 2, fills, row)
            carry[:, sl] = x[tm - 16:tm, :]
            return cb_ref[:, sl] + cw_ref[2:3, sl] * x + cw_ref[1:2, sl] * x1 + cw_ref[0:1, sl] * x2

        out = hv
        for c0 in range(0, F, chunk):
            sa, sv = slice(c0, c0 + chunk), slice(F + c0, F + c0 + chunk)
            a = conv(sa)
            v = conv(sv)
            av_ref[:, sa] = a.astype(MXU_DTYPE)
            av_ref[:, sv] = v.astype(MXU_DTYPE)
            out = out + _dot(_gelu(a) * v, wd_ref[sa, :])
        o_ref[...] = out

    full = lambda a: pl.BlockSpec(a.shape, lambda i: (0,) * a.ndim)
    once = lambda a: pl.BlockSpec(a.shape, lambda i: (0,) * a.ndim, pipeline_mode=pl.Buffered(1))
    rs = lambda c: pl.BlockSpec((tm, c), lambda i: (i, 0))
    return _hosted(
        body, comm, (h, g, w_up, conv_w, conv_b, w_down), name=name,
        out_shape=(jax.ShapeDtypeStruct((T, D), F32), jax.ShapeDtypeStruct((T, D), MXU_DTYPE),
                   jax.ShapeDtypeStruct((T, C), MXU_DTYPE), jax.ShapeDtypeStruct((T, C), MXU_DTYPE)),
        grid=(T // tm,),
        in_specs=[rs(D), full(g), once(w_up), full(conv_w), full(conv_b), once(w_down)],
        out_specs=(rs(D), rs(D), rs(C), rs(C)),
        scratch_shapes=[pltpu.VMEM((16, C), F32)],
        compiler_params=_params(("arbitrary",)),
    )


def ffn_fwd(up, h, conv_w, conv_b, w_down, tm, name, chunk=256, comm=None):
    T, C = up.shape
    F = C // 2
    D = h.shape[1]

    def body(up_ref, h_ref, cw_ref, cb_ref, wd_ref, o_ref, av_ref, carry):
        @pl.when(pl.program_id(0) == 0)
        def _():
            carry[...] = jnp.zeros_like(carry)

        row = lax.broadcasted_iota(jnp.int32, (tm, 1), 0)
        out = h_ref[...]
        for c0 in range(0, F, chunk):
            sa, sv = slice(c0, c0 + chunk), slice(F + c0, F + c0 + chunk)
            a = _ffn_conv_chunk(up_ref, cw_ref, cb_ref, carry, row, sa)
            v = _ffn_conv_chunk(up_ref, cw_ref, cb_ref, carry, row, sv)
            av_ref[:, sa] = a.astype(MXU_DTYPE)
            av_ref[:, sv] = v.astype(MXU_DTYPE)
            out = out + _dot(_gelu(a) * v, wd_ref[sa, :])
        carry[...] = up_ref[tm - 16:tm, :].astype(F32)
        o_ref[...] = out

    full = lambda a: pl.BlockSpec(a.shape, lambda i: (0,) * a.ndim)
    return _hosted(
        body, comm, (up, h, conv_w, conv_b, w_down), name=name,
        out_shape=(jax.ShapeDtypeStruct((T, D), F32), jax.ShapeDtypeStruct((T, C), MXU_DTYPE)),
        grid=(T // tm,),
        in_specs=[pl.BlockSpec((tm, C), lambda i: (i, 0)), pl.BlockSpec((tm, D), lambda i: (i, 0)),
                  full(conv_w), full(conv_b), full(w_down)],
        out_specs=(pl.BlockSpec((tm, D), lambda i: (i, 0)), pl.BlockSpec((tm, C), lambda i: (i, 0))),
        scratch_shapes=[pltpu.VMEM((16, C), F32)],
        compiler_params=_params(("arbitrary",)),
    )


def ffn_bwd_act(dout, av, w_down, tm, name, chunk=512, comm=None):
    T, C = av.shape
    F = C // 2
    D = dout.shape[1]

    def body(do_ref, av_ref, wd_ref, dup_ref, act_ref, dcb_ref):
        @pl.when(pl.program_id(0) == 0)
        def _():
            dcb_ref[...] = jnp.zeros_like(dcb_ref)

        dob = do_ref[...].astype(MXU_DTYPE)
        for c0 in range(0, F, chunk):
            sa, sv = slice(c0, c0 + chunk), slice(F + c0, F + c0 + chunk)
            dact = _dot_nt(dob, wd_ref[sa, :])
            v = av_ref[:, sv].astype(F32)
            ga, dga = _gelu_and_grad(av_ref[:, sa].astype(F32))
            act_ref[:, sa] = (ga * v).astype(MXU_DTYPE)
            da = dact * v * dga
            dv = dact * ga
            dup_ref[:, sa] = da.astype(MXU_DTYPE)
            dup_ref[:, sv] = dv.astype(MXU_DTYPE)
            dcb_ref[:, sa] += _rowsum8(da)
            dcb_ref[:, sv] += _rowsum8(dv)

    full = lambda a: pl.BlockSpec(a.shape, lambda i: (0,) * a.ndim)
    return _hosted(
        body, comm, (dout, av, w_down), name=name,
        out_shape=(jax.ShapeDtypeStruct((T, C), MXU_DTYPE), jax.ShapeDtypeStruct((T, F), MXU_DTYPE),
                   jax.ShapeDtypeStruct((8, C), F32)),
        grid=(T // tm,),
        in_specs=[pl.BlockSpec((tm, D), lambda i: (i, 0)), pl.BlockSpec((tm, C), lambda i: (i, 0)), full(w_down)],
        out_specs=(pl.BlockSpec((tm, C), lambda i: (i, 0)), pl.BlockSpec((tm, F), lambda i: (i, 0)),
                   pl.BlockSpec((8, C), lambda i: (0, 0))),
        scratch_shapes=[],
        compiler_params=_params(("arbitrary",)),
    )


def ffn_bwd(dout, av, up, w_down, conv_w, w_up, h, g, tm, name, chunk=256, comm=None):
    T, C = av.shape
    F = C // 2
    D = dout.shape[1]
    nt = T // tm
    kw = conv_w.shape[0]

    def body(do_ref, av_ref, x_ref, wd_ref, cw_ref, wu_ref, h_ref, g_ref,
             dh_ref, dg_ref, dzp_ref, act_ref, dcb_ref, dcw_ref, carry):
        @pl.when(pl.program_id(0) == 0)
        def _():
            for ref in (dg_ref, dcb_ref, dcw_ref, carry):
                ref[...] = jnp.zeros_like(ref)

        dov = do_ref[...]
        dob = dov.astype(MXU_DTYPE)
        row = lax.broadcasted_iota(jnp.int32, (tm, 1), 0)
        dhn = None
        for c0 in range(0, F, chunk):
            sa, sv = slice(c0, c0 + chunk), slice(F + c0, F + c0 + chunk)
            dact = _dot_nt(dob, wd_ref[sa, :])
            v = av_ref[:, sv].astype(F32)
            ga, dga = _gelu_and_grad(av_ref[:, sa].astype(F32))
            act_ref[:, sa] = (ga * v).astype(MXU_DTYPE)
            for d0, sl in ((dact * v * dga, sa), (dact * ga, sv)):
                dcb_ref[:, sl] += _rowsum8(d0)
                xv = x_ref[:, sl].astype(F32)
                fills = [carry[t:t + 1, sl] for t in range(kw - 1)]
                acc = cw_ref[kw - 1:kw, sl] * d0
                dcw_ref[kw - 1, :, sl] += _rowsum8(d0 * xv)
                for s in range(1, kw):
                    ds = _shift_up(d0, s, fills, row)
                    acc = acc + cw_ref[kw - 1 - s:kw - s, sl] * ds
                    dcw_ref[kw - 1 - s, :, sl] += _rowsum8(ds * xv)
                carry[:, sl] = d0[0:16, :]
                zb = acc.astype(MXU_DTYPE)
                dzp_ref[:, sl] = zb
                part = _dot_nt(zb, wu_ref[:, sl])
                dhn = part if dhn is None else dhn + part
        r, xhat = _rms(h_ref[...])
        dg_ref[...] += _rowsum8(dhn * xhat)
        dh_ref[...] = dov + _rms_bwd(dhn, xhat, r, g_ref[...])

    rt = lambda c: pl.BlockSpec((tm, c), lambda i: (nt - 1 - i, 0))
    full = lambda a: pl.BlockSpec(a.shape, lambda i: (0,) * a.ndim)
    once = lambda a: pl.BlockSpec(a.shape, lambda i: (0,) * a.ndim, pipeline_mode=pl.Buffered(1))
    acc = lambda shape: pl.BlockSpec(shape, lambda i: (0,) * len(shape))
    return _hosted(
        body, comm, (dout, av, up, w_down, conv_w, w_up, h, g), name=name,
        out_shape=(jax.ShapeDtypeStruct((T, D), F32), jax.ShapeDtypeStruct((8, D), F32),
                   jax.ShapeDtypeStruct((T, C), MXU_DTYPE), jax.ShapeDtypeStruct((T, F), MXU_DTYPE),
                   jax.ShapeDtypeStruct((8, C), F32), jax.ShapeDtypeStruct((kw, 8, C), F32)),
        grid=(nt,),
        in_specs=[rt(D), rt(C), rt(C), once(w_down), full(conv_w), once(w_up), rt(D), full(g)],
        out_specs=(rt(D), acc((8, D)), rt(C), rt(F), acc((8, C)), acc((kw, 8, C))),
        scratch_shapes=[pltpu.VMEM((16, C), F32)],
        compiler_params=_params(("arbitrary",)),
    )


def _rg_gates(xr, wa_ref, ba_ref, wx_ref, bx_ref, sp_ref):
    xb = xr.astype(MXU_DTYPE)
    pa, px = [], []
    for hd in range(RG_HEADS):
        sl = slice(hd * RG_BW, (hd + 1) * RG_BW)
        pa.append(jnp.dot(xb[:, sl], wa_ref[hd].astype(MXU_DTYPE), preferred_element_type=F32))
        px.append(jnp.dot(xb[:, sl], wx_ref[hd].astype(MXU_DTYPE), preferred_element_type=F32))
    r = _sigmoid(jnp.concatenate(pa, axis=1) + ba_ref[...])
    ig = _sigmoid(jnp.concatenate(px, axis=1) + bx_ref[...])
    la = -RG_C * r * sp_ref[...]
    a = jnp.exp(la)
    mult = jnp.sqrt(_neg_expm1(2.0 * la))
    return xb, r, ig, a, mult


def _rg_conv(x, fills, cw_ref, cb_ref, row):
    x1 = _shift_down(x, 1, fills, row)
    x2 = _shift_down(x, 2, fills, row)
    x3 = _shift_down(x, 3, fills, row)
    xr = cb_ref[...] + cw_ref[3:4, :] * x + cw_ref[2:3, :] * x1 + cw_ref[1:2, :] * x2 + cw_ref[0:1, :] * x3
    return xr, (x3, x2, x1, x)


def rg_fwd(xg, h, conv_w, conv_b, w_a, b_a, w_x, b_x, sp, w_out, tm, name, comm=None):
    T, D2 = xg.shape
    D = D2 // 2
    nb = tm // 8

    def body(xg_ref, h_ref, cw_ref, cb_ref, wa_ref, ba_ref, wx_ref, bx_ref, sp_ref, wo_ref, o_ref, hs_ref,
             xcarry, hcarry, a_s, b_s):
        @pl.when(pl.program_id(0) == 0)
        def _():
            xcarry[...] = jnp.zeros_like(xcarry)
            hcarry[...] = jnp.zeros_like(hcarry)

        row = lax.broadcasted_iota(jnp.int32, (tm, 1), 0)
        x = xg_ref[:, 0:D]
        fills = [xcarry[7:8, :], xcarry[6:7, :], xcarry[5:6, :]]
        xr, _ = _rg_conv(x, fills, cw_ref, cb_ref, row)
        xcarry[...] = xg_ref[tm - 8:tm, 0:D]
        _, r, ig, a, mult = _rg_gates(xr, wa_ref, ba_ref, wx_ref, bx_ref, sp_ref)
        a_s[...] = a
        b_s[...] = mult * ig * xr
        row8 = lax.broadcasted_iota(jnp.int32, (8, 1), 0)

        def blk(j, c):
            o = pl.multiple_of(j * 8, 8)
            A = a_s[pl.ds(o, 8), :]
            H = b_s[pl.ds(o, 8), :]
            for s in (1, 2, 4):
                m = row8 >= s
                H = H + A * jnp.where(m, pltpu.roll(H, s, 0), 0.0)
                A = A * jnp.where(m, pltpu.roll(A, s, 0), 1.0)
            H = H + A * c
            hs_ref[pl.ds(o, 8), :] = H
            return H[7:8, :]

        c = lax.fori_loop(0, nb, blk, hcarry[0:1, :])
        hcarry[0:1, :] = c
        y = hs_ref[...] * _gelu(xg_ref[:, D:D2])
        o_ref[...] = h_ref[...] + _dot(y, wo_ref[...])

    full = lambda a: pl.BlockSpec(a.shape, lambda i: (0,) * a.ndim)
    args = (xg, h, conv_w, conv_b, w_a, b_a, w_x, b_x, sp, w_out)
    return _hosted(
        body, comm, args, name=name,
        out_shape=(jax.ShapeDtypeStruct((T, D), F32), jax.ShapeDtypeStruct((T, D), F32)),
        grid=(T // tm,),
        in_specs=[pl.BlockSpec((tm, D2), lambda i: (i, 0)), pl.BlockSpec((tm, D), lambda i: (i, 0))]
        + [full(a) for a in args[2:]],
        out_specs=(pl.BlockSpec((tm, D), lambda i: (i, 0)), pl.BlockSpec((tm, D), lambda i: (i, 0))),
        scratch_shapes=[pltpu.VMEM((8, D), F32), pltpu.VMEM((8, D), F32), pltpu.VMEM((tm, D), F32),
                        pltpu.VMEM((tm, D), F32)],
        compiler_params=_params(("arbitrary",)),
    )


def rg_bwd(dout, xg, hs, conv_w, conv_b, w_a, b_a, w_x, b_x, sp, w_out, tm, name, comm=None):
    T, D2 = xg.shape
    D = D2 // 2
    nt = T // tm
    nb = tm // 8
    kw = conv_w.shape[0]

    def body(do_ref, xg_ref, xh_ref, hs_ref, hh_ref, cw_ref, cb_ref, wa_ref, ba_ref, wx_ref, bx_ref, sp_ref, wo_ref,
             dxg_ref, y_ref, dwa_ref, dwx_ref, dba_ref, dbx_ref, dsp_ref, dcb_ref, dcw_ref,
             acarry, lcarry, dcarry, a_s, b_s, l_s):
        i = pl.program_id(0)
        first_tile = i == nt - 1

        @pl.when(i == 0)
        def _():
            for ref in (acarry, lcarry, dcarry, dwa_ref, dwx_ref, dba_ref, dbx_ref, dsp_ref, dcb_ref, dcw_ref):
                ref[...] = jnp.zeros_like(ref)

        row = lax.broadcasted_iota(jnp.int32, (tm, 1), 0)
        keep = jnp.where(first_tile, 0.0, 1.0)
        x = xg_ref[:, 0:D]
        gate = xg_ref[:, D:D2]
        xh = xh_ref[...] * keep
        fills = [xh[7:8, :], xh[6:7, :], xh[5:6, :]]
        xr, taps = _rg_conv(x, fills, cw_ref, cb_ref, row)
        xb, r, ig, a, mult = _rg_gates(xr, wa_ref, ba_ref, wx_ref, bx_ref, sp_ref)
        hs = hs_ref[...]
        hprev = _shift_down(hs, 1, [hh_ref[7:8, :] * keep], row)
        dy = _dot_nt(do_ref[...], wo_ref[...])
        gg, dgg = _gelu_and_grad(gate)
        y_ref[...] = (hs * gg).astype(MXU_DTYPE)
        dxg_ref[:, D:D2] = (dy * hs * dgg).astype(MXU_DTYPE)
        a_s[...] = _shift_up(a, 1, [acarry[0:1, :]], row)
        b_s[...] = dy * gg
        row8 = lax.broadcasted_iota(jnp.int32, (8, 1), 0)

        def blk(jj, c):
            o = pl.multiple_of((nb - 1 - jj) * 8, 8)
            A = a_s[pl.ds(o, 8), :]
            H = b_s[pl.ds(o, 8), :]
            for s in (1, 2, 4):
                m = row8 < 8 - s
                H = H + A * jnp.where(m, pltpu.roll(H, 8 - s, 0), 0.0)
                A = A * jnp.where(m, pltpu.roll(A, 8 - s, 0), 1.0)
            H = H + A * c
            l_s[pl.ds(o, 8), :] = H
            return H[0:1, :]

        c = lax.fori_loop(0, nb, blk, lcarry[0:1, :])
        lcarry[0:1, :] = c
        acarry[0:1, :] = a[0:1, :]
        lam = l_s[...]
        dla = lam * hprev * a - (lam * ig * xr) * (a * a) / mult
        dig = lam * mult * xr
        dxr = lam * mult * ig
        spv = sp_ref[...]
        dsp_ref[...] += _rowsum8(dla * (-RG_C) * r)
        dpa = (dla * (-RG_C) * spv) * r * (1.0 - r)
        dpx = dig * ig * (1.0 - ig)
        dba_ref[...] += _rowsum8(dpa)
        dbx_ref[...] += _rowsum8(dpx)
        dpab = dpa.astype(MXU_DTYPE)
        dpxb = dpx.astype(MXU_DTYPE)
        back = []
        for hd in range(RG_HEADS):
            sl = slice(hd * RG_BW, (hd + 1) * RG_BW)
            dwa_ref[hd] += _dot_tn(xb[:, sl], dpab[:, sl])
            dwx_ref[hd] += _dot_tn(xb[:, sl], dpxb[:, sl])
            back.append(_dot_nt(dpab[:, sl], wa_ref[hd]) + _dot_nt(dpxb[:, sl], wx_ref[hd]))
        dxr = dxr + jnp.concatenate(back, axis=1)
        nfills = [dcarry[0:1, :], dcarry[1:2, :], dcarry[2:3, :]]
        dxp = cw_ref[kw - 1:kw, :] * dxr
        for s in range(1, kw):
            dxp = dxp + cw_ref[kw - 1 - s:kw - s, :] * _shift_up(dxr, s, nfills, row)
        dcarry[...] = dxr[0:8, :]
        dxg_ref[:, 0:D] = dxp.astype(MXU_DTYPE)
        dcb_ref[...] += _rowsum8(dxr)
        for k in range(kw):
            dcw_ref[k] += _rowsum8(dxr * taps[k])

    rt = lambda i: nt - 1 - i
    halo = lambda i: jnp.maximum((nt - 1 - i) * (tm // 8) - 1, 0)
    full = lambda a: pl.BlockSpec(a.shape, lambda i: (0,) * a.ndim)
    params = (conv_w, conv_b, w_a, b_a, w_x, b_x, sp, w_out)
    acc = lambda shape: pl.BlockSpec(shape, lambda i: (0,) * len(shape))
    return _hosted(
        body, comm, (dout, xg, xg, hs, hs, *params), name=name,
        out_shape=(jax.ShapeDtypeStruct((T, D2), MXU_DTYPE), jax.ShapeDtypeStruct((T, D), MXU_DTYPE),
                   jax.ShapeDtypeStruct((RG_HEADS, RG_BW, RG_BW), F32), jax.ShapeDtypeStruct((RG_HEADS, RG_BW, RG_BW), F32),
                   jax.ShapeDtypeStruct((8, D), F32), jax.ShapeDtypeStruct((8, D), F32), jax.ShapeDtypeStruct((8, D), F32),
                   jax.ShapeDtypeStruct((8, D), F32), jax.ShapeDtypeStruct((kw, 8, D), F32)),
        grid=(nt,),
        in_specs=[pl.BlockSpec((tm, D), lambda i: (rt(i), 0)), pl.BlockSpec((tm, D2), lambda i: (rt(i), 0)),
                  pl.BlockSpec((8, D), lambda i: (halo(i), 0)), pl.BlockSpec((tm, D), lambda i: (rt(i), 0)),
                  pl.BlockSpec((8, D), lambda i: (halo(i), 0))] + [full(a) for a in params],
        out_specs=(pl.BlockSpec((tm, D2), lambda i: (rt(i), 0)), pl.BlockSpec((tm, D), lambda i: (rt(i), 0)),
                   acc((RG_HEADS, RG_BW, RG_BW)), acc((RG_HEADS, RG_BW, RG_BW)), acc((8, D)), acc((8, D)), acc((8, D)),
                   acc((8, D)), acc((kw, 8, D))),
        scratch_shapes=[pltpu.VMEM((8, D), F32), pltpu.VMEM((8, D), F32), pltpu.VMEM((8, D), F32),
                        pltpu.VMEM((tm, D), F32), pltpu.VMEM((tm, D), F32), pltpu.VMEM((tm, D), F32)],
        compiler_params=_params(("arbitrary",)),
    )


_SBW = 2 * S5_SB * S5_P
_SBH = S5_SB * S5_P
_SBU = S5_SB * S5_GC


def _regroup(x, seg_len):
    n, c = x.shape
    return jnp.swapaxes(x.reshape(8, seg_len, c), 0, 1).reshape(n, c)


def _ungroup(x, seg_len):
    n, c = x.shape
    return jnp.swapaxes(x.reshape(seg_len, 8, c), 0, 1).reshape(n, c)


def _s5_lanes(sb):
    return (slice(sb * _SBW, sb * _SBW + _SBH), slice(sb * _SBW + _SBH, (sb + 1) * _SBW),
            slice(sb * _SBH, (sb + 1) * _SBH))


def _s5_scan_fwd(S, carry, cin, ap_r, ap_i, aq_r, aq_i, seg_len, sb):
    row8 = lax.broadcasted_iota(jnp.int32, (8, 1), 0)
    lr, li, la = _s5_lanes(sb)
    ar, ai = ap_r[0:1, la], ap_i[0:1, la]
    hr, hi = S[0:8, lr], S[0:8, li]
    for i in range(1, seg_len):
        blk = slice(i * 8, (i + 1) * 8)
        hr, hi = ar * hr - ai * hi + S[blk, lr], ar * hi + ai * hr + S[blk, li]
        S[blk, lr] = hr
        S[blk, li] = hi
    for s, idx in ((1, 0), (2, 1), (4, 3)):
        qr, qi = aq_r[idx:idx + 1, la], aq_i[idx:idx + 1, la]
        m = row8 >= s
        sr = jnp.where(m, pltpu.roll(hr, s, 0), 0.0)
        si = jnp.where(m, pltpu.roll(hi, s, 0), 0.0)
        hr, hi = hr + qr * sr - qi * si, hi + qr * si + qi * sr
    cr, ci = carry[0:1, lr], carry[0:1, li]
    pr, pi = aq_r[:, la], aq_i[:, la]
    hr, hi = hr + pr * cr - pi * ci, hi + pr * ci + pi * cr
    xr = jnp.where(row8 == 0, cr, pltpu.roll(hr, 1, 0))
    xi = jnp.where(row8 == 0, ci, pltpu.roll(hi, 1, 0))
    carry[0:1, lr] = hr[7:8, :]
    carry[0:1, li] = hi[7:8, :]
    if cin is not None:
        cin[:, lr] = xr
        cin[:, li] = xi
    for i in range(seg_len):
        blk = slice(i * 8, (i + 1) * 8)
        pr, pi = ap_r[i:i + 1, la], ap_i[i:i + 1, la]
        S[blk, lr] += pr * xr - pi * xi
        S[blk, li] += pr * xi + pi * xr


def s5_fwd(u, h, wb, wc, ap_r, ap_i, aq_r, aq_i, d, w_glu, w_out, tm, name):
    T, D = u.shape
    nt = T // tm
    seg = tm // 8

    def body(u_ref, h_ref, wb_ref, wc_ref, apr_ref, api_ref, aqr_ref, aqi_ref, d_ref, wg_ref, wo_ref,
             o_ref, yp_ref, gl_ref, st_ref, S, carry):
        @pl.when(pl.program_id(0) == 0)
        def _():
            carry[...] = jnp.zeros_like(carry)

        st_ref[...] = carry[...]
        uv = _regroup(u_ref[...], seg)
        ub = uv.astype(MXU_DTYPE)
        for sb in range(S5_NSB):
            S[:, sb * _SBW:(sb + 1) * _SBW] = jnp.dot(ub[:, sb * _SBU:(sb + 1) * _SBU], wb_ref[sb].astype(MXU_DTYPE),
                                                      preferred_element_type=F32)
        for sb in range(S5_NSB):
            _s5_scan_fwd(S, carry, None, apr_ref, api_ref, aqr_ref, aqi_ref, seg, sb)
        ys = [jnp.dot(S[:, sb * _SBW:(sb + 1) * _SBW].astype(MXU_DTYPE), wc_ref[sb].astype(MXU_DTYPE),
                      preferred_element_type=F32) for sb in range(S5_NSB)]
        yp = jnp.concatenate(ys, axis=1) + d_ref[...] * uv
        yp_ref[...] = _ungroup(yp, seg)
        gl = _dot(_gelu(yp), wg_ref[...])
        gl_ref[...] = _ungroup(gl, seg)
        out = gl[:, 0:D] * _sigmoid(gl[:, D:2 * D])
        o_ref[...] = h_ref[...] + _ungroup(_dot(out, wo_ref[...]), seg)

    full = lambda a: pl.BlockSpec(a.shape, lambda i: (0,) * a.ndim)
    args = (u, h, wb, wc, ap_r, ap_i, aq_r, aq_i, d, w_glu, w_out)
    return _call(
        body, name=name,
        out_shape=(jax.ShapeDtypeStruct((T, D), F32), jax.ShapeDtypeStruct((T, D), F32),
                   jax.ShapeDtypeStruct((T, 2 * D), F32), jax.ShapeDtypeStruct((nt, 8, S5_NS), F32)),
        grid=(nt,),
        in_specs=[pl.BlockSpec((tm, D), lambda i: (i, 0)), pl.BlockSpec((tm, D), lambda i: (i, 0))]
        + [full(a) for a in args[2:]],
        out_specs=(pl.BlockSpec((tm, D), lambda i: (i, 0)), pl.BlockSpec((tm, D), lambda i: (i, 0)),
                   pl.BlockSpec((tm, 2 * D), lambda i: (i, 0)), pl.BlockSpec((None, 8, S5_NS), lambda i: (i, 0, 0))),
        scratch_shapes=[pltpu.VMEM((tm, S5_NS), F32), pltpu.VMEM((8, S5_NS), F32)],
        compiler_params=_params(("arbitrary",)),
    )(*args)


def s5_bwd_glu(dout, gl, ypre, u, w_glu, w_out, tm, name):
    T, D = u.shape

    def body(do_ref, gl_ref, yp_ref, u_ref, wg_ref, wo_ref, dy_ref, oact_ref, dgl_ref, gact_ref, dd_ref):
        @pl.when(pl.program_id(0) == 0)
        def _():
            dd_ref[...] = jnp.zeros_like(dd_ref)

        gl1 = gl_ref[:, 0:D]
        sg = _sigmoid(gl_ref[:, D:2 * D])
        oact_ref[...] = (gl1 * sg).astype(MXU_DTYPE)
        dgo = _dot_nt(do_ref[...], wo_ref[...])
        d1 = (dgo * sg).astype(MXU_DTYPE)
        d2 = (dgo * gl1 * sg * (1.0 - sg)).astype(MXU_DTYPE)
        dgl_ref[:, 0:D] = d1
        dgl_ref[:, D:2 * D] = d2
        dg = _dot_nt(d1, wg_ref[:, 0:D]) + _dot_nt(d2, wg_ref[:, D:2 * D])
        g, gd = _gelu_and_grad(yp_ref[...])
        gact_ref[...] = g.astype(MXU_DTYPE)
        dy = dg * gd
        dy_ref[...] = dy
        dd_ref[...] += _rowsum8(dy * u_ref[...])

    full = lambda a: pl.BlockSpec(a.shape, lambda i: (0,) * a.ndim)
    rs = lambda c: pl.BlockSpec((tm, c), lambda i: (i, 0))
    return _call(
        body, name=name,
        out_shape=(jax.ShapeDtypeStruct((T, D), F32), jax.ShapeDtypeStruct((T, D), MXU_DTYPE),
                   jax.ShapeDtypeStruct((T, 2 * D), MXU_DTYPE), jax.ShapeDtypeStruct((T, D), MXU_DTYPE),
                   jax.ShapeDtypeStruct((8, D), F32)),
        grid=(T // tm,),
        in_specs=[rs(D), rs(2 * D), rs(D), rs(D), full(w_glu), full(w_out)],
        out_specs=(rs(D), rs(D), rs(2 * D), rs(D), pl.BlockSpec((8, D), lambda i: (0, 0))),
        compiler_params=_params(("arbitrary",)),
    )(dout, gl, ypre, u, w_glu, w_out)


def s5_bwd_ssm(dy, u, st, wb, wc, tables, d, tm, name, comm=None):
    T, D = u.shape
    nt = T // tm
    seg = tm // 8
    GP = S5_G * S5_P
    assert len(tables) == 8

    def body(dy_ref, u_ref, st_ref, wb_ref, wc_ref, apr_ref, api_ref, aqr_ref, aqi_ref, aprr_ref, apir_ref,
             aqrr_ref, aqir_ref, d_ref, du_ref, dwb_ref, dwc_ref, dar_ref, dai_ref, S, L, carry, lcarry, cin):
        @pl.when(pl.program_id(0) == 0)
        def _():
            for ref in (lcarry, dwb_ref, dwc_ref, dar_ref, dai_ref):
                ref[...] = jnp.zeros_like(ref)

        uv = _regroup(u_ref[...], seg)
        ub = uv.astype(MXU_DTYPE)
        dyv = _regroup(dy_ref[...], seg)
        dyb = dyv.astype(MXU_DTYPE)
        carry[...] = st_ref[...]
        row8 = lax.broadcasted_iota(jnp.int32, (8, 1), 0)
        dus = []
        for sb in range(S5_NSB):
            ls = slice(sb * _SBW, (sb + 1) * _SBW)
            us = slice(sb * _SBU, (sb + 1) * _SBU)
            S[:, ls] = jnp.dot(ub[:, us], wb_ref[sb].astype(MXU_DTYPE), preferred_element_type=F32)
            L[:, ls] = _dot_nt(dyb[:, us], wc_ref[sb])
            _s5_scan_fwd(S, carry, cin, apr_ref, api_ref, aqr_ref, aqi_ref, seg, sb)
            lr, li, la = _s5_lanes(sb)
            ar, ai = apr_ref[0:1, la], api_ref[0:1, la]
            last = slice((seg - 1) * 8, seg * 8)
            gr, gi = L[last, lr], L[last, li]
            for i in reversed(range(seg - 1)):
                blk = slice(i * 8, (i + 1) * 8)
                gr, gi = ar * gr + ai * gi + L[blk, lr], ar * gi - ai * gr + L[blk, li]
                L[blk, lr] = gr
                L[blk, li] = gi
            for s, idx in ((1, 0), (2, 1), (4, 3)):
                qr, qi = aqr_ref[idx:idx + 1, la], aqi_ref[idx:idx + 1, la]
                m = row8 < 8 - s
                sr = jnp.where(m, pltpu.roll(gr, 8 - s, 0), 0.0)
                si = jnp.where(m, pltpu.roll(gi, 8 - s, 0), 0.0)
                gr, gi = gr + qr * sr + qi * si, gi + qr * si - qi * sr
            cr, ci = lcarry[0:1, lr], lcarry[0:1, li]
            pr, pi = aqrr_ref[:, la], aqir_ref[:, la]
            gr, gi = gr + pr * cr + pi * ci, gi + pr * ci - pi * cr
            xr = jnp.where(row8 == 7, cr, pltpu.roll(gr, 7, 0))
            xi = jnp.where(row8 == 7, ci, pltpu.roll(gi, 7, 0))
            lcarry[0:1, lr] = gr[0:1, :]
            lcarry[0:1, li] = gi[0:1, :]
            acc_r = jnp.zeros((8, _SBH), F32)
            acc_i = jnp.zeros((8, _SBH), F32)
            for i in range(seg):
                blk = slice(i * 8, (i + 1) * 8)
                pr, pi = aprr_ref[i:i + 1, la], apir_ref[i:i + 1, la]
                gr = L[blk, lr] + pr * xr + pi * xi
                gi = L[blk, li] + pr * xi - pi * xr
                L[blk, lr] = gr
                L[blk, li] = gi
                if i == 0:
                    hpr, hpi = cin[:, lr], cin[:, li]
                else:
                    hpr, hpi = S[(i - 1) * 8:i * 8, lr], S[(i - 1) * 8:i * 8, li]
                acc_r = acc_r + gr * hpr + gi * hpi
                acc_i = acc_i + gi * hpr - gr * hpi
            dar_ref[:, la] += acc_r
            dai_ref[:, la] += acc_i
            lb = L[:, ls].astype(MXU_DTYPE)
            dwb_ref[sb] += _dot_tn(lb, ub[:, us])
            dwc_ref[sb] += _dot_tn(S[:, ls].astype(MXU_DTYPE), dyb[:, us])
            dus.append(_dot_nt(lb, wb_ref[sb]))
        du_ref[...] = _ungroup(jnp.concatenate(dus, axis=1) + dyv * d_ref[...], seg).astype(MXU_DTYPE)

    rt = lambda i: nt - 1 - i
    full = lambda a: pl.BlockSpec(a.shape, lambda i: (0,) * a.ndim)
    acc = lambda shape: pl.BlockSpec(shape, lambda i: (0,) * len(shape))
    return _hosted(
        body, comm, (dy, u, st, wb, wc, *tables, d), name=name,
        out_shape=(jax.ShapeDtypeStruct((T, D), MXU_DTYPE), jax.ShapeDtypeStruct((S5_NSB, _SBW, _SBU), F32),
                   jax.ShapeDtypeStruct((S5_NSB, _SBW, _SBU), F32), jax.ShapeDtypeStruct((8, GP), F32),
                   jax.ShapeDtypeStruct((8, GP), F32)),
        grid=(nt,),
        in_specs=[pl.BlockSpec((tm, D), lambda i: (rt(i), 0)), pl.BlockSpec((tm, D), lambda i: (rt(i), 0)),
                  pl.BlockSpec((None, 8, S5_NS), lambda i: (rt(i), 0, 0)), full(wb), full(wc)]
        + [full(t) for t in tables] + [full(d)],
        out_specs=(pl.BlockSpec((tm, D), lambda i: (rt(i), 0)), acc((S5_NSB, _SBW, _SBU)), acc((S5_NSB, _SBW, _SBU)),
                   acc((8, GP)), acc((8, GP))),
        scratch_shapes=[pltpu.VMEM((tm, S5_NS), F32), pltpu.VMEM((tm, S5_NS), F32), pltpu.VMEM((8, S5_NS), F32),
                        pltpu.VMEM((8, S5_NS), F32), pltpu.VMEM((8, S5_NS), F32)],
        compiler_params=_params(("arbitrary",)),
    )


def final_loss(h, g, target, tm, name):
    T, D = h.shape

    def body(h_ref, g_ref, t_ref, dh_ref, se_ref, dg_ref):
        @pl.when(pl.program_id(0) == 0)
        def _():
            se_ref[...] = jnp.zeros_like(se_ref)
            dg_ref[...] = jnp.zeros_like(dg_ref)

        r, xhat = _rms(h_ref[...])
        gv = g_ref[...]
        e = xhat * gv - t_ref[...]
        se_ref[...] += _rowsum8(e * e)
        dy = e * (1.0 / D)
        dg_ref[...] += _rowsum8(dy * xhat)
        dh_ref[...] = _rms_bwd(dy, xhat, r, gv)

    rs = pl.BlockSpec((tm, D), lambda i: (i, 0))
    acc = pl.BlockSpec((8, D), lambda i: (0, 0))
    return _call(
        body, name=name,
        out_shape=(jax.ShapeDtypeStruct((T, D), F32), jax.ShapeDtypeStruct((8, D), F32), jax.ShapeDtypeStruct((8, D), F32)),
        grid=(T // tm,), in_specs=[rs, pl.BlockSpec((1, D), lambda i: (0, 0)), rs], out_specs=(rs, acc, acc),
        compiler_params=_params(("arbitrary",)),
    )(h, g, target)


def _s5_discretize(a_re, a_im, log_dt, b_re, b_im):
    dt = jnp.exp(log_dt)[:, None]
    mag = jnp.exp(a_re * dt)
    abr = mag * jnp.cos(a_im * dt)
    abi = mag * jnp.sin(a_im * dt)
    ur, ui = abr - 1.0, abi
    den = a_re * a_re + a_im * a_im
    wr = (ur * a_re + ui * a_im) / den
    wi = (ui * a_re - ur * a_im) / den
    bbr = wr[..., None] * b_re - wi[..., None] * b_im
    bbi = wr[..., None] * b_im + wi[..., None] * b_re
    return abr, abi, bbr, bbi


def _s5_pack(abr, abi, bbr, bbi, c_re, c_im, seg_len):
    eye = jnp.eye(S5_SB, dtype=F32)
    b = jnp.stack([bbr, bbi], 0).reshape(2, S5_NSB, S5_SB, S5_P, S5_GC)
    wb = jnp.einsum('rsgpc,gh->shcrgp', b, eye).reshape(S5_NSB, _SBU, _SBW)
    c = jnp.stack([c_re, -c_im], 0).reshape(2, S5_NSB, S5_SB, S5_GC, S5_P)
    wc = jnp.einsum('rsgcp,gh->srgphc', c, eye).reshape(S5_NSB, _SBW, _SBU)
    def powers(r, i, n):
        fr, fi, br, bi = r, i, r, i
        m = 1
        while m < n:
            tr, ti = fr[m - 1:m], fi[m - 1:m]
            fr, fi = (jnp.concatenate([fr, fr * tr - fi * ti], 0), jnp.concatenate([fi, fr * ti + fi * tr], 0))
            br, bi = (jnp.concatenate([br * tr - bi * ti, br], 0), jnp.concatenate([br * ti + bi * tr, bi], 0))
            m *= 2
        return fr, fi, br, bi

    assert seg_len & (seg_len - 1) == 0
    ap = powers(abr.reshape(1, -1), abi.reshape(1, -1), seg_len)
    aq = powers(ap[0][seg_len - 1:seg_len], ap[1][seg_len - 1:seg_len], 8)
    return wb.astype(MXU_DTYPE), wc.astype(MXU_DTYPE), ap[0], ap[1], aq[0], aq[1], ap[2], ap[3], aq[2], aq[3]


def _s5_unpack_grads(dwb_t, dwc, dar8, dai8):
    eye = jnp.eye(S5_SB, dtype=F32)
    t = dwb_t.reshape(S5_NSB, 2, S5_SB, S5_P, S5_SB, S5_GC)
    db = jnp.einsum('srgphc,gh->rsgpc', t, eye).reshape(2, S5_G, S5_P, S5_GC)
    t = dwc.reshape(S5_NSB, 2, S5_SB, S5_P, S5_SB, S5_GC)
    dc = jnp.einsum('srgphc,gh->rsgcp', t, eye).reshape(2, S5_G, S5_GC, S5_P)
    return db[0], db[1], dc[0], -dc[1], dar8.sum(0).reshape(S5_G, S5_P), dai8.sum(0).reshape(S5_G, S5_P)


TM = 256
TM_FFN = 512
TM_S5 = 256


def _tn(a, b, name, comm=None):
    T, M, N = a.shape[0], a.shape[1], b.shape[1]
    bt = 4096 if a.dtype.itemsize + b.dtype.itemsize <= 4 else 2048
    return matmul_tn(a, b, min(M, 1024), min(N, 1024), bt if T % bt == 0 else T, name, comm=comm)


def local_step(x, target, W, sched):
    T, D = x.shape
    depth = W['norm_mix_g'].shape[0]
    row = lambda v: v.reshape(1, -1)
    saved = []
    h = x
    s5c = []
    tr = min(512, T)
    for j in range(W['s5_a_re'].shape[0]):
        prm = (W['s5_a_re'][j], W['s5_a_im'][j], W['s5_log_dt'][j], W['s5_b_re'][j], W['s5_b_im'][j])
        disc, disc_vjp = jax.vjp(_s5_discretize, *prm)
        s5c.append((*_s5_pack(*disc, W['s5_c_re'][j], W['s5_c_im'][j], min(TM_S5, T) // 8), disc_vjp))
    sp_all = jax.nn.softplus(-W['rg_lambda'])
    for i in range(depth):
        j = i // 2
        if i % 2 == 0:
            xg, hn = norm_matmul(h, row(W['norm_mix_g'][i]), W['rg_w_in'][j], F32, tr, None, f"rg_in_{i}")
            h1, hs = rg_fwd(xg, h, W['rg_conv_w'][j], row(W['rg_conv_b'][j]), W['rg_w_a'][j].astype(MXU_DTYPE),
                            row(W['rg_b_a'][j]), W['rg_w_x'][j].astype(MXU_DTYPE), row(W['rg_b_x'][j]), row(sp_all[j]),
                            W['rg_w_out'][j], TM, f"rg_fwd_{i}", comm=sched.comm(f"rg_fwd_{i}"))
            mix = (xg, hn, hs)
        else:
            u, hn = norm_matmul(h, row(W['norm_mix_g'][i]), W['s5_w_in'][j], F32, tr, None, f"s5_in_{i}")
            h1, ypre, gl, st = s5_fwd(u, h, *s5c[j][:6], row(W['s5_d'][j]), W['s5_w_glu'][j], W['s5_w_out'][j],
                                      min(TM_S5, T), f"s5_fwd_{i}")
            mix = (u, hn, ypre, gl, st)
        h2, hn2, up, av = ffn_block_fwd(h1, row(W['norm_ffn_g'][i]), W['ffn_w_up'][i], W['ffn_conv_w'][i],
                                        row(W['ffn_conv_b'][i]), W['ffn_w_down'][i], TM, f"ffn_fwd_{i}",
                                        comm=sched.comm(f"ffn_fwd_{i}"))
        saved.append((h, mix, h1, hn2, up, av))
        h = h2
    dh, se8, dgf8 = final_loss(h, row(W['norm_final_g']), target, tr, "final_loss")
    G = {k: [None] * len(v) for k, v in W.items() if k != 'norm_final_g'}
    G['norm_final_g'] = dgf8.sum(0)
    for i in reversed(range(depth)):
        j = i // 2
        h0, mix, h1, hn2, up, av = saved[i]
        dh1, dg8, dupp, act, dcb8, dcw8 = ffn_bwd(dh, av, up, W['ffn_w_down'][i], W['ffn_conv_w'][i], W['ffn_w_up'][i],
                                                  h1, row(W['norm_ffn_g'][i]), TM, f"ffn_bwd_{i}", chunk=1024,
                                                  comm=sched.comm(f"ffn_bwd_act_{i}", G))
        G['ffn_w_down'][i] = _tn(act, dh, f"ffn_dwdown_{i}")
        G['ffn_w_up'][i] = _tn(hn2, dupp, f"ffn_dwup_{i}")
        G['ffn_conv_b'][i] = dcb8.sum(0)
        G['ffn_conv_w'][i] = dcw8.sum(1)
        G['norm_ffn_g'][i] = dg8.sum(0)
        if i % 2 == 0:
            xg, hn, hs = mix
            dxg, y, dwa, dwx, dba8, dbx8, dsp8, dcb8, dcw8 = rg_bwd(
                dh1, xg, hs, W['rg_conv_w'][j], row(W['rg_conv_b'][j]), W['rg_w_a'][j].astype(MXU_DTYPE),
                row(W['rg_b_a'][j]), W['rg_w_x'][j].astype(MXU_DTYPE), row(W['rg_b_x'][j]), row(sp_all[j]),
                W['rg_w_out'][j], TM, f"rg_bwd_{i}", comm=sched.comm(f"rg_bwd_{i}", G))
            G['rg_w_out'][j] = _tn(y, dh1, f"rg_dwout_{i}")
            G['rg_w_a'][j], G['rg_w_x'][j] = dwa, dwx
            G['rg_b_a'][j] = dba8.sum(0).reshape(RG_HEADS, RG_BW)
            G['rg_b_x'][j] = dbx8.sum(0).reshape(RG_HEADS, RG_BW)
            G['rg_lambda'][j] = dsp8.sum(0) * (-jax.nn.sigmoid(-W['rg_lambda'][j]))
            G['rg_conv_b'][j] = dcb8.sum(0)
            G['rg_conv_w'][j] = dcw8.sum(1)
            dh, dg8 = dx_norm_bwd(dxg, W['rg_w_in'][j], h0, row(W['norm_mix_g'][i]), dh1, TM, f"rg_bwd_in_{i}",
                                  comm=sched.comm(f"rg_bwd_in_{i}", G))
            G['rg_w_in'][j] = _tn(hn, dxg, f"rg_dwin_{i}", comm=sched.comm(f"rg_dwin_{i}", G))
        else:
            u, hn, ypre, gl, st = mix
            disc_vjp = s5c[j][-1]
            dy, oact, dgl, gact, dd8 = s5_bwd_glu(dh1, gl, ypre, u, W['s5_w_glu'][j], W['s5_w_out'][j], TM,
                                                  f"s5_bwd_glu_{i}")
            G['s5_w_out'][j] = _tn(oact, dh1, f"s5_dwout_{i}")
            G['s5_w_glu'][j] = _tn(gact, dgl, f"s5_dwglu_{i}")
            du, dwb_t, dwc, dar8, dai8 = s5_bwd_ssm(dy, u, st, *s5c[j][:2], s5c[j][2:10], row(W['s5_d'][j]),
                                                    min(TM_S5, T), f"s5_bwd_ssm_{i}",
                                                    comm=sched.comm(f"s5_bwd_ssm_{i}", G))
            dh, dg8 = dx_norm_bwd(du, W['s5_w_in'][j], h0, row(W['norm_mix_g'][i]), dh1, TM, f"s5_bwd_in_{i}")
            G['s5_w_in'][j] = _tn(hn, du, f"s5_dwin_{i}")
            dbbr, dbbi, dcr, dci, dabr, dabi = _s5_unpack_grads(dwb_t, dwc, dar8, dai8)
            da_re, da_im, dlog_dt, db_re, db_im = disc_vjp((dabr, dabi, dbbr, dbbi))
            G['s5_a_re'][j], G['s5_a_im'][j], G['s5_log_dt'][j] = da_re, da_im, dlog_dt
            G['s5_b_re'][j], G['s5_b_im'][j], G['s5_c_re'][j], G['s5_c_im'][j] = db_re, db_im, dcr, dci
            G['s5_d'][j] = dd8.sum(0)
        G['norm_mix_g'][i] = dg8.sum(0)
    G = {k: (v if (k == 'norm_final_g' or k in BIG) else jnp.stack(v, 0)) for k, v in G.items()}
    return se8, dh, G


BIG = {'rg_w_in': 1, 'rg_w_out': 0, 's5_w_in': 0, 's5_w_glu': 1, 's5_w_out': 0, 'ffn_w_up': 1, 'ffn_w_down': 0}
SMALL_SHARDED = ('rg_conv_w', 'ffn_conv_w', 's5_d')
REPLICATED = ('norm_mix_g', 'norm_ffn_g', 'norm_final_g', 'rg_conv_b', 'rg_w_a', 'rg_b_a', 'rg_w_x', 'rg_b_x',
              'rg_lambda', 's5_a_re', 's5_a_im', 's5_log_dt', 's5_b_re', 's5_b_im', 's5_c_re', 's5_c_im', 'ffn_conv_b')
REP_LATE = ('norm_mix_g',)
REP_MAIN = tuple(k for k in REPLICATED if k not in REP_LATE)


def _me():
    x, y, c = lax.axis_index("x"), lax.axis_index("y"), lax.axis_index("c")
    return x, y, c, 4 * x + 2 * y + c


def _win(ref, axis, dev, width):
    idx = [slice(None)] * len(ref.shape)
    idx[axis] = pl.ds(pl.multiple_of(dev * width, width), width)
    return ref.at[tuple(idx)]


def gather_plan(items, deliver):
    n = len(items)

    def tools(ins, outs, sems):
        send_sems, recv_sems, local_sems = sems
        x, y, c, me = _me()
        sib = (x, y, 1 - c)
        chips = [(1 - x, y), (x, 1 - y), (1 - x, 1 - y)]
        num = lambda px, py, pc: 4 * px + 2 * py + pc

        def src_of(a):
            return ins[a] if items[a][1] is None else ins[a].at[items[a][1]]

        def block(a, dev):
            return _win(outs[a], items[a][2], dev, src_of(a).shape[items[a][2]])

        def copy(a, k, dev, to, own=False):
            return pltpu.make_async_remote_copy(
                src_ref=src_of(a) if own else block(a, dev), dst_ref=block(a, dev),
                send_sem=send_sems.at[a, k], recv_sem=recv_sems.at[a, k], device_id=to, device_id_type=MESH)

        mine = lambda: [pltpu.make_async_copy(src_of(a), block(a, me), local_sems.at[a]) for a in range(n)]
        own = lambda: [cp for a in range(n) for cp in
                       [copy(a, 0, me, sib, own=True)] + [copy(a, 1 + j, me, (*chip, c), own=True)
                                                          for j, chip in enumerate(chips)]]
        arrived = lambda j, a: copy(a, 1 + j, num(*chips[j], c), (x, y, c))
        passed = lambda j, a: copy(a, 4 + j, num(*chips[j], c), sib)
        from_sib = lambda: ([copy(a, 0, num(x, y, 1 - c), (x, y, c)) for a in range(n)]
                            + [copy(a, 4 + j, num(*chip, 1 - c), (x, y, c)) for j, chip in enumerate(chips)
                               for a in range(n)])
        return mine, own, arrived, passed, from_sib

    def start(ins, outs, sems):
        mine, own, _, _, _ = tools(ins, outs, sems)
        for cp in mine() + own():
            cp.start()

    def middle(ins, outs, sems):
        _, _, arrived, passed, _ = tools(ins, outs, sems)
        for j in range(3):
            for a in range(n):
                arrived(j, a).wait_recv()
                passed(j, a).start()

    def finish(ins, outs, sems):
        mine, own, _, passed, from_sib = tools(ins, outs, sems)
        for cp in from_sib():
            cp.wait_recv()
        for cp in own() + [passed(j, a) for j in range(3) for a in range(n)]:
            cp.wait_send()
        for cp in mine():
            cp.wait()

    return Comm([it[0] for it in items], [jax.ShapeDtypeStruct(it[3], it[0].dtype) for it in items],
                [pltpu.SemaphoreType.DMA((n, 7)), pltpu.SemaphoreType.DMA((n, 7)), pltpu.SemaphoreType.DMA((n,))],
                start, middle, finish, deliver)


def exchange_plan(items, deliver):
    n = len(items)
    width = [arr.shape[axis] // N_DEV for arr, axis in items]
    shard = [arr.shape[:axis] + (w,) + arr.shape[axis + 1:] for (arr, axis), w in zip(items, width)]

    def tools(ins, outs, sems):
        send_sems, recv_sems, local_sems = sems
        x, y, c, me = _me()
        piece = lambda a, dev: _win(ins[a], items[a][1], dev, width[a])
        mine = lambda: [pltpu.make_async_copy(piece(a, me), outs[a].at[me], local_sems.at[a]) for a in range(n)]

        def remote(sending):
            cps = []
            for k in range(1, N_DEV):
                px, py, pc = (1 - x) if k & 4 else x, (1 - y) if k & 2 else y, (1 - c) if k & 1 else c
                peer = 4 * px + 2 * py + pc
                for a in range(n):
                    src, dst = (piece(a, peer), outs[a].at[me]) if sending else (piece(a, me), outs[a].at[peer])
                    cps.append(pltpu.make_async_remote_copy(
                        src_ref=src, dst_ref=dst, send_sem=send_sems.at[a, k - 1], recv_sem=recv_sems.at[a, k - 1],
                        device_id=(px, py, pc), device_id_type=MESH))
            return cps

        return mine, remote

    def start(ins, outs, sems):
        mine, remote = tools(ins, outs, sems)
        for cp in mine() + remote(True):
            cp.start()

    def middle(ins, outs, sems):
        pass

    def finish(ins, outs, sems):
        mine, remote = tools(ins, outs, sems)
        for cp in remote(False):
            cp.wait_recv()
        for cp in remote(True):
            cp.wait_send()
        for cp in mine():
            cp.wait()

    return Comm([it[0] for it in items], [jax.ShapeDtypeStruct((N_DEV,) + s, it[0].dtype) for it, s in zip(items, shard)],
                [pltpu.SemaphoreType.DMA((n, 7)), pltpu.SemaphoreType.DMA((n, 7)), pltpu.SemaphoreType.DMA((n,))],
                start, middle, finish, deliver)


def adam_update(parts, w, m, v, name):
    R, C = w.shape
    br = next((b for b in (256, 128) if R > b and R % b == 0), R)
    np_ = parts.shape[0]

    def body(p_ref, w_ref, m_ref, v_ref, g_ref, d_ref, nm_ref, nv_ref):
        _adam_body(np_, p_ref, w_ref, m_ref, v_ref, g_ref, d_ref, nm_ref, nv_ref)

    bs = pl.BlockSpec((br, C), lambda i: (i, 0))
    out = jax.ShapeDtypeStruct((R, C), F32)
    return _call(
        body, name=name, out_shape=(out, out, out, out), grid=(R // br,),
        in_specs=[pl.BlockSpec((np_, br, C), lambda i: (0, i, 0)), bs, bs, bs], out_specs=(bs, bs, bs, bs),
        compiler_params=_params(("parallel",)),
    )(parts, w, m, v)


def _adam_body(np_, p_ref, w_ref, m_ref, v_ref, g_ref, d_ref, nm_ref, nv_ref):
    c1 = 1.0 / (1.0 - ADAM_B1 ** ADAM_STEP)
    c2 = 1.0 / (1.0 - ADAM_B2 ** ADAM_STEP)
    g = p_ref[0].astype(F32)
    for p in range(1, np_):
        g = g + p_ref[p].astype(F32)
    nm = ADAM_B1 * m_ref[...] + (1.0 - ADAM_B1) * g
    nv = ADAM_B2 * v_ref[...] + (1.0 - ADAM_B2) * (g * g)
    g_ref[...] = g
    nm_ref[...] = nm
    nv_ref[...] = nv
    d_ref[...] = -ADAM_LR * ((nm * c1) / (jnp.sqrt(nv * c2) + ADAM_EPS) + ADAM_WD * w_ref[...])


def adam_layer(parts, w, m, v, l, prev, name):
    L, R, C = w.shape
    br = next((b for b in (256, 128) if R > b and R % b == 0), R)

    def body(p_ref, w_ref, m_ref, v_ref, *rest):
        _adam_body(N_DEV, p_ref, w_ref, m_ref, v_ref, *rest[-4:])

    bs = pl.BlockSpec((None, br, C), lambda i: (l, i, 0))
    out = jax.ShapeDtypeStruct((L, R, C), F32)
    extra = {} if prev is None else dict(input_output_aliases={4 + q: q for q in range(4)})
    return _call(
        body, name=name, out_shape=(out, out, out, out), grid=(R // br,),
        in_specs=[pl.BlockSpec((N_DEV, br, C), lambda i: (0, i, 0)), bs, bs, bs] + ([] if prev is None else [ANY] * 4),
        out_specs=(bs, bs, bs, bs), compiler_params=_params(("parallel",)), **extra,
    )(parts, w, m, v, *(() if prev is None else prev))


def sum_parts(parts, name):
    n, R, C = parts.shape

    def body(p_ref, o_ref):
        g = p_ref[0]
        for p in range(1, n):
            g = g + p_ref[p]
        o_ref[...] = g

    return _call(body, name=name, out_shape=jax.ShapeDtypeStruct((R, C), parts.dtype),
                 compiler_params=pltpu.CompilerParams(vmem_limit_bytes=VMEM_LIMIT))(parts)


def _pack_rows(arrs):
    rows = []
    for a in arrs:
        f = a.reshape(-1)
        r = -(-f.shape[0] // 1024)
        r8 = -(-r // 8) * 8
        rows.append(jnp.pad(f, (0, r8 * 1024 - f.shape[0])).reshape(r8, 1024))
    packed = jnp.concatenate(rows, 0)
    return jnp.pad(packed, ((0, -packed.shape[0] % 128), (0, 0)))


def _unpack_rows(packed, shapes):
    out, o = [], 0
    for s in shapes:
        nel = math.prod(s)
        r8 = -(-(-(-nel // 1024)) // 8) * 8
        out.append(packed[o:o + r8].reshape(-1)[:nel].reshape(s))
        o += r8
    return out


def kernel(x, norm_mix_g, norm_ffn_g, norm_final_g, rg_w_in, rg_conv_w, rg_conv_b, rg_w_a, rg_b_a, rg_w_x, rg_b_x, rg_lambda, rg_w_out, s5_w_in, s5_a_re, s5_a_im, s5_log_dt, s5_b_re, s5_b_im, s5_c_re, s5_c_im, s5_d, s5_w_glu, s5_w_out, ffn_w_up, ffn_conv_w, ffn_conv_b, ffn_w_down, loss_target, m_norm_mix_g, m_norm_ffn_g, m_norm_final_g, m_rg_w_in, m_rg_conv_w, m_rg_conv_b, m_rg_w_a, m_rg_b_a, m_rg_w_x, m_rg_b_x, m_rg_lambda, m_rg_w_out, m_s5_w_in, m_s5_a_re, m_s5_a_im, m_s5_log_dt, m_s5_b_re, m_s5_b_im, m_s5_c_re, m_s5_c_im, m_s5_d, m_s5_w_glu, m_s5_w_out, m_ffn_w_up, m_ffn_conv_w, m_ffn_conv_b, m_ffn_w_down, v_norm_mix_g, v_norm_ffn_g, v_norm_final_g, v_rg_w_in, v_rg_conv_w, v_rg_conv_b, v_rg_w_a, v_rg_b_a, v_rg_w_x, v_rg_b_x, v_rg_lambda, v_rg_w_out, v_s5_w_in, v_s5_a_re, v_s5_a_im, v_s5_log_dt, v_s5_b_re, v_s5_b_im, v_s5_c_re, v_s5_c_im, v_s5_d, v_s5_w_glu, v_s5_w_out, v_ffn_w_up, v_ffn_conv_w, v_ffn_conv_b, v_ffn_w_down):
    names = ('norm_mix_g', 'norm_ffn_g', 'norm_final_g', 'rg_w_in', 'rg_conv_w', 'rg_conv_b', 'rg_w_a', 'rg_b_a',
             'rg_w_x', 'rg_b_x', 'rg_lambda', 'rg_w_out', 's5_w_in', 's5_a_re', 's5_a_im', 's5_log_dt', 's5_b_re',
             's5_b_im', 's5_c_re', 's5_c_im', 's5_d', 's5_w_glu', 's5_w_out', 'ffn_w_up', 'ffn_conv_w', 'ffn_conv_b',
             'ffn_w_down')
    loc = locals()
    Wl = {k: loc[k] for k in names}
    Ml = {k: loc['m_' + k] for k in names}
    Vl = {k: loc['v_' + k] for k in names}

    depth = norm_mix_g.shape[0]
    mixer_keys = lambda i: ([('rg_w_in', i // 2), ('rg_w_out', i // 2)] if i % 2 == 0 else
                            [('s5_w_in', i // 2), ('s5_w_glu', i // 2), ('s5_w_out', i // 2)])
    ffn_keys = lambda i: [('ffn_w_up', i), ('ffn_w_down', i)]
    shards = {k: Wl[k].astype(BF16) for k in BIG}
    W = {k: Wl[k] for k in REPLICATED}
    W.update({k: [None] * Wl[k].shape[0] for k in BIG})
    parts = {}

    def gather_of(keys, small=False):
        items = []
        for k, l in keys:
            _, r, c = shards[k].shape
            items.append((shards[k], l, BIG[k], (r * N_DEV, c) if BIG[k] == 0 else (r, c * N_DEV)))
        if small:
            items += [(Wl[k], None, Wl[k].ndim - 1, Wl[k].shape[:-1] + (Wl[k].shape[-1] * N_DEV,)) for k in SMALL_SHARDED]

        def deliver(outs):
            for (k, l), arr in zip(keys, outs):
                W[k][l] = arr
            if small:
                W.update(zip(SMALL_SHARDED, outs[len(keys):]))

        return gather_plan(items, deliver)

    def exchange_of(keys, G, extra=()):
        items = [(G[k][l], BIG[k]) for k, l in keys] + [(arr, axis) for _, arr, axis in extra]
        return exchange_plan(items, lambda outs: parts.update(zip(list(keys) + [e[0] for e in extra], outs)))

    class Sched:
        @staticmethod
        def comm(host, G=None):
            kind, _, i = host.rpartition("_")
            i = int(i)
            if host == "rg_fwd_0":
                return gather_of(ffn_keys(0))
            if kind == "ffn_fwd" and i + 1 < depth:
                return gather_of(mixer_keys(i + 1) + ffn_keys(i + 1))
            if kind == "ffn_bwd_act" and i + 1 < depth:
                return exchange_of(mixer_keys(i + 1), G)
            if kind in ("rg_bwd", "s5_bwd_ssm"):
                return exchange_of(ffn_keys(i), G)
            if host == "rg_bwd_in_0":
                stack = lambda k: G[k] if k == 'norm_final_g' else jnp.stack(G[k], 0)
                extra = [(k, stack(k), Wl[k].ndim - 1) for k in SMALL_SHARDED]
                extra.append(('rep_main', _pack_rows([stack(k).astype(F32) for k in REP_MAIN]), 0))
                return exchange_of(mixer_keys(0)[1:], G, extra)
            if host == "rg_dwin_0":
                rsum = sum_parts(parts['rep_main'], "sum_replicated")
                return gather_plan([(rsum, None, 0, (rsum.shape[0] * N_DEV, rsum.shape[1]))],
                                   lambda outs: parts.update(rep_main_full=outs[0]))
            return None

    run_comm(gather_of(mixer_keys(0), small=True), "gather_first")

    se8, gx, G = local_step(x[0], loss_target[0], W, Sched)
    loss = lax.psum(0.5 * jnp.sum(se8) / x.shape[-1], ("x", "y", "c"))

    gp_late = _pack_rows([G[k].astype(F32) for k in REP_LATE])
    run_comm(exchange_of(mixer_keys(0)[:1], G, [('rep_late', gp_late, 0)]), "exchange_last")
    out_g, out_d, out_m, out_v = {}, {}, {}, {}
    for k in BIG:
        res = None
        for l in range(Wl[k].shape[0]):
            res = adam_layer(parts[(k, l)], Wl[k], Ml[k], Vl[k], l, res, f"adam_{k}_{l}")
        out_g[k], out_d[k], out_m[k], out_v[k] = res
    for k in SMALL_SHARDED:
        shp = Wl[k].shape
        r2 = (math.prod(shp[:-1]), shp[-1])
        res = adam_update(parts[k].reshape((N_DEV,) + r2), Wl[k].reshape(r2), Ml[k].reshape(r2), Vl[k].reshape(r2),
                          f"adam_{k}")
        out_g[k], out_d[k], out_m[k], out_v[k] = [t.reshape(shp) for t in res]
    rsum = sum_parts(parts['rep_late'], "sum_replicated_late")
    run_comm(gather_plan([(rsum, None, 0, gp_late.shape)], lambda outs: parts.update(rep_late_full=outs[0])),
             "gather_small_grads")
    for keys, full, tag in ((REP_MAIN, parts['rep_main_full'], "main"), (REP_LATE, parts['rep_late_full'], "late")):
        res = adam_update(full[None], _pack_rows([Wl[k] for k in keys]), _pack_rows([Ml[k] for k in keys]),
                          _pack_rows([Vl[k] for k in keys]), f"adam_replicated_{tag}")
        for dst, packed in zip((out_g, out_d, out_m, out_v), res):
            dst.update(zip(keys, _unpack_rows(packed, [Wl[k].shape for k in keys])))
    return (loss, gx[None], *[out_g[k] for k in names], *[out_d[k] for k in names], *[out_m[k] for k in names],
            *[out_v[k] for k in names])
```
